```python
import jax, jax.numpy as jnp
from jax import lax
import numpy as np

D_MODEL = 1024
BATCH = 8
SEQ = 8192
DEPTH = 1

N_META = 16
EPS = 1e-6
DN_HEADS = 8
DN_DK = 128
DN_DV = 128
DN_CONV = 4
CHUNK = 64
QK_W = DN_HEADS * DN_DK
V_W = DN_HEADS * DN_DV
CF_CH = D_MODEL
CF_KERNEL = 31
IN_SIZES = (QK_W, QK_W, V_W, V_W, DN_HEADS, DN_HEADS, 2 * CF_CH, CF_CH, D_MODEL, D_MODEL)
IN_W = sum(IN_SIZES)

kernel_name = "hybrid_gdn_conformer_gated_merge"


def rmsnorm(x, w):
    xf = x.astype(jnp.float32)
    y = xf * lax.rsqrt(jnp.mean(xf * xf, axis=-1, keepdims=True) + EPS)
    return (y * w.astype(jnp.float32)).astype(x.dtype)


def layernorm(x, w, b):
    xf = x.astype(jnp.float32)
    mu = jnp.mean(xf, axis=-1, keepdims=True)
    var = jnp.mean(jnp.square(xf - mu), axis=-1, keepdims=True)
    y = (xf - mu) * lax.rsqrt(var + EPS)
    return (y * w.astype(jnp.float32) + b.astype(jnp.float32)).astype(x.dtype)


def l2norm(x):
    xf = x.astype(jnp.float32)
    return xf * lax.rsqrt(jnp.sum(xf * xf, axis=-1, keepdims=True) + EPS)


def causal_dwconv(x, w):
    k_w, ch = w.shape
    return lax.conv_general_dilated(
        x, w[:, None, :].astype(x.dtype), window_strides=(1,), padding=[(k_w - 1, 0)],
        dimension_numbers=("NWC", "WIO", "NWC"), feature_group_count=ch)


def chunk_gated_delta(q, k, v, beta, g):
    b, h, l, dk = q.shape
    dv = v.shape[-1]
    n = l // CHUNK
    q = q.reshape(b, h, n, CHUNK, dk)
    k = k.reshape(b, h, n, CHUNK, dk)
    v = v.reshape(b, h, n, CHUNK, dv)
    beta = beta.reshape(b, h, n, CHUNK)
    gc = jnp.cumsum(g.reshape(b, h, n, CHUNK), axis=-1)
    tri_incl = jnp.tril(jnp.ones((CHUNK, CHUNK), dtype=bool))
    tri_strict = jnp.tril(jnp.ones((CHUNK, CHUNK), dtype=bool), -1)
    decay = jnp.exp(jnp.where(tri_incl, gc[..., :, None] - gc[..., None, :], -jnp.inf))
    kk = jnp.einsum("bhncd,bhnsd->bhncs", k, k)
    lower = jnp.where(tri_strict, beta[..., :, None] * kk * decay, 0.0)
    a_mat = lower + jnp.eye(CHUNK, dtype=jnp.float32)
    rhs = jnp.concatenate([v * beta[..., None], k * (beta * jnp.exp(gc))[..., None]], axis=-1)
    sol = lax.linalg.triangular_solve(a_mat, rhs, left_side=True, lower=True, unit_diagonal=True)
    u, w = sol[..., :dv], sol[..., dv:]
    attn = jnp.einsum("bhncd,bhnsd->bhncs", q, k) * decay
    q_g = q * jnp.exp(gc)[..., None]
    g_last = gc[..., -1]
    k_end = k * jnp.exp(g_last[..., None] - gc)[..., None]

    def step(s, inp):
        qg_c, kend_c, u_c, w_c, at_c, gl_c = inp
        wv = u_c - jnp.einsum("bhck,bhkv->bhcv", w_c, s)
        o_c = jnp.einsum("bhck,bhkv->bhcv", qg_c, s) + jnp.einsum("bhcs,bhsv->bhcv", at_c, wv)
        s = s * jnp.exp(gl_c)[..., None, None] + jnp.einsum("bhck,bhcv->bhkv", kend_c, wv)
        return s, o_c

    xs = tuple(jnp.moveaxis(t, 2, 0) for t in (q_g, k_end, u, w, attn, g_last))
    s0 = jnp.zeros((b, h, dk, dv), jnp.float32)
    _, o = lax.scan(step, s0, xs)
    return jnp.moveaxis(o, 0, 2).reshape(b, h, l, dv)


def _fwd_setup_inputs(seed: int = 0) -> dict:
    key = jax.random.key(seed)
    ks = jax.random.split(key, 24)
    f32 = jnp.float32
    nrm = lambda kk, shape, s: jax.random.normal(kk, shape, f32) * s
    x = nrm(ks[0], (BATCH, SEQ, D_MODEL), 1.0)
    meta = nrm(ks[1], (N_META, D_MODEL), 1.0)
    norm_w = 1.0 + nrm(ks[2], (DEPTH, D_MODEL), 0.02)
    w_in = nrm(ks[3], (DEPTH, D_MODEL, IN_W), D_MODEL ** -0.5)
    conv_qkv_w = nrm(ks[4], (DEPTH, DN_CONV, 2 * QK_W + V_W), DN_CONV ** -0.5)
    a_log = jnp.log(jax.random.uniform(ks[5], (DEPTH, DN_HEADS), f32, 1.0, 16.0))
    dt = jnp.exp(jax.random.uniform(ks[6], (DEPTH, DN_HEADS), f32, np.log(1e-3), np.log(1e-1)))
    dt_bias = dt + jnp.log(-jnp.expm1(-dt))
    dn_norm_w = 1.0 + nrm(ks[7], (DEPTH, DN_DV), 0.02)
    w_dn_out = nrm(ks[8], (DEPTH, V_W, D_MODEL), V_W ** -0.5)
    dw_w = nrm(ks[9], (DEPTH, CF_KERNEL, CF_CH), CF_KERNEL ** -0.5)
    dw_b = nrm(ks[10], (DEPTH, CF_CH), 0.01)
    ln_w = 1.0 + nrm(ks[11], (DEPTH, CF_CH), 0.02)
    ln_b = nrm(ks[12], (DEPTH, CF_CH), 0.01)
    w_cf_out = nrm(ks[13], (DEPTH, CF_CH, D_MODEL), CF_CH ** -0.5)
    b_cf_out = nrm(ks[14], (DEPTH, D_MODEL), 0.01)
    w_o = nrm(ks[15], (DEPTH, D_MODEL, D_MODEL), D_MODEL ** -0.5)
    final_norm_w = 1.0 + nrm(ks[16], (D_MODEL,), 0.02)
    return {"x": x, "meta": meta, "norm_w": norm_w, "w_in": w_in, "conv_qkv_w": conv_qkv_w,
            "a_log": a_log, "dt_bias": dt_bias, "dn_norm_w": dn_norm_w, "w_dn_out": w_dn_out,
            "dw_w": dw_w, "dw_b": dw_b, "ln_w": ln_w, "ln_b": ln_b, "w_cf_out": w_cf_out,
            "b_cf_out": b_cf_out, "w_o": w_o, "final_norm_w": final_norm_w}


def _fwd_reference(x, meta, norm_w, w_in, conv_qkv_w, a_log, dt_bias, dn_norm_w, w_dn_out,
              dw_w, dw_b, ln_w, ln_b, w_cf_out, b_cf_out, w_o, final_norm_w):
    b = x.shape[0]
    x = jnp.concatenate([jnp.broadcast_to(meta[None].astype(x.dtype), (b, N_META, D_MODEL)), x], axis=1)
    l = x.shape[1]
    pad = (-l) % CHUNK
    split_idx = [int(v) for v in np.cumsum(IN_SIZES)[:-1]]
    for layer in range(DEPTH):
        h = rmsnorm(x, norm_w[layer])
        proj = h @ w_in[layer]
        q, k, v, za, b_lin, a_lin, glu, zb, ga, gb = jnp.split(proj, split_idx, axis=-1)

        qkv = jax.nn.silu(causal_dwconv(jnp.concatenate([q, k, v], axis=-1), conv_qkv_w[layer]))
        q, k, v = jnp.split(qkv, [QK_W, 2 * QK_W], axis=-1)
        q = l2norm(q.reshape(b, l, DN_HEADS, DN_DK)) * (DN_DK ** -0.5)
        k = l2norm(k.reshape(b, l, DN_HEADS, DN_DK))
        v = v.reshape(b, l, DN_HEADS, DN_DV).astype(jnp.float32)
        beta = jax.nn.sigmoid(b_lin.astype(jnp.float32))
        g = -jnp.exp(a_log[layer].astype(jnp.float32)) * jax.nn.softplus(
            a_lin.astype(jnp.float32) + dt_bias[layer].astype(jnp.float32))
        to_bh = lambda t: jnp.pad(jnp.moveaxis(t, 2, 1), [(0, 0), (0, 0), (pad, 0)] + [(0, 0)] * (t.ndim - 3))
        o = chunk_gated_delta(to_bh(q), to_bh(k), to_bh(v), to_bh(beta), to_bh(g))[:, :, pad:]
        o = jnp.moveaxis(o, 1, 2)
        o = rmsnorm(o, dn_norm_w[layer]) * jax.nn.silu(za.reshape(b, l, DN_HEADS, DN_DV).astype(jnp.float32))
        y_a = o.reshape(b, l, V_W).astype(x.dtype) @ w_dn_out[layer]

        glu_a, glu_b = jnp.split(glu, 2, axis=-1)
        c = glu_a * jax.nn.sigmoid(glu_b)
        c = causal_dwconv(c, dw_w[layer]) + dw_b[layer]
        c = jax.nn.silu(layernorm(c, ln_w[layer], ln_b[layer])) * jax.nn.silu(zb)
        y_b = c @ w_cf_out[layer] + b_cf_out[layer]

        merged = jax.nn.sigmoid(ga) * y_a + jax.nn.sigmoid(gb) * y_b
        x = x + merged @ w_o[layer]
    return rmsnorm(x, final_norm_w)[:, N_META:]


import jax as _jax
import jax.numpy as _jnp

TWIN_FORMAT = 'train_step'
FWD_PARAMS = ['x', 'meta', 'norm_w', 'w_in', 'conv_qkv_w', 'a_log', 'dt_bias', 'dn_norm_w', 'w_dn_out', 'dw_w', 'dw_b', 'ln_w', 'ln_b', 'w_cf_out', 'b_cf_out', 'w_o', 'final_norm_w']
TWIN_WEIGHTS = ['meta', 'norm_w', 'w_in', 'conv_qkv_w', 'a_log', 'dt_bias', 'dn_norm_w', 'w_dn_out', 'dw_w', 'dw_b', 'ln_w', 'ln_b', 'w_cf_out', 'b_cf_out', 'w_o', 'final_norm_w']
TWIN_DIFF_INPUT = 'x'
TWIN_INPUTS = ['x', 'meta', 'norm_w', 'w_in', 'conv_qkv_w', 'a_log', 'dt_bias', 'dn_norm_w', 'w_dn_out', 'dw_w', 'dw_b', 'ln_w', 'ln_b', 'w_cf_out', 'b_cf_out', 'w_o', 'final_norm_w', 'loss_target', 'm_meta', 'm_norm_w', 'm_w_in', 'm_conv_qkv_w', 'm_a_log', 'm_dt_bias', 'm_dn_norm_w', 'm_w_dn_out', 'm_dw_w', 'm_dw_b', 'm_ln_w', 'm_ln_b', 'm_w_cf_out', 'm_b_cf_out', 'm_w_o', 'm_final_norm_w', 'v_meta', 'v_norm_w', 'v_w_in', 'v_conv_qkv_w', 'v_a_log', 'v_dt_bias', 'v_dn_norm_w', 'v_w_dn_out', 'v_dw_w', 'v_dw_b', 'v_ln_w', 'v_ln_b', 'v_w_cf_out', 'v_b_cf_out', 'v_w_o', 'v_final_norm_w']
TWIN_OUTPUTS = ['loss', 'grad_x', 'grad_meta', 'grad_norm_w', 'grad_w_in', 'grad_conv_qkv_w', 'grad_a_log', 'grad_dt_bias', 'grad_dn_norm_w', 'grad_w_dn_out', 'grad_dw_w', 'grad_dw_b', 'grad_ln_w', 'grad_ln_b', 'grad_w_cf_out', 'grad_b_cf_out', 'grad_w_o', 'grad_final_norm_w', 'delta_meta', 'delta_norm_w', 'delta_w_in', 'delta_conv_qkv_w', 'delta_a_log', 'delta_dt_bias', 'delta_dn_norm_w', 'delta_w_dn_out', 'delta_dw_w', 'delta_dw_b', 'delta_ln_w', 'delta_ln_b', 'delta_w_cf_out', 'delta_b_cf_out', 'delta_w_o', 'delta_final_norm_w', 'new_m_meta', 'new_m_norm_w', 'new_m_w_in', 'new_m_conv_qkv_w', 'new_m_a_log', 'new_m_dt_bias', 'new_m_dn_norm_w', 'new_m_w_dn_out', 'new_m_dw_w', 'new_m_dw_b', 'new_m_ln_w', 'new_m_ln_b', 'new_m_w_cf_out', 'new_m_b_cf_out', 'new_m_w_o', 'new_m_final_norm_w', 'new_v_meta', 'new_v_norm_w', 'new_v_w_in', 'new_v_conv_qkv_w', 'new_v_a_log', 'new_v_dt_bias', 'new_v_dn_norm_w', 'new_v_w_dn_out', 'new_v_dw_w', 'new_v_dw_b', 'new_v_ln_w', 'new_v_ln_b', 'new_v_w_cf_out', 'new_v_b_cf_out', 'new_v_w_o', 'new_v_final_norm_w']
TWIN_LEAF_KINDS = {'loss': 'loss', 'grad_x': 'grad_x', 'grad_meta': 'grad_w', 'grad_norm_w': 'grad_w', 'grad_w_in': 'grad_w', 'grad_conv_qkv_w': 'grad_w', 'grad_a_log': 'grad_w', 'grad_dt_bias': 'grad_w', 'grad_dn_norm_w': 'grad_w', 'grad_w_dn_out': 'grad_w', 'grad_dw_w': 'grad_w', 'grad_dw_b': 'grad_w', 'grad_ln_w': 'grad_w', 'grad_ln_b': 'grad_w', 'grad_w_cf_out': 'grad_w', 'grad_b_cf_out': 'grad_w', 'grad_w_o': 'grad_w', 'grad_final_norm_w': 'grad_w', 'delta_meta': 'delta_w', 'delta_norm_w': 'delta_w', 'delta_w_in': 'delta_w', 'delta_conv_qkv_w': 'delta_w', 'delta_a_log': 'delta_w', 'delta_dt_bias': 'delta_w', 'delta_dn_norm_w': 'delta_w', 'delta_w_dn_out': 'delta_w', 'delta_dw_w': 'delta_w', 'delta_dw_b': 'delta_w', 'delta_ln_w': 'delta_w', 'delta_ln_b': 'delta_w', 'delta_w_cf_out': 'delta_w', 'delta_b_cf_out': 'delta_w', 'delta_w_o': 'delta_w', 'delta_final_norm_w': 'delta_w', 'new_m_meta': 'new_m', 'new_m_norm_w': 'new_m', 'new_m_w_in': 'new_m', 'new_m_conv_qkv_w': 'new_m', 'new_m_a_log': 'new_m', 'new_m_dt_bias': 'new_m', 'new_m_dn_norm_w': 'new_m', 'new_m_w_dn_out': 'new_m', 'new_m_dw_w': 'new_m', 'new_m_dw_b': 'new_m', 'new_m_ln_w': 'new_m', 'new_m_ln_b': 'new_m', 'new_m_w_cf_out': 'new_m', 'new_m_b_cf_out': 'new_m', 'new_m_w_o': 'new_m', 'new_m_final_norm_w': 'new_m', 'new_v_meta': 'new_v', 'new_v_norm_w': 'new_v', 'new_v_w_in': 'new_v', 'new_v_conv_qkv_w': 'new_v', 'new_v_a_log': 'new_v', 'new_v_dt_bias': 'new_v', 'new_v_dn_norm_w': 'new_v', 'new_v_w_dn_out': 'new_v', 'new_v_dw_w': 'new_v', 'new_v_dw_b': 'new_v', 'new_v_ln_w': 'new_v', 'new_v_ln_b': 'new_v', 'new_v_w_cf_out': 'new_v', 'new_v_b_cf_out': 'new_v', 'new_v_w_o': 'new_v', 'new_v_final_norm_w': 'new_v'}


def _forward(args):
    return _fwd_reference(*[args[k] for k in FWD_PARAMS])


def _output_shape():
    def fwd():
        inp = _fwd_setup_inputs(0)
        return _fwd_reference(*[inp[k] for k in FWD_PARAMS])
    out = _jax.eval_shape(fwd)
    return out.shape, out.dtype

N_MICROBATCH = 1
ADAM_LR = 0.001
ADAM_B1 = 0.9
ADAM_B2 = 0.999
ADAM_EPS = 1e-08
ADAM_WD = 0.01
ADAM_STEP = 10
PER_EXAMPLE_BATCH_AXIS = {'x': 0, 'loss_target': 0}
SHARED_INPUTS = []
_WEIGHT_DTYPES = {'meta': _jnp.float32, 'norm_w': _jnp.float32, 'w_in': _jnp.float32, 'conv_qkv_w': _jnp.float32, 'a_log': _jnp.float32, 'dt_bias': _jnp.float32, 'dn_norm_w': _jnp.float32, 'w_dn_out': _jnp.float32, 'dw_w': _jnp.float32, 'dw_b': _jnp.float32, 'ln_w': _jnp.float32, 'ln_b': _jnp.float32, 'w_cf_out': _jnp.float32, 'b_cf_out': _jnp.float32, 'w_o': _jnp.float32, 'final_norm_w': _jnp.float32}
MOMENT_SCALE = {'meta': 4.155832e-03, 'norm_w': 1.463104e-01, 'w_in': 4.899656e-02, 'conv_qkv_w': 5.747747e-02, 'a_log': 8.235398e-01, 'dt_bias': 7.890074e-01, 'dn_norm_w': 2.197675e-01, 'w_dn_out': 7.345637e-02, 'dw_w': 4.725356e-02, 'dw_b': 8.603768e-02, 'ln_w': 5.571296e-02, 'ln_b': 4.607869e-02, 'w_cf_out': 4.581623e-02, 'b_cf_out': 1.399647e-01, 'w_o': 8.720474e-02, 'final_norm_w': 6.399940e+01}


def _to_microbatches(a, axis):
    t = _jnp.moveaxis(a, axis, 0)
    t = t.reshape((N_MICROBATCH, t.shape[0] // N_MICROBATCH) + t.shape[1:])
    return _jnp.moveaxis(t, 1, axis + 1)


def setup_inputs(seed: int = 0) -> dict:
    inp = _fwd_setup_inputs(seed)
    key = _jax.random.fold_in(_jax.random.key(seed), 7919)
    shape, _ = _output_shape()
    out = dict(inp)
    out["loss_target"] = _jax.random.normal(_jax.random.fold_in(key, 0), shape, _jnp.float32)
    for i, name in enumerate(TWIN_WEIGHTS):
        w = inp[name].astype(_jnp.float32)
        if MOMENT_SCALE is None:
            s = _jnp.sqrt(_jnp.mean(_jnp.square(w)) + 1e-30)
        else:
            s = MOMENT_SCALE[name]
        km, kv = _jax.random.split(_jax.random.fold_in(key, i + 1))
        out[name] = w
        out["m_" + name] = s * _jax.random.normal(km, w.shape, _jnp.float32)
        out["v_" + name] = (s * s) * _jax.random.uniform(kv, w.shape, _jnp.float32, 0.5, 1.5)
    if N_MICROBATCH > 1:
        for name, axis in PER_EXAMPLE_BATCH_AXIS.items():
            out[name] = _to_microbatches(out[name], axis)
    return {'x': out['x'], 'meta': out['meta'], 'norm_w': out['norm_w'], 'w_in': out['w_in'], 'conv_qkv_w': out['conv_qkv_w'], 'a_log': out['a_log'], 'dt_bias': out['dt_bias'], 'dn_norm_w': out['dn_norm_w'], 'w_dn_out': out['w_dn_out'], 'dw_w': out['dw_w'], 'dw_b': out['dw_b'], 'ln_w': out['ln_w'], 'ln_b': out['ln_b'], 'w_cf_out': out['w_cf_out'], 'b_cf_out': out['b_cf_out'], 'w_o': out['w_o'], 'final_norm_w': out['final_norm_w'], 'loss_target': out['loss_target'], 'm_meta': out['m_meta'], 'm_norm_w': out['m_norm_w'], 'm_w_in': out['m_w_in'], 'm_conv_qkv_w': out['m_conv_qkv_w'], 'm_a_log': out['m_a_log'], 'm_dt_bias': out['m_dt_bias'], 'm_dn_norm_w': out['m_dn_norm_w'], 'm_w_dn_out': out['m_w_dn_out'], 'm_dw_w': out['m_dw_w'], 'm_dw_b': out['m_dw_b'], 'm_ln_w': out['m_ln_w'], 'm_ln_b': out['m_ln_b'], 'm_w_cf_out': out['m_w_cf_out'], 'm_b_cf_out': out['m_b_cf_out'], 'm_w_o': out['m_w_o'], 'm_final_norm_w': out['m_final_norm_w'], 'v_meta': out['v_meta'], 'v_norm_w': out['v_norm_w'], 'v_w_in': out['v_w_in'], 'v_conv_qkv_w': out['v_conv_qkv_w'], 'v_a_log': out['v_a_log'], 'v_dt_bias': out['v_dt_bias'], 'v_dn_norm_w': out['v_dn_norm_w'], 'v_w_dn_out': out['v_w_dn_out'], 'v_dw_w': out['v_dw_w'], 'v_dw_b': out['v_dw_b'], 'v_ln_w': out['v_ln_w'], 'v_ln_b': out['v_ln_b'], 'v_w_cf_out': out['v_w_cf_out'], 'v_b_cf_out': out['v_b_cf_out'], 'v_w_o': out['v_w_o'], 'v_final_norm_w': out['v_final_norm_w']}


def _loss(weights, diff, rest, loss_target):
    with _jax.named_scope("forward"):
        args = {**rest, TWIN_DIFF_INPUT: diff, **{k: w.astype(_WEIGHT_DTYPES[k]) for k, w in weights.items()}}
        y = _forward(args)
    with _jax.named_scope("loss_head"):
        err = _jnp.square(y.astype(_jnp.float32) - loss_target)
        return 0.5 * _jnp.sum(_jnp.mean(err, axis=-1)) if err.ndim else 0.5 * err


def _adamw(w, g, m, v):
    m = ADAM_B1 * m + (1.0 - ADAM_B1) * g
    v = ADAM_B2 * v + (1.0 - ADAM_B2) * _jnp.square(g)
    m_hat = m / (1.0 - ADAM_B1 ** ADAM_STEP)
    v_hat = v / (1.0 - ADAM_B2 ** ADAM_STEP)
    delta = -ADAM_LR * (m_hat / (_jnp.sqrt(v_hat) + ADAM_EPS) + ADAM_WD * w)
    return delta, m, v


def reference(x, meta, norm_w, w_in, conv_qkv_w, a_log, dt_bias, dn_norm_w, w_dn_out, dw_w, dw_b, ln_w, ln_b, w_cf_out, b_cf_out, w_o, final_norm_w, loss_target, m_meta, m_norm_w, m_w_in, m_conv_qkv_w, m_a_log, m_dt_bias, m_dn_norm_w, m_w_dn_out, m_dw_w, m_dw_b, m_ln_w, m_ln_b, m_w_cf_out, m_b_cf_out, m_w_o, m_final_norm_w, v_meta, v_norm_w, v_w_in, v_conv_qkv_w, v_a_log, v_dt_bias, v_dn_norm_w, v_w_dn_out, v_dw_w, v_dw_b, v_ln_w, v_ln_b, v_w_cf_out, v_b_cf_out, v_w_o, v_final_norm_w):
    given = dict(x=x, meta=meta, norm_w=norm_w, w_in=w_in, conv_qkv_w=conv_qkv_w, a_log=a_log, dt_bias=dt_bias, dn_norm_w=dn_norm_w, w_dn_out=w_dn_out, dw_w=dw_w, dw_b=dw_b, ln_w=ln_w, ln_b=ln_b, w_cf_out=w_cf_out, b_cf_out=b_cf_out, w_o=w_o, final_norm_w=final_norm_w, loss_target=loss_target, m_meta=m_meta, m_norm_w=m_norm_w, m_w_in=m_w_in, m_conv_qkv_w=m_conv_qkv_w, m_a_log=m_a_log, m_dt_bias=m_dt_bias, m_dn_norm_w=m_dn_norm_w, m_w_dn_out=m_w_dn_out, m_dw_w=m_dw_w, m_dw_b=m_dw_b, m_ln_w=m_ln_w, m_ln_b=m_ln_b, m_w_cf_out=m_w_cf_out, m_b_cf_out=m_b_cf_out, m_w_o=m_w_o, m_final_norm_w=m_final_norm_w, v_meta=v_meta, v_norm_w=v_norm_w, v_w_in=v_w_in, v_conv_qkv_w=v_conv_qkv_w, v_a_log=v_a_log, v_dt_bias=v_dt_bias, v_dn_norm_w=v_dn_norm_w, v_w_dn_out=v_w_dn_out, v_dw_w=v_dw_w, v_dw_b=v_dw_b, v_ln_w=v_ln_w, v_ln_b=v_ln_b, v_w_cf_out=v_w_cf_out, v_b_cf_out=v_b_cf_out, v_w_o=v_w_o, v_final_norm_w=v_final_norm_w)
    weights = {n: given[n] for n in TWIN_WEIGHTS}
    shared = {n: given[n] for n in SHARED_INPUTS}
    per_example = {n: given[n] for n in ['x']}
    grad_fn = _jax.value_and_grad(_loss, argnums=(0, 1))

    def one_microbatch(ex, loss_target):
        ex = dict(ex)
        diff = ex.pop(TWIN_DIFF_INPUT)
        return grad_fn(weights, diff, {**shared, **ex}, loss_target)

    if N_MICROBATCH == 1:
        loss, (grad_w, grad_x) = one_microbatch(per_example, given["loss_target"])
    else:
        def body(carry, xs):
            loss_sum, grad_sum = carry
            l_k, (gw_k, gx_k) = one_microbatch(xs[0], xs[1])
            with _jax.named_scope("update"):
                return (loss_sum + l_k, _jax.tree.map(_jnp.add, grad_sum, gw_k)), gx_k

        init = (_jnp.zeros((), _jnp.float32), _jax.tree.map(_jnp.zeros_like, weights))
        (loss, grad_w), grad_x = _jax.lax.scan(body, init, (per_example, given["loss_target"]))
    with _jax.named_scope("update"):
        delta_w, new_m, new_v = {}, {}, {}
        for n in TWIN_WEIGHTS:
            delta_w[n], new_m[n], new_v[n] = _adamw(weights[n], grad_w[n], given["m_" + n], given["v_" + n])
    return (loss, grad_x, *[grad_w[n] for n in TWIN_WEIGHTS], *[delta_w[n] for n in TWIN_WEIGHTS],
            *[new_m[n] for n in TWIN_WEIGHTS], *[new_v[n] for n in TWIN_WEIGHTS])
```

```python
import jax
import jax.numpy as jnp
from jax import lax
from jax.experimental import pallas as pl
from jax.experimental.pallas import tpu as pltpu

F32 = jnp.float32
MM_DTYPE = jnp.bfloat16
EPS = 1e-6
N_META = 16
HDR = 128
PAD_ROWS = HDR - N_META
CHUNK = 64
DK = 128
CF_TAPS = 31
DN_TAPS = 4
HALO = 64
CONV_ROWS = 128
LANES = 128
ADAM_LR, ADAM_B1, ADAM_B2, ADAM_EPS, ADAM_WD, ADAM_STEP = 0.001, 0.9, 0.999, 1e-08, 0.01, 10
VMEM_LIMIT = 48 * 1024 * 1024
MESH = pl.DeviceIdType.MESH
HI = lax.Precision.HIGHEST
LO = lax.Precision.DEFAULT
NN = (((1,), (0,)), ((), ()))
NT = (((1,), (1,)), ((), ()))
TN = (((0,), (0,)), ((), ()))
ANY = pl.BlockSpec(memory_space=pl.ANY)


def _dot(a, b, dims=NN, prec=LO):
    return lax.dot_general(a, b, dims, precision=prec, preferred_element_type=F32)


def _pick(n, options):
    for o in options:
        if n % o == 0:
            return o
    return n


def _cparams(*sem):
    return pltpu.CompilerParams(dimension_semantics=sem, vmem_limit_bytes=VMEM_LIMIT)


def _sig(x):
    return jax.nn.sigmoid(x)


def _sum_all(a):
    return jnp.sum(jnp.sum(a, axis=-1, keepdims=True), axis=0, keepdims=True)


def _dsilu(x, s):
    return s + x * s * (1.0 - s)


def _shift(win, s):
    n = win.shape[0]
    s = s % n
    return win if s == 0 else pltpu.roll(win, s, 0)


def _row_tile(t):
    return _pick(t, (320, 128))


def _mm(a, b, *, nt=False, out_dtype=F32, name):
    m, k = a.shape
    n = b.shape[0] if nt else b.shape[1]
    tm = _pick(m, (1024, 640, 512, 384, 256, 128))
    tn = _pick(n, (1024, 768, 512, 256, 128))
    tk = _pick(k, (1664, 1024, 640, 512, 256, 128))
    nk = k // tk
    dims = NT if nt else NN

    def body(a_ref, b_ref, o_ref, *acc):
        p = lax.dot_general(a_ref[...], b_ref[...], dims, preferred_element_type=F32)
        if nk == 1:
            o_ref[...] = p.astype(o_ref.dtype)
            return
        acc_ref, = acc
        kk = pl.program_id(2)

        @pl.when(kk == 0)
        def _():
            acc_ref[...] = p

        @pl.when(kk > 0)
        def _():
            acc_ref[...] += p

        @pl.when(kk == nk - 1)
        def _():
            o_ref[...] = acc_ref[...].astype(o_ref.dtype)

    b_spec = pl.BlockSpec((tn, tk), lambda i, j, kk: (j, kk)) if nt else pl.BlockSpec((tk, tn), lambda i, j, kk: (kk, j))
    return pl.pallas_call(
        body, name=name, grid=(m // tm, n // tn, nk),
        in_specs=[pl.BlockSpec((tm, tk), lambda i, j, kk: (i, kk)), b_spec],
        out_specs=pl.BlockSpec((tm, tn), lambda i, j, kk: (i, j)),
        out_shape=jax.ShapeDtypeStruct((m, n), out_dtype),
        scratch_shapes=[] if nk == 1 else [pltpu.VMEM((tm, tn), F32)],
        compiler_params=_cparams("parallel", "parallel", "arbitrary"),
    )(a, b)


def _rms_fwd(xs, w, name):
    t, d = xs.shape
    r = _row_tile(t)

    def body(x_ref, w_ref, h_ref):
        x = x_ref[...]
        rs = lax.rsqrt(jnp.mean(x * x, axis=-1, keepdims=True) + EPS)
        h_ref[...] = (x * rs * w_ref[...]).astype(h_ref.dtype)

    return pl.pallas_call(
        body, name=name, grid=(t // r,),
        in_specs=[pl.BlockSpec((r, d), lambda i: (i, 0)), pl.BlockSpec((1, d), lambda i: (0, 0))],
        out_specs=pl.BlockSpec((r, d), lambda i: (i, 0)),
        out_shape=jax.ShapeDtypeStruct((t, d), MM_DTYPE), compiler_params=_cparams("parallel"),
    )(xs, w)


def _rms_bwd(xs, w, dh1, dh2, dres, name):
    t, d = xs.shape
    r = _row_tile(t)

    def body(x_ref, w_ref, d1_ref, d2_ref, dr_ref, dx_ref, gw_ref):
        @pl.when(pl.program_id(0) == 0)
        def _():
            gw_ref[...] = jnp.zeros_like(gw_ref)

        x = x_ref[...]
        dh = d1_ref[...] + d2_ref[...]
        rs = lax.rsqrt(jnp.mean(x * x, axis=-1, keepdims=True) + EPS)
        xh = x * rs
        dxh = dh * w_ref[...]
        dx_ref[...] = rs * (dxh - xh * jnp.mean(dxh * xh, axis=-1, keepdims=True)) + dr_ref[...]
        gw_ref[0:1, :] += jnp.sum(dh * xh, axis=0, keepdims=True)

    row = pl.BlockSpec((r, d), lambda i: (i, 0))
    return pl.pallas_call(
        body, name=name, grid=(t // r,),
        in_specs=[row, pl.BlockSpec((1, d), lambda i: (0, 0)), row, row, row],
        out_specs=[row, pl.BlockSpec((8, d), lambda i: (0, 0))],
        out_shape=[jax.ShapeDtypeStruct((t, d), F32), jax.ShapeDtypeStruct((8, d), F32)],
        compiler_params=_cparams("arbitrary"),
    )(xs, w, dh1, dh2, dres)


def _final(xs, mo, w, tgt, name):
    t, d = xs.shape
    r = _row_tile(t)

    def body(x_ref, m_ref, w_ref, t_ref, dx_ref, dxm_ref, loss_ref, gw_ref):
        i = pl.program_id(0)

        @pl.when(i == 0)
        def _():
            loss_ref[...] = jnp.zeros_like(loss_ref)
            gw_ref[...] = jnp.zeros_like(gw_ref)

        xo = x_ref[...] + m_ref[...]
        rs = lax.rsqrt(jnp.mean(xo * xo, axis=-1, keepdims=True) + EPS)
        xh = xo * rs
        y = xh * w_ref[...]
        rows = i * r + lax.broadcasted_iota(jnp.int32, (r, 1), 0)
        err = jnp.where(rows >= HDR, y - t_ref[...], 0.0)
        loss_ref[...] += 0.5 * _sum_all(err * err) / d
        dy = err / d
        gw_ref[0:1, :] += jnp.sum(dy * xh, axis=0, keepdims=True)
        dxh = dy * w_ref[...]
        dx = rs * (dxh - xh * jnp.mean(dxh * xh, axis=-1, keepdims=True))
        dx_ref[...] = dx
        dxm_ref[...] = dx.astype(dxm_ref.dtype)

    row = pl.BlockSpec((r, d), lambda i: (i, 0))
    return pl.pallas_call(
        body, name=name, grid=(t // r,),
        in_specs=[row, row, pl.BlockSpec((1, d), lambda i: (0, 0)), row],
        out_specs=[row, row, pl.BlockSpec((8, LANES), lambda i: (0, 0)), pl.BlockSpec((8, d), lambda i: (0, 0))],
        out_shape=[jax.ShapeDtypeStruct((t, d), F32), jax.ShapeDtypeStruct((t, d), MM_DTYPE),
                   jax.ShapeDtypeStruct((8, LANES), F32), jax.ShapeDtypeStruct((8, d), F32)],
        compiler_params=_cparams("arbitrary"),
    )(xs, mo, w, tgt)


def _halo_prev(r):
    return lambda i: (jnp.maximum(i * (r // HALO) - 1, 0), 0)


def _halo_next(r, t):
    return lambda i: (jnp.minimum((i + 1) * (r // HALO), t // HALO - 1), 0)


def _qkv_conv_window(cur, prev_ref, i):
    prev = jnp.where(i > 0, prev_ref[HALO - 8:HALO, :], 0.0)
    return jnp.concatenate([prev, cur], axis=0)


def _qkv_act(win, cw, r, d, heads):
    y = None
    for s in range(DN_TAPS):
        term = _shift(win, s)[8:8 + r, :] * cw[DN_TAPS - 1 - s:DN_TAPS - s, :]
        y = term if y is None else y + term
    a = y * _sig(y)
    return y, a


def _dn_prep(proj, conv_w, d, heads, name):
    t = proj.shape[0]
    r = CONV_ROWS
    w3 = 3 * d

    def body(p_ref, ph_ref, cw_ref, o_ref):
        i = pl.program_id(0)
        win = _qkv_conv_window(p_ref[...], ph_ref, i)
        _, a = _qkv_act(win, cw_ref[...], r, d, heads)
        for hh in range(2 * heads):
            ah = a[:, hh * DK:(hh + 1) * DK]
            rn = lax.rsqrt(jnp.sum(ah * ah, axis=-1, keepdims=True) + EPS)
            sc = DK ** -0.5 if hh < heads else 1.0
            o_ref[:, hh * DK:(hh + 1) * DK] = ah * (rn * sc)
        o_ref[:, 2 * d:] = a[:, 2 * d:]

    return pl.pallas_call(
        body, name=name, grid=(t // r,),
        in_specs=[pl.BlockSpec((r, w3), lambda i: (i, 0)), pl.BlockSpec((HALO, w3), _halo_prev(r)),
                  pl.BlockSpec((DN_TAPS, w3), lambda i: (0, 0))],
        out_specs=pl.BlockSpec((r, w3), lambda i: (i, 0)),
        out_shape=jax.ShapeDtypeStruct((t, w3), F32), compiler_params=_cparams("parallel"),
    )(proj, proj, conv_w)


def _dn_prep_bwd1(proj, conv_w, dqkvn, d, heads, name):
    t = proj.shape[0]
    r = CONV_ROWS
    w3 = 3 * d

    def body(p_ref, ph_ref, cw_ref, dn_ref, o_ref):
        i = pl.program_id(0)
        win = _qkv_conv_window(p_ref[...], ph_ref, i)
        y, a = _qkv_act(win, cw_ref[...], r, d, heads)
        ds = _dsilu(y, _sig(y))
        for hh in range(2 * heads):
            sl = slice(hh * DK, (hh + 1) * DK)
            ah = a[:, sl]
            rn = lax.rsqrt(jnp.sum(ah * ah, axis=-1, keepdims=True) + EPS)
            sc = DK ** -0.5 if hh < heads else 1.0
            n0 = ah * rn
            dn0 = dn_ref[:, sl] * sc
            da = rn * (dn0 - n0 * jnp.sum(dn0 * n0, axis=-1, keepdims=True))
            o_ref[:, sl] = da * ds[:, sl]
        o_ref[:, 2 * d:] = dn_ref[:, 2 * d:] * ds[:, 2 * d:]

    row = pl.BlockSpec((r, w3), lambda i: (i, 0))
    return pl.pallas_call(
        body, name=name, grid=(t // r,),
        in_specs=[row, pl.BlockSpec((HALO, w3), _halo_prev(r)), pl.BlockSpec((DN_TAPS, w3), lambda i: (0, 0)), row],
        out_specs=row, out_shape=jax.ShapeDtypeStruct((t, w3), F32), compiler_params=_cparams("parallel"),
    )(proj, proj, conv_w, dqkvn)


def _dn_prep_bwd2(dconv, proj, conv_w, dproj, d, name):
    t = proj.shape[0]
    r = CONV_ROWS
    w3 = 3 * d
    last = t // r - 1

    def body(dc_ref, dcn_ref, p_ref, ph_ref, cw_ref, _, dp_ref, gw_ref):
        i = pl.program_id(0)

        @pl.when(i == 0)
        def _():
            gw_ref[...] = jnp.zeros_like(gw_ref)

        cw = cw_ref[...]
        dcur = dc_ref[...]
        nxt = jnp.where(i < last, dcn_ref[0:8, :], 0.0)
        dwin = jnp.concatenate([dcur, nxt], axis=0)
        win = _qkv_conv_window(p_ref[...], ph_ref, i)
        acc = None
        for j in range(DN_TAPS):
            s = DN_TAPS - 1 - j
            term = _shift(dwin, -s)[0:r, :] * cw[j:j + 1, :]
            acc = term if acc is None else acc + term
            gw_ref[j:j + 1, :] += jnp.sum(dcur * _shift(win, s)[8:8 + r, :], axis=0, keepdims=True)
        dp_ref[...] = acc.astype(dp_ref.dtype)

    row = pl.BlockSpec((r, w3), lambda i: (i, 0))
    return pl.pallas_call(
        body, name=name, grid=(t // r,),
        in_specs=[row, pl.BlockSpec((HALO, w3), _halo_next(r, t)), row, pl.BlockSpec((HALO, w3), _halo_prev(r)),
                  pl.BlockSpec((DN_TAPS, w3), lambda i: (0, 0)), ANY],
        out_specs=[row, pl.BlockSpec((8, w3), lambda i: (0, 0))],
        out_shape=[jax.ShapeDtypeStruct(dproj.shape, dproj.dtype), jax.ShapeDtypeStruct((8, w3), F32)],
        input_output_aliases={5: 0}, compiler_params=_cparams("arbitrary"),
    )(dconv, dconv, proj, proj, conv_w, dproj)


def _ba_terms(x, pv, heads):
    lane = lax.broadcasted_iota(jnp.int32, x.shape, 1)
    is_b = lane < heads
    is_a = jnp.logical_and(lane >= heads, lane < 2 * heads)
    beta = _sig(x)
    z = x + pv[1:2, :]
    nexp = -jnp.exp(pv[0:1, :])
    sp = jnp.maximum(z, 0.0) + jnp.log1p(jnp.exp(-jnp.abs(z)))
    return is_b, is_a, beta, z, nexp, nexp * sp


def _ba_fwd(pba, pvec, heads, name):
    t = pba.shape[0]
    r = _row_tile(t)

    def body(x_ref, pv_ref, o_ref):
        is_b, is_a, beta, _, _, g = _ba_terms(x_ref[...], pv_ref[...], heads)
        rows = pl.program_id(0) * r + lax.broadcasted_iota(jnp.int32, (r, 1), 0)
        o_ref[...] = jnp.where(rows >= PAD_ROWS, jnp.where(is_b, beta, jnp.where(is_a, g, 0.0)), 0.0)

    row = pl.BlockSpec((r, LANES), lambda i: (i, 0))
    return pl.pallas_call(
        body, name=name, grid=(t // r,), in_specs=[row, pl.BlockSpec((8, LANES), lambda i: (0, 0))],
        out_specs=row, out_shape=jax.ShapeDtypeStruct((t, LANES), F32), compiler_params=_cparams("parallel"),
    )(pba, pvec)


def _ba_bwd(pba, pvec, dbg, heads, name):
    t = pba.shape[0]
    r = _row_tile(t)

    def body(x_ref, pv_ref, d_ref, o_ref, g_ref):
        @pl.when(pl.program_id(0) == 0)
        def _():
            g_ref[...] = jnp.zeros_like(g_ref)

        is_b, is_a, beta, z, nexp, g = _ba_terms(x_ref[...], pv_ref[...], heads)
        rows = pl.program_id(0) * r + lax.broadcasted_iota(jnp.int32, (r, 1), 0)
        dd = jnp.where(rows >= PAD_ROWS, d_ref[...], 0.0)
        dz = dd * nexp * _sig(z)
        o_ref[...] = jnp.where(is_b, dd * beta * (1.0 - beta), jnp.where(is_a, dz, 0.0)).astype(o_ref.dtype)
        g_ref[0:1, :] += jnp.sum(jnp.where(is_a, dd * g, 0.0), axis=0, keepdims=True)
        g_ref[1:2, :] += jnp.sum(jnp.where(is_a, dz, 0.0), axis=0, keepdims=True)

    row = pl.BlockSpec((r, LANES), lambda i: (i, 0))
    par = pl.BlockSpec((8, LANES), lambda i: (0, 0))
    return pl.pallas_call(
        body, name=name, grid=(t // r,), in_specs=[row, par, row], out_specs=[row, par],
        out_shape=[jax.ShapeDtypeStruct((t, LANES), MM_DTYPE), jax.ShapeDtypeStruct((8, LANES), F32)],
        compiler_params=_cparams("arbitrary"),
    )(pba, pvec, dbg)


def _chunk_consts():
    ri = lax.broadcasted_iota(jnp.int32, (CHUNK, CHUNK), 0)
    ci = lax.broadcasted_iota(jnp.int32, (CHUNK, CHUNK), 1)
    return ri >= ci, ri > ci, (ri == ci).astype(F32), (ri >= ci).astype(F32), (ri <= ci).astype(F32)


def _chunk_decay(gc, gr, incl):
    return jnp.where(incl, jnp.exp(jnp.where(incl, gc - gr, 0.0)), 0.0)


def _dn_fwd(qkvn, bg, g_row, d, heads, name):
    t = qkvn.shape[0]
    nc = t // CHUNK

    def body(qkv_ref, bg_ref, gr_ref, o_ref, sall_ref, ainv_ref, s_ref):
        @pl.when(pl.program_id(0) == 0)
        def _():
            s_ref[...] = jnp.zeros_like(s_ref)

        incl, strict, eye, tril, triu = _chunk_consts()
        bgv = bg_ref[...]
        gc_all = _dot(tril, bgv, NN, HI)
        gr_all = _dot(gr_ref[0], triu, NN, HI)
        for h in range(heads):
            sl = slice(h * DK, (h + 1) * DK)
            q = qkv_ref[:, h * DK:(h + 1) * DK]
            k = qkv_ref[:, d + h * DK:d + (h + 1) * DK]
            v = qkv_ref[:, 2 * d + h * DK:2 * d + (h + 1) * DK]
            beta = bgv[:, h:h + 1]
            gc = gc_all[:, heads + h:heads + h + 1]
            gr = gr_all[h:h + 1, :]
            decay = _chunk_decay(gc, gr, incl)
            kk = _dot(k, k, NT)
            n = jnp.where(strict, -(beta * kk * decay), 0.0)
            x = eye + n
            p = n
            for _ in range(5):
                p = _dot(p, p, NN, HI)
                x = x + _dot(x, p, NN, HI)
            eg = jnp.exp(gc)
            sol = _dot(x, jnp.concatenate([v * beta, k * (beta * eg)], axis=1), NN, HI)
            u, w = sol[:, :DK], sol[:, DK:]
            attn = _dot(q, k, NT) * decay
            s = s_ref[h]
            wv = u - _dot(w, s)
            o_ref[:, sl] = _dot(q * eg, s) + _dot(attn, wv)
            glast = gc[CHUNK - 1:CHUNK, :]
            sall_ref[0, h] = s
            ainv_ref[0, h] = x
            s_ref[h] = s * jnp.exp(glast) + _dot(k * jnp.exp(glast - gc), wv, TN)

    return pl.pallas_call(
        body, name=name, grid=(nc,),
        in_specs=[pl.BlockSpec((CHUNK, 3 * d), lambda i: (i, 0)), pl.BlockSpec((CHUNK, LANES), lambda i: (i, 0)),
                  pl.BlockSpec((1, heads, CHUNK), lambda i: (i, 0, 0))],
        out_specs=[pl.BlockSpec((CHUNK, d), lambda i: (i, 0)), pl.BlockSpec((1, heads, DK, DK), lambda i: (i, 0, 0, 0)),
                   pl.BlockSpec((1, heads, CHUNK, CHUNK), lambda i: (i, 0, 0, 0))],
        out_shape=[jax.ShapeDtypeStruct((t, d), F32), jax.ShapeDtypeStruct((nc, heads, DK, DK), F32),
                   jax.ShapeDtypeStruct((nc, heads, CHUNK, CHUNK), F32)],
        scratch_shapes=[pltpu.VMEM((heads, DK, DK), F32)], compiler_params=_cparams("arbitrary"),
    )(qkvn, bg, g_row)


def _dn_bwd(qkvn, bg, g_row, do, sall, ainv, d, heads, name):
    t = qkvn.shape[0]
    nc = t // CHUNK
    rev = lambda i: nc - 1 - i

    def body(qkv_ref, bg_ref, gr_ref, do_ref, sall_ref, ainv_ref, dqkv_ref, dbg_ref, dgr_ref, ds_ref):
        @pl.when(pl.program_id(0) == 0)
        def _():
            ds_ref[...] = jnp.zeros_like(ds_ref)

        incl, strict, eye, tril, triu = _chunk_consts()
        bgv = bg_ref[...]
        gc_all = _dot(tril, bgv, NN, HI)
        gr_all = _dot(gr_ref[0], triu, NN, HI)
        lane = lax.broadcasted_iota(jnp.int32, (CHUNK, LANES), 1)
        hrow = lax.broadcasted_iota(jnp.int32, (heads, CHUNK), 0)
        last_row = lax.broadcasted_iota(jnp.int32, (CHUNK, 1), 0) == CHUNK - 1
        dbeta_slab = jnp.zeros((CHUNK, LANES), F32)
        dgc_slab = jnp.zeros((CHUNK, LANES), F32)
        dgr_slab = jnp.zeros((heads, CHUNK), F32)
        for h in range(heads):
            q = qkv_ref[:, h * DK:(h + 1) * DK]
            k = qkv_ref[:, d + h * DK:d + (h + 1) * DK]
            v = qkv_ref[:, 2 * d + h * DK:2 * d + (h + 1) * DK]
            dout = do_ref[:, h * DK:(h + 1) * DK]
            beta = bgv[:, h:h + 1]
            gc = gc_all[:, heads + h:heads + h + 1]
            gr = gr_all[h:h + 1, :]
            s = sall_ref[0, h]
            x = ainv_ref[0, h]
            dsn = ds_ref[h]
            decay = _chunk_decay(gc, gr, incl)
            eg = jnp.exp(gc)
            glast = gc[CHUNK - 1:CHUNK, :]
            eglast = jnp.exp(glast)
            ek = jnp.exp(glast - gc)
            kk = _dot(k, k, NT)
            qk = _dot(q, k, NT)
            sol = _dot(x, jnp.concatenate([v * beta, k * (beta * eg)], axis=1), NN, HI)
            u, w = sol[:, :DK], sol[:, DK:]
            attn = qk * decay
            wv = u - _dot(w, s)
            qg = q * eg
            kend = k * ek
            d_wv = _dot(attn, dout, TN) + _dot(kend, dsn)
            d_attn = _dot(dout, wv, NT)
            d_qg = _dot(dout, s, NT)
            d_kend = _dot(wv, dsn, NT)
            ds_ref[h] = _dot(qg, dout, TN) + eglast * dsn - _dot(w, d_wv, TN)
            d_glast = eglast * _sum_all(dsn * s) + _sum_all(d_kend * kend)
            d_w = -_dot(d_wv, s, NT)
            d_rhs = _dot(x, jnp.concatenate([d_wv, d_w], axis=1), TN, HI)
            d_ru, d_rw = d_rhs[:, :DK], d_rhs[:, DK:]
            d_low = jnp.where(strict, -(_dot(d_ru, u, NT) + _dot(d_rw, w, NT)), 0.0)
            rw_k = jnp.sum(d_rw * k, axis=-1, keepdims=True)
            dbeta = (jnp.sum(d_ru * v, axis=-1, keepdims=True) + rw_k * eg
                     + jnp.sum(d_low * kk * decay, axis=-1, keepdims=True))
            dgc = rw_k * beta * eg
            dv = d_ru * beta
            dk = d_rw * (beta * eg)
            d_kk = d_low * beta * decay
            dk = dk + _dot(d_kk, k) + _dot(d_kk, k, TN)
            d_decay = d_low * beta * kk + d_attn * qk
            d_qk = d_attn * decay
            dq = _dot(d_qk, k) + d_qg * eg
            dk = dk + _dot(d_qk, q, TN) + d_kend * ek
            e = d_decay * decay
            dgc = dgc + jnp.sum(e, axis=-1, keepdims=True) + jnp.sum(d_qg * qg, axis=-1, keepdims=True) \
                - jnp.sum(d_kend * kend, axis=-1, keepdims=True)
            dgc = dgc + jnp.where(last_row, d_glast, 0.0)
            dgr = -jnp.sum(e, axis=0, keepdims=True)
            dqkv_ref[:, h * DK:(h + 1) * DK] = dq
            dqkv_ref[:, d + h * DK:d + (h + 1) * DK] = dk
            dqkv_ref[:, 2 * d + h * DK:2 * d + (h + 1) * DK] = dv
            dbeta_slab = jnp.where(lane == h, dbeta, dbeta_slab)
            dgc_slab = jnp.where(lane == heads + h, dgc, dgc_slab)
            dgr_slab = jnp.where(hrow == h, dgr, dgr_slab)
        dbg_ref[...] = dbeta_slab + _dot(triu, dgc_slab, NN, HI)
        dgr_ref[0] = _dot(dgr_slab, tril, NN, HI)

    return pl.pallas_call(
        body, name=name, grid=(nc,),
        in_specs=[pl.BlockSpec((CHUNK, 3 * d), lambda i: (rev(i), 0)), pl.BlockSpec((CHUNK, LANES), lambda i: (rev(i), 0)),
                  pl.BlockSpec((1, heads, CHUNK), lambda i: (rev(i), 0, 0)), pl.BlockSpec((CHUNK, d), lambda i: (rev(i), 0)),
                  pl.BlockSpec((1, heads, DK, DK), lambda i: (rev(i), 0, 0, 0)),
                  pl.BlockSpec((1, heads, CHUNK, CHUNK), lambda i: (rev(i), 0, 0, 0))],
        out_specs=[pl.BlockSpec((CHUNK, 3 * d), lambda i: (rev(i), 0)), pl.BlockSpec((CHUNK, LANES), lambda i: (rev(i), 0)),
                   pl.BlockSpec((1, heads, CHUNK), lambda i: (rev(i), 0, 0))],
        out_shape=[jax.ShapeDtypeStruct((t, 3 * d), F32), jax.ShapeDtypeStruct((t, LANES), F32),
                   jax.ShapeDtypeStruct((nc, heads, CHUNK), F32)],
        scratch_shapes=[pltpu.VMEM((heads, DK, DK), F32)], compiler_params=_cparams("arbitrary"),
    )(qkvn, bg, g_row, do, sall, ainv)


def _dn_post(o, proj, w, d, heads, name):
    t = o.shape[0]
    r = _row_tile(t)

    def body(o_ref, z_ref, w_ref, y_ref):
        for h in range(heads):
            sl = slice(h * DK, (h + 1) * DK)
            oh, z = o_ref[:, sl], z_ref[:, sl]
            rs = lax.rsqrt(jnp.mean(oh * oh, axis=-1, keepdims=True) + EPS)
            y_ref[:, sl] = (oh * rs * w_ref[...] * (z * _sig(z))).astype(y_ref.dtype)

    row = pl.BlockSpec((r, d), lambda i: (i, 0))
    return pl.pallas_call(
        body, name=name, grid=(t // r,),
        in_specs=[row, pl.BlockSpec((r, d), lambda i: (i, 3)), pl.BlockSpec((1, DK), lambda i: (0, 0))],
        out_specs=row, out_shape=jax.ShapeDtypeStruct((t, d), MM_DTYPE), compiler_params=_cparams("parallel"),
    )(o, proj, w)


def _dn_post_bwd(o, proj, w, dy, dproj, d, heads, name):
    t = o.shape[0]
    r = _row_tile(t)

    def body(o_ref, z_ref, w_ref, dy_ref, _, do_ref, dz_ref, gw_ref):
        @pl.when(pl.program_id(0) == 0)
        def _():
            gw_ref[...] = jnp.zeros_like(gw_ref)

        gw = jnp.zeros((1, DK), F32)
        for h in range(heads):
            sl = slice(h * DK, (h + 1) * DK)
            oh, z, dyh = o_ref[:, sl], z_ref[:, sl], dy_ref[:, sl]
            sz = _sig(z)
            rs = lax.rsqrt(jnp.mean(oh * oh, axis=-1, keepdims=True) + EPS)
            xh = oh * rs
            dn = dyh * (z * sz)
            dz_ref[:, sl] = (dyh * (xh * w_ref[...]) * _dsilu(z, sz)).astype(dz_ref.dtype)
            dxh = dn * w_ref[...]
            do_ref[:, sl] = rs * (dxh - xh * jnp.mean(dxh * xh, axis=-1, keepdims=True))
            gw = gw + jnp.sum(dn * xh, axis=0, keepdims=True)
        gw_ref[0:1, :] += gw

    row = pl.BlockSpec((r, d), lambda i: (i, 0))
    za = pl.BlockSpec((r, d), lambda i: (i, 3))
    return pl.pallas_call(
        body, name=name, grid=(t // r,),
        in_specs=[row, za, pl.BlockSpec((1, DK), lambda i: (0, 0)), row, ANY],
        out_specs=[row, za, pl.BlockSpec((8, DK), lambda i: (0, 0))],
        out_shape=[jax.ShapeDtypeStruct((t, d), F32), jax.ShapeDtypeStruct(dproj.shape, dproj.dtype),
                   jax.ShapeDtypeStruct((8, DK), F32)],
        input_output_aliases={4: 1}, compiler_params=_cparams("arbitrary"),
    )(o, proj, w, dy, dproj)


def _glu_window(g_ref, gh_ref, i, d):
    cur = g_ref[:, :d] * _sig(g_ref[:, d:])
    prev = gh_ref[HALO - 32:HALO, :d] * _sig(gh_ref[HALO - 32:HALO, d:])
    return jnp.concatenate([jnp.where(i > 0, prev, 0.0), cur], axis=0)


def _ln_stats(c1):
    mu = jnp.mean(c1, axis=-1, keepdims=True)
    cc = c1 - mu
    return cc * lax.rsqrt(jnp.mean(cc * cc, axis=-1, keepdims=True) + EPS), lax.rsqrt(jnp.mean(cc * cc, axis=-1, keepdims=True) + EPS)


def _cf_fwd(proj, dw_w, dw_b, ln_w, ln_b, d, name):
    t = proj.shape[0]
    r = CONV_ROWS

    def body(g_ref, gh_ref, zb_ref, w_ref, b_ref, lw_ref, lb_ref, c1_ref, c3_ref):
        win = _glu_window(g_ref, gh_ref, pl.program_id(0), d)
        w = w_ref[...]
        acc = jnp.broadcast_to(b_ref[...], (r, d))
        for j in range(CF_TAPS):
            acc = acc + _shift(win, CF_TAPS - 1 - j)[32:32 + r, :] * w[j:j + 1, :]
        c1_ref[...] = acc
        xh, _ = _ln_stats(acc)
        ln = xh * lw_ref[...] + lb_ref[...]
        zb = zb_ref[...]
        c3_ref[...] = ((ln * _sig(ln)) * (zb * _sig(zb))).astype(c3_ref.dtype)

    row = pl.BlockSpec((r, d), lambda i: (i, 0))
    par = pl.BlockSpec((1, d), lambda i: (0, 0))
    return pl.pallas_call(
        body, name=name, grid=(t // r,),
        in_specs=[pl.BlockSpec((r, 2 * d), lambda i: (i, 2)), pl.BlockSpec((HALO, 2 * d), lambda i: (_halo_prev(r)(i)[0], 2)),
                  pl.BlockSpec((r, d), lambda i: (i, 8)), pl.BlockSpec((32, d), lambda i: (0, 0)), par, par, par],
        out_specs=[row, row],
        out_shape=[jax.ShapeDtypeStruct((t, d), F32), jax.ShapeDtypeStruct((t, d), MM_DTYPE)],
        compiler_params=_cparams("parallel"),
    )(proj, proj, proj, dw_w, dw_b, ln_w, ln_b)


def _cf_bwd1(c1, proj, ln_w, ln_b, dc3, dproj, d, name):
    t = c1.shape[0]
    r = _row_tile(t)

    def body(c1_ref, zb_ref, lw_ref, lb_ref, dc3_ref, _, dc1_ref, dzb_ref, g_ref):
        @pl.when(pl.program_id(0) == 0)
        def _():
            g_ref[...] = jnp.zeros_like(g_ref)

        xh, rs = _ln_stats(c1_ref[...])
        ln = xh * lw_ref[...] + lb_ref[...]
        sl, zb, dc3v = _sig(ln), zb_ref[...], dc3_ref[...]
        szb = _sig(zb)
        dzb_ref[...] = (dc3v * (ln * sl) * _dsilu(zb, szb)).astype(dzb_ref.dtype)
        dln = dc3v * (zb * szb) * _dsilu(ln, sl)
        dxh = dln * lw_ref[...]
        dc1 = rs * (dxh - jnp.mean(dxh, axis=-1, keepdims=True) - xh * jnp.mean(dxh * xh, axis=-1, keepdims=True))
        dc1_ref[...] = dc1
        g_ref[0:1, :] += jnp.sum(dln * xh, axis=0, keepdims=True)
        g_ref[1:2, :] += jnp.sum(dln, axis=0, keepdims=True)
        g_ref[2:3, :] += jnp.sum(dc1, axis=0, keepdims=True)

    row = pl.BlockSpec((r, d), lambda i: (i, 0))
    zbs = pl.BlockSpec((r, d), lambda i: (i, 8))
    par = pl.BlockSpec((1, d), lambda i: (0, 0))
    return pl.pallas_call(
        body, name=name, grid=(t // r,), in_specs=[row, zbs, par, par, row, ANY],
        out_specs=[row, zbs, pl.BlockSpec((8, d), lambda i: (0, 0))],
        out_shape=[jax.ShapeDtypeStruct((t, d), F32), jax.ShapeDtypeStruct(dproj.shape, dproj.dtype),
                   jax.ShapeDtypeStruct((8, d), F32)],
        input_output_aliases={5: 1}, compiler_params=_cparams("arbitrary"),
    )(c1, proj, ln_w, ln_b, dc3, dproj)


def _cf_bwd2(dc1, proj, dw_w, dproj, d, name):
    t = dc1.shape[0]
    r = CONV_ROWS
    last = t // r - 1

    def body(dc_ref, dcn_ref, g_ref, gh_ref, w_ref, _, dg_ref, gw_ref):
        i = pl.program_id(0)

        @pl.when(i == 0)
        def _():
            gw_ref[...] = jnp.zeros_like(gw_ref)

        w = w_ref[...]
        dcur = dc_ref[...]
        dwin = jnp.concatenate([dcur, jnp.where(i < last, dcn_ref[0:32, :], 0.0)], axis=0)
        win = _glu_window(g_ref, gh_ref, i, d)
        acc = None
        for j in range(CF_TAPS):
            s = CF_TAPS - 1 - j
            term = _shift(dwin, -s)[0:r, :] * w[j:j + 1, :]
            acc = term if acc is None else acc + term
            gw_ref[j:j + 1, :] += jnp.sum(dcur * _shift(win, s)[32:32 + r, :], axis=0, keepdims=True)
        ga, sb = g_ref[:, :d], _sig(g_ref[:, d:])
        dg_ref[:, :d] = (acc * sb).astype(dg_ref.dtype)
        dg_ref[:, d:] = (acc * ga * sb * (1.0 - sb)).astype(dg_ref.dtype)

    row = pl.BlockSpec((r, d), lambda i: (i, 0))
    glu = pl.BlockSpec((r, 2 * d), lambda i: (i, 2))
    return pl.pallas_call(
        body, name=name, grid=(t // r,),
        in_specs=[row, pl.BlockSpec((HALO, d), _halo_next(r, t)), glu,
                  pl.BlockSpec((HALO, 2 * d), lambda i: (_halo_prev(r)(i)[0], 2)), pl.BlockSpec((32, d), lambda i: (0, 0)), ANY],
        out_specs=[glu, pl.BlockSpec((32, d), lambda i: (0, 0))],
        out_shape=[jax.ShapeDtypeStruct(dproj.shape, dproj.dtype), jax.ShapeDtypeStruct((32, d), F32)],
        input_output_aliases={5: 0}, compiler_params=_cparams("arbitrary"),
    )(dc1, dc1, proj, proj, dw_w, dproj)


def _merge(proj, ya, yb, b, d, name):
    t = ya.shape[0]
    r = _row_tile(t)

    def body(g_ref, ya_ref, yb_ref, b_ref, m_ref):
        m_ref[...] = (_sig(g_ref[:, :d]) * ya_ref[...] + _sig(g_ref[:, d:]) * (yb_ref[...] + b_ref[...])).astype(m_ref.dtype)

    row = pl.BlockSpec((r, d), lambda i: (i, 0))
    return pl.pallas_call(
        body, name=name, grid=(t // r,),
        in_specs=[pl.BlockSpec((r, 2 * d), lambda i: (i, 3)), row, row, pl.BlockSpec((1, d), lambda i: (0, 0))],
        out_specs=row, out_shape=jax.ShapeDtypeStruct((t, d), MM_DTYPE), compiler_params=_cparams("parallel"),
    )(proj, ya, yb, b)


def _merge_bwd(proj, ya, yb, b, dm, d, name):
    t = ya.shape[0]
    r = _row_tile(t)

    def body(g_ref, ya_ref, yb_ref, b_ref, dm_ref, dya_ref, dyb_ref, dg_ref, gb_ref):
        @pl.when(pl.program_id(0) == 0)
        def _():
            gb_ref[...] = jnp.zeros_like(gb_ref)

        sa, sb, dmv = _sig(g_ref[:, :d]), _sig(g_ref[:, d:]), dm_ref[...]
        dyb = dmv * sb
        dya_ref[...] = (dmv * sa).astype(dya_ref.dtype)
        dyb_ref[...] = dyb.astype(dyb_ref.dtype)
        dg_ref[:, :d] = (dmv * ya_ref[...] * sa * (1.0 - sa)).astype(dg_ref.dtype)
        dg_ref[:, d:] = (dmv * (yb_ref[...] + b_ref[...]) * sb * (1.0 - sb)).astype(dg_ref.dtype)
        gb_ref[0:1, :] += jnp.sum(dyb, axis=0, keepdims=True)

    row = pl.BlockSpec((r, d), lambda i: (i, 0))
    gate = pl.BlockSpec((r, 2 * d), lambda i: (i, 3))
    return pl.pallas_call(
        body, name=name, grid=(t // r,),
        in_specs=[gate, row, row, pl.BlockSpec((1, d), lambda i: (0, 0)), row],
        out_specs=[row, row, gate, pl.BlockSpec((8, d), lambda i: (0, 0))],
        out_shape=[jax.ShapeDtypeStruct((t, d), MM_DTYPE), jax.ShapeDtypeStruct((t, d), MM_DTYPE),
                   jax.ShapeDtypeStruct((t, 9 * d), MM_DTYPE), jax.ShapeDtypeStruct((8, d), F32)],
        compiler_params=_cparams("arbitrary"),
    )(proj, ya, yb, b, dm)


def _place():
    return lax.axis_index("x"), lax.axis_index("y"), lax.axis_index("c")


def _slot(p):
    return 4 * p[0] + 2 * p[1] + p[2]


def _allgather(blocks, name):
    n = len(blocks)

    def body(*refs):
        ins, outs = refs[:n], refs[n:2 * n]
        send_sems, recv_sems, local_sems = refs[2 * n:]
        x, y, c = _place()
        me, sibling = (x, y, c), (x, y, 1 - c)
        chips = [(1 - x, y), (x, 1 - y), (1 - x, 1 - y)]

        def copy(a, k, block, to, src=None):
            dst = outs[a].at[_slot(block)]
            return pltpu.make_async_remote_copy(
                src_ref=dst if src is None else src, dst_ref=dst, send_sem=send_sems.at[7 * a + k],
                recv_sem=recv_sems.at[7 * a + k], device_id=to, device_id_type=MESH)

        mine = [pltpu.make_async_copy(ins[a], outs[a].at[_slot(me)], local_sems.at[a]) for a in range(n)]
        for cp in mine:
            cp.start()
        first = []
        for a in range(n):
            first.append(copy(a, 0, me, sibling, src=ins[a]))
            first += [copy(a, 1 + j, me, (*chip, c), src=ins[a]) for j, chip in enumerate(chips)]
        for cp in first:
            cp.start()
        passed = []
        for j, chip in enumerate(chips):
            for a in range(n):
                copy(a, 1 + j, (*chip, c), me).wait_recv()
                cp = copy(a, 4 + j, (*chip, c), sibling)
                cp.start()
                passed.append(cp)
        for a in range(n):
            copy(a, 0, sibling, me).wait_recv()
            for j, chip in enumerate(chips):
                copy(a, 4 + j, (*chip, 1 - c), me).wait_recv()
        for cp in first + passed:
            cp.wait_send()
        for cp in mine:
            cp.wait()

    return pl.pallas_call(
        body, name=name, in_specs=[ANY] * n, out_specs=[ANY] * n,
        out_shape=[jax.ShapeDtypeStruct((8, *b.shape), b.dtype) for b in blocks],
        scratch_shapes=[pltpu.SemaphoreType.DMA((7 * n,)), pltpu.SemaphoreType.DMA((7 * n,)), pltpu.SemaphoreType.DMA((n,))],
    )(*blocks)


_FLIPS = [(0, 0, 1), (0, 1, 0), (0, 1, 1), (1, 0, 0), (1, 0, 1), (1, 1, 0), (1, 1, 1)]


def _alltoall(arrays, name):
    n = len(arrays)

    def body(*refs):
        ins, outs = refs[:n], refs[n:2 * n]
        send_sems, recv_sems, local_sems = refs[2 * n:]
        me = _place()
        peers = [tuple(1 - me[ax] if f[ax] else me[ax] for ax in range(3)) for f in _FLIPS]

        def copy(a, k):
            return pltpu.make_async_remote_copy(
                src_ref=ins[a].at[_slot(peers[k])], dst_ref=outs[a].at[_slot(me)], send_sem=send_sems.at[7 * a + k],
                recv_sem=recv_sems.at[7 * a + k], device_id=peers[k], device_id_type=MESH)

        def arrival(a, k):
            return pltpu.make_async_remote_copy(
                src_ref=ins[a].at[_slot(me)], dst_ref=outs[a].at[_slot(peers[k])], send_sem=send_sems.at[7 * a + k],
                recv_sem=recv_sems.at[7 * a + k], device_id=peers[k], device_id_type=MESH)

        mine = [pltpu.make_async_copy(ins[a].at[_slot(me)], outs[a].at[_slot(me)], local_sems.at[a]) for a in range(n)]
        for cp in mine:
            cp.start()
        sent = [copy(a, k) for a in range(n) for k in range(7)]
        for cp in sent:
            cp.start()
        for a in range(n):
            for k in range(7):
                arrival(a, k).wait_recv()
        for cp in sent:
            cp.wait_send()
        for cp in mine:
            cp.wait()

    return pl.pallas_call(
        body, name=name, in_specs=[ANY] * n, out_specs=[ANY] * n,
        out_shape=[jax.ShapeDtypeStruct(a.shape, a.dtype) for a in arrays],
        scratch_shapes=[pltpu.SemaphoreType.DMA((7 * n,)), pltpu.SemaphoreType.DMA((7 * n,)), pltpu.SemaphoreType.DMA((n,))],
    )(*arrays)


def _sibling_exchange(arrays, name):
    n = len(arrays)

    def body(*refs):
        ins, outs = refs[:n], refs[n:2 * n]
        send_sems, recv_sems, local_sems = refs[2 * n:]
        x, y, c = _place()

        def copy(a, slot):
            return pltpu.make_async_remote_copy(
                src_ref=ins[a], dst_ref=outs[a].at[slot], send_sem=send_sems.at[a], recv_sem=recv_sems.at[a],
                device_id=(x, y, 1 - c), device_id_type=MESH)

        mine = [pltpu.make_async_copy(ins[a], outs[a].at[c], local_sems.at[a]) for a in range(n)]
        for cp in mine:
            cp.start()
        sent = [copy(a, c) for a in range(n)]
        for cp in sent:
            cp.start()
        for a in range(n):
            copy(a, 1 - c).wait_recv()
        for cp in sent:
            cp.wait_send()
        for cp in mine:
            cp.wait()

    return pl.pallas_call(
        body, name=name, in_specs=[ANY] * n, out_specs=[ANY] * n,
        out_shape=[jax.ShapeDtypeStruct((2, *a.shape), a.dtype) for a in arrays],
        scratch_shapes=[pltpu.SemaphoreType.DMA((n,)), pltpu.SemaphoreType.DMA((n,)), pltpu.SemaphoreType.DMA((n,))],
    )(*arrays)


def _sum8(parts, name):
    _, rr, cc = parts.shape
    r = _pick(rr, (128, 64, 32, 16, 8))

    def body(p_ref, o_ref):
        acc = p_ref[0]
        for j in range(1, 8):
            acc = acc + p_ref[j]
        o_ref[...] = acc

    return pl.pallas_call(
        body, name=name, grid=(rr // r,), in_specs=[pl.BlockSpec((8, r, cc), lambda i: (0, i, 0))],
        out_specs=pl.BlockSpec((r, cc), lambda i: (i, 0)), out_shape=jax.ShapeDtypeStruct((rr, cc), F32),
        compiler_params=_cparams("parallel"),
    )(parts)


def _adamw(g, w, m, v, name):
    rr, cc = g.shape
    r = _pick(rr, (128, 64, 32, 16, 8))

    def body(g_ref, w_ref, m_ref, v_ref, d_ref, nm_ref, nv_ref):
        gv = g_ref[...]
        nm = ADAM_B1 * m_ref[...] + (1.0 - ADAM_B1) * gv
        nv = ADAM_B2 * v_ref[...] + (1.0 - ADAM_B2) * (gv * gv)
        m_hat = nm / (1.0 - ADAM_B1 ** ADAM_STEP)
        v_hat = nv / (1.0 - ADAM_B2 ** ADAM_STEP)
        d_ref[...] = -ADAM_LR * (m_hat / (jnp.sqrt(v_hat) + ADAM_EPS) + ADAM_WD * w_ref[...])
        nm_ref[...] = nm
        nv_ref[...] = nv

    blk = pl.BlockSpec((r, cc), lambda i: (i, 0))
    return pl.pallas_call(
        body, name=name, grid=(rr // r,), in_specs=[blk] * 4, out_specs=[blk] * 3,
        out_shape=[jax.ShapeDtypeStruct((rr, cc), F32)] * 3, compiler_params=_cparams("parallel"),
    )(g, w, m, v)


def _flat_pack(parts, width):
    flat = jnp.concatenate([p.reshape(-1) for p in parts])
    total = -(-flat.shape[0] // (8 * width)) * (8 * width)
    return jnp.pad(flat, (0, total - flat.shape[0])).reshape(-1, width)


def _flat_unpack(pack, shapes):
    flat = pack.reshape(-1)
    out, at = [], 0
    for s in shapes:
        size = 1
        for e in s:
            size *= e
        out.append(flat[at:at + size].reshape(s))
        at += size
    return out


def kernel(x, meta, norm_w, w_in, conv_qkv_w, a_log, dt_bias, dn_norm_w, w_dn_out, dw_w, dw_b, ln_w, ln_b, w_cf_out, b_cf_out, w_o, final_norm_w, loss_target, m_meta, m_norm_w, m_w_in, m_conv_qkv_w, m_a_log, m_dt_bias, m_dn_norm_w, m_w_dn_out, m_dw_w, m_dw_b, m_ln_w, m_ln_b, m_w_cf_out, m_b_cf_out, m_w_o, m_final_norm_w, v_meta, v_norm_w, v_w_in, v_conv_qkv_w, v_a_log, v_dt_bias, v_dn_norm_w, v_w_dn_out, v_dw_w, v_dw_b, v_ln_w, v_ln_b, v_w_cf_out, v_b_cf_out, v_w_o, v_final_norm_w):
    d = x.shape[-1]
    heads = a_log.shape[-1]
    seq = x.shape[1]
    t = HDR + seq
    nc = t // CHUNK
    in_w = 9 * d + 2 * heads
    assert heads * DK == d and seq % CONV_ROWS == 0 and w_in.shape[-1] * 4 == in_w
    xi, yi, ci = _place()
    shard = 2 * xi + yi
    ds = d // 4

    w_in_half = lax.dynamic_slice_in_dim(w_in[0].astype(MM_DTYPE), ci * (d // 2), d // 2, axis=0)
    w3_half = lax.dynamic_slice_in_dim(jnp.stack([w_dn_out[0], w_cf_out[0], w_o[0]]).astype(MM_DTYPE), ci * (d // 8), d // 8, axis=1)
    small = jnp.concatenate([
        jnp.pad(conv_qkv_w[0], ((0, 4), (0, 0))), jnp.pad(meta, ((0, 0), (0, 2 * ds))),
        jnp.pad(dw_w[0], ((0, 1), (0, 2 * ds)))], axis=0)
    g_in, g_w3, g_small = _allgather([w_in_half, w3_half, small], "gather_weights")
    w_full = g_in.reshape(4, d, in_w // 4).transpose(1, 0, 2).reshape(d, in_w)
    o_glu = 4 * d + 2 * heads
    w_main = jnp.concatenate([w_full[:, :4 * d], w_full[:, o_glu:o_glu + 2 * d], w_full[:, o_glu + 3 * d:o_glu + 5 * d],
                              w_full[:, o_glu + 2 * d:o_glu + 3 * d]], axis=1)
    w_ba = jnp.pad(w_full[:, 4 * d:o_glu], ((0, 0), (0, LANES - 2 * heads)))
    w3_full = g_w3.transpose(1, 0, 2, 3).reshape(3, d, d)
    w_dn_f, w_cf_f, w_o_f = w3_full[0], w3_full[1], w3_full[2]
    small4 = g_small[0::2]
    conv_w_f = small4[:, 0:DN_TAPS, :].transpose(1, 0, 2).reshape(DN_TAPS, 3 * d)
    meta_f = small4[:, 8:8 + N_META, :ds].transpose(1, 0, 2).reshape(N_META, d)
    dw_w_f = small4[:, 24:56, :ds].transpose(1, 0, 2).reshape(32, d)

    xs = jnp.concatenate([jnp.pad(meta_f, ((PAD_ROWS, 0), (0, 0))), x[0]], axis=0)
    tgt = jnp.pad(loss_target[0], ((HDR, 0), (0, 0)))
    h = _rms_fwd(xs, norm_w, "rms_fwd")
    proj = _mm(h, w_main, name="proj_main")
    pba = _mm(h, w_ba, name="proj_ba")
    qkvn = _dn_prep(proj, conv_w_f, d, heads, "dn_prep")
    pvec = jnp.zeros((8, LANES), F32).at[0, heads:2 * heads].set(a_log[0]).at[1, heads:2 * heads].set(dt_bias[0])
    bg = _ba_fwd(pba, pvec, heads, "ba_fwd")
    g_row = bg[:, heads:2 * heads].reshape(nc, CHUNK, heads).transpose(0, 2, 1)
    o, sall, ainv = _dn_fwd(qkvn, bg, g_row, d, heads, "dn_fwd")
    ya_in = _dn_post(o, proj, dn_norm_w, d, heads, "dn_post")
    ya = _mm(ya_in, w_dn_f, name="ya")
    c1, c3 = _cf_fwd(proj, dw_w_f, dw_b, ln_w, ln_b, d, "cf_fwd")
    yb = _mm(c3, w_cf_f, name="yb")
    merged = _merge(proj, ya, yb, b_cf_out, d, "merge")
    mo = _mm(merged, w_o_f, name="mo")
    dxo, dxo_mm, loss_acc, g_fnw = _final(xs, mo, final_norm_w.reshape(1, d), tgt, "final")

    dmerged = _mm(dxo_mm, w_o_f, nt=True, name="d_merged")
    gw_o = _mm(merged.T, dxo_mm, name="gw_o")
    dya, dyb, dproj, g_bcf = _merge_bwd(proj, ya, yb, b_cf_out, dmerged, d, "merge_bwd")
    dc3 = _mm(dyb, w_cf_f, nt=True, name="d_c3")
    gw_cf = _mm(c3.T, dyb, name="gw_cf")
    dyain = _mm(dya, w_dn_f, nt=True, name="d_ya_in")
    gw_dn = _mm(ya_in.T, dya, name="gw_dn")
    dc1, dproj, g_ln = _cf_bwd1(c1, proj, ln_w, ln_b, dc3, dproj, d, "cf_bwd1")
    dproj, g_dww = _cf_bwd2(dc1, proj, dw_w_f, dproj, d, "cf_bwd2")
    do, dproj, g_dnw = _dn_post_bwd(o, proj, dn_norm_w, dyain, dproj, d, heads, "dn_post_bwd")
    dqkvn, dbg, dg_row = _dn_bwd(qkvn, bg, g_row, do, sall, ainv, d, heads, "dn_bwd")
    dconv = _dn_prep_bwd1(proj, conv_w_f, dqkvn, d, heads, "dn_prep_bwd1")
    dproj, g_convw = _dn_prep_bwd2(dconv, proj, conv_w_f, dproj, d, "dn_prep_bwd2")
    dbg = dbg + jnp.pad(dg_row.transpose(0, 2, 1).reshape(t, heads), ((0, 0), (heads, LANES - 2 * heads)))
    dpba, g_ba = _ba_bwd(pba, pvec, dbg, heads, "ba_bwd")
    dh1 = _mm(dproj, w_main, nt=True, name="d_h_main")
    dh2 = _mm(dpba, w_ba, nt=True, name="d_h_ba")
    h_t = h.T
    gw_main = _mm(h_t, dproj, name="gw_main")
    gw_ba = _mm(h_t, dpba, name="gw_ba")
    dxs, g_nw = _rms_bwd(xs, norm_w, dh1, dh2, dxo, "rms_bwd")
    grad_x = dxs[HDR:][None]

    gw_in = jnp.concatenate([gw_main[:, :4 * d], gw_ba[:, :2 * heads], gw_main[:, 4 * d:6 * d], gw_main[:, 8 * d:],
                             gw_main[:, 6 * d:8 * d]], axis=1)
    send_in = gw_in.reshape(d, 4, in_w // 4).transpose(1, 0, 2).reshape(8, d // 2, in_w // 4)
    send_w3 = jnp.stack([gw_dn, gw_cf, gw_o]).reshape(3, 8, d // 8, d).transpose(1, 0, 2, 3).reshape(8, 3 * d // 8, d)
    misc = jnp.concatenate([g_ba[0, heads:2 * heads], g_ba[1, heads:2 * heads], g_dnw[0], loss_acc[0, 0:1]])
    small_g = jnp.concatenate([
        g_convw[0:DN_TAPS].reshape(DN_TAPS * 3, d), dxs[PAD_ROWS:HDR], g_dww, g_nw[0:1], g_ln[2:3], g_ln[0:1], g_ln[1:2],
        g_bcf[0:1], g_fnw[0:1], jnp.pad(misc, (0, d - misc.shape[0]))[None], jnp.zeros((5, d), F32)], axis=0)
    got_in, got_w3 = _alltoall([send_in, send_w3], "exchange_grads")
    red_in = _sum8(got_in, "sum_w_in")
    red_w3 = _sum8(got_w3, "sum_w3")
    pair_in, pair_w3 = _sibling_exchange([red_in, red_w3], "pair_grads")
    g_w_in = pair_in.reshape(d, in_w // 4)
    g_w3 = pair_w3.reshape(2, 3, d // 8, d).transpose(1, 0, 2, 3).reshape(3 * ds, d)
    all_small, = _allgather([small_g], "gather_small_grads")
    sg = _sum8(all_small, "sum_small")
    g_conv = lax.dynamic_slice_in_dim(sg[0:12].reshape(DN_TAPS, 3 * d), shard * 3 * ds, 3 * ds, axis=1)
    g_meta = lax.dynamic_slice_in_dim(sg[12:28], shard * ds, ds, axis=1)
    g_dw_w = lax.dynamic_slice_in_dim(sg[28:28 + CF_TAPS], shard * ds, ds, axis=1)
    r0 = 28 + 32
    g_rep = {"norm_w": sg[r0:r0 + 1], "dw_b": sg[r0 + 1:r0 + 2], "ln_w": sg[r0 + 2:r0 + 3], "ln_b": sg[r0 + 3:r0 + 4],
             "b_cf_out": sg[r0 + 4:r0 + 5], "final_norm_w": sg[r0 + 5], "a_log": sg[r0 + 6:r0 + 7, 0:heads],
             "dt_bias": sg[r0 + 6:r0 + 7, heads:2 * heads], "dn_norm_w": sg[r0 + 6:r0 + 7, 2 * heads:2 * heads + DK]}
    loss = sg[r0 + 6, 2 * heads + DK]

    res = {}
    dl, nm, nv = _adamw(g_w_in, w_in[0], m_w_in[0], v_w_in[0], "adamw_w_in")
    res["w_in"] = (g_w_in[None], dl[None], nm[None], nv[None])
    stack3 = lambda a, b, c: jnp.concatenate([a[0], b[0], c[0]], axis=0)
    dl, nm, nv = _adamw(g_w3, stack3(w_dn_out, w_cf_out, w_o), stack3(m_w_dn_out, m_w_cf_out, m_w_o),
                        stack3(v_w_dn_out, v_w_cf_out, v_w_o), "adamw_w3")
    for j, nme in enumerate(("w_dn_out", "w_cf_out", "w_o")):
        res[nme] = tuple(a[j * ds:(j + 1) * ds][None] for a in (g_w3, dl, nm, nv))
    names = ["meta", "conv_qkv_w", "dw_w", "norm_w", "dw_b", "ln_w", "ln_b", "b_cf_out", "final_norm_w", "a_log", "dt_bias",
             "dn_norm_w"]
    grads = {"meta": g_meta, "conv_qkv_w": g_conv[None], "dw_w": g_dw_w[None], **g_rep}
    given = dict(meta=(meta, m_meta, v_meta), conv_qkv_w=(conv_qkv_w, m_conv_qkv_w, v_conv_qkv_w), dw_w=(dw_w, m_dw_w, v_dw_w),
                 norm_w=(norm_w, m_norm_w, v_norm_w), dw_b=(dw_b, m_dw_b, v_dw_b), ln_w=(ln_w, m_ln_w, v_ln_w),
                 ln_b=(ln_b, m_ln_b, v_ln_b), b_cf_out=(b_cf_out, m_b_cf_out, v_b_cf_out),
                 final_norm_w=(final_norm_w, m_final_norm_w, v_final_norm_w), a_log=(a_log, m_a_log, v_a_log),
                 dt_bias=(dt_bias, m_dt_bias, v_dt_bias), dn_norm_w=(dn_norm_w, m_dn_norm_w, v_dn_norm_w))
    shapes = [given[nme][0].shape for nme in names]
    packs = [_flat_pack([grads[nme] for nme in names], LANES)] + [_flat_pack([given[nme][j] for nme in names], LANES) for j in range(3)]
    outs = _adamw(*packs, "adamw_small")
    unpacked = [_flat_unpack(p, shapes) for p in (packs[0], *outs)]
    for j, nme in enumerate(names):
        res[nme] = tuple(u[j] for u in unpacked)

    order = ["meta", "norm_w", "w_in", "conv_qkv_w", "a_log", "dt_bias", "dn_norm_w", "w_dn_out", "dw_w", "dw_b", "ln_w", "ln_b",
             "w_cf_out", "b_cf_out", "w_o", "final_norm_w"]
    return (loss, grad_x, *[res[nme][0] for nme in order], *[res[nme][1] for nme in order],
            *[res[nme][2] for nme in order], *[res[nme][3] for nme in order])
```

```python
import jax
import jax.numpy as jnp
from jax import lax
from jax.experimental import pallas as pl
from jax.experimental.pallas import tpu as pltpu

F32 = jnp.float32
MM_DTYPE = jnp.bfloat16
EPS = 1e-6
N_META = 16
HDR = 128
PAD_ROWS = HDR - N_META
CHUNK = 64
DK = 128
CF_TAPS = 31
DN_TAPS = 4
HALO = 64
CONV_ROWS = 128
LANES = 128
ADAM_LR, ADAM_B1, ADAM_B2, ADAM_EPS, ADAM_WD, ADAM_STEP = 0.001, 0.9, 0.999, 1e-08, 0.01, 10
VMEM_LIMIT = 48 * 1024 * 1024
MESH = pl.DeviceIdType.MESH
HI = lax.Precision.HIGHEST
LO = lax.Precision.DEFAULT
NN = (((1,), (0,)), ((), ()))
NT = (((1,), (1,)), ((), ()))
TN = (((0,), (0,)), ((), ()))
ANY = pl.BlockSpec(memory_space=pl.ANY)


def _dot(a, b, dims=NN, prec=LO):
    if prec == LO:
        a, b = a.astype(MM_DTYPE), b.astype(MM_DTYPE)
    return lax.dot_general(a, b, dims, precision=prec, preferred_element_type=F32)


def _split(a):
    hi = a.astype(MM_DTYPE)
    return hi, (a - hi.astype(F32)).astype(MM_DTYPE)


def _dot3(a, b, dims=NN):
    ah, al = _split(a)
    bh, bl = _split(b)
    return _dot(ah, bh, dims) + (_dot(ah, bl, dims) + _dot(al, bh, dims))


def _pick(n, options):
    for o in options:
        if n % o == 0:
            return o
    return n


def _cparams(*sem):
    return pltpu.CompilerParams(dimension_semantics=sem, vmem_limit_bytes=VMEM_LIMIT)


def _sig(x):
    return jax.nn.sigmoid(x)


def _sum_all(a):
    return jnp.sum(jnp.sum(a, axis=-1, keepdims=True), axis=0, keepdims=True)


def _dsilu(x, s):
    return s + x * s * (1.0 - s)


def _shift(win, s):
    n = win.shape[0]
    s = s % n
    return win if s == 0 else pltpu.roll(win, s, 0)


def _row_tile(t):
    return _pick(t, (320, 128))


def _mm(a, b, *, nt=False, out_dtype=F32, name):
    m, k = a.shape
    n = b.shape[0] if nt else b.shape[1]
    tm = _pick(m, (1024, 640, 512, 384, 256, 128))
    tn = _pick(n, (1024, 768, 512, 256, 128))
    tk = _pick(k, (1664, 1024, 640, 512, 256, 128))
    nk = k // tk
    dims = NT if nt else NN

    def body(a_ref, b_ref, o_ref, *acc):
        p = lax.dot_general(a_ref[...], b_ref[...], dims, preferred_element_type=F32)
        if nk == 1:
            o_ref[...] = p.astype(o_ref.dtype)
            return
        acc_ref, = acc
        kk = pl.program_id(2)

        @pl.when(kk == 0)
        def _():
            acc_ref[...] = p

        @pl.when(kk > 0)
        def _():
            acc_ref[...] += p

        @pl.when(kk == nk - 1)
        def _():
            o_ref[...] = acc_ref[...].astype(o_ref.dtype)

    b_spec = pl.BlockSpec((tn, tk), lambda i, j, kk: (j, kk)) if nt else pl.BlockSpec((tk, tn), lambda i, j, kk: (kk, j))
    return pl.pallas_call(
        body, name=name, grid=(m // tm, n // tn, nk),
        in_specs=[pl.BlockSpec((tm, tk), lambda i, j, kk: (i, kk)), b_spec],
        out_specs=pl.BlockSpec((tm, tn), lambda i, j, kk: (i, j)),
        out_shape=jax.ShapeDtypeStruct((m, n), out_dtype),
        scratch_shapes=[] if nk == 1 else [pltpu.VMEM((tm, tn), F32)],
        compiler_params=_cparams("parallel", "parallel", "arbitrary"),
    )(a, b)


def _rms_fwd(xs, w, name):
    t, d = xs.shape
    r = _row_tile(t)

    def body(x_ref, w_ref, h_ref):
        x = x_ref[...]
        rs = lax.rsqrt(jnp.mean(x * x, axis=-1, keepdims=True) + EPS)
        h_ref[...] = (x * rs * w_ref[...]).astype(h_ref.dtype)

    return pl.pallas_call(
        body, name=name, grid=(t // r,),
        in_specs=[pl.BlockSpec((r, d), lambda i: (i, 0)), pl.BlockSpec((1, d), lambda i: (0, 0))],
        out_specs=pl.BlockSpec((r, d), lambda i: (i, 0)),
        out_shape=jax.ShapeDtypeStruct((t, d), MM_DTYPE), compiler_params=_cparams("parallel"),
    )(xs, w)


def _rms_bwd(xs, w, dh1, dh2, dres, name):
    t, d = xs.shape
    r = _row_tile(t)

    def body(x_ref, w_ref, d1_ref, d2_ref, dr_ref, dx_ref, gw_ref):
        @pl.when(pl.program_id(0) == 0)
        def _():
            gw_ref[...] = jnp.zeros_like(gw_ref)

        x = x_ref[...]
        dh = d1_ref[...] + d2_ref[...]
        rs = lax.rsqrt(jnp.mean(x * x, axis=-1, keepdims=True) + EPS)
        xh = x * rs
        dxh = dh * w_ref[...]
        dx_ref[...] = rs * (dxh - xh * jnp.mean(dxh * xh, axis=-1, keepdims=True)) + dr_ref[...]
        gw_ref[0:1, :] += jnp.sum(dh * xh, axis=0, keepdims=True)

    row = pl.BlockSpec((r, d), lambda i: (i, 0))
    return pl.pallas_call(
        body, name=name, grid=(t // r,),
        in_specs=[row, pl.BlockSpec((1, d), lambda i: (0, 0)), row, row, row],
        out_specs=[row, pl.BlockSpec((8, d), lambda i: (0, 0))],
        out_shape=[jax.ShapeDtypeStruct((t, d), F32), jax.ShapeDtypeStruct((8, d), F32)],
        compiler_params=_cparams("arbitrary"),
    )(xs, w, dh1, dh2, dres)


def _final(xs, mo, w, tgt, name):
    t, d = xs.shape
    r = _row_tile(t)

    def body(x_ref, m_ref, w_ref, t_ref, dx_ref, dxm_ref, loss_ref, gw_ref):
        i = pl.program_id(0)

        @pl.when(i == 0)
        def _():
            loss_ref[...] = jnp.zeros_like(loss_ref)
            gw_ref[...] = jnp.zeros_like(gw_ref)

        xo = x_ref[...] + m_ref[...]
        rs = lax.rsqrt(jnp.mean(xo * xo, axis=-1, keepdims=True) + EPS)
        xh = xo * rs
        y = xh * w_ref[...]
        rows = i * r + lax.broadcasted_iota(jnp.int32, (r, 1), 0)
        err = jnp.where(rows >= HDR, y - t_ref[...], 0.0)
        loss_ref[...] += 0.5 * _sum_all(err * err) / d
        dy = err / d
        gw_ref[0:1, :] += jnp.sum(dy * xh, axis=0, keepdims=True)
        dxh = dy * w_ref[...]
        dx = rs * (dxh - xh * jnp.mean(dxh * xh, axis=-1, keepdims=True))
        dx_ref[...] = dx
        dxm_ref[...] = dx.astype(dxm_ref.dtype)

    row = pl.BlockSpec((r, d), lambda i: (i, 0))
    return pl.pallas_call(
        body, name=name, grid=(t // r,),
        in_specs=[row, row, pl.BlockSpec((1, d), lambda i: (0, 0)), row],
        out_specs=[row, row, pl.BlockSpec((8, LANES), lambda i: (0, 0)), pl.BlockSpec((8, d), lambda i: (0, 0))],
        out_shape=[jax.ShapeDtypeStruct((t, d), F32), jax.ShapeDtypeStruct((t, d), MM_DTYPE),
                   jax.ShapeDtypeStruct((8, LANES), F32), jax.ShapeDtypeStruct((8, d), F32)],
        compiler_params=_cparams("arbitrary"),
    )(xs, mo, w, tgt)


def _halo_prev(r):
    return lambda i: (jnp.maximum(i * (r // HALO) - 1, 0), 0)


def _halo_next(r, t):
    return lambda i: (jnp.minimum((i + 1) * (r // HALO), t // HALO - 1), 0)


def _qkv_conv_window(cur, prev_ref, i):
    prev = jnp.where(i > 0, prev_ref[HALO - 8:HALO, :], 0.0)
    return jnp.concatenate([prev, cur], axis=0)


def _qkv_act(win, cw, r, d, heads):
    y = None
    for s in range(DN_TAPS):
        term = _shift(win, s)[8:8 + r, :] * cw[DN_TAPS - 1 - s:DN_TAPS - s, :]
        y = term if y is None else y + term
    a = y * _sig(y)
    return y, a


def _dn_prep(proj, conv_w, d, heads, name):
    t = proj.shape[0]
    r = CONV_ROWS
    w3 = 3 * d

    def body(p_ref, ph_ref, cw_ref, o_ref):
        i = pl.program_id(0)
        win = _qkv_conv_window(p_ref[...], ph_ref, i)
        _, a = _qkv_act(win, cw_ref[...], r, d, heads)
        for hh in range(2 * heads):
            ah = a[:, hh * DK:(hh + 1) * DK]
            rn = lax.rsqrt(jnp.sum(ah * ah, axis=-1, keepdims=True) + EPS)
            sc = DK ** -0.5 if hh < heads else 1.0
            o_ref[:, hh * DK:(hh + 1) * DK] = ah * (rn * sc)
        o_ref[:, 2 * d:] = a[:, 2 * d:]

    return pl.pallas_call(
        body, name=name, grid=(t // r,),
        in_specs=[pl.BlockSpec((r, w3), lambda i: (i, 0)), pl.BlockSpec((HALO, w3), _halo_prev(r)),
                  pl.BlockSpec((DN_TAPS, w3), lambda i: (0, 0))],
        out_specs=pl.BlockSpec((r, w3), lambda i: (i, 0)),
        out_shape=jax.ShapeDtypeStruct((t, w3), F32), compiler_params=_cparams("parallel"),
    )(proj, proj, conv_w)


def _dn_prep_bwd1(proj, conv_w, dqkvn, d, heads, name):
    t = proj.shape[0]
    r = CONV_ROWS
    w3 = 3 * d

    def body(p_ref, ph_ref, cw_ref, dn_ref, o_ref):
        i = pl.program_id(0)
        win = _qkv_conv_window(p_ref[...], ph_ref, i)
        y, a = _qkv_act(win, cw_ref[...], r, d, heads)
        ds = _dsilu(y, _sig(y))
        for hh in range(2 * heads):
            sl = slice(hh * DK, (hh + 1) * DK)
            ah = a[:, sl]
            rn = lax.rsqrt(jnp.sum(ah * ah, axis=-1, keepdims=True) + EPS)
            sc = DK ** -0.5 if hh < heads else 1.0
            n0 = ah * rn
            dn0 = dn_ref[:, sl] * sc
            da = rn * (dn0 - n0 * jnp.sum(dn0 * n0, axis=-1, keepdims=True))
            o_ref[:, sl] = da * ds[:, sl]
        o_ref[:, 2 * d:] = dn_ref[:, 2 * d:] * ds[:, 2 * d:]

    row = pl.BlockSpec((r, w3), lambda i: (i, 0))
    return pl.pallas_call(
        body, name=name, grid=(t // r,),
        in_specs=[row, pl.BlockSpec((HALO, w3), _halo_prev(r)), pl.BlockSpec((DN_TAPS, w3), lambda i: (0, 0)), row],
        out_specs=row, out_shape=jax.ShapeDtypeStruct((t, w3), F32), compiler_params=_cparams("parallel"),
    )(proj, proj, conv_w, dqkvn)


def _dn_prep_bwd2(dconv, proj, conv_w, dproj, d, name):
    t = proj.shape[0]
    r = CONV_ROWS
    w3 = 3 * d
    last = t // r - 1

    def body(dc_ref, dcn_ref, p_ref, ph_ref, cw_ref, _, dp_ref, gw_ref):
        i = pl.program_id(0)

        @pl.when(i == 0)
        def _():
            gw_ref[...] = jnp.zeros_like(gw_ref)

        cw = cw_ref[...]
        dcur = dc_ref[...]
        nxt = jnp.where(i < last, dcn_ref[0:8, :], 0.0)
        dwin = jnp.concatenate([dcur, nxt], axis=0)
        win = _qkv_conv_window(p_ref[...], ph_ref, i)
        acc = None
        for j in range(DN_TAPS):
            s = DN_TAPS - 1 - j
            term = _shift(dwin, -s)[0:r, :] * cw[j:j + 1, :]
            acc = term if acc is None else acc + term
            gw_ref[j:j + 1, :] += jnp.sum(dcur * _shift(win, s)[8:8 + r, :], axis=0, keepdims=True)
        dp_ref[...] = acc.astype(dp_ref.dtype)

    row = pl.BlockSpec((r, w3), lambda i: (i, 0))
    return pl.pallas_call(
        body, name=name, grid=(t // r,),
        in_specs=[row, pl.BlockSpec((HALO, w3), _halo_next(r, t)), row, pl.BlockSpec((HALO, w3), _halo_prev(r)),
                  pl.BlockSpec((DN_TAPS, w3), lambda i: (0, 0)), ANY],
        out_specs=[row, pl.BlockSpec((8, w3), lambda i: (0, 0))],
        out_shape=[jax.ShapeDtypeStruct(dproj.shape, dproj.dtype), jax.ShapeDtypeStruct((8, w3), F32)],
        input_output_aliases={5: 0}, compiler_params=_cparams("arbitrary"),
    )(dconv, dconv, proj, proj, conv_w, dproj)


def _ba_terms(x, pv, heads):
    lane = lax.broadcasted_iota(jnp.int32, x.shape, 1)
    is_b = lane < heads
    is_a = jnp.logical_and(lane >= heads, lane < 2 * heads)
    beta = _sig(x)
    z = x + pv[1:2, :]
    nexp = -jnp.exp(pv[0:1, :])
    sp = jnp.maximum(z, 0.0) + jnp.log1p(jnp.exp(-jnp.abs(z)))
    return is_b, is_a, beta, z, nexp, nexp * sp


def _ba_fwd(pba, pvec, heads, name):
    t = pba.shape[0]
    r = _row_tile(t)

    def body(x_ref, pv_ref, o_ref):
        is_b, is_a, beta, _, _, g = _ba_terms(x_ref[...], pv_ref[...], heads)
        rows = pl.program_id(0) * r + lax.broadcasted_iota(jnp.int32, (r, 1), 0)
        o_ref[...] = jnp.where(rows >= PAD_ROWS, jnp.where(is_b, beta, jnp.where(is_a, g, 0.0)), 0.0)

    row = pl.BlockSpec((r, LANES), lambda i: (i, 0))
    return pl.pallas_call(
        body, name=name, grid=(t // r,), in_specs=[row, pl.BlockSpec((8, LANES), lambda i: (0, 0))],
        out_specs=row, out_shape=jax.ShapeDtypeStruct((t, LANES), F32), compiler_params=_cparams("parallel"),
    )(pba, pvec)


def _ba_bwd(pba, pvec, dbg, heads, name):
    t = pba.shape[0]
    r = _row_tile(t)

    def body(x_ref, pv_ref, d_ref, o_ref, g_ref):
        @pl.when(pl.program_id(0) == 0)
        def _():
            g_ref[...] = jnp.zeros_like(g_ref)

        is_b, is_a, beta, z, nexp, g = _ba_terms(x_ref[...], pv_ref[...], heads)
        rows = pl.program_id(0) * r + lax.broadcasted_iota(jnp.int32, (r, 1), 0)
        dd = jnp.where(rows >= PAD_ROWS, d_ref[...], 0.0)
        dz = dd * nexp * _sig(z)
        o_ref[...] = jnp.where(is_b, dd * beta * (1.0 - beta), jnp.where(is_a, dz, 0.0)).astype(o_ref.dtype)
        g_ref[0:1, :] += jnp.sum(jnp.where(is_a, dd * g, 0.0), axis=0, keepdims=True)
        g_ref[1:2, :] += jnp.sum(jnp.where(is_a, dz, 0.0), axis=0, keepdims=True)

    row = pl.BlockSpec((r, LANES), lambda i: (i, 0))
    par = pl.BlockSpec((8, LANES), lambda i: (0, 0))
    return pl.pallas_call(
        body, name=name, grid=(t // r,), in_specs=[row, par, row], out_specs=[row, par],
        out_shape=[jax.ShapeDtypeStruct((t, LANES), MM_DTYPE), jax.ShapeDtypeStruct((8, LANES), F32)],
        compiler_params=_cparams("arbitrary"),
    )(pba, pvec, dbg)


def _chunk_consts():
    ri = lax.broadcasted_iota(jnp.int32, (CHUNK, CHUNK), 0)
    ci = lax.broadcasted_iota(jnp.int32, (CHUNK, CHUNK), 1)
    return ri >= ci, ri > ci, (ri == ci).astype(F32), (ri >= ci).astype(F32), (ri <= ci).astype(F32)


def _chunk_decay(gc, gr, incl):
    return jnp.where(incl, jnp.exp(jnp.where(incl, gc - gr, 0.0)), 0.0)


def _unit_lower_inverse(n, eye):
    x = [eye + a for a in n]
    p = list(n)
    for _ in range(5):
        p = [_dot(a, a) for a in p]
        x = [a + _dot(a, b) for a, b in zip(x, p)]
    r = [eye - a + _dot3(b, a) for a, b in zip(x, n)]
    return [a + _dot(a, b) for a, b in zip(x, r)]


def _dn_fwd(qkvn, bg, g_row, d, heads, name):
    t = qkvn.shape[0]
    nc = t // CHUNK

    def body(qkv_ref, bg_ref, gr_ref, o_ref, sall_ref, ainv_ref, s_ref):
        @pl.when(pl.program_id(0) == 0)
        def _():
            s_ref[...] = jnp.zeros_like(s_ref)

        incl, strict, eye, tril, triu = _chunk_consts()
        bgv = bg_ref[...]
        gc_all = _dot(tril, bgv, NN, HI)
        gr_all = _dot(gr_ref[0], triu, NN, HI)
        hs = range(heads)
        q = [qkv_ref[:, h * DK:(h + 1) * DK] for h in hs]
        k = [qkv_ref[:, d + h * DK:d + (h + 1) * DK] for h in hs]
        v = [qkv_ref[:, 2 * d + h * DK:2 * d + (h + 1) * DK] for h in hs]
        kb = [a.astype(MM_DTYPE) for a in k]
        beta = [bgv[:, h:h + 1] for h in hs]
        gc = [gc_all[:, heads + h:heads + h + 1] for h in hs]
        decay = [_chunk_decay(gc[h], gr_all[h:h + 1, :], incl) for h in hs]
        eg = [jnp.exp(a) for a in gc]
        n = [jnp.where(strict, -(beta[h] * _dot(kb[h], kb[h], NT) * decay[h]), 0.0) for h in hs]
        x = _unit_lower_inverse(n, eye)
        sol = [_dot3(x[h], jnp.concatenate([v[h] * beta[h], k[h] * (beta[h] * eg[h])], axis=1)) for h in hs]
        attn = [_dot(q[h], kb[h], NT) * decay[h] for h in hs]
        s = [s_ref[h] for h in hs]
        sb = [a.astype(MM_DTYPE) for a in s]
        wv = [sol[h][:, :DK] - _dot(sol[h][:, DK:], sb[h]) for h in hs]
        o = [_dot(q[h] * eg[h], sb[h]) + _dot(attn[h], wv[h]) for h in hs]
        glast = [a[CHUNK - 1:CHUNK, :] for a in gc]
        s_new = [s[h] * jnp.exp(glast[h]) + _dot(k[h] * jnp.exp(glast[h] - gc[h]), wv[h], TN) for h in hs]
        for h in hs:
            o_ref[:, h * DK:(h + 1) * DK] = o[h]
            sall_ref[0, h] = s[h]
            ainv_ref[0, h] = x[h]
            s_ref[h] = s_new[h]

    return pl.pallas_call(
        body, name=name, grid=(nc,),
        in_specs=[pl.BlockSpec((CHUNK, 3 * d), lambda i: (i, 0)), pl.BlockSpec((CHUNK, LANES), lambda i: (i, 0)),
                  pl.BlockSpec((1, heads, CHUNK), lambda i: (i, 0, 0))],
        out_specs=[pl.BlockSpec((CHUNK, d), lambda i: (i, 0)), pl.BlockSpec((1, heads, DK, DK), lambda i: (i, 0, 0, 0)),
                   pl.BlockSpec((1, heads, CHUNK, CHUNK), lambda i: (i, 0, 0, 0))],
        out_shape=[jax.ShapeDtypeStruct((t, d), F32), jax.ShapeDtypeStruct((nc, heads, DK, DK), F32),
                   jax.ShapeDtypeStruct((nc, heads, CHUNK, CHUNK), F32)],
        scratch_shapes=[pltpu.VMEM((heads, DK, DK), F32)], compiler_params=_cparams("arbitrary"),
    )(qkvn, bg, g_row)


def _dn_bwd(qkvn, bg, g_row, do, sall, ainv, d, heads, name):
    t = qkvn.shape[0]
    nc = t // CHUNK
    rev = lambda i: nc - 1 - i

    def body(qkv_ref, bg_ref, gr_ref, do_ref, sall_ref, ainv_ref, dqkv_ref, dbg_ref, dgr_ref, ds_ref):
        @pl.when(pl.program_id(0) == 0)
        def _():
            ds_ref[...] = jnp.zeros_like(ds_ref)

        incl, strict, eye, tril, triu = _chunk_consts()
        bgv = bg_ref[...]
        gc_all = _dot(tril, bgv, NN, HI)
        gr_all = _dot(gr_ref[0], triu, NN, HI)
        lane = lax.broadcasted_iota(jnp.int32, (CHUNK, LANES), 1)
        hrow = lax.broadcasted_iota(jnp.int32, (heads, CHUNK), 0)
        last_row = lax.broadcasted_iota(jnp.int32, (CHUNK, 1), 0) == CHUNK - 1
        dbeta_slab = jnp.zeros((CHUNK, LANES), F32)
        dgc_slab = jnp.zeros((CHUNK, LANES), F32)
        dgr_slab = jnp.zeros((heads, CHUNK), F32)
        hs = range(heads)
        cols = lambda ref, off: [ref[:, off + h * DK:off + (h + 1) * DK] for h in hs]
        qs, ks, vs, douts = cols(qkv_ref, 0), cols(qkv_ref, d), cols(qkv_ref, 2 * d), cols(do_ref, 0)
        ss, xs_, dsns = [sall_ref[0, h] for h in hs], [ainv_ref[0, h] for h in hs], [ds_ref[h] for h in hs]

        def recompute(h, z):
            z.q, z.k, z.v, z.dout, z.s, z.x, z.dsn = qs[h], ks[h], vs[h], douts[h], ss[h], xs_[h], dsns[h]
            z.kb, z.qb, z.sb, z.dsnb, z.doutb = (a.astype(MM_DTYPE) for a in (z.k, z.q, z.s, z.dsn, z.dout))
            z.beta = bgv[:, h:h + 1]
            gc = gc_all[:, heads + h:heads + h + 1]
            z.decay = _chunk_decay(gc, gr_all[h:h + 1, :], incl)
            z.eg = jnp.exp(gc)
            glast = gc[CHUNK - 1:CHUNK, :]
            z.eglast = jnp.exp(glast)
            z.ek = jnp.exp(glast - gc)
            z.kk = _dot(z.kb, z.kb, NT)
            z.qk = _dot(z.qb, z.kb, NT)
            sol = _dot3(z.x, jnp.concatenate([z.v * z.beta, z.k * (z.beta * z.eg)], axis=1))
            z.u, z.wb = sol[:, :DK], sol[:, DK:].astype(MM_DTYPE)
            z.attn = z.qk * z.decay
            z.qg = z.q * z.eg
            z.kend = z.k * z.ek

        def pseudo_values(h, z):
            z.wvb = (z.u - _dot(z.wb, z.sb)).astype(MM_DTYPE)

        def scan_step(h, z):
            z.d_wv = _dot(z.attn, z.doutb, TN) + _dot(z.kend, z.dsnb)
            z.d_attn = _dot(z.doutb, z.wvb, NT)
            z.d_qg = _dot(z.doutb, z.sb, NT)
            z.d_kend = _dot(z.wvb, z.dsnb, NT)
            z.ds_new = _dot(z.qg, z.doutb, TN) + z.eglast * z.dsn
            z.d_glast = z.eglast * _sum_all(z.dsn * z.s) + _sum_all(z.d_kend * z.kend)

        def state_terms(h, z):
            z.ds_new = z.ds_new - _dot(z.wb, z.d_wv, TN)
            z.d_w = -_dot(z.d_wv, z.sb, NT)

        def solve_transpose(h, z):
            d_rhs = _dot3(z.x, jnp.concatenate([z.d_wv, z.d_w], axis=1), TN)
            z.d_ru, z.d_rw = d_rhs[:, :DK], d_rhs[:, DK:]

        def lower_terms(h, z):
            z.d_low = jnp.where(strict, -(_dot(z.d_ru, z.u, NT) + _dot(z.d_rw, z.wb, NT)), 0.0)

        def outputs(h, z):
            rw_k = jnp.sum(z.d_rw * z.k, axis=-1, keepdims=True)
            z.dbeta = (jnp.sum(z.d_ru * z.v, axis=-1, keepdims=True) + rw_k * z.eg
                       + jnp.sum(z.d_low * z.kk * z.decay, axis=-1, keepdims=True))
            z.dv = z.d_ru * z.beta
            d_kk = (z.d_low * z.beta * z.decay).astype(MM_DTYPE)
            d_qk = (z.d_attn * z.decay).astype(MM_DTYPE)
            z.dk = (z.d_rw * (z.beta * z.eg) + _dot(d_kk, z.kb) + _dot(d_kk, z.kb, TN) + _dot(d_qk, z.qb, TN)
                    + z.d_kend * z.ek)
            z.dq = _dot(d_qk, z.kb) + z.d_qg * z.eg
            e = (z.d_low * z.beta * z.kk + z.d_attn * z.qk) * z.decay
            z.dgc = (rw_k * z.beta * z.eg + jnp.sum(e, axis=-1, keepdims=True)
                     + jnp.sum(z.d_qg * z.qg, axis=-1, keepdims=True) - jnp.sum(z.d_kend * z.kend, axis=-1, keepdims=True)
                     + jnp.where(last_row, z.d_glast, 0.0))
            z.dgr = -jnp.sum(e, axis=0, keepdims=True)

        class _Head:
            pass

        st = [_Head() for _ in hs]
        for phase in (recompute, pseudo_values, scan_step, state_terms, solve_transpose, lower_terms, outputs):
            for h in hs:
                phase(h, st[h])
        res = [(z.dq, z.dk, z.dv, z.ds_new, z.dbeta, z.dgc, z.dgr) for z in st]
        for h, (dq, dk, dv, ds_new, dbeta, dgc, dgr) in enumerate(res):
            dqkv_ref[:, h * DK:(h + 1) * DK] = dq
            dqkv_ref[:, d + h * DK:d + (h + 1) * DK] = dk
            dqkv_ref[:, 2 * d + h * DK:2 * d + (h + 1) * DK] = dv
            ds_ref[h] = ds_new
            dbeta_slab = jnp.where(lane == h, dbeta, dbeta_slab)
            dgc_slab = jnp.where(lane == heads + h, dgc, dgc_slab)
            dgr_slab = jnp.where(hrow == h, dgr, dgr_slab)
        dbg_ref[...] = dbeta_slab + _dot(triu, dgc_slab, NN, HI)
        dgr_ref[0] = _dot(dgr_slab, tril, NN, HI)

    return pl.pallas_call(
        body, name=name, grid=(nc,),
        in_specs=[pl.BlockSpec((CHUNK, 3 * d), lambda i: (rev(i), 0)), pl.BlockSpec((CHUNK, LANES), lambda i: (rev(i), 0)),
                  pl.BlockSpec((1, heads, CHUNK), lambda i: (rev(i), 0, 0)), pl.BlockSpec((CHUNK, d), lambda i: (rev(i), 0)),
                  pl.BlockSpec((1, heads, DK, DK), lambda i: (rev(i), 0, 0, 0)),
                  pl.BlockSpec((1, heads, CHUNK, CHUNK), lambda i: (rev(i), 0, 0, 0))],
        out_specs=[pl.BlockSpec((CHUNK, 3 * d), lambda i: (rev(i), 0)), pl.BlockSpec((CHUNK, LANES), lambda i: (rev(i), 0)),
                   pl.BlockSpec((1, heads, CHUNK), lambda i: (rev(i), 0, 0))],
        out_shape=[jax.ShapeDtypeStruct((t, 3 * d), F32), jax.ShapeDtypeStruct((t, LANES), F32),
                   jax.ShapeDtypeStruct((nc, heads, CHUNK), F32)],
        scratch_shapes=[pltpu.VMEM((heads, DK, DK), F32)], compiler_params=_cparams("arbitrary"),
    )(qkvn, bg, g_row, do, sall, ainv)


def _dn_post(o, proj, w, d, heads, name):
    t = o.shape[0]
    r = _row_tile(t)

    def body(o_ref, z_ref, w_ref, y_ref):
        for h in range(heads):
            sl = slice(h * DK, (h + 1) * DK)
            oh, z = o_ref[:, sl], z_ref[:, sl]
            rs = lax.rsqrt(jnp.mean(oh * oh, axis=-1, keepdims=True) + EPS)
            y_ref[:, sl] = (oh * rs * w_ref[...] * (z * _sig(z))).astype(y_ref.dtype)

    row = pl.BlockSpec((r, d), lambda i: (i, 0))
    return pl.pallas_call(
        body, name=name, grid=(t // r,),
        in_specs=[row, pl.BlockSpec((r, d), lambda i: (i, 3)), pl.BlockSpec((1, DK), lambda i: (0, 0))],
        out_specs=row, out_shape=jax.ShapeDtypeStruct((t, d), MM_DTYPE), compiler_params=_cparams("parallel"),
    )(o, proj, w)


def _dn_post_bwd(o, proj, w, dy, dproj, d, heads, name):
    t = o.shape[0]
    r = _row_tile(t)

    def body(o_ref, z_ref, w_ref, dy_ref, _, do_ref, dz_ref, gw_ref):
        @pl.when(pl.program_id(0) == 0)
        def _():
            gw_ref[...] = jnp.zeros_like(gw_ref)

        gw = jnp.zeros((1, DK), F32)
        for h in range(heads):
            sl = slice(h * DK, (h + 1) * DK)
            oh, z, dyh = o_ref[:, sl], z_ref[:, sl], dy_ref[:, sl]
            sz = _sig(z)
            rs = lax.rsqrt(jnp.mean(oh * oh, axis=-1, keepdims=True) + EPS)
            xh = oh * rs
            dn = dyh * (z * sz)
            dz_ref[:, sl] = (dyh * (xh * w_ref[...]) * _dsilu(z, sz)).astype(dz_ref.dtype)
            dxh = dn * w_ref[...]
            do_ref[:, sl] = rs * (dxh - xh * jnp.mean(dxh * xh, axis=-1, keepdims=True))
            gw = gw + jnp.sum(dn * xh, axis=0, keepdims=True)
        gw_ref[0:1, :] += gw

    row = pl.BlockSpec((r, d), lambda i: (i, 0))
    za = pl.BlockSpec((r, d), lambda i: (i, 3))
    return pl.pallas_call(
        body, name=name, grid=(t // r,),
        in_specs=[row, za, pl.BlockSpec((1, DK), lambda i: (0, 0)), row, ANY],
        out_specs=[row, za, pl.BlockSpec((8, DK), lambda i: (0, 0))],
        out_shape=[jax.ShapeDtypeStruct((t, d), F32), jax.ShapeDtypeStruct(dproj.shape, dproj.dtype),
                   jax.ShapeDtypeStruct((8, DK), F32)],
        input_output_aliases={4: 1}, compiler_params=_cparams("arbitrary"),
    )(o, proj, w, dy, dproj)


def _glu_window(g_ref, gh_ref, i, d):
    cur = g_ref[:, :d] * _sig(g_ref[:, d:])
    prev = gh_ref[HALO - 32:HALO, :d] * _sig(gh_ref[HALO - 32:HALO, d:])
    return jnp.concatenate([jnp.where(i > 0, prev, 0.0), cur], axis=0)


def _ln_stats(c1):
    mu = jnp.mean(c1, axis=-1, keepdims=True)
    cc = c1 - mu
    return cc * lax.rsqrt(jnp.mean(cc * cc, axis=-1, keepdims=True) + EPS), lax.rsqrt(jnp.mean(cc * cc, axis=-1, keepdims=True) + EPS)


def _cf_fwd(proj, dw_w, dw_b, ln_w, ln_b, d, name):
    t = proj.shape[0]
    r = CONV_ROWS

    def body(g_ref, gh_ref, zb_ref, w_ref, b_ref, lw_ref, lb_ref, c1_ref, c3_ref):
        win = _glu_window(g_ref, gh_ref, pl.program_id(0), d)
        w = w_ref[...]
        acc = jnp.broadcast_to(b_ref[...], (r, d))
        for sub in range(8):
            rot = _shift(win, sub)
            for s in range(sub, CF_TAPS, 8):
                acc = acc + rot[32 - (s - sub):32 - (s - sub) + r, :] * w[CF_TAPS - 1 - s:CF_TAPS - s, :]
        c1_ref[...] = acc
        xh, _ = _ln_stats(acc)
        ln = xh * lw_ref[...] + lb_ref[...]
        zb = zb_ref[...]
        c3_ref[...] = ((ln * _sig(ln)) * (zb * _sig(zb))).astype(c3_ref.dtype)

    row = pl.BlockSpec((r, d), lambda i: (i, 0))
    par = pl.BlockSpec((1, d), lambda i: (0, 0))
    return pl.pallas_call(
        body, name=name, grid=(t // r,),
        in_specs=[pl.BlockSpec((r, 2 * d), lambda i: (i, 2)), pl.BlockSpec((HALO, 2 * d), lambda i: (_halo_prev(r)(i)[0], 2)),
                  pl.BlockSpec((r, d), lambda i: (i, 8)), pl.BlockSpec((32, d), lambda i: (0, 0)), par, par, par],
        out_specs=[row, row],
        out_shape=[jax.ShapeDtypeStruct((t, d), F32), jax.ShapeDtypeStruct((t, d), MM_DTYPE)],
        compiler_params=_cparams("parallel"),
    )(proj, proj, proj, dw_w, dw_b, ln_w, ln_b)


def _cf_bwd1(c1, proj, ln_w, ln_b, dc3, dproj, d, name):
    t = c1.shape[0]
    r = _row_tile(t)

    def body(c1_ref, zb_ref, lw_ref, lb_ref, dc3_ref, _, dc1_ref, dzb_ref, g_ref):
        @pl.when(pl.program_id(0) == 0)
        def _():
            g_ref[...] = jnp.zeros_like(g_ref)

        xh, rs = _ln_stats(c1_ref[...])
        ln = xh * lw_ref[...] + lb_ref[...]
        sl, zb, dc3v = _sig(ln), zb_ref[...], dc3_ref[...]
        szb = _sig(zb)
        dzb_ref[...] = (dc3v * (ln * sl) * _dsilu(zb, szb)).astype(dzb_ref.dtype)
        dln = dc3v * (zb * szb) * _dsilu(ln, sl)
        dxh = dln * lw_ref[...]
        dc1 = rs * (dxh - jnp.mean(dxh, axis=-1, keepdims=True) - xh * jnp.mean(dxh * xh, axis=-1, keepdims=True))
        dc1_ref[...] = dc1
        g_ref[0:1, :] += jnp.sum(dln * xh, axis=0, keepdims=True)
        g_ref[1:2, :] += jnp.sum(dln, axis=0, keepdims=True)
        g_ref[2:3, :] += jnp.sum(dc1, axis=0, keepdims=True)

    row = pl.BlockSpec((r, d), lambda i: (i, 0))
    zbs = pl.BlockSpec((r, d), lambda i: (i, 8))
    par = pl.BlockSpec((1, d), lambda i: (0, 0))
    return pl.pallas_call(
        body, name=name, grid=(t // r,), in_specs=[row, zbs, par, par, row, ANY],
        out_specs=[row, zbs, pl.BlockSpec((8, d), lambda i: (0, 0))],
        out_shape=[jax.ShapeDtypeStruct((t, d), F32), jax.ShapeDtypeStruct(dproj.shape, dproj.dtype),
                   jax.ShapeDtypeStruct((8, d), F32)],
        input_output_aliases={5: 1}, compiler_params=_cparams("arbitrary"),
    )(c1, proj, ln_w, ln_b, dc3, dproj)


def _cf_bwd2(dc1, proj, dw_w, dproj, d, name):
    t = dc1.shape[0]
    r = CONV_ROWS
    last = t // r - 1

    def body(dc_ref, dcn_ref, g_ref, gh_ref, w_ref, _, dg_ref, gw_ref):
        i = pl.program_id(0)

        @pl.when(i == 0)
        def _():
            gw_ref[...] = jnp.zeros_like(gw_ref)

        w = w_ref[...]
        dcur = dc_ref[...]
        dwin = jnp.concatenate([dcur, jnp.where(i < last, dcn_ref[0:32, :], 0.0)], axis=0)
        win = _glu_window(g_ref, gh_ref, i, d)
        acc = None
        for sub in range(8):
            drot = _shift(dwin, -sub)
            rot = _shift(win, sub)
            for s in range(sub, CF_TAPS, 8):
                j = CF_TAPS - 1 - s
                term = drot[s - sub:s - sub + r, :] * w[j:j + 1, :]
                acc = term if acc is None else acc + term
                gw_ref[j:j + 1, :] += jnp.sum(dcur * rot[32 - (s - sub):32 - (s - sub) + r, :], axis=0, keepdims=True)
        ga, sb = g_ref[:, :d], _sig(g_ref[:, d:])
        dg_ref[:, :d] = (acc * sb).astype(dg_ref.dtype)
        dg_ref[:, d:] = (acc * ga * sb * (1.0 - sb)).astype(dg_ref.dtype)

    row = pl.BlockSpec((r, d), lambda i: (i, 0))
    glu = pl.BlockSpec((r, 2 * d), lambda i: (i, 2))
    return pl.pallas_call(
        body, name=name, grid=(t // r,),
        in_specs=[row, pl.BlockSpec((HALO, d), _halo_next(r, t)), glu,
                  pl.BlockSpec((HALO, 2 * d), lambda i: (_halo_prev(r)(i)[0], 2)), pl.BlockSpec((32, d), lambda i: (0, 0)), ANY],
        out_specs=[glu, pl.BlockSpec((32, d), lambda i: (0, 0))],
        out_shape=[jax.ShapeDtypeStruct(dproj.shape, dproj.dtype), jax.ShapeDtypeStruct((32, d), F32)],
        input_output_aliases={5: 0}, compiler_params=_cparams("arbitrary"),
    )(dc1, dc1, proj, proj, dw_w, dproj)


def _merge(proj, ya, yb, b, d, name):
    t = ya.shape[0]
    r = _row_tile(t)

    def body(g_ref, ya_ref, yb_ref, b_ref, m_ref):
        m_ref[...] = (_sig(g_ref[:, :d]) * ya_ref[...] + _sig(g_ref[:, d:]) * (yb_ref[...] + b_ref[...])).astype(m_ref.dtype)

    row = pl.BlockSpec((r, d), lambda i: (i, 0))
    return pl.pallas_call(
        body, name=name, grid=(t // r,),
        in_specs=[pl.BlockSpec((r, 2 * d), lambda i: (i, 3)), row, row, pl.BlockSpec((1, d), lambda i: (0, 0))],
        out_specs=row, out_shape=jax.ShapeDtypeStruct((t, d), MM_DTYPE), compiler_params=_cparams("parallel"),
    )(proj, ya, yb, b)


def _merge_bwd(proj, ya, yb, b, dm, d, name):
    t = ya.shape[0]
    r = _row_tile(t)

    def body(g_ref, ya_ref, yb_ref, b_ref, dm_ref, dya_ref, dyb_ref, dg_ref, gb_ref):
        @pl.when(pl.program_id(0) == 0)
        def _():
            gb_ref[...] = jnp.zeros_like(gb_ref)

        sa, sb, dmv = _sig(g_ref[:, :d]), _sig(g_ref[:, d:]), dm_ref[...]
        dyb = dmv * sb
        dya_ref[...] = (dmv * sa).astype(dya_ref.dtype)
        dyb_ref[...] = dyb.astype(dyb_ref.dtype)
        dg_ref[:, :d] = (dmv * ya_ref[...] * sa * (1.0 - sa)).astype(dg_ref.dtype)
        dg_ref[:, d:] = (dmv * (yb_ref[...] + b_ref[...]) * sb * (1.0 - sb)).astype(dg_ref.dtype)
        gb_ref[0:1, :] += jnp.sum(dyb, axis=0, keepdims=True)

    row = pl.BlockSpec((r, d), lambda i: (i, 0))
    gate = pl.BlockSpec((r, 2 * d), lambda i: (i, 3))
    return pl.pallas_call(
        body, name=name, grid=(t // r,),
        in_specs=[gate, row, row, pl.BlockSpec((1, d), lambda i: (0, 0)), row],
        out_specs=[row, row, gate, pl.BlockSpec((8, d), lambda i: (0, 0))],
        out_shape=[jax.ShapeDtypeStruct((t, d), MM_DTYPE), jax.ShapeDtypeStruct((t, d), MM_DTYPE),
                   jax.ShapeDtypeStruct((t, 9 * d), MM_DTYPE), jax.ShapeDtypeStruct((8, d), F32)],
        compiler_params=_cparams("arbitrary"),
    )(proj, ya, yb, b, dm)


def _place():
    return lax.axis_index("x"), lax.axis_index("y"), lax.axis_index("c")


def _slot(p):
    return 4 * p[0] + 2 * p[1] + p[2]


def _allgather(blocks, name):
    n = len(blocks)

    def body(*refs):
        ins, outs = refs[:n], refs[n:2 * n]
        send_sems, recv_sems, local_sems = refs[2 * n:]
        x, y, c = _place()
        me, sibling = (x, y, c), (x, y, 1 - c)
        chips = [(1 - x, y), (x, 1 - y), (1 - x, 1 - y)]

        def copy(a, k, block, to, src=None):
            dst = outs[a].at[_slot(block)]
            return pltpu.make_async_remote_copy(
                src_ref=dst if src is None else src, dst_ref=dst, send_sem=send_sems.at[7 * a + k],
                recv_sem=recv_sems.at[7 * a + k], device_id=to, device_id_type=MESH)

        mine = [pltpu.make_async_copy(ins[a], outs[a].at[_slot(me)], local_sems.at[a]) for a in range(n)]
        for cp in mine:
            cp.start()
        first = []
        for a in range(n):
            first.append(copy(a, 0, me, sibling, src=ins[a]))
            first += [copy(a, 1 + j, me, (*chip, c), src=ins[a]) for j, chip in enumerate(chips)]
        for cp in first:
            cp.start()
        passed = []
        for j, chip in enumerate(chips):
            for a in range(n):
                copy(a, 1 + j, (*chip, c), me).wait_recv()
                cp = copy(a, 4 + j, (*chip, c), sibling)
                cp.start()
                passed.append(cp)
        for a in range(n):
            copy(a, 0, sibling, me).wait_recv()
            for j, chip in enumerate(chips):
                copy(a, 4 + j, (*chip, 1 - c), me).wait_recv()
        for cp in first + passed:
            cp.wait_send()
        for cp in mine:
            cp.wait()

    return pl.pallas_call(
        body, name=name, in_specs=[ANY] * n, out_specs=[ANY] * n,
        out_shape=[jax.ShapeDtypeStruct((8, *b.shape), b.dtype) for b in blocks],
        scratch_shapes=[pltpu.SemaphoreType.DMA((7 * n,)), pltpu.SemaphoreType.DMA((7 * n,)), pltpu.SemaphoreType.DMA((n,))],
    )(*blocks)


_FLIPS = [(0, 0, 1), (0, 1, 0), (0, 1, 1), (1, 0, 0), (1, 0, 1), (1, 1, 0), (1, 1, 1)]


def _alltoall(arrays, name):
    n = len(arrays)

    def body(*refs):
        ins, outs = refs[:n], refs[n:2 * n]
        send_sems, recv_sems, local_sems = refs[2 * n:]
        me = _place()
        peers = [tuple(1 - me[ax] if f[ax] else me[ax] for ax in range(3)) for f in _FLIPS]

        def copy(a, k):
            return pltpu.make_async_remote_copy(
                src_ref=ins[a].at[_slot(peers[k])], dst_ref=outs[a].at[_slot(me)], send_sem=send_sems.at[7 * a + k],
                recv_sem=recv_sems.at[7 * a + k], device_id=peers[k], device_id_type=MESH)

        def arrival(a, k):
            return pltpu.make_async_remote_copy(
                src_ref=ins[a].at[_slot(me)], dst_ref=outs[a].at[_slot(peers[k])], send_sem=send_sems.at[7 * a + k],
                recv_sem=recv_sems.at[7 * a + k], device_id=peers[k], device_id_type=MESH)

        mine = [pltpu.make_async_copy(ins[a].at[_slot(me)], outs[a].at[_slot(me)], local_sems.at[a]) for a in range(n)]
        for cp in mine:
            cp.start()
        sent = [copy(a, k) for a in range(n) for k in range(7)]
        for cp in sent:
            cp.start()
        for a in range(n):
            for k in range(7):
                arrival(a, k).wait_recv()
        for cp in sent:
            cp.wait_send()
        for cp in mine:
            cp.wait()

    return pl.pallas_call(
        body, name=name, in_specs=[ANY] * n, out_specs=[ANY] * n,
        out_shape=[jax.ShapeDtypeStruct(a.shape, a.dtype) for a in arrays],
        scratch_shapes=[pltpu.SemaphoreType.DMA((7 * n,)), pltpu.SemaphoreType.DMA((7 * n,)), pltpu.SemaphoreType.DMA((n,))],
    )(*arrays)


def _sibling_exchange(arrays, name):
    n = len(arrays)

    def body(*refs):
        ins, outs = refs[:n], refs[n:2 * n]
        send_sems, recv_sems, local_sems = refs[2 * n:]
        x, y, c = _place()

        def copy(a, slot):
            return pltpu.make_async_remote_copy(
                src_ref=ins[a], dst_ref=outs[a].at[slot], send_sem=send_sems.at[a], recv_sem=recv_sems.at[a],
                device_id=(x, y, 1 - c), device_id_type=MESH)

        mine = [pltpu.make_async_copy(ins[a], outs[a].at[c], local_sems.at[a]) for a in range(n)]
        for cp in mine:
            cp.start()
        sent = [copy(a, c) for a in range(n)]
        for cp in sent:
            cp.start()
        for a in range(n):
            copy(a, 1 - c).wait_recv()
        for cp in sent:
            cp.wait_send()
        for cp in mine:
            cp.wait()

    return pl.pallas_call(
        body, name=name, in_specs=[ANY] * n, out_specs=[ANY] * n,
        out_shape=[jax.ShapeDtypeStruct((2, *a.shape), a.dtype) for a in arrays],
        scratch_shapes=[pltpu.SemaphoreType.DMA((n,)), pltpu.SemaphoreType.DMA((n,)), pltpu.SemaphoreType.DMA((n,))],
    )(*arrays)


def _sum8(parts, name):
    _, rr, cc = parts.shape
    r = _pick(rr, (128, 64, 32, 16, 8))

    def body(p_ref, o_ref):
        acc = p_ref[0].astype(F32)
        for j in range(1, 8):
            acc = acc + p_ref[j].astype(F32)
        o_ref[...] = acc

    return pl.pallas_call(
        body, name=name, grid=(rr // r,), in_specs=[pl.BlockSpec((8, r, cc), lambda i: (0, i, 0))],
        out_specs=pl.BlockSpec((r, cc), lambda i: (i, 0)), out_shape=jax.ShapeDtypeStruct((rr, cc), F32),
        compiler_params=_cparams("parallel"),
    )(parts)


def _adamw(g, w, m, v, name):
    rr, cc = g.shape
    r = _pick(rr, (128, 64, 32, 16, 8))

    def body(g_ref, w_ref, m_ref, v_ref, d_ref, nm_ref, nv_ref):
        gv = g_ref[...]
        nm = ADAM_B1 * m_ref[...] + (1.0 - ADAM_B1) * gv
        nv = ADAM_B2 * v_ref[...] + (1.0 - ADAM_B2) * (gv * gv)
        m_hat = nm / (1.0 - ADAM_B1 ** ADAM_STEP)
        v_hat = nv / (1.0 - ADAM_B2 ** ADAM_STEP)
        d_ref[...] = -ADAM_LR * (m_hat / (jnp.sqrt(v_hat) + ADAM_EPS) + ADAM_WD * w_ref[...])
        nm_ref[...] = nm
        nv_ref[...] = nv

    blk = pl.BlockSpec((r, cc), lambda i: (i, 0))
    return pl.pallas_call(
        body, name=name, grid=(rr // r,), in_specs=[blk] * 4, out_specs=[blk] * 3,
        out_shape=[jax.ShapeDtypeStruct((rr, cc), F32)] * 3, compiler_params=_cparams("parallel"),
    )(g, w, m, v)


def _flat_pack(parts, width):
    flat = jnp.concatenate([p.reshape(-1) for p in parts])
    total = -(-flat.shape[0] // (8 * width)) * (8 * width)
    return jnp.pad(flat, (0, total - flat.shape[0])).reshape(-1, width)


def _flat_unpack(pack, shapes):
    flat = pack.reshape(-1)
    out, at = [], 0
    for s in shapes:
        size = 1
        for e in s:
            size *= e
        out.append(flat[at:at + size].reshape(s))
        at += size
    return out


def kernel(x, meta, norm_w, w_in, conv_qkv_w, a_log, dt_bias, dn_norm_w, w_dn_out, dw_w, dw_b, ln_w, ln_b, w_cf_out, b_cf_out, w_o, final_norm_w, loss_target, m_meta, m_norm_w, m_w_in, m_conv_qkv_w, m_a_log, m_dt_bias, m_dn_norm_w, m_w_dn_out, m_dw_w, m_dw_b, m_ln_w, m_ln_b, m_w_cf_out, m_b_cf_out, m_w_o, m_final_norm_w, v_meta, v_norm_w, v_w_in, v_conv_qkv_w, v_a_log, v_dt_bias, v_dn_norm_w, v_w_dn_out, v_dw_w, v_dw_b, v_ln_w, v_ln_b, v_w_cf_out, v_b_cf_out, v_w_o, v_final_norm_w):
    d = x.shape[-1]
    heads = a_log.shape[-1]
    seq = x.shape[1]
    t = HDR + seq
    nc = t // CHUNK
    in_w = 9 * d + 2 * heads
    assert heads * DK == d and seq % CONV_ROWS == 0 and w_in.shape[-1] * 4 == in_w
    xi, yi, ci = _place()
    shard = 2 * xi + yi
    ds = d // 4

    w_in_half = lax.dynamic_slice_in_dim(w_in[0].astype(MM_DTYPE), ci * (d // 2), d // 2, axis=0)
    w3_half = lax.dynamic_slice_in_dim(jnp.stack([w_dn_out[0], w_cf_out[0], w_o[0]]).astype(MM_DTYPE), ci * (d // 8), d // 8, axis=1)
    small = jnp.concatenate([
        jnp.pad(conv_qkv_w[0], ((0, 4), (0, 0))), jnp.pad(meta, ((0, 0), (0, 2 * ds))),
        jnp.pad(dw_w[0], ((0, 1), (0, 2 * ds)))], axis=0)
    g_in, g_w3, g_small = _allgather([w_in_half, w3_half, small], "gather_weights")
    w_full = g_in.reshape(4, d, in_w // 4).transpose(1, 0, 2).reshape(d, in_w)
    o_glu = 4 * d + 2 * heads
    w_main = jnp.concatenate([w_full[:, :4 * d], w_full[:, o_glu:o_glu + 2 * d], w_full[:, o_glu + 3 * d:o_glu + 5 * d],
                              w_full[:, o_glu + 2 * d:o_glu + 3 * d]], axis=1)
    w_ba = jnp.pad(w_full[:, 4 * d:o_glu], ((0, 0), (0, LANES - 2 * heads)))
    w3_full = g_w3.transpose(1, 0, 2, 3).reshape(3, d, d)
    w_dn_f, w_cf_f, w_o_f = w3_full[0], w3_full[1], w3_full[2]
    small4 = g_small[0::2]
    conv_w_f = small4[:, 0:DN_TAPS, :].transpose(1, 0, 2).reshape(DN_TAPS, 3 * d)
    meta_f = small4[:, 8:8 + N_META, :ds].transpose(1, 0, 2).reshape(N_META, d)
    dw_w_f = small4[:, 24:56, :ds].transpose(1, 0, 2).reshape(32, d)

    xs = jnp.concatenate([jnp.pad(meta_f, ((PAD_ROWS, 0), (0, 0))), x[0]], axis=0)
    tgt = jnp.pad(loss_target[0], ((HDR, 0), (0, 0)))
    h = _rms_fwd(xs, norm_w, "rms_fwd")
    proj = _mm(h, w_main, name="proj_main")
    pba = _mm(h, w_ba, name="proj_ba")
    qkvn = _dn_prep(proj, conv_w_f, d, heads, "dn_prep")
    pvec = jnp.zeros((8, LANES), F32).at[0, heads:2 * heads].set(a_log[0]).at[1, heads:2 * heads].set(dt_bias[0])
    bg = _ba_fwd(pba, pvec, heads, "ba_fwd")
    g_row = bg[:, heads:2 * heads].reshape(nc, CHUNK, heads).transpose(0, 2, 1)
    o, sall, ainv = _dn_fwd(qkvn, bg, g_row, d, heads, "dn_fwd")
    ya_in = _dn_post(o, proj, dn_norm_w, d, heads, "dn_post")
    ya = _mm(ya_in, w_dn_f, name="ya")
    c1, c3 = _cf_fwd(proj, dw_w_f, dw_b, ln_w, ln_b, d, "cf_fwd")
    yb = _mm(c3, w_cf_f, name="yb")
    merged = _merge(proj, ya, yb, b_cf_out, d, "merge")
    mo = _mm(merged, w_o_f, name="mo")
    dxo, dxo_mm, loss_acc, g_fnw = _final(xs, mo, final_norm_w.reshape(1, d), tgt, "final")

    dmerged = _mm(dxo_mm, w_o_f, nt=True, name="d_merged")
    gw_o = _mm(merged.T, dxo_mm, out_dtype=MM_DTYPE, name="gw_o")
    dya, dyb, dproj, g_bcf = _merge_bwd(proj, ya, yb, b_cf_out, dmerged, d, "merge_bwd")
    dc3 = _mm(dyb, w_cf_f, nt=True, name="d_c3")
    gw_cf = _mm(c3.T, dyb, out_dtype=MM_DTYPE, name="gw_cf")
    dyain = _mm(dya, w_dn_f, nt=True, name="d_ya_in")
    gw_dn = _mm(ya_in.T, dya, out_dtype=MM_DTYPE, name="gw_dn")
    dc1, dproj, g_ln = _cf_bwd1(c1, proj, ln_w, ln_b, dc3, dproj, d, "cf_bwd1")
    dproj, g_dww = _cf_bwd2(dc1, proj, dw_w_f, dproj, d, "cf_bwd2")
    do, dproj, g_dnw = _dn_post_bwd(o, proj, dn_norm_w, dyain, dproj, d, heads, "dn_post_bwd")
    dqkvn, dbg, dg_row = _dn_bwd(qkvn, bg, g_row, do, sall, ainv, d, heads, "dn_bwd")
    dconv = _dn_prep_bwd1(proj, conv_w_f, dqkvn, d, heads, "dn_prep_bwd1")
    dproj, g_convw = _dn_prep_bwd2(dconv, proj, conv_w_f, dproj, d, "dn_prep_bwd2")
    dbg = dbg + jnp.pad(dg_row.transpose(0, 2, 1).reshape(t, heads), ((0, 0), (heads, LANES - 2 * heads)))
    dpba, g_ba = _ba_bwd(pba, pvec, dbg, heads, "ba_bwd")
    dh1 = _mm(dproj, w_main, nt=True, name="d_h_main")
    dh2 = _mm(dpba, w_ba, nt=True, name="d_h_ba")
    h_t = h.T
    gw_main = _mm(h_t, dproj, out_dtype=MM_DTYPE, name="gw_main")
    gw_ba = _mm(h_t, dpba, out_dtype=MM_DTYPE, name="gw_ba")
    dxs, g_nw = _rms_bwd(xs, norm_w, dh1, dh2, dxo, "rms_bwd")
    grad_x = dxs[HDR:][None]

    gw_in = jnp.concatenate([gw_main[:, :4 * d], gw_ba[:, :2 * heads], gw_main[:, 4 * d:6 * d], gw_main[:, 8 * d:],
                             gw_main[:, 6 * d:8 * d]], axis=1)
    send_in = gw_in.reshape(d, 4, in_w // 4).transpose(1, 0, 2).reshape(8, d // 2, in_w // 4)
    send_w3 = jnp.stack([gw_dn, gw_cf, gw_o]).reshape(3, 8, d // 8, d).transpose(1, 0, 2, 3).reshape(8, 3 * d // 8, d)
    misc = jnp.concatenate([g_ba[0, heads:2 * heads], g_ba[1, heads:2 * heads], g_dnw[0], loss_acc[0, 0:1]])
    small_g = jnp.concatenate([
        g_convw[0:DN_TAPS].reshape(DN_TAPS * 3, d), dxs[PAD_ROWS:HDR], g_dww, g_nw[0:1], g_ln[2:3], g_ln[0:1], g_ln[1:2],
        g_bcf[0:1], g_fnw[0:1], jnp.pad(misc, (0, d - misc.shape[0]))[None], jnp.zeros((5, d), F32)], axis=0)
    got_in, got_w3 = _alltoall([send_in, send_w3], "exchange_grads")
    red_in = _sum8(got_in, "sum_w_in")
    red_w3 = _sum8(got_w3, "sum_w3")
    pair_in, pair_w3 = _sibling_exchange([red_in, red_w3], "pair_grads")
    g_w_in = pair_in.reshape(d, in_w // 4)
    g_w3 = pair_w3.reshape(2, 3, d // 8, d).transpose(1, 0, 2, 3).reshape(3 * ds, d)
    all_small, = _allgather([small_g], "gather_small_grads")
    sg = _sum8(all_small, "sum_small")
    g_conv = lax.dynamic_slice_in_dim(sg[0:12].reshape(DN_TAPS, 3 * d), shard * 3 * ds, 3 * ds, axis=1)
    g_meta = lax.dynamic_slice_in_dim(sg[12:28], shard * ds, ds, axis=1)
    g_dw_w = lax.dynamic_slice_in_dim(sg[28:28 + CF_TAPS], shard * ds, ds, axis=1)
    r0 = 28 + 32
    g_rep = {"norm_w": sg[r0:r0 + 1], "dw_b": sg[r0 + 1:r0 + 2], "ln_w": sg[r0 + 2:r0 + 3], "ln_b": sg[r0 + 3:r0 + 4],
             "b_cf_out": sg[r0 + 4:r0 + 5], "final_norm_w": sg[r0 + 5], "a_log": sg[r0 + 6:r0 + 7, 0:heads],
             "dt_bias": sg[r0 + 6:r0 + 7, heads:2 * heads], "dn_norm_w": sg[r0 + 6:r0 + 7, 2 * heads:2 * heads + DK]}
    loss = sg[r0 + 6, 2 * heads + DK]

    res = {}
    dl, nm, nv = _adamw(g_w_in, w_in[0], m_w_in[0], v_w_in[0], "adamw_w_in")
    res["w_in"] = (g_w_in[None], dl[None], nm[None], nv[None])
    stack3 = lambda a, b, c: jnp.concatenate([a[0], b[0], c[0]], axis=0)
    dl, nm, nv = _adamw(g_w3, stack3(w_dn_out, w_cf_out, w_o), stack3(m_w_dn_out, m_w_cf_out, m_w_o),
                        stack3(v_w_dn_out, v_w_cf_out, v_w_o), "adamw_w3")
    for j, nme in enumerate(("w_dn_out", "w_cf_out", "w_o")):
        res[nme] = tuple(a[j * ds:(j + 1) * ds][None] for a in (g_w3, dl, nm, nv))
    names = ["meta", "conv_qkv_w", "dw_w", "norm_w", "dw_b", "ln_w", "ln_b", "b_cf_out", "final_norm_w", "a_log", "dt_bias",
             "dn_norm_w"]
    grads = {"meta": g_meta, "conv_qkv_w": g_conv[None], "dw_w": g_dw_w[None], **g_rep}
    given = dict(meta=(meta, m_meta, v_meta), conv_qkv_w=(conv_qkv_w, m_conv_qkv_w, v_conv_qkv_w), dw_w=(dw_w, m_dw_w, v_dw_w),
                 norm_w=(norm_w, m_norm_w, v_norm_w), dw_b=(dw_b, m_dw_b, v_dw_b), ln_w=(ln_w, m_ln_w, v_ln_w),
                 ln_b=(ln_b, m_ln_b, v_ln_b), b_cf_out=(b_cf_out, m_b_cf_out, v_b_cf_out),
                 final_norm_w=(final_norm_w, m_final_norm_w, v_final_norm_w), a_log=(a_log, m_a_log, v_a_log),
                 dt_bias=(dt_bias, m_dt_bias, v_dt_bias), dn_norm_w=(dn_norm_w, m_dn_norm_w, v_dn_norm_w))
    shapes = [given[nme][0].shape for nme in names]
    packs = [_flat_pack([grads[nme] for nme in names], LANES)] + [_flat_pack([given[nme][j] for nme in names], LANES) for j in range(3)]
    outs = _adamw(*packs, "adamw_small")
    unpacked = [_flat_unpack(p, shapes) for p in (packs[0], *outs)]
    for j, nme in enumerate(names):
        res[nme] = tuple(u[j] for u in unpacked)

    order = ["meta", "norm_w", "w_in", "conv_qkv_w", "a_log", "dt_bias", "dn_norm_w", "w_dn_out", "dw_w", "dw_b", "ln_w", "ln_b",
             "w_cf_out", "b_cf_out", "w_o", "final_norm_w"]
    return (loss, grad_x, *[res[nme][0] for nme in order], *[res[nme][1] for nme in order],
            *[res[nme][2] for nme in order], *[res[nme][3] for nme in order])
```

```python
import jax
import jax.numpy as jnp
from jax import lax
from jax.experimental import pallas as pl
from jax.experimental.pallas import tpu as pltpu

F32 = jnp.float32
MM_DTYPE = jnp.bfloat16
EPS = 1e-6
N_META = 16
HDR = 128
PAD_ROWS = HDR - N_META
CHUNK = 64
DK = 128
CF_TAPS = 31
DN_TAPS = 4
HALO = 64
LANES = 128
ADAM_LR, ADAM_B1, ADAM_B2, ADAM_EPS, ADAM_WD, ADAM_STEP = 0.001, 0.9, 0.999, 1e-08, 0.01, 10
VMEM_LIMIT = 48 * 1024 * 1024
MESH = pl.DeviceIdType.MESH
HI = lax.Precision.HIGHEST
LO = lax.Precision.DEFAULT
NN = (((1,), (0,)), ((), ()))
NT = (((1,), (1,)), ((), ()))
TN = (((0,), (0,)), ((), ()))
ANY = pl.BlockSpec(memory_space=pl.ANY)


def _dot(a, b, dims=NN, prec=LO):
    if prec == LO:
        a, b = a.astype(MM_DTYPE), b.astype(MM_DTYPE)
    return lax.dot_general(a, b, dims, precision=prec, preferred_element_type=F32)


def _split(a):
    hi = a.astype(MM_DTYPE)
    return hi, (a - hi.astype(F32)).astype(MM_DTYPE)


def _dot3(a, b, dims=NN):
    ah, al = _split(a)
    bh, bl = _split(b)
    return _dot(ah, bh, dims) + (_dot(ah, bl, dims) + _dot(al, bh, dims))


def _pick(n, options):
    for o in options:
        if n % o == 0:
            return o
    return n


def _cparams(*sem):
    return pltpu.CompilerParams(dimension_semantics=sem, vmem_limit_bytes=VMEM_LIMIT)


def _sig(x):
    return jax.nn.sigmoid(x)


def _sum_all(a):
    return jnp.sum(jnp.sum(a, axis=-1, keepdims=True), axis=0, keepdims=True)


def _dsilu(x, s):
    return s + x * s * (1.0 - s)


def _shift(win, s):
    n = win.shape[0]
    s = s % n
    return win if s == 0 else pltpu.roll(win, s, 0)


def _row_tile(t):
    return _pick(t, (320, 128))


def _mm(a, b, *, nt=False, out_dtype=F32, name):
    m, k = a.shape
    n = b.shape[0] if nt else b.shape[1]
    tm = _pick(m, (1024, 640, 512, 384, 256, 128))
    tn = _pick(n, (1024, 768, 512, 256, 128))
    tk = _pick(k, (1664, 1024, 640, 512, 256, 128))
    nk = k // tk
    dims = NT if nt else NN

    def body(a_ref, b_ref, o_ref, *acc):
        p = lax.dot_general(a_ref[...], b_ref[...], dims, preferred_element_type=F32)
        if nk == 1:
            o_ref[...] = p.astype(o_ref.dtype)
            return
        acc_ref, = acc
        kk = pl.program_id(2)

        @pl.when(kk == 0)
        def _():
            acc_ref[...] = p

        @pl.when(kk > 0)
        def _():
            acc_ref[...] += p

        @pl.when(kk == nk - 1)
        def _():
            o_ref[...] = acc_ref[...].astype(o_ref.dtype)

    b_spec = pl.BlockSpec((tn, tk), lambda i, j, kk: (j, kk)) if nt else pl.BlockSpec((tk, tn), lambda i, j, kk: (kk, j))
    return pl.pallas_call(
        body, name=name, grid=(m // tm, n // tn, nk),
        in_specs=[pl.BlockSpec((tm, tk), lambda i, j, kk: (i, kk)), b_spec],
        out_specs=pl.BlockSpec((tm, tn), lambda i, j, kk: (i, j)),
        out_shape=jax.ShapeDtypeStruct((m, n), out_dtype),
        scratch_shapes=[] if nk == 1 else [pltpu.VMEM((tm, tn), F32)],
        compiler_params=_cparams("parallel", "parallel", "arbitrary"),
    )(a, b)


def _rms_fwd(xs, w, name):
    t, d = xs.shape
    r = _row_tile(t)

    def body(x_ref, w_ref, h_ref):
        x = x_ref[...]
        rs = lax.rsqrt(jnp.mean(x * x, axis=-1, keepdims=True) + EPS)
        h_ref[...] = (x * rs * w_ref[...]).astype(h_ref.dtype)

    return pl.pallas_call(
        body, name=name, grid=(t // r,),
        in_specs=[pl.BlockSpec((r, d), lambda i: (i, 0)), pl.BlockSpec((1, d), lambda i: (0, 0))],
        out_specs=pl.BlockSpec((r, d), lambda i: (i, 0)),
        out_shape=jax.ShapeDtypeStruct((t, d), MM_DTYPE), compiler_params=_cparams("parallel"),
    )(xs, w)


def _rms_bwd(xs, w, dh1, dh2, dres, name):
    t, d = xs.shape
    r = _row_tile(t)

    def body(x_ref, w_ref, d1_ref, d2_ref, dr_ref, dx_ref, gw_ref):
        @pl.when(pl.program_id(0) == 0)
        def _():
            gw_ref[...] = jnp.zeros_like(gw_ref)

        x = x_ref[...]
        dh = d1_ref[...] + d2_ref[...]
        rs = lax.rsqrt(jnp.mean(x * x, axis=-1, keepdims=True) + EPS)
        xh = x * rs
        dxh = dh * w_ref[...]
        dx_ref[...] = rs * (dxh - xh * jnp.mean(dxh * xh, axis=-1, keepdims=True)) + dr_ref[...]
        gw_ref[0:1, :] += jnp.sum(dh * xh, axis=0, keepdims=True)

    row = pl.BlockSpec((r, d), lambda i: (i, 0))
    return pl.pallas_call(
        body, name=name, grid=(t // r,),
        in_specs=[row, pl.BlockSpec((1, d), lambda i: (0, 0)), row, row, row],
        out_specs=[row, pl.BlockSpec((8, d), lambda i: (0, 0))],
        out_shape=[jax.ShapeDtypeStruct((t, d), F32), jax.ShapeDtypeStruct((8, d), F32)],
        compiler_params=_cparams("arbitrary"),
    )(xs, w, dh1, dh2, dres)


def _final(xs, mo, w, tgt, name):
    t, d = xs.shape
    r = _row_tile(t)

    def body(x_ref, m_ref, w_ref, t_ref, dx_ref, dxm_ref, loss_ref, gw_ref):
        i = pl.program_id(0)

        @pl.when(i == 0)
        def _():
            loss_ref[...] = jnp.zeros_like(loss_ref)
            gw_ref[...] = jnp.zeros_like(gw_ref)

        xo = x_ref[...] + m_ref[...]
        rs = lax.rsqrt(jnp.mean(xo * xo, axis=-1, keepdims=True) + EPS)
        xh = xo * rs
        y = xh * w_ref[...]
        rows = i * r + lax.broadcasted_iota(jnp.int32, (r, 1), 0)
        err = jnp.where(rows >= HDR, y - t_ref[...], 0.0)
        loss_ref[...] += 0.5 * _sum_all(err * err) / d
        dy = err / d
        gw_ref[0:1, :] += jnp.sum(dy * xh, axis=0, keepdims=True)
        dxh = dy * w_ref[...]
        dx = rs * (dxh - xh * jnp.mean(dxh * xh, axis=-1, keepdims=True))
        dx_ref[...] = dx
        dxm_ref[...] = dx.astype(dxm_ref.dtype)

    row = pl.BlockSpec((r, d), lambda i: (i, 0))
    return pl.pallas_call(
        body, name=name, grid=(t // r,),
        in_specs=[row, row, pl.BlockSpec((1, d), lambda i: (0, 0)), row],
        out_specs=[row, row, pl.BlockSpec((8, LANES), lambda i: (0, 0)), pl.BlockSpec((8, d), lambda i: (0, 0))],
        out_shape=[jax.ShapeDtypeStruct((t, d), F32), jax.ShapeDtypeStruct((t, d), MM_DTYPE),
                   jax.ShapeDtypeStruct((8, LANES), F32), jax.ShapeDtypeStruct((8, d), F32)],
        compiler_params=_cparams("arbitrary"),
    )(xs, mo, w, tgt)


def _halo_prev(r):
    return lambda i: (jnp.maximum(i * (r // HALO) - 1, 0), 0)


def _halo_next(r, t):
    return lambda i: (jnp.minimum((i + 1) * (r // HALO), t // HALO - 1), 0)


def _conv_rows(t):
    return _pick(t, (320, 128))


def _qkv_window(p_ref, ph_ref, i, cs):
    return jnp.concatenate([jnp.where(i > 0, ph_ref[HALO - 8:HALO, cs], 0.0), p_ref[:, cs]], axis=0)


def _qkv_conv(win, cw, r):
    y = None
    for s in range(DN_TAPS):
        term = _shift(win, s)[8:8 + r, :] * cw[DN_TAPS - 1 - s:DN_TAPS - s, :]
        y = term if y is None else y + term
    return y


def _dn_prep(proj, conv_w, d, heads, name):
    t = proj.shape[0]
    r = _conv_rows(t)
    w3 = 3 * d

    def body(p_ref, ph_ref, cw_ref, o_ref):
        i = pl.program_id(0)
        for cb in range(w3 // DK):
            cs = slice(cb * DK, (cb + 1) * DK)
            y = _qkv_conv(_qkv_window(p_ref, ph_ref, i, cs), cw_ref[:, cs], r)
            a = y * _sig(y)
            if cb < 2 * heads:
                sc = DK ** -0.5 if cb < heads else 1.0
                a = a * (lax.rsqrt(jnp.sum(a * a, axis=-1, keepdims=True) + EPS) * sc)
            o_ref[:, cs] = a

    return pl.pallas_call(
        body, name=name, grid=(t // r,),
        in_specs=[pl.BlockSpec((r, w3), lambda i: (i, 0)), pl.BlockSpec((HALO, w3), _halo_prev(r)),
                  pl.BlockSpec((DN_TAPS, w3), lambda i: (0, 0))],
        out_specs=pl.BlockSpec((r, w3), lambda i: (i, 0)),
        out_shape=jax.ShapeDtypeStruct((t, w3), F32), compiler_params=_cparams("parallel"),
    )(proj, proj, conv_w)


def _dn_prep_bwd1(proj, conv_w, dqkvn, d, heads, name):
    t = proj.shape[0]
    r = _conv_rows(t)
    w3 = 3 * d

    def body(p_ref, ph_ref, cw_ref, dn_ref, o_ref):
        i = pl.program_id(0)
        for cb in range(w3 // DK):
            cs = slice(cb * DK, (cb + 1) * DK)
            y = _qkv_conv(_qkv_window(p_ref, ph_ref, i, cs), cw_ref[:, cs], r)
            sy = _sig(y)
            da = dn_ref[:, cs]
            if cb < 2 * heads:
                a = y * sy
                rn = lax.rsqrt(jnp.sum(a * a, axis=-1, keepdims=True) + EPS)
                n0 = a * rn
                dn0 = da * (DK ** -0.5 if cb < heads else 1.0)
                da = rn * (dn0 - n0 * jnp.sum(dn0 * n0, axis=-1, keepdims=True))
            o_ref[:, cs] = da * _dsilu(y, sy)

    row = pl.BlockSpec((r, w3), lambda i: (i, 0))
    return pl.pallas_call(
        body, name=name, grid=(t // r,),
        in_specs=[row, pl.BlockSpec((HALO, w3), _halo_prev(r)), pl.BlockSpec((DN_TAPS, w3), lambda i: (0, 0)), row],
        out_specs=row, out_shape=jax.ShapeDtypeStruct((t, w3), F32), compiler_params=_cparams("parallel"),
    )(proj, proj, conv_w, dqkvn)


def _dn_prep_bwd2(dconv, proj, conv_w, dproj, d, name):
    t = proj.shape[0]
    r = _conv_rows(t)
    w3 = 3 * d
    last = t // r - 1

    def body(dc_ref, dcn_ref, p_ref, ph_ref, cw_ref, _, dp_ref, gw_ref):
        i = pl.program_id(0)

        @pl.when(i == 0)
        def _():
            gw_ref[...] = jnp.zeros_like(gw_ref)

        for cb in range(w3 // DK):
            cs = slice(cb * DK, (cb + 1) * DK)
            cw = cw_ref[:, cs]
            dcur = dc_ref[:, cs]
            dwin = jnp.concatenate([dcur, jnp.where(i < last, dcn_ref[0:8, cs], 0.0)], axis=0)
            win = _qkv_window(p_ref, ph_ref, i, cs)
            acc = None
            for j in range(DN_TAPS):
                s = DN_TAPS - 1 - j
                term = _shift(dwin, -s)[0:r, :] * cw[j:j + 1, :]
                acc = term if acc is None else acc + term
                gw_ref[j:j + 1, cs] += jnp.sum(dcur * _shift(win, s)[8:8 + r, :], axis=0, keepdims=True)
            dp_ref[:, cs] = acc.astype(dp_ref.dtype)

    row = pl.BlockSpec((r, w3), lambda i: (i, 0))
    return pl.pallas_call(
        body, name=name, grid=(t // r,),
        in_specs=[row, pl.BlockSpec((HALO, w3), _halo_next(r, t)), row, pl.BlockSpec((HALO, w3), _halo_prev(r)),
                  pl.BlockSpec((DN_TAPS, w3), lambda i: (0, 0)), ANY],
        out_specs=[row, pl.BlockSpec((8, w3), lambda i: (0, 0))],
        out_shape=[jax.ShapeDtypeStruct(dproj.shape, dproj.dtype), jax.ShapeDtypeStruct((8, w3), F32)],
        input_output_aliases={5: 0}, compiler_params=_cparams("arbitrary"),
    )(dconv, dconv, proj, proj, conv_w, dproj)


def _ba_terms(x, pv, heads):
    lane = lax.broadcasted_iota(jnp.int32, x.shape, 1)
    is_b = lane < heads
    is_a = jnp.logical_and(lane >= heads, lane < 2 * heads)
    beta = _sig(x)
    z = x + pv[1:2, :]
    nexp = -jnp.exp(pv[0:1, :])
    sp = jnp.maximum(z, 0.0) + jnp.log1p(jnp.exp(-jnp.abs(z)))
    return is_b, is_a, beta, z, nexp, nexp * sp


def _ba_fwd(pba, pvec, heads, name):
    t = pba.shape[0]
    r = _row_tile(t)

    def body(x_ref, pv_ref, o_ref):
        is_b, is_a, beta, _, _, g = _ba_terms(x_ref[...], pv_ref[...], heads)
        rows = pl.program_id(0) * r + lax.broadcasted_iota(jnp.int32, (r, 1), 0)
        o_ref[...] = jnp.where(rows >= PAD_ROWS, jnp.where(is_b, beta, jnp.where(is_a, g, 0.0)), 0.0)

    row = pl.BlockSpec((r, LANES), lambda i: (i, 0))
    return pl.pallas_call(
        body, name=name, grid=(t // r,), in_specs=[row, pl.BlockSpec((8, LANES), lambda i: (0, 0))],
        out_specs=row, out_shape=jax.ShapeDtypeStruct((t, LANES), F32), compiler_params=_cparams("parallel"),
    )(pba, pvec)


def _ba_bwd(pba, pvec, dbg, heads, name):
    t = pba.shape[0]
    r = _row_tile(t)

    def body(x_ref, pv_ref, d_ref, o_ref, g_ref):
        @pl.when(pl.program_id(0) == 0)
        def _():
            g_ref[...] = jnp.zeros_like(g_ref)

        is_b, is_a, beta, z, nexp, g = _ba_terms(x_ref[...], pv_ref[...], heads)
        rows = pl.program_id(0) * r + lax.broadcasted_iota(jnp.int32, (r, 1), 0)
        dd = jnp.where(rows >= PAD_ROWS, d_ref[...], 0.0)
        dz = dd * nexp * _sig(z)
        o_ref[...] = jnp.where(is_b, dd * beta * (1.0 - beta), jnp.where(is_a, dz, 0.0)).astype(o_ref.dtype)
        g_ref[0:1, :] += jnp.sum(jnp.where(is_a, dd * g, 0.0), axis=0, keepdims=True)
        g_ref[1:2, :] += jnp.sum(jnp.where(is_a, dz, 0.0), axis=0, keepdims=True)

    row = pl.BlockSpec((r, LANES), lambda i: (i, 0))
    par = pl.BlockSpec((8, LANES), lambda i: (0, 0))
    return pl.pallas_call(
        body, name=name, grid=(t // r,), in_specs=[row, par, row], out_specs=[row, par],
        out_shape=[jax.ShapeDtypeStruct((t, LANES), MM_DTYPE), jax.ShapeDtypeStruct((8, LANES), F32)],
        compiler_params=_cparams("arbitrary"),
    )(pba, pvec, dbg)


def _chunk_consts():
    ri = lax.broadcasted_iota(jnp.int32, (CHUNK, CHUNK), 0)
    ci = lax.broadcasted_iota(jnp.int32, (CHUNK, CHUNK), 1)
    return ri >= ci, ri > ci, (ri == ci).astype(F32), (ri >= ci).astype(F32), (ri <= ci).astype(F32)


def _chunk_decay(gc, gr, incl):
    return jnp.where(incl, jnp.exp(jnp.where(incl, gc - gr, 0.0)), 0.0)


def _unit_lower_inverse(n, eye):
    x = [eye + a for a in n]
    p = list(n)
    for _ in range(5):
        p = [_dot(a, a) for a in p]
        x = [a + _dot(a, b) for a, b in zip(x, p)]
    r = [eye - a + _dot3(b, a) for a, b in zip(x, n)]
    return [a + _dot(a, b) for a, b in zip(x, r)]


def _dn_fwd(qkvn, bg, g_row, d, heads, name):
    t = qkvn.shape[0]
    nc = t // CHUNK

    def body(qkv_ref, bg_ref, gr_ref, o_ref, sall_ref, ainv_ref, s_ref):
        @pl.when(pl.program_id(0) == 0)
        def _():
            s_ref[...] = jnp.zeros_like(s_ref)

        incl, strict, eye, tril, triu = _chunk_consts()
        bgv = bg_ref[...]
        gc_all = _dot(tril, bgv, NN, HI)
        gr_all = _dot(gr_ref[0], triu, NN, HI)
        hs = range(heads)
        q = [qkv_ref[:, h * DK:(h + 1) * DK] for h in hs]
        k = [qkv_ref[:, d + h * DK:d + (h + 1) * DK] for h in hs]
        v = [qkv_ref[:, 2 * d + h * DK:2 * d + (h + 1) * DK] for h in hs]
        kb = [a.astype(MM_DTYPE) for a in k]
        beta = [bgv[:, h:h + 1] for h in hs]
        gc = [gc_all[:, heads + h:heads + h + 1] for h in hs]
        decay = [_chunk_decay(gc[h], gr_all[h:h + 1, :], incl) for h in hs]
        eg = [jnp.exp(a) for a in gc]
        n = [jnp.where(strict, -(beta[h] * _dot(kb[h], kb[h], NT) * decay[h]), 0.0) for h in hs]
        x = _unit_lower_inverse(n, eye)
        sol = [_dot3(x[h], jnp.concatenate([v[h] * beta[h], k[h] * (beta[h] * eg[h])], axis=1)) for h in hs]
        attn = [_dot(q[h], kb[h], NT) * decay[h] for h in hs]
        s = [s_ref[h] for h in hs]
        sb = [a.astype(MM_DTYPE) for a in s]
        wv = [sol[h][:, :DK] - _dot(sol[h][:, DK:], sb[h]) for h in hs]
        o = [_dot(q[h] * eg[h], sb[h]) + _dot(attn[h], wv[h]) for h in hs]
        glast = [a[CHUNK - 1:CHUNK, :] for a in gc]
        s_new = [s[h] * jnp.exp(glast[h]) + _dot(k[h] * jnp.exp(glast[h] - gc[h]), wv[h], TN) for h in hs]
        for h in hs:
            o_ref[:, h * DK:(h + 1) * DK] = o[h]
            sall_ref[0, h] = s[h]
            ainv_ref[0, h] = x[h]
            s_ref[h] = s_new[h]

    return pl.pallas_call(
        body, name=name, grid=(nc,),
        in_specs=[pl.BlockSpec((CHUNK, 3 * d), lambda i: (i, 0)), pl.BlockSpec((CHUNK, LANES), lambda i: (i, 0)),
                  pl.BlockSpec((1, heads, CHUNK), lambda i: (i, 0, 0))],
        out_specs=[pl.BlockSpec((CHUNK, d), lambda i: (i, 0)), pl.BlockSpec((1, heads, DK, DK), lambda i: (i, 0, 0, 0)),
                   pl.BlockSpec((1, heads, CHUNK, CHUNK), lambda i: (i, 0, 0, 0))],
        out_shape=[jax.ShapeDtypeStruct((t, d), F32), jax.ShapeDtypeStruct((nc, heads, DK, DK), F32),
                   jax.ShapeDtypeStruct((nc, heads, CHUNK, CHUNK), F32)],
        scratch_shapes=[pltpu.VMEM((heads, DK, DK), F32)], compiler_params=_cparams("arbitrary"),
    )(qkvn, bg, g_row)


def _dn_bwd(qkvn, bg, g_row, do, sall, ainv, d, heads, name):
    t = qkvn.shape[0]
    nc = t // CHUNK
    rev = lambda i: nc - 1 - i

    def body(qkv_ref, bg_ref, gr_ref, do_ref, sall_ref, ainv_ref, dqkv_ref, dbg_ref, dgr_ref, ds_ref):
        @pl.when(pl.program_id(0) == 0)
        def _():
            ds_ref[...] = jnp.zeros_like(ds_ref)

        incl, strict, eye, tril, triu = _chunk_consts()
        bgv = bg_ref[...]
        gc_all = _dot(tril, bgv, NN, HI)
        gr_all = _dot(gr_ref[0], triu, NN, HI)
        lane = lax.broadcasted_iota(jnp.int32, (CHUNK, LANES), 1)
        hrow = lax.broadcasted_iota(jnp.int32, (heads, CHUNK), 0)
        last_row = lax.broadcasted_iota(jnp.int32, (CHUNK, 1), 0) == CHUNK - 1
        dbeta_slab = jnp.zeros((CHUNK, LANES), F32)
        dgc_slab = jnp.zeros((CHUNK, LANES), F32)
        dgr_slab = jnp.zeros((heads, CHUNK), F32)
        hs = range(heads)
        cols = lambda ref, off: [ref[:, off + h * DK:off + (h + 1) * DK] for h in hs]
        qs, ks, vs, douts = cols(qkv_ref, 0), cols(qkv_ref, d), cols(qkv_ref, 2 * d), cols(do_ref, 0)
        ss, xs_, dsns = [sall_ref[0, h] for h in hs], [ainv_ref[0, h] for h in hs], [ds_ref[h] for h in hs]

        def recompute(h, z):
            z.q, z.k, z.v, z.dout, z.s, z.x, z.dsn = qs[h], ks[h], vs[h], douts[h], ss[h], xs_[h], dsns[h]
            z.kb, z.qb, z.sb, z.dsnb, z.doutb = (a.astype(MM_DTYPE) for a in (z.k, z.q, z.s, z.dsn, z.dout))
            z.beta = bgv[:, h:h + 1]
            gc = gc_all[:, heads + h:heads + h + 1]
            z.decay = _chunk_decay(gc, gr_all[h:h + 1, :], incl)
            z.eg = jnp.exp(gc)
            glast = gc[CHUNK - 1:CHUNK, :]
            z.eglast = jnp.exp(glast)
            z.ek = jnp.exp(glast - gc)
            z.kk = _dot(z.kb, z.kb, NT)
            z.qk = _dot(z.qb, z.kb, NT)
            sol = _dot3(z.x, jnp.concatenate([z.v * z.beta, z.k * (z.beta * z.eg)], axis=1))
            z.u, z.wb = sol[:, :DK], sol[:, DK:].astype(MM_DTYPE)
            z.attn = z.qk * z.decay
            z.qg = z.q * z.eg
            z.kend = z.k * z.ek

        def pseudo_values(h, z):
            z.wvb = (z.u - _dot(z.wb, z.sb)).astype(MM_DTYPE)

        def scan_step(h, z):
            z.d_wv = _dot(z.attn, z.doutb, TN) + _dot(z.kend, z.dsnb)
            z.d_attn = _dot(z.doutb, z.wvb, NT)
            z.d_qg = _dot(z.doutb, z.sb, NT)
            z.d_kend = _dot(z.wvb, z.dsnb, NT)
            z.ds_new = _dot(z.qg, z.doutb, TN) + z.eglast * z.dsn
            z.d_glast = z.eglast * _sum_all(z.dsn * z.s) + _sum_all(z.d_kend * z.kend)

        def state_terms(h, z):
            z.ds_new = z.ds_new - _dot(z.wb, z.d_wv, TN)
            z.d_w = -_dot(z.d_wv, z.sb, NT)

        def solve_transpose(h, z):
            d_rhs = _dot3(z.x, jnp.concatenate([z.d_wv, z.d_w], axis=1), TN)
            z.d_ru, z.d_rw = d_rhs[:, :DK], d_rhs[:, DK:]

        def lower_terms(h, z):
            z.d_low = jnp.where(strict, -(_dot(z.d_ru, z.u, NT) + _dot(z.d_rw, z.wb, NT)), 0.0)

        def outputs(h, z):
            rw_k = jnp.sum(z.d_rw * z.k, axis=-1, keepdims=True)
            z.dbeta = (jnp.sum(z.d_ru * z.v, axis=-1, keepdims=True) + rw_k * z.eg
                       + jnp.sum(z.d_low * z.kk * z.decay, axis=-1, keepdims=True))
            z.dv = z.d_ru * z.beta
            d_kk = (z.d_low * z.beta * z.decay).astype(MM_DTYPE)
            d_qk = (z.d_attn * z.decay).astype(MM_DTYPE)
            z.dk = (z.d_rw * (z.beta * z.eg) + _dot(d_kk, z.kb) + _dot(d_kk, z.kb, TN) + _dot(d_qk, z.qb, TN)
                    + z.d_kend * z.ek)
            z.dq = _dot(d_qk, z.kb) + z.d_qg * z.eg
            e = (z.d_low * z.beta * z.kk + z.d_attn * z.qk) * z.decay
            z.dgc = (rw_k * z.beta * z.eg + jnp.sum(e, axis=-1, keepdims=True)
                     + jnp.sum(z.d_qg * z.qg, axis=-1, keepdims=True) - jnp.sum(z.d_kend * z.kend, axis=-1, keepdims=True)
                     + jnp.where(last_row, z.d_glast, 0.0))
            z.dgr = -jnp.sum(e, axis=0, keepdims=True)

        class _Head:
            pass

        st = [_Head() for _ in hs]
        for phase in (recompute, pseudo_values, scan_step, state_terms, solve_transpose, lower_terms, outputs):
            for h in hs:
                phase(h, st[h])
        res = [(z.dq, z.dk, z.dv, z.ds_new, z.dbeta, z.dgc, z.dgr) for z in st]
        for h, (dq, dk, dv, ds_new, dbeta, dgc, dgr) in enumerate(res):
            dqkv_ref[:, h * DK:(h + 1) * DK] = dq
            dqkv_ref[:, d + h * DK:d + (h + 1) * DK] = dk
            dqkv_ref[:, 2 * d + h * DK:2 * d + (h + 1) * DK] = dv
            ds_ref[h] = ds_new
            dbeta_slab = jnp.where(lane == h, dbeta, dbeta_slab)
            dgc_slab = jnp.where(lane == heads + h, dgc, dgc_slab)
            dgr_slab = jnp.where(hrow == h, dgr, dgr_slab)
        dbg_ref[...] = dbeta_slab + _dot(triu, dgc_slab, NN, HI)
        dgr_ref[0] = _dot(dgr_slab, tril, NN, HI)

    return pl.pallas_call(
        body, name=name, grid=(nc,),
        in_specs=[pl.BlockSpec((CHUNK, 3 * d), lambda i: (rev(i), 0)), pl.BlockSpec((CHUNK, LANES), lambda i: (rev(i), 0)),
                  pl.BlockSpec((1, heads, CHUNK), lambda i: (rev(i), 0, 0)), pl.BlockSpec((CHUNK, d), lambda i: (rev(i), 0)),
                  pl.BlockSpec((1, heads, DK, DK), lambda i: (rev(i), 0, 0, 0)),
                  pl.BlockSpec((1, heads, CHUNK, CHUNK), lambda i: (rev(i), 0, 0, 0))],
        out_specs=[pl.BlockSpec((CHUNK, 3 * d), lambda i: (rev(i), 0)), pl.BlockSpec((CHUNK, LANES), lambda i: (rev(i), 0)),
                   pl.BlockSpec((1, heads, CHUNK), lambda i: (rev(i), 0, 0))],
        out_shape=[jax.ShapeDtypeStruct((t, 3 * d), F32), jax.ShapeDtypeStruct((t, LANES), F32),
                   jax.ShapeDtypeStruct((nc, heads, CHUNK), F32)],
        scratch_shapes=[pltpu.VMEM((heads, DK, DK), F32)], compiler_params=_cparams("arbitrary"),
    )(qkvn, bg, g_row, do, sall, ainv)


def _dn_post(o, proj, w, d, heads, name):
    t = o.shape[0]
    r = _row_tile(t)

    def body(o_ref, z_ref, w_ref, y_ref):
        for h in range(heads):
            sl = slice(h * DK, (h + 1) * DK)
            oh, z = o_ref[:, sl], z_ref[:, sl]
            rs = lax.rsqrt(jnp.mean(oh * oh, axis=-1, keepdims=True) + EPS)
            y_ref[:, sl] = (oh * rs * w_ref[...] * (z * _sig(z))).astype(y_ref.dtype)

    row = pl.BlockSpec((r, d), lambda i: (i, 0))
    return pl.pallas_call(
        body, name=name, grid=(t // r,),
        in_specs=[row, pl.BlockSpec((r, d), lambda i: (i, 3)), pl.BlockSpec((1, DK), lambda i: (0, 0))],
        out_specs=row, out_shape=jax.ShapeDtypeStruct((t, d), MM_DTYPE), compiler_params=_cparams("parallel"),
    )(o, proj, w)


def _dn_post_bwd(o, proj, w, dy, dproj, d, heads, name):
    t = o.shape[0]
    r = _row_tile(t)

    def body(o_ref, z_ref, w_ref, dy_ref, _, do_ref, dz_ref, gw_ref):
        @pl.when(pl.program_id(0) == 0)
        def _():
            gw_ref[...] = jnp.zeros_like(gw_ref)

        gw = jnp.zeros((1, DK), F32)
        for h in range(heads):
            sl = slice(h * DK, (h + 1) * DK)
            oh, z, dyh = o_ref[:, sl], z_ref[:, sl], dy_ref[:, sl]
            sz = _sig(z)
            rs = lax.rsqrt(jnp.mean(oh * oh, axis=-1, keepdims=True) + EPS)
            xh = oh * rs
            dn = dyh * (z * sz)
            dz_ref[:, sl] = (dyh * (xh * w_ref[...]) * _dsilu(z, sz)).astype(dz_ref.dtype)
            dxh = dn * w_ref[...]
            do_ref[:, sl] = rs * (dxh - xh * jnp.mean(dxh * xh, axis=-1, keepdims=True))
            gw = gw + jnp.sum(dn * xh, axis=0, keepdims=True)
        gw_ref[0:1, :] += gw

    row = pl.BlockSpec((r, d), lambda i: (i, 0))
    za = pl.BlockSpec((r, d), lambda i: (i, 3))
    return pl.pallas_call(
        body, name=name, grid=(t // r,),
        in_specs=[row, za, pl.BlockSpec((1, DK), lambda i: (0, 0)), row, ANY],
        out_specs=[row, za, pl.BlockSpec((8, DK), lambda i: (0, 0))],
        out_shape=[jax.ShapeDtypeStruct((t, d), F32), jax.ShapeDtypeStruct(dproj.shape, dproj.dtype),
                   jax.ShapeDtypeStruct((8, DK), F32)],
        input_output_aliases={4: 1}, compiler_params=_cparams("arbitrary"),
    )(o, proj, w, dy, dproj)


def _glu_window(g_ref, gh_ref, i, d, cs):
    cs2 = slice(d + cs.start, d + cs.stop)
    cur = g_ref[:, cs] * _sig(g_ref[:, cs2])
    prev = gh_ref[HALO - 32:HALO, cs] * _sig(gh_ref[HALO - 32:HALO, cs2])
    return jnp.concatenate([jnp.where(i > 0, prev, 0.0), cur], axis=0)


def _ln_stats(c1):
    mu = jnp.mean(c1, axis=-1, keepdims=True)
    cc = c1 - mu
    return cc * lax.rsqrt(jnp.mean(cc * cc, axis=-1, keepdims=True) + EPS), lax.rsqrt(jnp.mean(cc * cc, axis=-1, keepdims=True) + EPS)


def _cf_fwd(proj, dw_w, dw_b, ln_w, ln_b, d, name):
    t = proj.shape[0]
    r = _conv_rows(t)

    def body(g_ref, gh_ref, zb_ref, w_ref, b_ref, lw_ref, lb_ref, c1_ref, c3_ref):
        for cb in range(d // LANES):
            cs = slice(cb * LANES, (cb + 1) * LANES)
            win = _glu_window(g_ref, gh_ref, pl.program_id(0), d, cs)
            acc = jnp.broadcast_to(b_ref[:, cs], (r, LANES))
            for sub in range(8):
                rot = _shift(win, sub)
                for s in range(sub, CF_TAPS, 8):
                    acc = acc + rot[32 - (s - sub):32 - (s - sub) + r, :] * w_ref[CF_TAPS - 1 - s:CF_TAPS - s, cs]
            c1_ref[:, cs] = acc
        xh, _ = _ln_stats(c1_ref[...])
        ln = xh * lw_ref[...] + lb_ref[...]
        zb = zb_ref[...]
        c3_ref[...] = ((ln * _sig(ln)) * (zb * _sig(zb))).astype(c3_ref.dtype)

    row = pl.BlockSpec((r, d), lambda i: (i, 0))
    par = pl.BlockSpec((1, d), lambda i: (0, 0))
    return pl.pallas_call(
        body, name=name, grid=(t // r,),
        in_specs=[pl.BlockSpec((r, 2 * d), lambda i: (i, 2)), pl.BlockSpec((HALO, 2 * d), lambda i: (_halo_prev(r)(i)[0], 2)),
                  pl.BlockSpec((r, d), lambda i: (i, 8)), pl.BlockSpec((32, d), lambda i: (0, 0)), par, par, par],
        out_specs=[row, row],
        out_shape=[jax.ShapeDtypeStruct((t, d), F32), jax.ShapeDtypeStruct((t, d), MM_DTYPE)],
        compiler_params=_cparams("parallel"),
    )(proj, proj, proj, dw_w, dw_b, ln_w, ln_b)


def _cf_bwd1(c1, proj, ln_w, ln_b, dc3, dproj, d, name):
    t = c1.shape[0]
    r = _row_tile(t)

    def body(c1_ref, zb_ref, lw_ref, lb_ref, dc3_ref, _, dc1_ref, dzb_ref, g_ref):
        @pl.when(pl.program_id(0) == 0)
        def _():
            g_ref[...] = jnp.zeros_like(g_ref)

        xh, rs = _ln_stats(c1_ref[...])
        ln = xh * lw_ref[...] + lb_ref[...]
        sl, zb, dc3v = _sig(ln), zb_ref[...], dc3_ref[...]
        szb = _sig(zb)
        dzb_ref[...] = (dc3v * (ln * sl) * _dsilu(zb, szb)).astype(dzb_ref.dtype)
        dln = dc3v * (zb * szb) * _dsilu(ln, sl)
        dxh = dln * lw_ref[...]
        dc1 = rs * (dxh - jnp.mean(dxh, axis=-1, keepdims=True) - xh * jnp.mean(dxh * xh, axis=-1, keepdims=True))
        dc1_ref[...] = dc1
        g_ref[0:1, :] += jnp.sum(dln * xh, axis=0, keepdims=True)
        g_ref[1:2, :] += jnp.sum(dln, axis=0, keepdims=True)
        g_ref[2:3, :] += jnp.sum(dc1, axis=0, keepdims=True)

    row = pl.BlockSpec((r, d), lambda i: (i, 0))
    zbs = pl.BlockSpec((r, d), lambda i: (i, 8))
    par = pl.BlockSpec((1, d), lambda i: (0, 0))
    return pl.pallas_call(
        body, name=name, grid=(t // r,), in_specs=[row, zbs, par, par, row, ANY],
        out_specs=[row, zbs, pl.BlockSpec((8, d), lambda i: (0, 0))],
        out_shape=[jax.ShapeDtypeStruct((t, d), F32), jax.ShapeDtypeStruct(dproj.shape, dproj.dtype),
                   jax.ShapeDtypeStruct((8, d), F32)],
        input_output_aliases={5: 1}, compiler_params=_cparams("arbitrary"),
    )(c1, proj, ln_w, ln_b, dc3, dproj)


def _cf_bwd2(dc1, proj, dw_w, dproj, d, name):
    t = dc1.shape[0]
    r = _conv_rows(t)
    last = t // r - 1

    def body(dc_ref, dcn_ref, g_ref, gh_ref, w_ref, _, dg_ref, gw_ref):
        i = pl.program_id(0)

        @pl.when(i == 0)
        def _():
            gw_ref[...] = jnp.zeros_like(gw_ref)

        for cb in range(d // LANES):
            cs = slice(cb * LANES, (cb + 1) * LANES)
            cs2 = slice(d + cs.start, d + cs.stop)
            dcur = dc_ref[:, cs]
            dwin = jnp.concatenate([dcur, jnp.where(i < last, dcn_ref[0:32, cs], 0.0)], axis=0)
            win = _glu_window(g_ref, gh_ref, i, d, cs)
            acc = None
            for sub in range(8):
                drot = _shift(dwin, -sub)
                rot = _shift(win, sub)
                for s in range(sub, CF_TAPS, 8):
                    j = CF_TAPS - 1 - s
                    term = drot[s - sub:s - sub + r, :] * w_ref[j:j + 1, cs]
                    acc = term if acc is None else acc + term
                    gw_ref[j:j + 1, cs] += jnp.sum(dcur * rot[32 - (s - sub):32 - (s - sub) + r, :], axis=0, keepdims=True)
            ga, sb = g_ref[:, cs], _sig(g_ref[:, cs2])
            dg_ref[:, cs] = (acc * sb).astype(dg_ref.dtype)
            dg_ref[:, cs2] = (acc * ga * sb * (1.0 - sb)).astype(dg_ref.dtype)

    row = pl.BlockSpec((r, d), lambda i: (i, 0))
    glu = pl.BlockSpec((r, 2 * d), lambda i: (i, 2))
    return pl.pallas_call(
        body, name=name, grid=(t // r,),
        in_specs=[row, pl.BlockSpec((HALO, d), _halo_next(r, t)), glu,
                  pl.BlockSpec((HALO, 2 * d), lambda i: (_halo_prev(r)(i)[0], 2)), pl.BlockSpec((32, d), lambda i: (0, 0)), ANY],
        out_specs=[glu, pl.BlockSpec((32, d), lambda i: (0, 0))],
        out_shape=[jax.ShapeDtypeStruct(dproj.shape, dproj.dtype), jax.ShapeDtypeStruct((32, d), F32)],
        input_output_aliases={5: 0}, compiler_params=_cparams("arbitrary"),
    )(dc1, dc1, proj, proj, dw_w, dproj)


def _merge(proj, ya, yb, b, d, name):
    t = ya.shape[0]
    r = _row_tile(t)

    def body(g_ref, ya_ref, yb_ref, b_ref, m_ref):
        m_ref[...] = (_sig(g_ref[:, :d]) * ya_ref[...] + _sig(g_ref[:, d:]) * (yb_ref[...] + b_ref[...])).astype(m_ref.dtype)

    row = pl.BlockSpec((r, d), lambda i: (i, 0))
    return pl.pallas_call(
        body, name=name, grid=(t // r,),
        in_specs=[pl.BlockSpec((r, 2 * d), lambda i: (i, 3)), row, row, pl.BlockSpec((1, d), lambda i: (0, 0))],
        out_specs=row, out_shape=jax.ShapeDtypeStruct((t, d), MM_DTYPE), compiler_params=_cparams("parallel"),
    )(proj, ya, yb, b)


def _merge_bwd(proj, ya, yb, b, dm, d, name):
    t = ya.shape[0]
    r = _row_tile(t)

    def body(g_ref, ya_ref, yb_ref, b_ref, dm_ref, dya_ref, dyb_ref, dg_ref, gb_ref):
        @pl.when(pl.program_id(0) == 0)
        def _():
            gb_ref[...] = jnp.zeros_like(gb_ref)

        sa, sb, dmv = _sig(g_ref[:, :d]), _sig(g_ref[:, d:]), dm_ref[...]
        dyb = dmv * sb
        dya_ref[...] = (dmv * sa).astype(dya_ref.dtype)
        dyb_ref[...] = dyb.astype(dyb_ref.dtype)
        dg_ref[:, :d] = (dmv * ya_ref[...] * sa * (1.0 - sa)).astype(dg_ref.dtype)
        dg_ref[:, d:] = (dmv * (yb_ref[...] + b_ref[...]) * sb * (1.0 - sb)).astype(dg_ref.dtype)
        gb_ref[0:1, :] += jnp.sum(dyb, axis=0, keepdims=True)

    row = pl.BlockSpec((r, d), lambda i: (i, 0))
    gate = pl.BlockSpec((r, 2 * d), lambda i: (i, 3))
    return pl.pallas_call(
        body, name=name, grid=(t // r,),
        in_specs=[gate, row, row, pl.BlockSpec((1, d), lambda i: (0, 0)), row],
        out_specs=[row, row, gate, pl.BlockSpec((8, d), lambda i: (0, 0))],
        out_shape=[jax.ShapeDtypeStruct((t, d), MM_DTYPE), jax.ShapeDtypeStruct((t, d), MM_DTYPE),
                   jax.ShapeDtypeStruct((t, 9 * d), MM_DTYPE), jax.ShapeDtypeStruct((8, d), F32)],
        compiler_params=_cparams("arbitrary"),
    )(proj, ya, yb, b, dm)


def _place():
    return lax.axis_index("x"), lax.axis_index("y"), lax.axis_index("c")


def _slot(p):
    return 4 * p[0] + 2 * p[1] + p[2]


def _allgather(blocks, name):
    n = len(blocks)

    def body(*refs):
        ins, outs = refs[:n], refs[n:2 * n]
        send_sems, recv_sems, local_sems = refs[2 * n:]
        x, y, c = _place()
        me, sibling = (x, y, c), (x, y, 1 - c)
        chips = [(1 - x, y), (x, 1 - y), (1 - x, 1 - y)]

        def copy(a, k, block, to, src=None):
            dst = outs[a].at[_slot(block)]
            return pltpu.make_async_remote_copy(
                src_ref=dst if src is None else src, dst_ref=dst, send_sem=send_sems.at[7 * a + k],
                recv_sem=recv_sems.at[7 * a + k], device_id=to, device_id_type=MESH)

        mine = [pltpu.make_async_copy(ins[a], outs[a].at[_slot(me)], local_sems.at[a]) for a in range(n)]
        for cp in mine:
            cp.start()
        first = []
        for a in range(n):
            first.append(copy(a, 0, me, sibling, src=ins[a]))
            first += [copy(a, 1 + j, me, (*chip, c), src=ins[a]) for j, chip in enumerate(chips)]
        for cp in first:
            cp.start()
        passed = []
        for j, chip in enumerate(chips):
            for a in range(n):
                copy(a, 1 + j, (*chip, c), me).wait_recv()
                cp = copy(a, 4 + j, (*chip, c), sibling)
                cp.start()
                passed.append(cp)
        for a in range(n):
            copy(a, 0, sibling, me).wait_recv()
            for j, chip in enumerate(chips):
                copy(a, 4 + j, (*chip, 1 - c), me).wait_recv()
        for cp in first + passed:
            cp.wait_send()
        for cp in mine:
            cp.wait()

    return pl.pallas_call(
        body, name=name, in_specs=[ANY] * n, out_specs=[ANY] * n,
        out_shape=[jax.ShapeDtypeStruct((8, *b.shape), b.dtype) for b in blocks],
        scratch_shapes=[pltpu.SemaphoreType.DMA((7 * n,)), pltpu.SemaphoreType.DMA((7 * n,)), pltpu.SemaphoreType.DMA((n,))],
    )(*blocks)


_FLIPS = [(0, 0, 1), (0, 1, 0), (0, 1, 1), (1, 0, 0), (1, 0, 1), (1, 1, 0), (1, 1, 1)]


def _alltoall(arrays, name):
    n = len(arrays)

    def body(*refs):
        ins, outs = refs[:n], refs[n:2 * n]
        send_sems, recv_sems, local_sems = refs[2 * n:]
        me = _place()
        peers = [tuple(1 - me[ax] if f[ax] else me[ax] for ax in range(3)) for f in _FLIPS]

        def copy(a, k):
            return pltpu.make_async_remote_copy(
                src_ref=ins[a].at[_slot(peers[k])], dst_ref=outs[a].at[_slot(me)], send_sem=send_sems.at[7 * a + k],
                recv_sem=recv_sems.at[7 * a + k], device_id=peers[k], device_id_type=MESH)

        def arrival(a, k):
            return pltpu.make_async_remote_copy(
                src_ref=ins[a].at[_slot(me)], dst_ref=outs[a].at[_slot(peers[k])], send_sem=send_sems.at[7 * a + k],
                recv_sem=recv_sems.at[7 * a + k], device_id=peers[k], device_id_type=MESH)

        mine = [pltpu.make_async_copy(ins[a].at[_slot(me)], outs[a].at[_slot(me)], local_sems.at[a]) for a in range(n)]
        for cp in mine:
            cp.start()
        sent = [copy(a, k) for a in range(n) for k in range(7)]
        for cp in sent:
            cp.start()
        for a in range(n):
            for k in range(7):
                arrival(a, k).wait_recv()
        for cp in sent:
            cp.wait_send()
        for cp in mine:
            cp.wait()

    return pl.pallas_call(
        body, name=name, in_specs=[ANY] * n, out_specs=[ANY] * n,
        out_shape=[jax.ShapeDtypeStruct(a.shape, a.dtype) for a in arrays],
        scratch_shapes=[pltpu.SemaphoreType.DMA((7 * n,)), pltpu.SemaphoreType.DMA((7 * n,)), pltpu.SemaphoreType.DMA((n,))],
    )(*arrays)


def _sibling_exchange(arrays, name):
    n = len(arrays)

    def body(*refs):
        ins, outs = refs[:n], refs[n:2 * n]
        send_sems, recv_sems, local_sems = refs[2 * n:]
        x, y, c = _place()

        def copy(a, slot, rows=None):
            src, dst = ins[a], outs[a].at[slot]
            if rows is not None:
                src, dst = src.at[rows], dst.at[rows]
            return pltpu.make_async_remote_copy(
                src_ref=src, dst_ref=dst, send_sem=send_sems.at[a], recv_sem=recv_sems.at[a],
                device_id=(x, y, 1 - c), device_id_type=MESH)

        mine = [pltpu.make_async_copy(ins[a], outs[a].at[c], local_sems.at[a]) for a in range(n)]
        for cp in mine:
            cp.start()
        for a in range(n):
            rows = ins[a].shape[0]
            k = [kk for kk in (8, 4, 2, 1) if rows % (8 * kk) == 0][0]
            for j in range(k):
                copy(a, c, pl.ds(j * (rows // k), rows // k)).start()
        for a in range(n):
            copy(a, 1 - c).wait_recv()
        for a in range(n):
            copy(a, c).wait_send()
        for cp in mine:
            cp.wait()

    return pl.pallas_call(
        body, name=name, in_specs=[ANY] * n, out_specs=[ANY] * n,
        out_shape=[jax.ShapeDtypeStruct((2, *a.shape), a.dtype) for a in arrays],
        scratch_shapes=[pltpu.SemaphoreType.DMA((n,)), pltpu.SemaphoreType.DMA((n,)), pltpu.SemaphoreType.DMA((n,))],
    )(*arrays)


def _sum8(parts, name):
    _, rr, cc = parts.shape
    r = _pick(rr, (128, 64, 32, 16, 8))

    def body(p_ref, o_ref):
        acc = p_ref[0].astype(F32)
        for j in range(1, 8):
            acc = acc + p_ref[j].astype(F32)
        o_ref[...] = acc

    return pl.pallas_call(
        body, name=name, grid=(rr // r,), in_specs=[pl.BlockSpec((8, r, cc), lambda i: (0, i, 0))],
        out_specs=pl.BlockSpec((r, cc), lambda i: (i, 0)), out_shape=jax.ShapeDtypeStruct((rr, cc), F32),
        compiler_params=_cparams("parallel"),
    )(parts)


def _adamw(g, w, m, v, name):
    rr, cc = g.shape
    r = _pick(rr, (128, 64, 32, 16, 8))

    def body(g_ref, w_ref, m_ref, v_ref, d_ref, nm_ref, nv_ref):
        gv = g_ref[...]
        nm = ADAM_B1 * m_ref[...] + (1.0 - ADAM_B1) * gv
        nv = ADAM_B2 * v_ref[...] + (1.0 - ADAM_B2) * (gv * gv)
        m_hat = nm / (1.0 - ADAM_B1 ** ADAM_STEP)
        v_hat = nv / (1.0 - ADAM_B2 ** ADAM_STEP)
        d_ref[...] = -ADAM_LR * (m_hat / (jnp.sqrt(v_hat) + ADAM_EPS) + ADAM_WD * w_ref[...])
        nm_ref[...] = nm
        nv_ref[...] = nv

    blk = pl.BlockSpec((r, cc), lambda i: (i, 0))
    return pl.pallas_call(
        body, name=name, grid=(rr // r,), in_specs=[blk] * 4, out_specs=[blk] * 3,
        out_shape=[jax.ShapeDtypeStruct((rr, cc), F32)] * 3, compiler_params=_cparams("parallel"),
    )(g, w, m, v)


def _flat_pack(parts, width):
    flat = jnp.concatenate([p.reshape(-1) for p in parts])
    total = -(-flat.shape[0] // (8 * width)) * (8 * width)
    return jnp.pad(flat, (0, total - flat.shape[0])).reshape(-1, width)


def _flat_unpack(pack, shapes):
    flat = pack.reshape(-1)
    out, at = [], 0
    for s in shapes:
        size = 1
        for e in s:
            size *= e
        out.append(flat[at:at + size].reshape(s))
        at += size
    return out


def kernel(x, meta, norm_w, w_in, conv_qkv_w, a_log, dt_bias, dn_norm_w, w_dn_out, dw_w, dw_b, ln_w, ln_b, w_cf_out, b_cf_out, w_o, final_norm_w, loss_target, m_meta, m_norm_w, m_w_in, m_conv_qkv_w, m_a_log, m_dt_bias, m_dn_norm_w, m_w_dn_out, m_dw_w, m_dw_b, m_ln_w, m_ln_b, m_w_cf_out, m_b_cf_out, m_w_o, m_final_norm_w, v_meta, v_norm_w, v_w_in, v_conv_qkv_w, v_a_log, v_dt_bias, v_dn_norm_w, v_w_dn_out, v_dw_w, v_dw_b, v_ln_w, v_ln_b, v_w_cf_out, v_b_cf_out, v_w_o, v_final_norm_w):
    d = x.shape[-1]
    heads = a_log.shape[-1]
    seq = x.shape[1]
    t = HDR + seq
    nc = t // CHUNK
    in_w = 9 * d + 2 * heads
    assert heads * DK == d and seq % LANES == 0 and w_in.shape[-1] * 4 == in_w
    xi, yi, ci = _place()
    shard = 2 * xi + yi
    ds = d // 4

    w_in_half = lax.dynamic_slice_in_dim(w_in[0].astype(MM_DTYPE), ci * (d // 2), d // 2, axis=0)
    w3_half = lax.dynamic_slice_in_dim(jnp.stack([w_dn_out[0], w_cf_out[0], w_o[0]]).astype(MM_DTYPE), ci * (d // 8), d // 8, axis=1)
    small = jnp.concatenate([
        jnp.pad(conv_qkv_w[0], ((0, 4), (0, 0))), jnp.pad(meta, ((0, 0), (0, 2 * ds))),
        jnp.pad(dw_w[0], ((0, 1), (0, 2 * ds)))], axis=0)
    g_in, g_w3, g_small = _allgather([w_in_half, w3_half, small], "gather_weights")
    w_full = g_in.reshape(4, d, in_w // 4).transpose(1, 0, 2).reshape(d, in_w)
    o_glu = 4 * d + 2 * heads
    w_main = jnp.concatenate([w_full[:, :4 * d], w_full[:, o_glu:o_glu + 2 * d], w_full[:, o_glu + 3 * d:o_glu + 5 * d],
                              w_full[:, o_glu + 2 * d:o_glu + 3 * d]], axis=1)
    w_ba = jnp.pad(w_full[:, 4 * d:o_glu], ((0, 0), (0, LANES - 2 * heads)))
    w3_full = g_w3.transpose(1, 0, 2, 3).reshape(3, d, d)
    w_dn_f, w_cf_f, w_o_f = w3_full[0], w3_full[1], w3_full[2]
    small4 = g_small[0::2]
    conv_w_f = small4[:, 0:DN_TAPS, :].transpose(1, 0, 2).reshape(DN_TAPS, 3 * d)
    meta_f = small4[:, 8:8 + N_META, :ds].transpose(1, 0, 2).reshape(N_META, d)
    dw_w_f = small4[:, 24:56, :ds].transpose(1, 0, 2).reshape(32, d)

    xs = jnp.concatenate([jnp.pad(meta_f, ((PAD_ROWS, 0), (0, 0))), x[0]], axis=0)
    tgt = jnp.pad(loss_target[0], ((HDR, 0), (0, 0)))
    h = _rms_fwd(xs, norm_w, "rms_fwd")
    proj = _mm(h, w_main, name="proj_main")
    pba = _mm(h, w_ba, name="proj_ba")
    qkvn = _dn_prep(proj, conv_w_f, d, heads, "dn_prep")
    pvec = jnp.zeros((8, LANES), F32).at[0, heads:2 * heads].set(a_log[0]).at[1, heads:2 * heads].set(dt_bias[0])
    bg = _ba_fwd(pba, pvec, heads, "ba_fwd")
    g_row = bg[:, heads:2 * heads].reshape(nc, CHUNK, heads).transpose(0, 2, 1)
    o, sall, ainv = _dn_fwd(qkvn, bg, g_row, d, heads, "dn_fwd")
    ya_in = _dn_post(o, proj, dn_norm_w, d, heads, "dn_post")
    ya = _mm(ya_in, w_dn_f, name="ya")
    c1, c3 = _cf_fwd(proj, dw_w_f, dw_b, ln_w, ln_b, d, "cf_fwd")
    yb = _mm(c3, w_cf_f, name="yb")
    merged = _merge(proj, ya, yb, b_cf_out, d, "merge")
    mo = _mm(merged, w_o_f, name="mo")
    dxo, dxo_mm, loss_acc, g_fnw = _final(xs, mo, final_norm_w.reshape(1, d), tgt, "final")

    dmerged = _mm(dxo_mm, w_o_f, nt=True, name="d_merged")
    gw_o = _mm(merged.T, dxo_mm, out_dtype=MM_DTYPE, name="gw_o")
    dya, dyb, dproj, g_bcf = _merge_bwd(proj, ya, yb, b_cf_out, dmerged, d, "merge_bwd")
    dc3 = _mm(dyb, w_cf_f, nt=True, name="d_c3")
    gw_cf = _mm(c3.T, dyb, out_dtype=MM_DTYPE, name="gw_cf")
    dyain = _mm(dya, w_dn_f, nt=True, name="d_ya_in")
    gw_dn = _mm(ya_in.T, dya, out_dtype=MM_DTYPE, name="gw_dn")
    dc1, dproj, g_ln = _cf_bwd1(c1, proj, ln_w, ln_b, dc3, dproj, d, "cf_bwd1")
    dproj, g_dww = _cf_bwd2(dc1, proj, dw_w_f, dproj, d, "cf_bwd2")
    do, dproj, g_dnw = _dn_post_bwd(o, proj, dn_norm_w, dyain, dproj, d, heads, "dn_post_bwd")
    dqkvn, dbg, dg_row = _dn_bwd(qkvn, bg, g_row, do, sall, ainv, d, heads, "dn_bwd")
    dconv = _dn_prep_bwd1(proj, conv_w_f, dqkvn, d, heads, "dn_prep_bwd1")
    dproj, g_convw = _dn_prep_bwd2(dconv, proj, conv_w_f, dproj, d, "dn_prep_bwd2")
    dbg = dbg + jnp.pad(dg_row.transpose(0, 2, 1).reshape(t, heads), ((0, 0), (heads, LANES - 2 * heads)))
    dpba, g_ba = _ba_bwd(pba, pvec, dbg, heads, "ba_bwd")
    dh1 = _mm(dproj, w_main, nt=True, name="d_h_main")
    dh2 = _mm(dpba, w_ba, nt=True, name="d_h_ba")
    h_t = h.T
    gw_main = _mm(h_t, dproj, out_dtype=MM_DTYPE, name="gw_main")
    gw_ba = _mm(h_t, dpba, out_dtype=MM_DTYPE, name="gw_ba")
    dxs, g_nw = _rms_bwd(xs, norm_w, dh1, dh2, dxo, "rms_bwd")
    grad_x = dxs[HDR:][None]

    gw_in = jnp.concatenate([gw_main[:, :4 * d], gw_ba[:, :2 * heads], gw_main[:, 4 * d:6 * d], gw_main[:, 8 * d:],
                             gw_main[:, 6 * d:8 * d]], axis=1)
    send_in = gw_in.reshape(d, 4, in_w // 4).transpose(1, 0, 2).reshape(8, d // 2, in_w // 4)
    send_w3 = jnp.stack([gw_dn, gw_cf, gw_o]).reshape(3, 8, d // 8, d).transpose(1, 0, 2, 3).reshape(8, 3 * d // 8, d)
    misc = jnp.concatenate([g_ba[0, heads:2 * heads], g_ba[1, heads:2 * heads], g_dnw[0], loss_acc[0, 0:1]])
    small_g = jnp.concatenate([
        g_convw[0:DN_TAPS].reshape(DN_TAPS * 3, d), dxs[PAD_ROWS:HDR], g_dww, g_nw[0:1], g_ln[2:3], g_ln[0:1], g_ln[1:2],
        g_bcf[0:1], g_fnw[0:1], jnp.pad(misc, (0, d - misc.shape[0]))[None], jnp.zeros((5, d), F32)], axis=0)
    got_in, got_w3 = _alltoall([send_in, send_w3], "exchange_grads")
    red_in = _sum8(got_in, "sum_w_in")
    red_w3 = _sum8(got_w3, "sum_w3")
    pair_in, pair_w3 = _sibling_exchange([red_in, red_w3], "pair_grads")
    g_w_in = pair_in.reshape(d, in_w // 4)
    g_w3 = pair_w3.reshape(2, 3, d // 8, d).transpose(1, 0, 2, 3).reshape(3 * ds, d)
    all_small, = _allgather([small_g], "gather_small_grads")
    sg = _sum8(all_small, "sum_small")
    g_conv = lax.dynamic_slice_in_dim(sg[0:12].reshape(DN_TAPS, 3 * d), shard * 3 * ds, 3 * ds, axis=1)
    g_meta = lax.dynamic_slice_in_dim(sg[12:28], shard * ds, ds, axis=1)
    g_dw_w = lax.dynamic_slice_in_dim(sg[28:28 + CF_TAPS], shard * ds, ds, axis=1)
    r0 = 28 + 32
    g_rep = {"norm_w": sg[r0:r0 + 1], "dw_b": sg[r0 + 1:r0 + 2], "ln_w": sg[r0 + 2:r0 + 3], "ln_b": sg[r0 + 3:r0 + 4],
             "b_cf_out": sg[r0 + 4:r0 + 5], "final_norm_w": sg[r0 + 5], "a_log": sg[r0 + 6:r0 + 7, 0:heads],
             "dt_bias": sg[r0 + 6:r0 + 7, heads:2 * heads], "dn_norm_w": sg[r0 + 6:r0 + 7, 2 * heads:2 * heads + DK]}
    loss = sg[r0 + 6, 2 * heads + DK]

    res = {}
    dl, nm, nv = _adamw(g_w_in, w_in[0], m_w_in[0], v_w_in[0], "adamw_w_in")
    res["w_in"] = (g_w_in[None], dl[None], nm[None], nv[None])
    stack3 = lambda a, b, c: jnp.concatenate([a[0], b[0], c[0]], axis=0)
    dl, nm, nv = _adamw(g_w3, stack3(w_dn_out, w_cf_out, w_o), stack3(m_w_dn_out, m_w_cf_out, m_w_o),
                        stack3(v_w_dn_out, v_w_cf_out, v_w_o), "adamw_w3")
    for j, nme in enumerate(("w_dn_out", "w_cf_out", "w_o")):
        res[nme] = tuple(a[j * ds:(j + 1) * ds][None] for a in (g_w3, dl, nm, nv))
    names = ["meta", "conv_qkv_w", "dw_w", "norm_w", "dw_b", "ln_w", "ln_b", "b_cf_out", "final_norm_w", "a_log", "dt_bias",
             "dn_norm_w"]
    grads = {"meta": g_meta, "conv_qkv_w": g_conv[None], "dw_w": g_dw_w[None], **g_rep}
    given = dict(meta=(meta, m_meta, v_meta), conv_qkv_w=(conv_qkv_w, m_conv_qkv_w, v_conv_qkv_w), dw_w=(dw_w, m_dw_w, v_dw_w),
                 norm_w=(norm_w, m_norm_w, v_norm_w), dw_b=(dw_b, m_dw_b, v_dw_b), ln_w=(ln_w, m_ln_w, v_ln_w),
                 ln_b=(ln_b, m_ln_b, v_ln_b), b_cf_out=(b_cf_out, m_b_cf_out, v_b_cf_out),
                 final_norm_w=(final_norm_w, m_final_norm_w, v_final_norm_w), a_log=(a_log, m_a_log, v_a_log),
                 dt_bias=(dt_bias, m_dt_bias, v_dt_bias), dn_norm_w=(dn_norm_w, m_dn_norm_w, v_dn_norm_w))
    shapes = [given[nme][0].shape for nme in names]
    packs = [_flat_pack([grads[nme] for nme in names], LANES)] + [_flat_pack([given[nme][j] for nme in names], LANES) for j in range(3)]
    outs = _adamw(*packs, "adamw_small")
    unpacked = [_flat_unpack(p, shapes) for p in (packs[0], *outs)]
    for j, nme in enumerate(names):
        res[nme] = tuple(u[j] for u in unpacked)

    order = ["meta", "norm_w", "w_in", "conv_qkv_w", "a_log", "dt_bias", "dn_norm_w", "w_dn_out", "dw_w", "dw_b", "ln_w", "ln_b",
             "w_cf_out", "b_cf_out", "w_o", "final_norm_w"]
    return (loss, grad_x, *[res[nme][0] for nme in order], *[res[nme][1] for nme in order],
            *[res[nme][2] for nme in order], *[res[nme][3] for nme in order])
```

```python
import jax
import jax.numpy as jnp
from jax import lax
from jax.experimental import pallas as pl
from jax.experimental.pallas import tpu as pltpu

F32 = jnp.float32
MM_DTYPE = jnp.bfloat16
EPS = 1e-6
N_META = 16
HDR = 128
PAD_ROWS = HDR - N_META
CHUNK = 64
DK = 128
CF_TAPS = 31
DN_TAPS = 4
HALO = 64
LANES = 128
ADAM_LR, ADAM_B1, ADAM_B2, ADAM_EPS, ADAM_WD, ADAM_STEP = 0.001, 0.9, 0.999, 1e-08, 0.01, 10
VMEM_LIMIT = 48 * 1024 * 1024
MESH = pl.DeviceIdType.MESH
HI = lax.Precision.HIGHEST
LO = lax.Precision.DEFAULT
NN = (((1,), (0,)), ((), ()))
NT = (((1,), (1,)), ((), ()))
TN = (((0,), (0,)), ((), ()))
ANY = pl.BlockSpec(memory_space=pl.ANY)


def _dot(a, b, dims=NN, prec=LO):
    if prec == LO:
        a, b = a.astype(MM_DTYPE), b.astype(MM_DTYPE)
    return lax.dot_general(a, b, dims, precision=prec, preferred_element_type=F32)


def _split(a):
    hi = a.astype(MM_DTYPE)
    return hi, (a - hi.astype(F32)).astype(MM_DTYPE)


def _dot3(a, b, dims=NN):
    ah, al = _split(a)
    bh, bl = _split(b)
    return _dot(ah, bh, dims) + (_dot(ah, bl, dims) + _dot(al, bh, dims))


def _pick(n, options):
    for o in options:
        if n % o == 0:
            return o
    return n


def _cparams(*sem):
    return pltpu.CompilerParams(dimension_semantics=sem, vmem_limit_bytes=VMEM_LIMIT)


def _sig(x):
    return jax.nn.sigmoid(x)


def _sum_all(a):
    return jnp.sum(jnp.sum(a, axis=-1, keepdims=True), axis=0, keepdims=True)


def _dsilu(x, s):
    return s + x * s * (1.0 - s)


def _shift(win, s):
    n = win.shape[0]
    s = s % n
    return win if s == 0 else pltpu.roll(win, s, 0)


def _row_tile(t):
    return _pick(t, (320, 128))


def _mm(a, b, *, nt=False, ta=False, out_dtype=F32, tiles=(None, None, None), name):
    k, m = a.shape if ta else a.shape[::-1]
    n = b.shape[0] if nt else b.shape[1]
    tm = tiles[0] or _pick(m, (1664, 1024, 640, 512, 384, 256, 128))
    tn = tiles[1] or _pick(n, (1024, 768, 512, 256, 128))
    tk = tiles[2] or _pick(k, (1664, 1024, 640, 512, 256, 128))
    assert m % tm == 0 and n % tn == 0 and k % tk == 0
    nk = k // tk
    dims = NT if nt else (TN if ta else NN)

    def body(a_ref, b_ref, o_ref, *acc):
        p = lax.dot_general(a_ref[...], b_ref[...], dims, preferred_element_type=F32)
        if nk == 1:
            o_ref[...] = p.astype(o_ref.dtype)
            return
        acc_ref, = acc
        kk = pl.program_id(2)

        @pl.when(kk == 0)
        def _():
            acc_ref[...] = p

        @pl.when(kk > 0)
        def _():
            acc_ref[...] += p

        @pl.when(kk == nk - 1)
        def _():
            o_ref[...] = acc_ref[...].astype(o_ref.dtype)

    a_spec = pl.BlockSpec((tk, tm), lambda i, j, kk: (kk, i)) if ta else pl.BlockSpec((tm, tk), lambda i, j, kk: (i, kk))
    b_spec = pl.BlockSpec((tn, tk), lambda i, j, kk: (j, kk)) if nt else pl.BlockSpec((tk, tn), lambda i, j, kk: (kk, j))
    return pl.pallas_call(
        body, name=name, grid=(m // tm, n // tn, nk),
        in_specs=[a_spec, b_spec],
        out_specs=pl.BlockSpec((tm, tn), lambda i, j, kk: (i, j)),
        out_shape=jax.ShapeDtypeStruct((m, n), out_dtype),
        scratch_shapes=[] if nk == 1 else [pltpu.VMEM((tm, tn), F32)],
        compiler_params=_cparams("parallel", "parallel", "arbitrary"),
    )(a, b)


def _rms_fwd(xs, w, name):
    t, d = xs.shape
    r = _row_tile(t)

    def body(x_ref, w_ref, h_ref):
        x = x_ref[...]
        rs = lax.rsqrt(jnp.mean(x * x, axis=-1, keepdims=True) + EPS)
        h_ref[...] = (x * rs * w_ref[...]).astype(h_ref.dtype)

    return pl.pallas_call(
        body, name=name, grid=(t // r,),
        in_specs=[pl.BlockSpec((r, d), lambda i: (i, 0)), pl.BlockSpec((1, d), lambda i: (0, 0))],
        out_specs=pl.BlockSpec((r, d), lambda i: (i, 0)),
        out_shape=jax.ShapeDtypeStruct((t, d), MM_DTYPE), compiler_params=_cparams("parallel"),
    )(xs, w)


def _rms_bwd(xs, w, dh1, dh2, dres, name):
    t, d = xs.shape
    r = _row_tile(t)

    def body(x_ref, w_ref, d1_ref, d2_ref, dr_ref, dx_ref, gw_ref):
        @pl.when(pl.program_id(0) == 0)
        def _():
            gw_ref[...] = jnp.zeros_like(gw_ref)

        x = x_ref[...]
        dh = d1_ref[...] + d2_ref[...]
        rs = lax.rsqrt(jnp.mean(x * x, axis=-1, keepdims=True) + EPS)
        xh = x * rs
        dxh = dh * w_ref[...]
        dx_ref[...] = rs * (dxh - xh * jnp.mean(dxh * xh, axis=-1, keepdims=True)) + dr_ref[...]
        gw_ref[0:1, :] += jnp.sum(dh * xh, axis=0, keepdims=True)

    row = pl.BlockSpec((r, d), lambda i: (i, 0))
    return pl.pallas_call(
        body, name=name, grid=(t // r,),
        in_specs=[row, pl.BlockSpec((1, d), lambda i: (0, 0)), row, row, row],
        out_specs=[row, pl.BlockSpec((8, d), lambda i: (0, 0))],
        out_shape=[jax.ShapeDtypeStruct((t, d), F32), jax.ShapeDtypeStruct((8, d), F32)],
        compiler_params=_cparams("arbitrary"),
    )(xs, w, dh1, dh2, dres)


def _final(xs, mo, w, tgt, name):
    t, d = xs.shape
    r = _row_tile(t)

    def body(x_ref, m_ref, w_ref, t_ref, dx_ref, dxm_ref, loss_ref, gw_ref):
        i = pl.program_id(0)

        @pl.when(i == 0)
        def _():
            loss_ref[...] = jnp.zeros_like(loss_ref)
            gw_ref[...] = jnp.zeros_like(gw_ref)

        xo = x_ref[...] + m_ref[...]
        rs = lax.rsqrt(jnp.mean(xo * xo, axis=-1, keepdims=True) + EPS)
        xh = xo * rs
        y = xh * w_ref[...]
        rows = i * r + lax.broadcasted_iota(jnp.int32, (r, 1), 0)
        err = jnp.where(rows >= HDR, y - t_ref[...], 0.0)
        loss_ref[...] += 0.5 * _sum_all(err * err) / d
        dy = err / d
        gw_ref[0:1, :] += jnp.sum(dy * xh, axis=0, keepdims=True)
        dxh = dy * w_ref[...]
        dx = rs * (dxh - xh * jnp.mean(dxh * xh, axis=-1, keepdims=True))
        dx_ref[...] = dx
        dxm_ref[...] = dx.astype(dxm_ref.dtype)

    row = pl.BlockSpec((r, d), lambda i: (i, 0))
    return pl.pallas_call(
        body, name=name, grid=(t // r,),
        in_specs=[row, row, pl.BlockSpec((1, d), lambda i: (0, 0)), row],
        out_specs=[row, row, pl.BlockSpec((8, LANES), lambda i: (0, 0)), pl.BlockSpec((8, d), lambda i: (0, 0))],
        out_shape=[jax.ShapeDtypeStruct((t, d), F32), jax.ShapeDtypeStruct((t, d), MM_DTYPE),
                   jax.ShapeDtypeStruct((8, LANES), F32), jax.ShapeDtypeStruct((8, d), F32)],
        compiler_params=_cparams("arbitrary"),
    )(xs, mo, w, tgt)


def _halo_prev(r):
    return lambda i: (jnp.maximum(i * (r // HALO) - 1, 0), 0)


def _halo_next(r, t):
    return lambda i: (jnp.minimum((i + 1) * (r // HALO), t // HALO - 1), 0)


def _conv_rows(t):
    return _pick(t, (320, 128))


def _qkv_window(p_ref, ph_ref, i, cs):
    return jnp.concatenate([jnp.where(i > 0, ph_ref[HALO - 8:HALO, cs], 0.0), p_ref[:, cs]], axis=0)


def _qkv_conv(win, cw, r):
    y = None
    for s in range(DN_TAPS):
        term = _shift(win, s)[8:8 + r, :] * cw[DN_TAPS - 1 - s:DN_TAPS - s, :]
        y = term if y is None else y + term
    return y


def _dn_prep(proj, conv_w, d, heads, name):
    t = proj.shape[0]
    r = _conv_rows(t)
    w3 = 3 * d

    def body(p_ref, ph_ref, cw_ref, o_ref):
        i = pl.program_id(0)
        for cb in range(w3 // DK):
            cs = slice(cb * DK, (cb + 1) * DK)
            y = _qkv_conv(_qkv_window(p_ref, ph_ref, i, cs), cw_ref[:, cs], r)
            a = y * _sig(y)
            if cb < 2 * heads:
                sc = DK ** -0.5 if cb < heads else 1.0
                a = a * (lax.rsqrt(jnp.sum(a * a, axis=-1, keepdims=True) + EPS) * sc)
            o_ref[:, cs] = a

    return pl.pallas_call(
        body, name=name, grid=(t // r,),
        in_specs=[pl.BlockSpec((r, w3), lambda i: (i, 0)), pl.BlockSpec((HALO, w3), _halo_prev(r)),
                  pl.BlockSpec((DN_TAPS, w3), lambda i: (0, 0))],
        out_specs=pl.BlockSpec((r, w3), lambda i: (i, 0)),
        out_shape=jax.ShapeDtypeStruct((t, w3), F32), compiler_params=_cparams("parallel"),
    )(proj, proj, conv_w)


def _dn_prep_bwd1(proj, conv_w, dqkvn, d, heads, name):
    t = proj.shape[0]
    r = _conv_rows(t)
    w3 = 3 * d

    def body(p_ref, ph_ref, cw_ref, dn_ref, o_ref):
        i = pl.program_id(0)
        for cb in range(w3 // DK):
            cs = slice(cb * DK, (cb + 1) * DK)
            y = _qkv_conv(_qkv_window(p_ref, ph_ref, i, cs), cw_ref[:, cs], r)
            sy = _sig(y)
            da = dn_ref[:, cs]
            if cb < 2 * heads:
                a = y * sy
                rn = lax.rsqrt(jnp.sum(a * a, axis=-1, keepdims=True) + EPS)
                n0 = a * rn
                dn0 = da * (DK ** -0.5 if cb < heads else 1.0)
                da = rn * (dn0 - n0 * jnp.sum(dn0 * n0, axis=-1, keepdims=True))
            o_ref[:, cs] = da * _dsilu(y, sy)

    row = pl.BlockSpec((r, w3), lambda i: (i, 0))
    return pl.pallas_call(
        body, name=name, grid=(t // r,),
        in_specs=[row, pl.BlockSpec((HALO, w3), _halo_prev(r)), pl.BlockSpec((DN_TAPS, w3), lambda i: (0, 0)), row],
        out_specs=row, out_shape=jax.ShapeDtypeStruct((t, w3), F32), compiler_params=_cparams("parallel"),
    )(proj, proj, conv_w, dqkvn)


def _dn_prep_bwd2(dconv, proj, conv_w, dproj, d, name):
    t = proj.shape[0]
    r = _conv_rows(t)
    w3 = 3 * d
    last = t // r - 1

    def body(dc_ref, dcn_ref, p_ref, ph_ref, cw_ref, _, dp_ref, gw_ref):
        i = pl.program_id(0)

        @pl.when(i == 0)
        def _():
            gw_ref[...] = jnp.zeros_like(gw_ref)

        for cb in range(w3 // DK):
            cs = slice(cb * DK, (cb + 1) * DK)
            cw = cw_ref[:, cs]
            dcur = dc_ref[:, cs]
            dwin = jnp.concatenate([dcur, jnp.where(i < last, dcn_ref[0:8, cs], 0.0)], axis=0)
            win = _qkv_window(p_ref, ph_ref, i, cs)
            acc = None
            for j in range(DN_TAPS):
                s = DN_TAPS - 1 - j
                term = _shift(dwin, -s)[0:r, :] * cw[j:j + 1, :]
                acc = term if acc is None else acc + term
                gw_ref[j:j + 1, cs] += jnp.sum(dcur * _shift(win, s)[8:8 + r, :], axis=0, keepdims=True)
            dp_ref[:, cs] = acc.astype(dp_ref.dtype)

    row = pl.BlockSpec((r, w3), lambda i: (i, 0))
    return pl.pallas_call(
        body, name=name, grid=(t // r,),
        in_specs=[row, pl.BlockSpec((HALO, w3), _halo_next(r, t)), row, pl.BlockSpec((HALO, w3), _halo_prev(r)),
                  pl.BlockSpec((DN_TAPS, w3), lambda i: (0, 0)), ANY],
        out_specs=[row, pl.BlockSpec((8, w3), lambda i: (0, 0))],
        out_shape=[jax.ShapeDtypeStruct(dproj.shape, dproj.dtype), jax.ShapeDtypeStruct((8, w3), F32)],
        input_output_aliases={5: 0}, compiler_params=_cparams("arbitrary"),
    )(dconv, dconv, proj, proj, conv_w, dproj)


def _ba_terms(x, pv, heads):
    lane = lax.broadcasted_iota(jnp.int32, x.shape, 1)
    is_b = lane < heads
    is_a = jnp.logical_and(lane >= heads, lane < 2 * heads)
    beta = _sig(x)
    z = x + pv[1:2, :]
    nexp = -jnp.exp(pv[0:1, :])
    sp = jnp.maximum(z, 0.0) + jnp.log1p(jnp.exp(-jnp.abs(z)))
    return is_b, is_a, beta, z, nexp, nexp * sp


def _ba_fwd(pba, pvec, heads, name):
    t = pba.shape[0]
    r = _row_tile(t)

    def body(x_ref, pv_ref, o_ref):
        is_b, is_a, beta, _, _, g = _ba_terms(x_ref[...], pv_ref[...], heads)
        rows = pl.program_id(0) * r + lax.broadcasted_iota(jnp.int32, (r, 1), 0)
        o_ref[...] = jnp.where(rows >= PAD_ROWS, jnp.where(is_b, beta, jnp.where(is_a, g, 0.0)), 0.0)

    row = pl.BlockSpec((r, LANES), lambda i: (i, 0))
    return pl.pallas_call(
        body, name=name, grid=(t // r,), in_specs=[row, pl.BlockSpec((8, LANES), lambda i: (0, 0))],
        out_specs=row, out_shape=jax.ShapeDtypeStruct((t, LANES), F32), compiler_params=_cparams("parallel"),
    )(pba, pvec)


def _ba_bwd(pba, pvec, dbg, heads, name):
    t = pba.shape[0]
    r = _row_tile(t)

    def body(x_ref, pv_ref, d_ref, o_ref, g_ref):
        @pl.when(pl.program_id(0) == 0)
        def _():
            g_ref[...] = jnp.zeros_like(g_ref)

        is_b, is_a, beta, z, nexp, g = _ba_terms(x_ref[...], pv_ref[...], heads)
        rows = pl.program_id(0) * r + lax.broadcasted_iota(jnp.int32, (r, 1), 0)
        dd = jnp.where(rows >= PAD_ROWS, d_ref[...], 0.0)
        dz = dd * nexp * _sig(z)
        o_ref[...] = jnp.where(is_b, dd * beta * (1.0 - beta), jnp.where(is_a, dz, 0.0)).astype(o_ref.dtype)
        g_ref[0:1, :] += jnp.sum(jnp.where(is_a, dd * g, 0.0), axis=0, keepdims=True)
        g_ref[1:2, :] += jnp.sum(jnp.where(is_a, dz, 0.0), axis=0, keepdims=True)

    row = pl.BlockSpec((r, LANES), lambda i: (i, 0))
    par = pl.BlockSpec((8, LANES), lambda i: (0, 0))
    return pl.pallas_call(
        body, name=name, grid=(t // r,), in_specs=[row, par, row], out_specs=[row, par],
        out_shape=[jax.ShapeDtypeStruct((t, LANES), MM_DTYPE), jax.ShapeDtypeStruct((8, LANES), F32)],
        compiler_params=_cparams("arbitrary"),
    )(pba, pvec, dbg)


def _chunk_consts():
    ri = lax.broadcasted_iota(jnp.int32, (CHUNK, CHUNK), 0)
    ci = lax.broadcasted_iota(jnp.int32, (CHUNK, CHUNK), 1)
    return ri >= ci, ri > ci, (ri == ci).astype(F32), (ri >= ci).astype(F32), (ri <= ci).astype(F32)


def _chunk_decay(gc, gr, incl):
    return jnp.where(incl, jnp.exp(jnp.where(incl, gc - gr, 0.0)), 0.0)


def _unit_lower_inverse(n, eye):
    x = [eye + a for a in n]
    p = list(n)
    for _ in range(5):
        p = [_dot(a, a) for a in p]
        x = [a + _dot(a, b) for a, b in zip(x, p)]
    r = [eye - a + _dot3(b, a) for a, b in zip(x, n)]
    return [a + _dot(a, b) for a, b in zip(x, r)]


def _dn_fwd(qkvn, bg, g_row, d, heads, name):
    t = qkvn.shape[0]
    nc = t // CHUNK

    def body(qkv_ref, bg_ref, gr_ref, o_ref, sall_ref, ainv_ref, s_ref):
        @pl.when(pl.program_id(0) == 0)
        def _():
            s_ref[...] = jnp.zeros_like(s_ref)

        incl, strict, eye, tril, triu = _chunk_consts()
        bgv = bg_ref[...]
        gc_all = _dot(tril, bgv, NN, HI)
        gr_all = _dot(gr_ref[0], triu, NN, HI)
        hs = range(heads)
        q = [qkv_ref[:, h * DK:(h + 1) * DK] for h in hs]
        k = [qkv_ref[:, d + h * DK:d + (h + 1) * DK] for h in hs]
        v = [qkv_ref[:, 2 * d + h * DK:2 * d + (h + 1) * DK] for h in hs]
        kb = [a.astype(MM_DTYPE) for a in k]
        beta = [bgv[:, h:h + 1] for h in hs]
        gc = [gc_all[:, heads + h:heads + h + 1] for h in hs]
        decay = [_chunk_decay(gc[h], gr_all[h:h + 1, :], incl) for h in hs]
        eg = [jnp.exp(a) for a in gc]
        n = [jnp.where(strict, -(beta[h] * _dot(kb[h], kb[h], NT) * decay[h]), 0.0) for h in hs]
        x = _unit_lower_inverse(n, eye)
        sol = [_dot3(x[h], jnp.concatenate([v[h] * beta[h], k[h] * (beta[h] * eg[h])], axis=1)) for h in hs]
        attn = [_dot(q[h], kb[h], NT) * decay[h] for h in hs]
        s = [s_ref[h] for h in hs]
        sb = [a.astype(MM_DTYPE) for a in s]
        wv = [sol[h][:, :DK] - _dot(sol[h][:, DK:], sb[h]) for h in hs]
        o = [_dot(q[h] * eg[h], sb[h]) + _dot(attn[h], wv[h]) for h in hs]
        glast = [a[CHUNK - 1:CHUNK, :] for a in gc]
        s_new = [s[h] * jnp.exp(glast[h]) + _dot(k[h] * jnp.exp(glast[h] - gc[h]), wv[h], TN) for h in hs]
        for h in hs:
            o_ref[:, h * DK:(h + 1) * DK] = o[h]
            sall_ref[0, h] = s[h]
            ainv_ref[0, h] = x[h]
            s_ref[h] = s_new[h]

    return pl.pallas_call(
        body, name=name, grid=(nc,),
        in_specs=[pl.BlockSpec((CHUNK, 3 * d), lambda i: (i, 0)), pl.BlockSpec((CHUNK, LANES), lambda i: (i, 0)),
                  pl.BlockSpec((1, heads, CHUNK), lambda i: (i, 0, 0))],
        out_specs=[pl.BlockSpec((CHUNK, d), lambda i: (i, 0)), pl.BlockSpec((1, heads, DK, DK), lambda i: (i, 0, 0, 0)),
                   pl.BlockSpec((1, heads, CHUNK, CHUNK), lambda i: (i, 0, 0, 0))],
        out_shape=[jax.ShapeDtypeStruct((t, d), F32), jax.ShapeDtypeStruct((nc, heads, DK, DK), F32),
                   jax.ShapeDtypeStruct((nc, heads, CHUNK, CHUNK), F32)],
        scratch_shapes=[pltpu.VMEM((heads, DK, DK), F32)], compiler_params=_cparams("arbitrary"),
    )(qkvn, bg, g_row)


def _dn_bwd(qkvn, bg, g_row, do, sall, ainv, d, heads, name):
    t = qkvn.shape[0]
    nc = t // CHUNK
    rev = lambda i: nc - 1 - i

    def body(qkv_ref, bg_ref, gr_ref, do_ref, sall_ref, ainv_ref, dqkv_ref, dbg_ref, dgr_ref, ds_ref):
        @pl.when(pl.program_id(0) == 0)
        def _():
            ds_ref[...] = jnp.zeros_like(ds_ref)

        incl, strict, eye, tril, triu = _chunk_consts()
        bgv = bg_ref[...]
        gc_all = _dot(tril, bgv, NN, HI)
        gr_all = _dot(gr_ref[0], triu, NN, HI)
        lane = lax.broadcasted_iota(jnp.int32, (CHUNK, LANES), 1)
        hrow = lax.broadcasted_iota(jnp.int32, (heads, CHUNK), 0)
        last_row = lax.broadcasted_iota(jnp.int32, (CHUNK, 1), 0) == CHUNK - 1
        dbeta_slab = jnp.zeros((CHUNK, LANES), F32)
        dgc_slab = jnp.zeros((CHUNK, LANES), F32)
        dgr_slab = jnp.zeros((heads, CHUNK), F32)
        hs = range(heads)
        cols = lambda ref, off: [ref[:, off + h * DK:off + (h + 1) * DK] for h in hs]
        qs, ks, vs, douts = cols(qkv_ref, 0), cols(qkv_ref, d), cols(qkv_ref, 2 * d), cols(do_ref, 0)
        ss, xs_, dsns = [sall_ref[0, h] for h in hs], [ainv_ref[0, h] for h in hs], [ds_ref[h] for h in hs]

        def recompute(h, z):
            z.q, z.k, z.v, z.dout, z.s, z.x, z.dsn = qs[h], ks[h], vs[h], douts[h], ss[h], xs_[h], dsns[h]
            z.kb, z.qb, z.sb, z.dsnb, z.doutb = (a.astype(MM_DTYPE) for a in (z.k, z.q, z.s, z.dsn, z.dout))
            z.beta = bgv[:, h:h + 1]
            gc = gc_all[:, heads + h:heads + h + 1]
            z.decay = _chunk_decay(gc, gr_all[h:h + 1, :], incl)
            z.eg = jnp.exp(gc)
            glast = gc[CHUNK - 1:CHUNK, :]
            z.eglast = jnp.exp(glast)
            z.ek = jnp.exp(glast - gc)
            z.kk = _dot(z.kb, z.kb, NT)
            z.qk = _dot(z.qb, z.kb, NT)
            sol = _dot3(z.x, jnp.concatenate([z.v * z.beta, z.k * (z.beta * z.eg)], axis=1))
            z.u, z.wb = sol[:, :DK], sol[:, DK:].astype(MM_DTYPE)
            z.attn = z.qk * z.decay
            z.qg = z.q * z.eg
            z.kend = z.k * z.ek

        def pseudo_values(h, z):
            z.wvb = (z.u - _dot(z.wb, z.sb)).astype(MM_DTYPE)

        def scan_step(h, z):
            z.d_wv = _dot(z.attn, z.doutb, TN) + _dot(z.kend, z.dsnb)
            z.d_attn = _dot(z.doutb, z.wvb, NT)
            z.d_qg = _dot(z.doutb, z.sb, NT)
            z.d_kend = _dot(z.wvb, z.dsnb, NT)
            z.ds_new = _dot(z.qg, z.doutb, TN) + z.eglast * z.dsn
            z.d_glast = z.eglast * _sum_all(z.dsn * z.s) + _sum_all(z.d_kend * z.kend)

        def state_terms(h, z):
            z.ds_new = z.ds_new - _dot(z.wb, z.d_wv, TN)
            z.d_w = -_dot(z.d_wv, z.sb, NT)

        def solve_transpose(h, z):
            d_rhs = _dot3(z.x, jnp.concatenate([z.d_wv, z.d_w], axis=1), TN)
            z.d_ru, z.d_rw = d_rhs[:, :DK], d_rhs[:, DK:]

        def lower_terms(h, z):
            z.d_low = jnp.where(strict, -(_dot(z.d_ru, z.u, NT) + _dot(z.d_rw, z.wb, NT)), 0.0)

        def outputs(h, z):
            rw_k = jnp.sum(z.d_rw * z.k, axis=-1, keepdims=True)
            z.dbeta = (jnp.sum(z.d_ru * z.v, axis=-1, keepdims=True) + rw_k * z.eg
                       + jnp.sum(z.d_low * z.kk * z.decay, axis=-1, keepdims=True))
            z.dv = z.d_ru * z.beta
            d_kk = (z.d_low * z.beta * z.decay).astype(MM_DTYPE)
            d_qk = (z.d_attn * z.decay).astype(MM_DTYPE)
            z.dk = (z.d_rw * (z.beta * z.eg) + _dot(d_kk, z.kb) + _dot(d_kk, z.kb, TN) + _dot(d_qk, z.qb, TN)
                    + z.d_kend * z.ek)
            z.dq = _dot(d_qk, z.kb) + z.d_qg * z.eg
            e = (z.d_low * z.beta * z.kk + z.d_attn * z.qk) * z.decay
            z.dgc = (rw_k * z.beta * z.eg + jnp.sum(e, axis=-1, keepdims=True)
                     + jnp.sum(z.d_qg * z.qg, axis=-1, keepdims=True) - jnp.sum(z.d_kend * z.kend, axis=-1, keepdims=True)
                     + jnp.where(last_row, z.d_glast, 0.0))
            z.dgr = -jnp.sum(e, axis=0, keepdims=True)

        class _Head:
            pass

        st = [_Head() for _ in hs]
        for phase in (recompute, pseudo_values, scan_step, state_terms, solve_transpose, lower_terms, outputs):
            for h in hs:
                phase(h, st[h])
        res = [(z.dq, z.dk, z.dv, z.ds_new, z.dbeta, z.dgc, z.dgr) for z in st]
        for h, (dq, dk, dv, ds_new, dbeta, dgc, dgr) in enumerate(res):
            dqkv_ref[:, h * DK:(h + 1) * DK] = dq
            dqkv_ref[:, d + h * DK:d + (h + 1) * DK] = dk
            dqkv_ref[:, 2 * d + h * DK:2 * d + (h + 1) * DK] = dv
            ds_ref[h] = ds_new
            dbeta_slab = jnp.where(lane == h, dbeta, dbeta_slab)
            dgc_slab = jnp.where(lane == heads + h, dgc, dgc_slab)
            dgr_slab = jnp.where(hrow == h, dgr, dgr_slab)
        dbg_ref[...] = dbeta_slab + _dot(triu, dgc_slab, NN, HI)
        dgr_ref[0] = _dot(dgr_slab, tril, NN, HI)

    return pl.pallas_call(
        body, name=name, grid=(nc,),
        in_specs=[pl.BlockSpec((CHUNK, 3 * d), lambda i: (rev(i), 0)), pl.BlockSpec((CHUNK, LANES), lambda i: (rev(i), 0)),
                  pl.BlockSpec((1, heads, CHUNK), lambda i: (rev(i), 0, 0)), pl.BlockSpec((CHUNK, d), lambda i: (rev(i), 0)),
                  pl.BlockSpec((1, heads, DK, DK), lambda i: (rev(i), 0, 0, 0)),
                  pl.BlockSpec((1, heads, CHUNK, CHUNK), lambda i: (rev(i), 0, 0, 0))],
        out_specs=[pl.BlockSpec((CHUNK, 3 * d), lambda i: (rev(i), 0)), pl.BlockSpec((CHUNK, LANES), lambda i: (rev(i), 0)),
                   pl.BlockSpec((1, heads, CHUNK), lambda i: (rev(i), 0, 0))],
        out_shape=[jax.ShapeDtypeStruct((t, 3 * d), F32), jax.ShapeDtypeStruct((t, LANES), F32),
                   jax.ShapeDtypeStruct((nc, heads, CHUNK), F32)],
        scratch_shapes=[pltpu.VMEM((heads, DK, DK), F32)], compiler_params=_cparams("arbitrary"),
    )(qkvn, bg, g_row, do, sall, ainv)


def _dn_post(o, proj, w, d, heads, name):
    t = o.shape[0]
    r = _row_tile(t)

    def body(o_ref, z_ref, w_ref, y_ref):
        for h in range(heads):
            sl = slice(h * DK, (h + 1) * DK)
            oh, z = o_ref[:, sl], z_ref[:, sl]
            rs = lax.rsqrt(jnp.mean(oh * oh, axis=-1, keepdims=True) + EPS)
            y_ref[:, sl] = (oh * rs * w_ref[...] * (z * _sig(z))).astype(y_ref.dtype)

    row = pl.BlockSpec((r, d), lambda i: (i, 0))
    return pl.pallas_call(
        body, name=name, grid=(t // r,),
        in_specs=[row, pl.BlockSpec((r, d), lambda i: (i, 3)), pl.BlockSpec((1, DK), lambda i: (0, 0))],
        out_specs=row, out_shape=jax.ShapeDtypeStruct((t, d), MM_DTYPE), compiler_params=_cparams("parallel"),
    )(o, proj, w)


def _dn_post_bwd(o, proj, w, dy, dproj, d, heads, name):
    t = o.shape[0]
    r = _row_tile(t)

    def body(o_ref, z_ref, w_ref, dy_ref, _, do_ref, dz_ref, gw_ref):
        @pl.when(pl.program_id(0) == 0)
        def _():
            gw_ref[...] = jnp.zeros_like(gw_ref)

        gw = jnp.zeros((1, DK), F32)
        for h in range(heads):
            sl = slice(h * DK, (h + 1) * DK)
            oh, z, dyh = o_ref[:, sl], z_ref[:, sl], dy_ref[:, sl]
            sz = _sig(z)
            rs = lax.rsqrt(jnp.mean(oh * oh, axis=-1, keepdims=True) + EPS)
            xh = oh * rs
            dn = dyh * (z * sz)
            dz_ref[:, sl] = (dyh * (xh * w_ref[...]) * _dsilu(z, sz)).astype(dz_ref.dtype)
            dxh = dn * w_ref[...]
            do_ref[:, sl] = rs * (dxh - xh * jnp.mean(dxh * xh, axis=-1, keepdims=True))
            gw = gw + jnp.sum(dn * xh, axis=0, keepdims=True)
        gw_ref[0:1, :] += gw

    row = pl.BlockSpec((r, d), lambda i: (i, 0))
    za = pl.BlockSpec((r, d), lambda i: (i, 3))
    return pl.pallas_call(
        body, name=name, grid=(t // r,),
        in_specs=[row, za, pl.BlockSpec((1, DK), lambda i: (0, 0)), row, ANY],
        out_specs=[row, za, pl.BlockSpec((8, DK), lambda i: (0, 0))],
        out_shape=[jax.ShapeDtypeStruct((t, d), F32), jax.ShapeDtypeStruct(dproj.shape, dproj.dtype),
                   jax.ShapeDtypeStruct((8, DK), F32)],
        input_output_aliases={4: 1}, compiler_params=_cparams("arbitrary"),
    )(o, proj, w, dy, dproj)


def _glu_window(g_ref, gh_ref, i, d, cs):
    cs2 = slice(d + cs.start, d + cs.stop)
    cur = g_ref[:, cs] * _sig(g_ref[:, cs2])
    prev = gh_ref[HALO - 32:HALO, cs] * _sig(gh_ref[HALO - 32:HALO, cs2])
    return jnp.concatenate([jnp.where(i > 0, prev, 0.0), cur], axis=0)


def _ln_stats(c1):
    mu = jnp.mean(c1, axis=-1, keepdims=True)
    cc = c1 - mu
    return cc * lax.rsqrt(jnp.mean(cc * cc, axis=-1, keepdims=True) + EPS), lax.rsqrt(jnp.mean(cc * cc, axis=-1, keepdims=True) + EPS)


def _cf_fwd(proj, dw_w, dw_b, ln_w, ln_b, d, name):
    t = proj.shape[0]
    r = _conv_rows(t)

    def body(g_ref, gh_ref, zb_ref, w_ref, b_ref, lw_ref, lb_ref, c1_ref, c3_ref):
        for cb in range(d // LANES):
            cs = slice(cb * LANES, (cb + 1) * LANES)
            win = _glu_window(g_ref, gh_ref, pl.program_id(0), d, cs)
            acc = jnp.broadcast_to(b_ref[:, cs], (r, LANES))
            for sub in range(8):
                rot = _shift(win, sub)
                for s in range(sub, CF_TAPS, 8):
                    acc = acc + rot[32 - (s - sub):32 - (s - sub) + r, :] * w_ref[CF_TAPS - 1 - s:CF_TAPS - s, cs]
            c1_ref[:, cs] = acc
        xh, _ = _ln_stats(c1_ref[...])
        ln = xh * lw_ref[...] + lb_ref[...]
        zb = zb_ref[...]
        c3_ref[...] = ((ln * _sig(ln)) * (zb * _sig(zb))).astype(c3_ref.dtype)

    row = pl.BlockSpec((r, d), lambda i: (i, 0))
    par = pl.BlockSpec((1, d), lambda i: (0, 0))
    return pl.pallas_call(
        body, name=name, grid=(t // r,),
        in_specs=[pl.BlockSpec((r, 2 * d), lambda i: (i, 2)), pl.BlockSpec((HALO, 2 * d), lambda i: (_halo_prev(r)(i)[0], 2)),
                  pl.BlockSpec((r, d), lambda i: (i, 8)), pl.BlockSpec((32, d), lambda i: (0, 0)), par, par, par],
        out_specs=[row, row],
        out_shape=[jax.ShapeDtypeStruct((t, d), F32), jax.ShapeDtypeStruct((t, d), MM_DTYPE)],
        compiler_params=_cparams("parallel"),
    )(proj, proj, proj, dw_w, dw_b, ln_w, ln_b)


def _cf_bwd1(c1, proj, ln_w, ln_b, dc3, dproj, d, name):
    t = c1.shape[0]
    r = _row_tile(t)

    def body(c1_ref, zb_ref, lw_ref, lb_ref, dc3_ref, _, dc1_ref, dzb_ref, g_ref):
        @pl.when(pl.program_id(0) == 0)
        def _():
            g_ref[...] = jnp.zeros_like(g_ref)

        xh, rs = _ln_stats(c1_ref[...])
        ln = xh * lw_ref[...] + lb_ref[...]
        sl, zb, dc3v = _sig(ln), zb_ref[...], dc3_ref[...]
        szb = _sig(zb)
        dzb_ref[...] = (dc3v * (ln * sl) * _dsilu(zb, szb)).astype(dzb_ref.dtype)
        dln = dc3v * (zb * szb) * _dsilu(ln, sl)
        dxh = dln * lw_ref[...]
        dc1 = rs * (dxh - jnp.mean(dxh, axis=-1, keepdims=True) - xh * jnp.mean(dxh * xh, axis=-1, keepdims=True))
        dc1_ref[...] = dc1
        g_ref[0:1, :] += jnp.sum(dln * xh, axis=0, keepdims=True)
        g_ref[1:2, :] += jnp.sum(dln, axis=0, keepdims=True)
        g_ref[2:3, :] += jnp.sum(dc1, axis=0, keepdims=True)

    row = pl.BlockSpec((r, d), lambda i: (i, 0))
    zbs = pl.BlockSpec((r, d), lambda i: (i, 8))
    par = pl.BlockSpec((1, d), lambda i: (0, 0))
    return pl.pallas_call(
        body, name=name, grid=(t // r,), in_specs=[row, zbs, par, par, row, ANY],
        out_specs=[row, zbs, pl.BlockSpec((8, d), lambda i: (0, 0))],
        out_shape=[jax.ShapeDtypeStruct((t, d), F32), jax.ShapeDtypeStruct(dproj.shape, dproj.dtype),
                   jax.ShapeDtypeStruct((8, d), F32)],
        input_output_aliases={5: 1}, compiler_params=_cparams("arbitrary"),
    )(c1, proj, ln_w, ln_b, dc3, dproj)


def _cf_bwd2(dc1, proj, dw_w, dproj, d, name):
    t = dc1.shape[0]
    r = _conv_rows(t)
    last = t // r - 1

    def body(dc_ref, dcn_ref, g_ref, gh_ref, w_ref, _, dg_ref, gw_ref):
        i = pl.program_id(0)

        @pl.when(i == 0)
        def _():
            gw_ref[...] = jnp.zeros_like(gw_ref)

        for cb in range(d // LANES):
            cs = slice(cb * LANES, (cb + 1) * LANES)
            cs2 = slice(d + cs.start, d + cs.stop)
            dcur = dc_ref[:, cs]
            dwin = jnp.concatenate([dcur, jnp.where(i < last, dcn_ref[0:32, cs], 0.0)], axis=0)
            win = _glu_window(g_ref, gh_ref, i, d, cs)
            acc = None
            for sub in range(8):
                drot = _shift(dwin, -sub)
                rot = _shift(win, sub)
                for s in range(sub, CF_TAPS, 8):
                    j = CF_TAPS - 1 - s
                    term = drot[s - sub:s - sub + r, :] * w_ref[j:j + 1, cs]
                    acc = term if acc is None else acc + term
                    gw_ref[j:j + 1, cs] += jnp.sum(dcur * rot[32 - (s - sub):32 - (s - sub) + r, :], axis=0, keepdims=True)
            ga, sb = g_ref[:, cs], _sig(g_ref[:, cs2])
            dg_ref[:, cs] = (acc * sb).astype(dg_ref.dtype)
            dg_ref[:, cs2] = (acc * ga * sb * (1.0 - sb)).astype(dg_ref.dtype)

    row = pl.BlockSpec((r, d), lambda i: (i, 0))
    glu = pl.BlockSpec((r, 2 * d), lambda i: (i, 2))
    return pl.pallas_call(
        body, name=name, grid=(t // r,),
        in_specs=[row, pl.BlockSpec((HALO, d), _halo_next(r, t)), glu,
                  pl.BlockSpec((HALO, 2 * d), lambda i: (_halo_prev(r)(i)[0], 2)), pl.BlockSpec((32, d), lambda i: (0, 0)), ANY],
        out_specs=[glu, pl.BlockSpec((32, d), lambda i: (0, 0))],
        out_shape=[jax.ShapeDtypeStruct(dproj.shape, dproj.dtype), jax.ShapeDtypeStruct((32, d), F32)],
        input_output_aliases={5: 0}, compiler_params=_cparams("arbitrary"),
    )(dc1, dc1, proj, proj, dw_w, dproj)


def _merge(proj, ya, yb, b, d, name):
    t = ya.shape[0]
    r = _row_tile(t)

    def body(g_ref, ya_ref, yb_ref, b_ref, m_ref):
        m_ref[...] = (_sig(g_ref[:, :d]) * ya_ref[...] + _sig(g_ref[:, d:]) * (yb_ref[...] + b_ref[...])).astype(m_ref.dtype)

    row = pl.BlockSpec((r, d), lambda i: (i, 0))
    return pl.pallas_call(
        body, name=name, grid=(t // r,),
        in_specs=[pl.BlockSpec((r, 2 * d), lambda i: (i, 3)), row, row, pl.BlockSpec((1, d), lambda i: (0, 0))],
        out_specs=row, out_shape=jax.ShapeDtypeStruct((t, d), MM_DTYPE), compiler_params=_cparams("parallel"),
    )(proj, ya, yb, b)


def _merge_bwd(proj, ya, yb, b, dm, d, name):
    t = ya.shape[0]
    r = _row_tile(t)

    def body(g_ref, ya_ref, yb_ref, b_ref, dm_ref, dya_ref, dyb_ref, dg_ref, gb_ref):
        @pl.when(pl.program_id(0) == 0)
        def _():
            gb_ref[...] = jnp.zeros_like(gb_ref)

        sa, sb, dmv = _sig(g_ref[:, :d]), _sig(g_ref[:, d:]), dm_ref[...]
        dyb = dmv * sb
        dya_ref[...] = (dmv * sa).astype(dya_ref.dtype)
        dyb_ref[...] = dyb.astype(dyb_ref.dtype)
        dg_ref[:, :d] = (dmv * ya_ref[...] * sa * (1.0 - sa)).astype(dg_ref.dtype)
        dg_ref[:, d:] = (dmv * (yb_ref[...] + b_ref[...]) * sb * (1.0 - sb)).astype(dg_ref.dtype)
        gb_ref[0:1, :] += jnp.sum(dyb, axis=0, keepdims=True)

    row = pl.BlockSpec((r, d), lambda i: (i, 0))
    gate = pl.BlockSpec((r, 2 * d), lambda i: (i, 3))
    return pl.pallas_call(
        body, name=name, grid=(t // r,),
        in_specs=[gate, row, row, pl.BlockSpec((1, d), lambda i: (0, 0)), row],
        out_specs=[row, row, gate, pl.BlockSpec((8, d), lambda i: (0, 0))],
        out_shape=[jax.ShapeDtypeStruct((t, d), MM_DTYPE), jax.ShapeDtypeStruct((t, d), MM_DTYPE),
                   jax.ShapeDtypeStruct((t, 9 * d), MM_DTYPE), jax.ShapeDtypeStruct((8, d), F32)],
        compiler_params=_cparams("arbitrary"),
    )(proj, ya, yb, b, dm)


def _place():
    return lax.axis_index("x"), lax.axis_index("y"), lax.axis_index("c")


def _slot(p):
    return 4 * p[0] + 2 * p[1] + p[2]


def _allgather(blocks, name):
    n = len(blocks)

    def body(*refs):
        ins, outs = refs[:n], refs[n:2 * n]
        send_sems, recv_sems, local_sems = refs[2 * n:]
        x, y, c = _place()
        me, sibling = (x, y, c), (x, y, 1 - c)
        chips = [(1 - x, y), (x, 1 - y), (1 - x, 1 - y)]

        def copy(a, k, block, to, src=None):
            dst = outs[a].at[_slot(block)]
            return pltpu.make_async_remote_copy(
                src_ref=dst if src is None else src, dst_ref=dst, send_sem=send_sems.at[7 * a + k],
                recv_sem=recv_sems.at[7 * a + k], device_id=to, device_id_type=MESH)

        mine = [pltpu.make_async_copy(ins[a], outs[a].at[_slot(me)], local_sems.at[a]) for a in range(n)]
        for cp in mine:
            cp.start()
        first = []
        for a in range(n):
            first.append(copy(a, 0, me, sibling, src=ins[a]))
            first += [copy(a, 1 + j, me, (*chip, c), src=ins[a]) for j, chip in enumerate(chips)]
        for cp in first:
            cp.start()
        passed = []
        for j, chip in enumerate(chips):
            for a in range(n):
                copy(a, 1 + j, (*chip, c), me).wait_recv()
                cp = copy(a, 4 + j, (*chip, c), sibling)
                cp.start()
                passed.append(cp)
        for a in range(n):
            copy(a, 0, sibling, me).wait_recv()
            for j, chip in enumerate(chips):
                copy(a, 4 + j, (*chip, 1 - c), me).wait_recv()
        for cp in first + passed:
            cp.wait_send()
        for cp in mine:
            cp.wait()

    return pl.pallas_call(
        body, name=name, in_specs=[ANY] * n, out_specs=[ANY] * n,
        out_shape=[jax.ShapeDtypeStruct((8, *b.shape), b.dtype) for b in blocks],
        scratch_shapes=[pltpu.SemaphoreType.DMA((7 * n,)), pltpu.SemaphoreType.DMA((7 * n,)), pltpu.SemaphoreType.DMA((n,))],
    )(*blocks)


_FLIPS = [(0, 0, 1), (0, 1, 0), (0, 1, 1), (1, 0, 0), (1, 0, 1), (1, 1, 0), (1, 1, 1)]


def _alltoall(arrays, name):
    n = len(arrays)

    def body(*refs):
        ins, outs = refs[:n], refs[n:2 * n]
        send_sems, recv_sems, local_sems = refs[2 * n:]
        me = _place()
        peers = [tuple(1 - me[ax] if f[ax] else me[ax] for ax in range(3)) for f in _FLIPS]

        def copy(a, k):
            return pltpu.make_async_remote_copy(
                src_ref=ins[a].at[_slot(peers[k])], dst_ref=outs[a].at[_slot(me)], send_sem=send_sems.at[7 * a + k],
                recv_sem=recv_sems.at[7 * a + k], device_id=peers[k], device_id_type=MESH)

        def arrival(a, k):
            return pltpu.make_async_remote_copy(
                src_ref=ins[a].at[_slot(me)], dst_ref=outs[a].at[_slot(peers[k])], send_sem=send_sems.at[7 * a + k],
                recv_sem=recv_sems.at[7 * a + k], device_id=peers[k], device_id_type=MESH)

        mine = [pltpu.make_async_copy(ins[a].at[_slot(me)], outs[a].at[_slot(me)], local_sems.at[a]) for a in range(n)]
        for cp in mine:
            cp.start()
        sent = [copy(a, k) for a in range(n) for k in range(7)]
        for cp in sent:
            cp.start()
        for a in range(n):
            for k in range(7):
                arrival(a, k).wait_recv()
        for cp in sent:
            cp.wait_send()
        for cp in mine:
            cp.wait()

    return pl.pallas_call(
        body, name=name, in_specs=[ANY] * n, out_specs=[ANY] * n,
        out_shape=[jax.ShapeDtypeStruct(a.shape, a.dtype) for a in arrays],
        scratch_shapes=[pltpu.SemaphoreType.DMA((7 * n,)), pltpu.SemaphoreType.DMA((7 * n,)), pltpu.SemaphoreType.DMA((n,))],
    )(*arrays)


def _sibling_exchange(arrays, name):
    n = len(arrays)

    def body(*refs):
        ins, outs = refs[:n], refs[n:2 * n]
        send_sems, recv_sems, local_sems = refs[2 * n:]
        x, y, c = _place()

        def copy(a, slot):
            return pltpu.make_async_remote_copy(
                src_ref=ins[a], dst_ref=outs[a].at[slot], send_sem=send_sems.at[a], recv_sem=recv_sems.at[a],
                device_id=(x, y, 1 - c), device_id_type=MESH)

        mine = [pltpu.make_async_copy(ins[a], outs[a].at[c], local_sems.at[a]) for a in range(n)]
        for cp in mine:
            cp.start()
        sent = [copy(a, c) for a in range(n)]
        for cp in sent:
            cp.start()
        for a in range(n):
            copy(a, 1 - c).wait_recv()
        for cp in sent:
            cp.wait_send()
        for cp in mine:
            cp.wait()

    vmem = pl.BlockSpec(memory_space=pltpu.VMEM)
    return pl.pallas_call(
        body, name=name, in_specs=[vmem] * n, out_specs=[vmem] * n,
        compiler_params=pltpu.CompilerParams(vmem_limit_bytes=VMEM_LIMIT),
        out_shape=[jax.ShapeDtypeStruct((2, *a.shape), a.dtype) for a in arrays],
        scratch_shapes=[pltpu.SemaphoreType.DMA((n,)), pltpu.SemaphoreType.DMA((n,)), pltpu.SemaphoreType.DMA((n,))],
    )(*arrays)


def _sum8(parts, name):
    _, rr, cc = parts.shape
    r = _pick(rr, (128, 64, 32, 16, 8))

    def body(p_ref, o_ref):
        acc = p_ref[0].astype(F32)
        for j in range(1, 8):
            acc = acc + p_ref[j].astype(F32)
        o_ref[...] = acc

    return pl.pallas_call(
        body, name=name, grid=(rr // r,), in_specs=[pl.BlockSpec((8, r, cc), lambda i: (0, i, 0))],
        out_specs=pl.BlockSpec((r, cc), lambda i: (i, 0)), out_shape=jax.ShapeDtypeStruct((rr, cc), F32),
        compiler_params=_cparams("parallel"),
    )(parts)


def _adamw(g, w, m, v, name):
    rr, cc = g.shape
    r = _pick(rr, (128, 64, 32, 16, 8))

    def body(g_ref, w_ref, m_ref, v_ref, d_ref, nm_ref, nv_ref):
        gv = g_ref[...]
        nm = ADAM_B1 * m_ref[...] + (1.0 - ADAM_B1) * gv
        nv = ADAM_B2 * v_ref[...] + (1.0 - ADAM_B2) * (gv * gv)
        m_hat = nm / (1.0 - ADAM_B1 ** ADAM_STEP)
        v_hat = nv / (1.0 - ADAM_B2 ** ADAM_STEP)
        d_ref[...] = -ADAM_LR * (m_hat / (jnp.sqrt(v_hat) + ADAM_EPS) + ADAM_WD * w_ref[...])
        nm_ref[...] = nm
        nv_ref[...] = nv

    blk = pl.BlockSpec((r, cc), lambda i: (i, 0))
    return pl.pallas_call(
        body, name=name, grid=(rr // r,), in_specs=[blk] * 4, out_specs=[blk] * 3,
        out_shape=[jax.ShapeDtypeStruct((rr, cc), F32)] * 3, compiler_params=_cparams("parallel"),
    )(g, w, m, v)


def _columns(sources, start, stop):
    out, at = [], 0
    for arr, off, width in sources:
        lo, hi = max(start, at), min(stop, at + width)
        if lo < hi:
            out.append(arr[:, off + lo - at:off + hi - at])
        at += width
    return out


def _flat_pack(parts, width):
    flat = jnp.concatenate([p.reshape(-1) for p in parts])
    total = -(-flat.shape[0] // (8 * width)) * (8 * width)
    return jnp.pad(flat, (0, total - flat.shape[0])).reshape(-1, width)


def _flat_unpack(pack, shapes):
    flat = pack.reshape(-1)
    out, at = [], 0
    for s in shapes:
        size = 1
        for e in s:
            size *= e
        out.append(flat[at:at + size].reshape(s))
        at += size
    return out


def kernel(x, meta, norm_w, w_in, conv_qkv_w, a_log, dt_bias, dn_norm_w, w_dn_out, dw_w, dw_b, ln_w, ln_b, w_cf_out, b_cf_out, w_o, final_norm_w, loss_target, m_meta, m_norm_w, m_w_in, m_conv_qkv_w, m_a_log, m_dt_bias, m_dn_norm_w, m_w_dn_out, m_dw_w, m_dw_b, m_ln_w, m_ln_b, m_w_cf_out, m_b_cf_out, m_w_o, m_final_norm_w, v_meta, v_norm_w, v_w_in, v_conv_qkv_w, v_a_log, v_dt_bias, v_dn_norm_w, v_w_dn_out, v_dw_w, v_dw_b, v_ln_w, v_ln_b, v_w_cf_out, v_b_cf_out, v_w_o, v_final_norm_w):
    d = x.shape[-1]
    heads = a_log.shape[-1]
    seq = x.shape[1]
    t = HDR + seq
    nc = t // CHUNK
    in_w = 9 * d + 2 * heads
    assert heads * DK == d and seq % LANES == 0 and w_in.shape[-1] * 4 == in_w
    xi, yi, ci = _place()
    shard = 2 * xi + yi
    ds = d // 4

    w_in_half = lax.dynamic_slice_in_dim(w_in[0].astype(MM_DTYPE), ci * (d // 2), d // 2, axis=0)
    w3_half = lax.dynamic_slice_in_dim(jnp.stack([w_dn_out[0], w_cf_out[0], w_o[0]]).astype(MM_DTYPE), ci * (d // 8), d // 8, axis=1)
    small = jnp.concatenate([
        jnp.pad(conv_qkv_w[0], ((0, 4), (0, 0))), jnp.pad(meta, ((0, 0), (0, 2 * ds))),
        jnp.pad(dw_w[0], ((0, 1), (0, 2 * ds)))], axis=0)
    g_in, g_w3, g_small = _allgather([w_in_half, w3_half, small], "gather_weights")
    sw = in_w // 4
    g_in = g_in.reshape(4, d, sw)
    w_cols = [(g_in[s], 0, sw) for s in range(4)]
    o_glu = 4 * d + 2 * heads
    w_main = jnp.concatenate(
        _columns(w_cols, 0, 4 * d) + _columns(w_cols, o_glu, o_glu + 2 * d) + _columns(w_cols, o_glu + 3 * d, o_glu + 5 * d)
        + _columns(w_cols, o_glu + 2 * d, o_glu + 3 * d), axis=1)
    w_ba = jnp.pad(jnp.concatenate(_columns(w_cols, 4 * d, o_glu), axis=1), ((0, 0), (0, LANES - 2 * heads)))
    w3_full = g_w3.transpose(1, 0, 2, 3).reshape(3, d, d)
    w_dn_f, w_cf_f, w_o_f = w3_full[0], w3_full[1], w3_full[2]
    small4 = g_small[0::2]
    conv_w_f = small4[:, 0:DN_TAPS, :].transpose(1, 0, 2).reshape(DN_TAPS, 3 * d)
    meta_f = small4[:, 8:8 + N_META, :ds].transpose(1, 0, 2).reshape(N_META, d)
    dw_w_f = small4[:, 24:56, :ds].transpose(1, 0, 2).reshape(32, d)

    xs = jnp.concatenate([jnp.pad(meta_f, ((PAD_ROWS, 0), (0, 0))), x[0]], axis=0)
    tgt = jnp.pad(loss_target[0], ((HDR, 0), (0, 0)))
    h = _rms_fwd(xs, norm_w, "rms_fwd")
    proj = _mm(h, w_main, name="proj_main")
    pba = _mm(h, w_ba, name="proj_ba")
    qkvn = _dn_prep(proj, conv_w_f, d, heads, "dn_prep")
    pvec = jnp.zeros((8, LANES), F32).at[0, heads:2 * heads].set(a_log[0]).at[1, heads:2 * heads].set(dt_bias[0])
    bg = _ba_fwd(pba, pvec, heads, "ba_fwd")
    g_row = bg[:, heads:2 * heads].reshape(nc, CHUNK, heads).transpose(0, 2, 1)
    o, sall, ainv = _dn_fwd(qkvn, bg, g_row, d, heads, "dn_fwd")
    ya_in = _dn_post(o, proj, dn_norm_w, d, heads, "dn_post")
    ya = _mm(ya_in, w_dn_f, name="ya")
    c1, c3 = _cf_fwd(proj, dw_w_f, dw_b, ln_w, ln_b, d, "cf_fwd")
    yb = _mm(c3, w_cf_f, name="yb")
    merged = _merge(proj, ya, yb, b_cf_out, d, "merge")
    mo = _mm(merged, w_o_f, name="mo")
    dxo, dxo_mm, loss_acc, g_fnw = _final(xs, mo, final_norm_w.reshape(1, d), tgt, "final")

    dmerged = _mm(dxo_mm, w_o_f, nt=True, name="d_merged")
    gw_o = _mm(merged, dxo_mm, ta=True, out_dtype=MM_DTYPE, name="gw_o")
    dya, dyb, dproj, g_bcf = _merge_bwd(proj, ya, yb, b_cf_out, dmerged, d, "merge_bwd")
    dc3 = _mm(dyb, w_cf_f, nt=True, name="d_c3")
    gw_cf = _mm(c3, dyb, ta=True, out_dtype=MM_DTYPE, name="gw_cf")
    dyain = _mm(dya, w_dn_f, nt=True, name="d_ya_in")
    gw_dn = _mm(ya_in, dya, ta=True, out_dtype=MM_DTYPE, name="gw_dn")
    dc1, dproj, g_ln = _cf_bwd1(c1, proj, ln_w, ln_b, dc3, dproj, d, "cf_bwd1")
    dproj, g_dww = _cf_bwd2(dc1, proj, dw_w_f, dproj, d, "cf_bwd2")
    do, dproj, g_dnw = _dn_post_bwd(o, proj, dn_norm_w, dyain, dproj, d, heads, "dn_post_bwd")
    dqkvn, dbg, dg_row = _dn_bwd(qkvn, bg, g_row, do, sall, ainv, d, heads, "dn_bwd")
    dconv = _dn_prep_bwd1(proj, conv_w_f, dqkvn, d, heads, "dn_prep_bwd1")
    dproj, g_convw = _dn_prep_bwd2(dconv, proj, conv_w_f, dproj, d, "dn_prep_bwd2")
    dbg = dbg + jnp.pad(dg_row.transpose(0, 2, 1).reshape(t, heads), ((0, 0), (heads, LANES - 2 * heads)))
    dpba, g_ba = _ba_bwd(pba, pvec, dbg, heads, "ba_bwd")
    dh1 = _mm(dproj, w_main, nt=True, tiles=(_pick(t, (640, 128)), None, _pick(9 * d, (4608, 768))), name="d_h_main")
    dh2 = _mm(dpba, w_ba, nt=True, name="d_h_ba")
    gw_main = _mm(h, dproj, ta=True, out_dtype=MM_DTYPE, name="gw_main")
    gw_ba = _mm(h, dpba, ta=True, out_dtype=MM_DTYPE, name="gw_ba")
    dxs, g_nw = _rms_bwd(xs, norm_w, dh1, dh2, dxo, "rms_bwd")
    grad_x = dxs[HDR:][None]

    g_cols = [(gw_main, 0, 4 * d), (gw_ba, 0, 2 * heads), (gw_main, 4 * d, 2 * d), (gw_main, 8 * d, d),
              (gw_main, 6 * d, 2 * d)]
    send_in = jnp.stack([jnp.concatenate(_columns(g_cols, s * sw, (s + 1) * sw), axis=1) for s in range(4)])
    send_in = send_in.reshape(8, d // 2, sw)
    send_w3 = jnp.stack([gw_dn, gw_cf, gw_o]).reshape(3, 8, d // 8, d).transpose(1, 0, 2, 3).reshape(8, 3 * d // 8, d)
    wide = lambda a: jnp.pad(a, ((0, 0), (0, d - a.shape[1])))
    small_g = jnp.concatenate([
        g_convw.reshape(24, d), dxs[PAD_ROWS:HDR], g_dww, g_nw, g_ln, g_bcf, g_fnw, wide(g_ba), wide(g_dnw), wide(loss_acc)],
        axis=0)
    got_in, got_w3 = _alltoall([send_in, send_w3], "exchange_grads")
    red_in = _sum8(got_in, "sum_w_in")
    red_w3 = _sum8(got_w3, "sum_w3")
    pair_in, pair_w3 = _sibling_exchange([red_in, red_w3], "pair_grads")
    g_w_in = pair_in.reshape(d, in_w // 4)
    g_w3 = pair_w3.reshape(2, 3, d // 8, d).transpose(1, 0, 2, 3).reshape(3 * ds, d)
    all_small, = _allgather([small_g], "gather_small_grads")
    sg = _sum8(all_small, "sum_small")
    g_conv = lax.dynamic_slice_in_dim(sg[0:24].reshape(8, 3 * d)[0:DN_TAPS], shard * 3 * ds, 3 * ds, axis=1)
    g_meta = lax.dynamic_slice_in_dim(sg[24:40], shard * ds, ds, axis=1)
    g_dw_w = lax.dynamic_slice_in_dim(sg[40:40 + CF_TAPS], shard * ds, ds, axis=1)
    g_rep = {"norm_w": sg[72:73], "ln_w": sg[80:81], "ln_b": sg[81:82], "dw_b": sg[82:83], "b_cf_out": sg[88:89],
             "final_norm_w": sg[96], "a_log": sg[104:105, heads:2 * heads], "dt_bias": sg[105:106, heads:2 * heads],
             "dn_norm_w": sg[112:113, 0:DK]}
    loss = sg[120, 0]

    res = {}
    dl, nm, nv = _adamw(g_w_in, w_in[0], m_w_in[0], v_w_in[0], "adamw_w_in")
    res["w_in"] = (g_w_in[None], dl[None], nm[None], nv[None])
    stack3 = lambda a, b, c: jnp.concatenate([a[0], b[0], c[0]], axis=0)
    dl, nm, nv = _adamw(g_w3, stack3(w_dn_out, w_cf_out, w_o), stack3(m_w_dn_out, m_w_cf_out, m_w_o),
                        stack3(v_w_dn_out, v_w_cf_out, v_w_o), "adamw_w3")
    for j, nme in enumerate(("w_dn_out", "w_cf_out", "w_o")):
        res[nme] = tuple(a[j * ds:(j + 1) * ds][None] for a in (g_w3, dl, nm, nv))
    names = ["meta", "conv_qkv_w", "dw_w", "norm_w", "dw_b", "ln_w", "ln_b", "b_cf_out", "final_norm_w", "a_log", "dt_bias",
             "dn_norm_w"]
    grads = {"meta": g_meta, "conv_qkv_w": g_conv[None], "dw_w": g_dw_w[None], **g_rep}
    given = dict(meta=(meta, m_meta, v_meta), conv_qkv_w=(conv_qkv_w, m_conv_qkv_w, v_conv_qkv_w), dw_w=(dw_w, m_dw_w, v_dw_w),
                 norm_w=(norm_w, m_norm_w, v_norm_w), dw_b=(dw_b, m_dw_b, v_dw_b), ln_w=(ln_w, m_ln_w, v_ln_w),
                 ln_b=(ln_b, m_ln_b, v_ln_b), b_cf_out=(b_cf_out, m_b_cf_out, v_b_cf_out),
                 final_norm_w=(final_norm_w, m_final_norm_w, v_final_norm_w), a_log=(a_log, m_a_log, v_a_log),
                 dt_bias=(dt_bias, m_dt_bias, v_dt_bias), dn_norm_w=(dn_norm_w, m_dn_norm_w, v_dn_norm_w))
    shapes = [given[nme][0].shape for nme in names]
    packs = [_flat_pack([grads[nme] for nme in names], LANES)] + [_flat_pack([given[nme][j] for nme in names], LANES) for j in range(3)]
    outs = _adamw(*packs, "adamw_small")
    unpacked = [_flat_unpack(p, shapes) for p in (packs[0], *outs)]
    for j, nme in enumerate(names):
        res[nme] = tuple(u[j] for u in unpacked)

    order = ["meta", "norm_w", "w_in", "conv_qkv_w", "a_log", "dt_bias", "dn_norm_w", "w_dn_out", "dw_w", "dw_b", "ln_w", "ln_b",
             "w_cf_out", "b_cf_out", "w_o", "final_norm_w"]
    return (loss, grad_x, *[res[nme][0] for nme in order], *[res[nme][1] for nme in order],
            *[res[nme][2] for nme in order], *[res[nme][3] for nme in order])
```

```python
import jax
import jax.numpy as jnp
from jax import lax
from jax.experimental import pallas as pl
from jax.experimental.pallas import tpu as pltpu

F32 = jnp.float32
MM_DTYPE = jnp.bfloat16
EPS = 1e-6
N_META = 16
HDR = 128
PAD_ROWS = HDR - N_META
CHUNK = 64
DK = 128
CF_TAPS = 31
DN_TAPS = 4
HALO = 64
LANES = 128
ADAM_LR, ADAM_B1, ADAM_B2, ADAM_EPS, ADAM_WD, ADAM_STEP = 0.001, 0.9, 0.999, 1e-08, 0.01, 10
VMEM_LIMIT = 48 * 1024 * 1024
MESH = pl.DeviceIdType.MESH
HI = lax.Precision.HIGHEST
LO = lax.Precision.DEFAULT
NN = (((1,), (0,)), ((), ()))
NT = (((1,), (1,)), ((), ()))
TN = (((0,), (0,)), ((), ()))
ANY = pl.BlockSpec(memory_space=pl.ANY)


def _dot(a, b, dims=NN, prec=LO):
    if prec == LO:
        a, b = a.astype(MM_DTYPE), b.astype(MM_DTYPE)
    return lax.dot_general(a, b, dims, precision=prec, preferred_element_type=F32)


def _split(a):
    hi = a.astype(MM_DTYPE)
    return hi, (a - hi.astype(F32)).astype(MM_DTYPE)


def _dot3(a, b, dims=NN):
    ah, al = _split(a)
    bh, bl = _split(b)
    return _dot(ah, bh, dims) + (_dot(ah, bl, dims) + _dot(al, bh, dims))


def _pick(n, options):
    for o in options:
        if n % o == 0:
            return o
    return n


def _cparams(*sem):
    return pltpu.CompilerParams(dimension_semantics=sem, vmem_limit_bytes=VMEM_LIMIT)


def _sig(x):
    return jax.nn.sigmoid(x)


def _sum_all(a):
    return jnp.sum(jnp.sum(a, axis=-1, keepdims=True), axis=0, keepdims=True)


def _dsilu(x, s):
    return s + x * s * (1.0 - s)


def _shift(win, s):
    n = win.shape[0]
    s = s % n
    return win if s == 0 else pltpu.roll(win, s, 0)


def _row_tile(t):
    return _pick(t, (320, 128))


def _mm(a, b, *, nt=False, ta=False, out_dtype=F32, tiles=(None, None, None), after=None, name):
    k, m = a.shape if ta else a.shape[::-1]
    n = b.shape[0] if nt else b.shape[1]
    tm = tiles[0] or _pick(m, (1664, 1024, 640, 512, 384, 256, 128))
    tn = tiles[1] or _pick(n, (1024, 768, 512, 256, 128))
    tk = tiles[2] or _pick(k, (1664, 1024, 640, 512, 256, 128))
    assert m % tm == 0 and n % tn == 0 and k % tk == 0
    nk = k // tk
    dims = NT if nt else (TN if ta else NN)

    follows = [] if after is None else [after]

    def body(a_ref, b_ref, *rest):
        o_ref = rest[len(follows)]
        p = lax.dot_general(a_ref[...], b_ref[...], dims, preferred_element_type=F32)
        if nk == 1:
            o_ref[...] = p.astype(o_ref.dtype)
            return
        acc_ref = rest[-1]
        kk = pl.program_id(2)

        @pl.when(kk == 0)
        def _():
            acc_ref[...] = p

        @pl.when(kk > 0)
        def _():
            acc_ref[...] += p

        @pl.when(kk == nk - 1)
        def _():
            o_ref[...] = acc_ref[...].astype(o_ref.dtype)

    a_spec = pl.BlockSpec((tk, tm), lambda i, j, kk: (kk, i)) if ta else pl.BlockSpec((tm, tk), lambda i, j, kk: (i, kk))
    b_spec = pl.BlockSpec((tn, tk), lambda i, j, kk: (j, kk)) if nt else pl.BlockSpec((tk, tn), lambda i, j, kk: (kk, j))
    return pl.pallas_call(
        body, name=name, grid=(m // tm, n // tn, nk),
        in_specs=[a_spec, b_spec] + [ANY] * len(follows),
        out_specs=pl.BlockSpec((tm, tn), lambda i, j, kk: (i, j)),
        out_shape=jax.ShapeDtypeStruct((m, n), out_dtype),
        scratch_shapes=[] if nk == 1 else [pltpu.VMEM((tm, tn), F32)],
        compiler_params=_cparams("parallel", "parallel", "arbitrary"),
    )(a, b, *follows)


def _rms_fwd(xs, w, name):
    t, d = xs.shape
    r = _row_tile(t)

    def body(x_ref, w_ref, h_ref):
        x = x_ref[...]
        rs = lax.rsqrt(jnp.mean(x * x, axis=-1, keepdims=True) + EPS)
        h_ref[...] = (x * rs * w_ref[...]).astype(h_ref.dtype)

    return pl.pallas_call(
        body, name=name, grid=(t // r,),
        in_specs=[pl.BlockSpec((r, d), lambda i: (i, 0)), pl.BlockSpec((1, d), lambda i: (0, 0))],
        out_specs=pl.BlockSpec((r, d), lambda i: (i, 0)),
        out_shape=jax.ShapeDtypeStruct((t, d), MM_DTYPE), compiler_params=_cparams("parallel"),
    )(xs, w)


def _rms_bwd(xs, w, dh1, dh2, dres, name):
    t, d = xs.shape
    r = _row_tile(t)

    def body(x_ref, w_ref, d1_ref, d2_ref, dr_ref, dx_ref, gw_ref):
        @pl.when(pl.program_id(0) == 0)
        def _():
            gw_ref[...] = jnp.zeros_like(gw_ref)

        x = x_ref[...]
        dh = d1_ref[...] + d2_ref[...]
        rs = lax.rsqrt(jnp.mean(x * x, axis=-1, keepdims=True) + EPS)
        xh = x * rs
        dxh = dh * w_ref[...]
        dx_ref[...] = rs * (dxh - xh * jnp.mean(dxh * xh, axis=-1, keepdims=True)) + dr_ref[...]
        gw_ref[0:1, :] += jnp.sum(dh * xh, axis=0, keepdims=True)

    row = pl.BlockSpec((r, d), lambda i: (i, 0))
    return pl.pallas_call(
        body, name=name, grid=(t // r,),
        in_specs=[row, pl.BlockSpec((1, d), lambda i: (0, 0)), row, row, row],
        out_specs=[row, pl.BlockSpec((8, d), lambda i: (0, 0))],
        out_shape=[jax.ShapeDtypeStruct((t, d), F32), jax.ShapeDtypeStruct((8, d), F32)],
        compiler_params=_cparams("arbitrary"),
    )(xs, w, dh1, dh2, dres)


def _final(xs, mo, w, tgt, name):
    t, d = xs.shape
    r = _row_tile(t)

    def body(x_ref, m_ref, w_ref, t_ref, dx_ref, dxm_ref, loss_ref, gw_ref):
        i = pl.program_id(0)

        @pl.when(i == 0)
        def _():
            loss_ref[...] = jnp.zeros_like(loss_ref)
            gw_ref[...] = jnp.zeros_like(gw_ref)

        xo = x_ref[...] + m_ref[...]
        rs = lax.rsqrt(jnp.mean(xo * xo, axis=-1, keepdims=True) + EPS)
        xh = xo * rs
        y = xh * w_ref[...]
        rows = i * r + lax.broadcasted_iota(jnp.int32, (r, 1), 0)
        err = jnp.where(rows >= HDR, y - t_ref[...], 0.0)
        loss_ref[...] += 0.5 * _sum_all(err * err) / d
        dy = err / d
        gw_ref[0:1, :] += jnp.sum(dy * xh, axis=0, keepdims=True)
        dxh = dy * w_ref[...]
        dx = rs * (dxh - xh * jnp.mean(dxh * xh, axis=-1, keepdims=True))
        dx_ref[...] = dx
        dxm_ref[...] = dx.astype(dxm_ref.dtype)

    row = pl.BlockSpec((r, d), lambda i: (i, 0))
    return pl.pallas_call(
        body, name=name, grid=(t // r,),
        in_specs=[row, row, pl.BlockSpec((1, d), lambda i: (0, 0)), row],
        out_specs=[row, row, pl.BlockSpec((8, LANES), lambda i: (0, 0)), pl.BlockSpec((8, d), lambda i: (0, 0))],
        out_shape=[jax.ShapeDtypeStruct((t, d), F32), jax.ShapeDtypeStruct((t, d), MM_DTYPE),
                   jax.ShapeDtypeStruct((8, LANES), F32), jax.ShapeDtypeStruct((8, d), F32)],
        compiler_params=_cparams("arbitrary"),
    )(xs, mo, w, tgt)


def _halo_prev(r):
    return lambda i: (jnp.maximum(i * (r // HALO) - 1, 0), 0)


def _halo_next(r, t):
    return lambda i: (jnp.minimum((i + 1) * (r // HALO), t // HALO - 1), 0)


def _conv_rows(t):
    return _pick(t, (320, 128))


def _qkv_window(p_ref, ph_ref, i, cs):
    return jnp.concatenate([jnp.where(i > 0, ph_ref[HALO - 8:HALO, cs], 0.0), p_ref[:, cs]], axis=0)


def _qkv_conv(win, cw, r):
    y = None
    for s in range(DN_TAPS):
        term = _shift(win, s)[8:8 + r, :] * cw[DN_TAPS - 1 - s:DN_TAPS - s, :]
        y = term if y is None else y + term
    return y


def _dn_prep(proj, conv_w, d, heads, name):
    t = proj.shape[0]
    r = _conv_rows(t)
    w3 = 3 * d

    def body(p_ref, ph_ref, cw_ref, o_ref):
        i = pl.program_id(0)
        for cb in range(w3 // DK):
            cs = slice(cb * DK, (cb + 1) * DK)
            y = _qkv_conv(_qkv_window(p_ref, ph_ref, i, cs), cw_ref[:, cs], r)
            a = y * _sig(y)
            if cb < 2 * heads:
                sc = DK ** -0.5 if cb < heads else 1.0
                a = a * (lax.rsqrt(jnp.sum(a * a, axis=-1, keepdims=True) + EPS) * sc)
            o_ref[:, cs] = a

    return pl.pallas_call(
        body, name=name, grid=(t // r,),
        in_specs=[pl.BlockSpec((r, w3), lambda i: (i, 0)), pl.BlockSpec((HALO, w3), _halo_prev(r)),
                  pl.BlockSpec((DN_TAPS, w3), lambda i: (0, 0))],
        out_specs=pl.BlockSpec((r, w3), lambda i: (i, 0)),
        out_shape=jax.ShapeDtypeStruct((t, w3), F32), compiler_params=_cparams("parallel"),
    )(proj, proj, conv_w)


def _dn_prep_bwd1(proj, conv_w, dqkvn, d, heads, name):
    t = proj.shape[0]
    r = _conv_rows(t)
    w3 = 3 * d

    def body(p_ref, ph_ref, cw_ref, dn_ref, o_ref):
        i = pl.program_id(0)
        for cb in range(w3 // DK):
            cs = slice(cb * DK, (cb + 1) * DK)
            y = _qkv_conv(_qkv_window(p_ref, ph_ref, i, cs), cw_ref[:, cs], r)
            sy = _sig(y)
            da = dn_ref[:, cs]
            if cb < 2 * heads:
                a = y * sy
                rn = lax.rsqrt(jnp.sum(a * a, axis=-1, keepdims=True) + EPS)
                n0 = a * rn
                dn0 = da * (DK ** -0.5 if cb < heads else 1.0)
                da = rn * (dn0 - n0 * jnp.sum(dn0 * n0, axis=-1, keepdims=True))
            o_ref[:, cs] = da * _dsilu(y, sy)

    row = pl.BlockSpec((r, w3), lambda i: (i, 0))
    return pl.pallas_call(
        body, name=name, grid=(t // r,),
        in_specs=[row, pl.BlockSpec((HALO, w3), _halo_prev(r)), pl.BlockSpec((DN_TAPS, w3), lambda i: (0, 0)), row],
        out_specs=row, out_shape=jax.ShapeDtypeStruct((t, w3), F32), compiler_params=_cparams("parallel"),
    )(proj, proj, conv_w, dqkvn)


def _dn_prep_bwd2(dconv, proj, conv_w, dproj, d, name):
    t = proj.shape[0]
    r = _conv_rows(t)
    w3 = 3 * d
    last = t // r - 1

    def body(dc_ref, dcn_ref, p_ref, ph_ref, cw_ref, _, dp_ref, gw_ref):
        i = pl.program_id(0)

        @pl.when(i == 0)
        def _():
            gw_ref[...] = jnp.zeros_like(gw_ref)

        for cb in range(w3 // DK):
            cs = slice(cb * DK, (cb + 1) * DK)
            cw = cw_ref[:, cs]
            dcur = dc_ref[:, cs]
            dwin = jnp.concatenate([dcur, jnp.where(i < last, dcn_ref[0:8, cs], 0.0)], axis=0)
            win = _qkv_window(p_ref, ph_ref, i, cs)
            acc = None
            for j in range(DN_TAPS):
                s = DN_TAPS - 1 - j
                term = _shift(dwin, -s)[0:r, :] * cw[j:j + 1, :]
                acc = term if acc is None else acc + term
                gw_ref[j:j + 1, cs] += jnp.sum(dcur * _shift(win, s)[8:8 + r, :], axis=0, keepdims=True)
            dp_ref[:, cs] = acc.astype(dp_ref.dtype)

    row = pl.BlockSpec((r, w3), lambda i: (i, 0))
    return pl.pallas_call(
        body, name=name, grid=(t // r,),
        in_specs=[row, pl.BlockSpec((HALO, w3), _halo_next(r, t)), row, pl.BlockSpec((HALO, w3), _halo_prev(r)),
                  pl.BlockSpec((DN_TAPS, w3), lambda i: (0, 0)), ANY],
        out_specs=[row, pl.BlockSpec((8, w3), lambda i: (0, 0))],
        out_shape=[jax.ShapeDtypeStruct(dproj.shape, dproj.dtype), jax.ShapeDtypeStruct((8, w3), F32)],
        input_output_aliases={5: 0}, compiler_params=_cparams("arbitrary"),
    )(dconv, dconv, proj, proj, conv_w, dproj)


def _ba_terms(x, pv, heads):
    lane = lax.broadcasted_iota(jnp.int32, x.shape, 1)
    is_b = lane < heads
    is_a = jnp.logical_and(lane >= heads, lane < 2 * heads)
    beta = _sig(x)
    z = x + pv[1:2, :]
    nexp = -jnp.exp(pv[0:1, :])
    sp = jnp.maximum(z, 0.0) + jnp.log1p(jnp.exp(-jnp.abs(z)))
    return is_b, is_a, beta, z, nexp, nexp * sp


def _ba_fwd(pba, pvec, heads, name):
    t = pba.shape[0]
    r = _row_tile(t)

    def body(x_ref, pv_ref, o_ref):
        is_b, is_a, beta, _, _, g = _ba_terms(x_ref[...], pv_ref[...], heads)
        rows = pl.program_id(0) * r + lax.broadcasted_iota(jnp.int32, (r, 1), 0)
        o_ref[...] = jnp.where(rows >= PAD_ROWS, jnp.where(is_b, beta, jnp.where(is_a, g, 0.0)), 0.0)

    row = pl.BlockSpec((r, LANES), lambda i: (i, 0))
    return pl.pallas_call(
        body, name=name, grid=(t // r,), in_specs=[row, pl.BlockSpec((8, LANES), lambda i: (0, 0))],
        out_specs=row, out_shape=jax.ShapeDtypeStruct((t, LANES), F32), compiler_params=_cparams("parallel"),
    )(pba, pvec)


def _ba_bwd(pba, pvec, dbg, heads, name):
    t = pba.shape[0]
    r = _row_tile(t)

    def body(x_ref, pv_ref, d_ref, o_ref, g_ref):
        @pl.when(pl.program_id(0) == 0)
        def _():
            g_ref[...] = jnp.zeros_like(g_ref)

        is_b, is_a, beta, z, nexp, g = _ba_terms(x_ref[...], pv_ref[...], heads)
        rows = pl.program_id(0) * r + lax.broadcasted_iota(jnp.int32, (r, 1), 0)
        dd = jnp.where(rows >= PAD_ROWS, d_ref[...], 0.0)
        dz = dd * nexp * _sig(z)
        o_ref[...] = jnp.where(is_b, dd * beta * (1.0 - beta), jnp.where(is_a, dz, 0.0)).astype(o_ref.dtype)
        g_ref[0:1, :] += jnp.sum(jnp.where(is_a, dd * g, 0.0), axis=0, keepdims=True)
        g_ref[1:2, :] += jnp.sum(jnp.where(is_a, dz, 0.0), axis=0, keepdims=True)

    row = pl.BlockSpec((r, LANES), lambda i: (i, 0))
    par = pl.BlockSpec((8, LANES), lambda i: (0, 0))
    return pl.pallas_call(
        body, name=name, grid=(t // r,), in_specs=[row, par, row], out_specs=[row, par],
        out_shape=[jax.ShapeDtypeStruct((t, LANES), MM_DTYPE), jax.ShapeDtypeStruct((8, LANES), F32)],
        compiler_params=_cparams("arbitrary"),
    )(pba, pvec, dbg)


def _chunk_consts():
    ri = lax.broadcasted_iota(jnp.int32, (CHUNK, CHUNK), 0)
    ci = lax.broadcasted_iota(jnp.int32, (CHUNK, CHUNK), 1)
    return ri >= ci, ri > ci, (ri == ci).astype(F32), (ri >= ci).astype(F32), (ri <= ci).astype(F32)


def _chunk_decay(gc, gr, incl):
    return jnp.where(incl, jnp.exp(jnp.where(incl, gc - gr, 0.0)), 0.0)


def _unit_lower_inverse(n, eye):
    x = [eye + a for a in n]
    p = list(n)
    for _ in range(5):
        p = [_dot(a, a) for a in p]
        x = [a + _dot(a, b) for a, b in zip(x, p)]
    r = [eye - a + _dot3(b, a) for a, b in zip(x, n)]
    return [a + _dot(a, b) for a, b in zip(x, r)]


def _dn_fwd(qkvn, bg, g_row, d, heads, name):
    t = qkvn.shape[0]
    nc = t // CHUNK

    def body(qkv_ref, bg_ref, gr_ref, o_ref, sall_ref, ainv_ref, s_ref):
        @pl.when(pl.program_id(0) == 0)
        def _():
            s_ref[...] = jnp.zeros_like(s_ref)

        incl, strict, eye, tril, triu = _chunk_consts()
        bgv = bg_ref[...]
        gc_all = _dot(tril, bgv, NN, HI)
        gr_all = _dot(gr_ref[0], triu, NN, HI)
        hs = range(heads)
        q = [qkv_ref[:, h * DK:(h + 1) * DK] for h in hs]
        k = [qkv_ref[:, d + h * DK:d + (h + 1) * DK] for h in hs]
        v = [qkv_ref[:, 2 * d + h * DK:2 * d + (h + 1) * DK] for h in hs]
        kb = [a.astype(MM_DTYPE) for a in k]
        beta = [bgv[:, h:h + 1] for h in hs]
        gc = [gc_all[:, heads + h:heads + h + 1] for h in hs]
        decay = [_chunk_decay(gc[h], gr_all[h:h + 1, :], incl) for h in hs]
        eg = [jnp.exp(a) for a in gc]
        n = [jnp.where(strict, -(beta[h] * _dot(kb[h], kb[h], NT) * decay[h]), 0.0) for h in hs]
        x = _unit_lower_inverse(n, eye)
        sol = [_dot3(x[h], jnp.concatenate([v[h] * beta[h], k[h] * (beta[h] * eg[h])], axis=1)) for h in hs]
        attn = [_dot(q[h], kb[h], NT) * decay[h] for h in hs]
        s = [s_ref[h] for h in hs]
        sb = [a.astype(MM_DTYPE) for a in s]
        wv = [sol[h][:, :DK] - _dot(sol[h][:, DK:], sb[h]) for h in hs]
        o = [_dot(q[h] * eg[h], sb[h]) + _dot(attn[h], wv[h]) for h in hs]
        glast = [a[CHUNK - 1:CHUNK, :] for a in gc]
        s_new = [s[h] * jnp.exp(glast[h]) + _dot(k[h] * jnp.exp(glast[h] - gc[h]), wv[h], TN) for h in hs]
        for h in hs:
            o_ref[:, h * DK:(h + 1) * DK] = o[h]
            sall_ref[0, h] = s[h]
            ainv_ref[0, h] = x[h]
            s_ref[h] = s_new[h]

    return pl.pallas_call(
        body, name=name, grid=(nc,),
        in_specs=[pl.BlockSpec((CHUNK, 3 * d), lambda i: (i, 0)), pl.BlockSpec((CHUNK, LANES), lambda i: (i, 0)),
                  pl.BlockSpec((1, heads, CHUNK), lambda i: (i, 0, 0))],
        out_specs=[pl.BlockSpec((CHUNK, d), lambda i: (i, 0)), pl.BlockSpec((1, heads, DK, DK), lambda i: (i, 0, 0, 0)),
                   pl.BlockSpec((1, heads, CHUNK, CHUNK), lambda i: (i, 0, 0, 0))],
        out_shape=[jax.ShapeDtypeStruct((t, d), F32), jax.ShapeDtypeStruct((nc, heads, DK, DK), F32),
                   jax.ShapeDtypeStruct((nc, heads, CHUNK, CHUNK), F32)],
        scratch_shapes=[pltpu.VMEM((heads, DK, DK), F32)], compiler_params=_cparams("arbitrary"),
    )(qkvn, bg, g_row)


def _dn_bwd(qkvn, bg, g_row, do, sall, ainv, d, heads, name):
    t = qkvn.shape[0]
    nc = t // CHUNK
    rev = lambda i: nc - 1 - i

    def body(qkv_ref, bg_ref, gr_ref, do_ref, sall_ref, ainv_ref, dqkv_ref, dbg_ref, dgr_ref, ds_ref):
        @pl.when(pl.program_id(0) == 0)
        def _():
            ds_ref[...] = jnp.zeros_like(ds_ref)

        incl, strict, eye, tril, triu = _chunk_consts()
        bgv = bg_ref[...]
        gc_all = _dot(tril, bgv, NN, HI)
        gr_all = _dot(gr_ref[0], triu, NN, HI)
        lane = lax.broadcasted_iota(jnp.int32, (CHUNK, LANES), 1)
        hrow = lax.broadcasted_iota(jnp.int32, (heads, CHUNK), 0)
        last_row = lax.broadcasted_iota(jnp.int32, (CHUNK, 1), 0) == CHUNK - 1
        dbeta_slab = jnp.zeros((CHUNK, LANES), F32)
        dgc_slab = jnp.zeros((CHUNK, LANES), F32)
        dgr_slab = jnp.zeros((heads, CHUNK), F32)
        hs = range(heads)
        cols = lambda ref, off: [ref[:, off + h * DK:off + (h + 1) * DK] for h in hs]
        qs, ks, vs, douts = cols(qkv_ref, 0), cols(qkv_ref, d), cols(qkv_ref, 2 * d), cols(do_ref, 0)
        ss, xs_, dsns = [sall_ref[0, h] for h in hs], [ainv_ref[0, h] for h in hs], [ds_ref[h] for h in hs]

        def recompute(h, z):
            z.q, z.k, z.v, z.dout, z.s, z.x, z.dsn = qs[h], ks[h], vs[h], douts[h], ss[h], xs_[h], dsns[h]
            z.kb, z.qb, z.sb, z.dsnb, z.doutb = (a.astype(MM_DTYPE) for a in (z.k, z.q, z.s, z.dsn, z.dout))
            z.beta = bgv[:, h:h + 1]
            gc = gc_all[:, heads + h:heads + h + 1]
            z.decay = _chunk_decay(gc, gr_all[h:h + 1, :], incl)
            z.eg = jnp.exp(gc)
            glast = gc[CHUNK - 1:CHUNK, :]
            z.eglast = jnp.exp(glast)
            z.ek = jnp.exp(glast - gc)
            z.kk = _dot(z.kb, z.kb, NT)
            z.qk = _dot(z.qb, z.kb, NT)
            sol = _dot3(z.x, jnp.concatenate([z.v * z.beta, z.k * (z.beta * z.eg)], axis=1))
            z.u, z.wb = sol[:, :DK], sol[:, DK:].astype(MM_DTYPE)
            z.attn = z.qk * z.decay
            z.qg = z.q * z.eg
            z.kend = z.k * z.ek

        def pseudo_values(h, z):
            z.wvb = (z.u - _dot(z.wb, z.sb)).astype(MM_DTYPE)

        def scan_step(h, z):
            z.d_wv = _dot(z.attn, z.doutb, TN) + _dot(z.kend, z.dsnb)
            z.d_attn = _dot(z.doutb, z.wvb, NT)
            z.d_qg = _dot(z.doutb, z.sb, NT)
            z.d_kend = _dot(z.wvb, z.dsnb, NT)
            z.ds_new = _dot(z.qg, z.doutb, TN) + z.eglast * z.dsn
            z.d_glast = z.eglast * _sum_all(z.dsn * z.s) + _sum_all(z.d_kend * z.kend)

        def state_terms(h, z):
            z.ds_new = z.ds_new - _dot(z.wb, z.d_wv, TN)
            z.d_w = -_dot(z.d_wv, z.sb, NT)

        def solve_transpose(h, z):
            d_rhs = _dot3(z.x, jnp.concatenate([z.d_wv, z.d_w], axis=1), TN)
            z.d_ru, z.d_rw = d_rhs[:, :DK], d_rhs[:, DK:]

        def lower_terms(h, z):
            z.d_low = jnp.where(strict, -(_dot(z.d_ru, z.u, NT) + _dot(z.d_rw, z.wb, NT)), 0.0)

        def outputs(h, z):
            rw_k = jnp.sum(z.d_rw * z.k, axis=-1, keepdims=True)
            z.dbeta = (jnp.sum(z.d_ru * z.v, axis=-1, keepdims=True) + rw_k * z.eg
                       + jnp.sum(z.d_low * z.kk * z.decay, axis=-1, keepdims=True))
            z.dv = z.d_ru * z.beta
            d_kk = (z.d_low * z.beta * z.decay).astype(MM_DTYPE)
            d_qk = (z.d_attn * z.decay).astype(MM_DTYPE)
            z.dk = (z.d_rw * (z.beta * z.eg) + _dot(d_kk, z.kb) + _dot(d_kk, z.kb, TN) + _dot(d_qk, z.qb, TN)
                    + z.d_kend * z.ek)
            z.dq = _dot(d_qk, z.kb) + z.d_qg * z.eg
            e = (z.d_low * z.beta * z.kk + z.d_attn * z.qk) * z.decay
            z.dgc = (rw_k * z.beta * z.eg + jnp.sum(e, axis=-1, keepdims=True)
                     + jnp.sum(z.d_qg * z.qg, axis=-1, keepdims=True) - jnp.sum(z.d_kend * z.kend, axis=-1, keepdims=True)
                     + jnp.where(last_row, z.d_glast, 0.0))
            z.dgr = -jnp.sum(e, axis=0, keepdims=True)

        class _Head:
            pass

        st = [_Head() for _ in hs]
        for phase in (recompute, pseudo_values, scan_step, state_terms, solve_transpose, lower_terms, outputs):
            for h in hs:
                phase(h, st[h])
        res = [(z.dq, z.dk, z.dv, z.ds_new, z.dbeta, z.dgc, z.dgr) for z in st]
        for h, (dq, dk, dv, ds_new, dbeta, dgc, dgr) in enumerate(res):
            dqkv_ref[:, h * DK:(h + 1) * DK] = dq
            dqkv_ref[:, d + h * DK:d + (h + 1) * DK] = dk
            dqkv_ref[:, 2 * d + h * DK:2 * d + (h + 1) * DK] = dv
            ds_ref[h] = ds_new
            dbeta_slab = jnp.where(lane == h, dbeta, dbeta_slab)
            dgc_slab = jnp.where(lane == heads + h, dgc, dgc_slab)
            dgr_slab = jnp.where(hrow == h, dgr, dgr_slab)
        dbg_ref[...] = dbeta_slab + _dot(triu, dgc_slab, NN, HI)
        dgr_ref[0] = _dot(dgr_slab, tril, NN, HI)

    return pl.pallas_call(
        body, name=name, grid=(nc,),
        in_specs=[pl.BlockSpec((CHUNK, 3 * d), lambda i: (rev(i), 0)), pl.BlockSpec((CHUNK, LANES), lambda i: (rev(i), 0)),
                  pl.BlockSpec((1, heads, CHUNK), lambda i: (rev(i), 0, 0)), pl.BlockSpec((CHUNK, d), lambda i: (rev(i), 0)),
                  pl.BlockSpec((1, heads, DK, DK), lambda i: (rev(i), 0, 0, 0)),
                  pl.BlockSpec((1, heads, CHUNK, CHUNK), lambda i: (rev(i), 0, 0, 0))],
        out_specs=[pl.BlockSpec((CHUNK, 3 * d), lambda i: (rev(i), 0)), pl.BlockSpec((CHUNK, LANES), lambda i: (rev(i), 0)),
                   pl.BlockSpec((1, heads, CHUNK), lambda i: (rev(i), 0, 0))],
        out_shape=[jax.ShapeDtypeStruct((t, 3 * d), F32), jax.ShapeDtypeStruct((t, LANES), F32),
                   jax.ShapeDtypeStruct((nc, heads, CHUNK), F32)],
        scratch_shapes=[pltpu.VMEM((heads, DK, DK), F32)], compiler_params=_cparams("arbitrary"),
    )(qkvn, bg, g_row, do, sall, ainv)


def _dn_post(o, proj, w, d, heads, name):
    t = o.shape[0]
    r = _row_tile(t)

    def body(o_ref, z_ref, w_ref, y_ref):
        for h in range(heads):
            sl = slice(h * DK, (h + 1) * DK)
            oh, z = o_ref[:, sl], z_ref[:, sl]
            rs = lax.rsqrt(jnp.mean(oh * oh, axis=-1, keepdims=True) + EPS)
            y_ref[:, sl] = (oh * rs * w_ref[...] * (z * _sig(z))).astype(y_ref.dtype)

    row = pl.BlockSpec((r, d), lambda i: (i, 0))
    return pl.pallas_call(
        body, name=name, grid=(t // r,),
        in_specs=[row, pl.BlockSpec((r, d), lambda i: (i, 3)), pl.BlockSpec((1, DK), lambda i: (0, 0))],
        out_specs=row, out_shape=jax.ShapeDtypeStruct((t, d), MM_DTYPE), compiler_params=_cparams("parallel"),
    )(o, proj, w)


def _dn_post_bwd(o, proj, w, dy, dproj, d, heads, name):
    t = o.shape[0]
    r = _row_tile(t)

    def body(o_ref, z_ref, w_ref, dy_ref, _, do_ref, dz_ref, gw_ref):
        @pl.when(pl.program_id(0) == 0)
        def _():
            gw_ref[...] = jnp.zeros_like(gw_ref)

        gw = jnp.zeros((1, DK), F32)
        for h in range(heads):
            sl = slice(h * DK, (h + 1) * DK)
            oh, z, dyh = o_ref[:, sl], z_ref[:, sl], dy_ref[:, sl]
            sz = _sig(z)
            rs = lax.rsqrt(jnp.mean(oh * oh, axis=-1, keepdims=True) + EPS)
            xh = oh * rs
            dn = dyh * (z * sz)
            dz_ref[:, sl] = (dyh * (xh * w_ref[...]) * _dsilu(z, sz)).astype(dz_ref.dtype)
            dxh = dn * w_ref[...]
            do_ref[:, sl] = rs * (dxh - xh * jnp.mean(dxh * xh, axis=-1, keepdims=True))
            gw = gw + jnp.sum(dn * xh, axis=0, keepdims=True)
        gw_ref[0:1, :] += gw

    row = pl.BlockSpec((r, d), lambda i: (i, 0))
    za = pl.BlockSpec((r, d), lambda i: (i, 3))
    return pl.pallas_call(
        body, name=name, grid=(t // r,),
        in_specs=[row, za, pl.BlockSpec((1, DK), lambda i: (0, 0)), row, ANY],
        out_specs=[row, za, pl.BlockSpec((8, DK), lambda i: (0, 0))],
        out_shape=[jax.ShapeDtypeStruct((t, d), F32), jax.ShapeDtypeStruct(dproj.shape, dproj.dtype),
                   jax.ShapeDtypeStruct((8, DK), F32)],
        input_output_aliases={4: 1}, compiler_params=_cparams("arbitrary"),
    )(o, proj, w, dy, dproj)


def _glu_window(g_ref, gh_ref, i, d, cs):
    cs2 = slice(d + cs.start, d + cs.stop)
    cur = g_ref[:, cs] * _sig(g_ref[:, cs2])
    prev = gh_ref[HALO - 32:HALO, cs] * _sig(gh_ref[HALO - 32:HALO, cs2])
    return jnp.concatenate([jnp.where(i > 0, prev, 0.0), cur], axis=0)


def _ln_stats(c1):
    mu = jnp.mean(c1, axis=-1, keepdims=True)
    cc = c1 - mu
    return cc * lax.rsqrt(jnp.mean(cc * cc, axis=-1, keepdims=True) + EPS), lax.rsqrt(jnp.mean(cc * cc, axis=-1, keepdims=True) + EPS)


def _cf_fwd(proj, dw_w, dw_b, ln_w, ln_b, d, name):
    t = proj.shape[0]
    r = _conv_rows(t)

    def body(g_ref, gh_ref, zb_ref, w_ref, b_ref, lw_ref, lb_ref, c1_ref, c3_ref):
        for cb in range(d // LANES):
            cs = slice(cb * LANES, (cb + 1) * LANES)
            win = _glu_window(g_ref, gh_ref, pl.program_id(0), d, cs)
            acc = jnp.broadcast_to(b_ref[:, cs], (r, LANES))
            for sub in range(8):
                rot = _shift(win, sub)
                for s in range(sub, CF_TAPS, 8):
                    acc = acc + rot[32 - (s - sub):32 - (s - sub) + r, :] * w_ref[CF_TAPS - 1 - s:CF_TAPS - s, cs]
            c1_ref[:, cs] = acc
        xh, _ = _ln_stats(c1_ref[...])
        ln = xh * lw_ref[...] + lb_ref[...]
        zb = zb_ref[...]
        c3_ref[...] = ((ln * _sig(ln)) * (zb * _sig(zb))).astype(c3_ref.dtype)

    row = pl.BlockSpec((r, d), lambda i: (i, 0))
    par = pl.BlockSpec((1, d), lambda i: (0, 0))
    return pl.pallas_call(
        body, name=name, grid=(t // r,),
        in_specs=[pl.BlockSpec((r, 2 * d), lambda i: (i, 2)), pl.BlockSpec((HALO, 2 * d), lambda i: (_halo_prev(r)(i)[0], 2)),
                  pl.BlockSpec((r, d), lambda i: (i, 8)), pl.BlockSpec((32, d), lambda i: (0, 0)), par, par, par],
        out_specs=[row, row],
        out_shape=[jax.ShapeDtypeStruct((t, d), F32), jax.ShapeDtypeStruct((t, d), MM_DTYPE)],
        compiler_params=_cparams("parallel"),
    )(proj, proj, proj, dw_w, dw_b, ln_w, ln_b)


def _cf_bwd1(c1, proj, ln_w, ln_b, dc3, dproj, d, name):
    t = c1.shape[0]
    r = _row_tile(t)

    def body(c1_ref, zb_ref, lw_ref, lb_ref, dc3_ref, _, dc1_ref, dzb_ref, g_ref):
        @pl.when(pl.program_id(0) == 0)
        def _():
            g_ref[...] = jnp.zeros_like(g_ref)

        xh, rs = _ln_stats(c1_ref[...])
        ln = xh * lw_ref[...] + lb_ref[...]
        sl, zb, dc3v = _sig(ln), zb_ref[...], dc3_ref[...]
        szb = _sig(zb)
        dzb_ref[...] = (dc3v * (ln * sl) * _dsilu(zb, szb)).astype(dzb_ref.dtype)
        dln = dc3v * (zb * szb) * _dsilu(ln, sl)
        dxh = dln * lw_ref[...]
        dc1 = rs * (dxh - jnp.mean(dxh, axis=-1, keepdims=True) - xh * jnp.mean(dxh * xh, axis=-1, keepdims=True))
        dc1_ref[...] = dc1
        g_ref[0:1, :] += jnp.sum(dln * xh, axis=0, keepdims=True)
        g_ref[1:2, :] += jnp.sum(dln, axis=0, keepdims=True)
        g_ref[2:3, :] += jnp.sum(dc1, axis=0, keepdims=True)

    row = pl.BlockSpec((r, d), lambda i: (i, 0))
    zbs = pl.BlockSpec((r, d), lambda i: (i, 8))
    par = pl.BlockSpec((1, d), lambda i: (0, 0))
    return pl.pallas_call(
        body, name=name, grid=(t // r,), in_specs=[row, zbs, par, par, row, ANY],
        out_specs=[row, zbs, pl.BlockSpec((8, d), lambda i: (0, 0))],
        out_shape=[jax.ShapeDtypeStruct((t, d), F32), jax.ShapeDtypeStruct(dproj.shape, dproj.dtype),
                   jax.ShapeDtypeStruct((8, d), F32)],
        input_output_aliases={5: 1}, compiler_params=_cparams("arbitrary"),
    )(c1, proj, ln_w, ln_b, dc3, dproj)


def _cf_bwd2(dc1, proj, dw_w, dproj, d, name):
    t = dc1.shape[0]
    r = _conv_rows(t)
    last = t // r - 1

    def body(dc_ref, dcn_ref, g_ref, gh_ref, w_ref, _, dg_ref, gw_ref):
        i = pl.program_id(0)

        @pl.when(i == 0)
        def _():
            gw_ref[...] = jnp.zeros_like(gw_ref)

        for cb in range(d // LANES):
            cs = slice(cb * LANES, (cb + 1) * LANES)
            cs2 = slice(d + cs.start, d + cs.stop)
            dcur = dc_ref[:, cs]
            dwin = jnp.concatenate([dcur, jnp.where(i < last, dcn_ref[0:32, cs], 0.0)], axis=0)
            win = _glu_window(g_ref, gh_ref, i, d, cs)
            acc = None
            for sub in range(8):
                drot = _shift(dwin, -sub)
                rot = _shift(win, sub)
                for s in range(sub, CF_TAPS, 8):
                    j = CF_TAPS - 1 - s
                    term = drot[s - sub:s - sub + r, :] * w_ref[j:j + 1, cs]
                    acc = term if acc is None else acc + term
                    gw_ref[j:j + 1, cs] += jnp.sum(dcur * rot[32 - (s - sub):32 - (s - sub) + r, :], axis=0, keepdims=True)
            ga, sb = g_ref[:, cs], _sig(g_ref[:, cs2])
            dg_ref[:, cs] = (acc * sb).astype(dg_ref.dtype)
            dg_ref[:, cs2] = (acc * ga * sb * (1.0 - sb)).astype(dg_ref.dtype)

    row = pl.BlockSpec((r, d), lambda i: (i, 0))
    glu = pl.BlockSpec((r, 2 * d), lambda i: (i, 2))
    return pl.pallas_call(
        body, name=name, grid=(t // r,),
        in_specs=[row, pl.BlockSpec((HALO, d), _halo_next(r, t)), glu,
                  pl.BlockSpec((HALO, 2 * d), lambda i: (_halo_prev(r)(i)[0], 2)), pl.BlockSpec((32, d), lambda i: (0, 0)), ANY],
        out_specs=[glu, pl.BlockSpec((32, d), lambda i: (0, 0))],
        out_shape=[jax.ShapeDtypeStruct(dproj.shape, dproj.dtype), jax.ShapeDtypeStruct((32, d), F32)],
        input_output_aliases={5: 0}, compiler_params=_cparams("arbitrary"),
    )(dc1, dc1, proj, proj, dw_w, dproj)


def _merge(proj, ya, yb, b, d, name):
    t = ya.shape[0]
    r = _row_tile(t)

    def body(g_ref, ya_ref, yb_ref, b_ref, m_ref):
        m_ref[...] = (_sig(g_ref[:, :d]) * ya_ref[...] + _sig(g_ref[:, d:]) * (yb_ref[...] + b_ref[...])).astype(m_ref.dtype)

    row = pl.BlockSpec((r, d), lambda i: (i, 0))
    return pl.pallas_call(
        body, name=name, grid=(t // r,),
        in_specs=[pl.BlockSpec((r, 2 * d), lambda i: (i, 3)), row, row, pl.BlockSpec((1, d), lambda i: (0, 0))],
        out_specs=row, out_shape=jax.ShapeDtypeStruct((t, d), MM_DTYPE), compiler_params=_cparams("parallel"),
    )(proj, ya, yb, b)


def _merge_bwd(proj, ya, yb, b, dm, d, name):
    t = ya.shape[0]
    r = _row_tile(t)

    def body(g_ref, ya_ref, yb_ref, b_ref, dm_ref, dya_ref, dyb_ref, dg_ref, gb_ref):
        @pl.when(pl.program_id(0) == 0)
        def _():
            gb_ref[...] = jnp.zeros_like(gb_ref)

        sa, sb, dmv = _sig(g_ref[:, :d]), _sig(g_ref[:, d:]), dm_ref[...]
        dyb = dmv * sb
        dya_ref[...] = (dmv * sa).astype(dya_ref.dtype)
        dyb_ref[...] = dyb.astype(dyb_ref.dtype)
        dg_ref[:, :d] = (dmv * ya_ref[...] * sa * (1.0 - sa)).astype(dg_ref.dtype)
        dg_ref[:, d:] = (dmv * (yb_ref[...] + b_ref[...]) * sb * (1.0 - sb)).astype(dg_ref.dtype)
        gb_ref[0:1, :] += jnp.sum(dyb, axis=0, keepdims=True)

    row = pl.BlockSpec((r, d), lambda i: (i, 0))
    gate = pl.BlockSpec((r, 2 * d), lambda i: (i, 3))
    return pl.pallas_call(
        body, name=name, grid=(t // r,),
        in_specs=[gate, row, row, pl.BlockSpec((1, d), lambda i: (0, 0)), row],
        out_specs=[row, row, gate, pl.BlockSpec((8, d), lambda i: (0, 0))],
        out_shape=[jax.ShapeDtypeStruct((t, d), MM_DTYPE), jax.ShapeDtypeStruct((t, d), MM_DTYPE),
                   jax.ShapeDtypeStruct((t, 9 * d), MM_DTYPE), jax.ShapeDtypeStruct((8, d), F32)],
        compiler_params=_cparams("arbitrary"),
    )(proj, ya, yb, b, dm)


def _place():
    return lax.axis_index("x"), lax.axis_index("y"), lax.axis_index("c")


def _slot(p):
    return 4 * p[0] + 2 * p[1] + p[2]


def _allgather(blocks, name):
    n = len(blocks)

    def body(*refs):
        ins, outs = refs[:n], refs[n:2 * n]
        send_sems, recv_sems, local_sems = refs[2 * n:]
        x, y, c = _place()
        me, sibling = (x, y, c), (x, y, 1 - c)
        chips = [(1 - x, y), (x, 1 - y), (1 - x, 1 - y)]

        def copy(a, k, block, to, src=None):
            dst = outs[a].at[_slot(block)]
            return pltpu.make_async_remote_copy(
                src_ref=dst if src is None else src, dst_ref=dst, send_sem=send_sems.at[7 * a + k],
                recv_sem=recv_sems.at[7 * a + k], device_id=to, device_id_type=MESH)

        mine = [pltpu.make_async_copy(ins[a], outs[a].at[_slot(me)], local_sems.at[a]) for a in range(n)]
        for cp in mine:
            cp.start()
        first = []
        for a in range(n):
            first.append(copy(a, 0, me, sibling, src=ins[a]))
            first += [copy(a, 1 + j, me, (*chip, c), src=ins[a]) for j, chip in enumerate(chips)]
        for cp in first:
            cp.start()
        passed = []
        for j, chip in enumerate(chips):
            for a in range(n):
                copy(a, 1 + j, (*chip, c), me).wait_recv()
                cp = copy(a, 4 + j, (*chip, c), sibling)
                cp.start()
                passed.append(cp)
        for a in range(n):
            copy(a, 0, sibling, me).wait_recv()
            for j, chip in enumerate(chips):
                copy(a, 4 + j, (*chip, 1 - c), me).wait_recv()
        for cp in first + passed:
            cp.wait_send()
        for cp in mine:
            cp.wait()

    return pl.pallas_call(
        body, name=name, in_specs=[ANY] * n, out_specs=[ANY] * n,
        out_shape=[jax.ShapeDtypeStruct((8, *b.shape), b.dtype) for b in blocks],
        scratch_shapes=[pltpu.SemaphoreType.DMA((7 * n,)), pltpu.SemaphoreType.DMA((7 * n,)), pltpu.SemaphoreType.DMA((n,))],
    )(*blocks)


_FLIPS = [(0, 0, 1), (0, 1, 0), (0, 1, 1), (1, 0, 0), (1, 0, 1), (1, 1, 0), (1, 1, 1)]


_HBM = pl.BlockSpec(memory_space=pltpu.HBM)
_SEM = pl.BlockSpec(memory_space=pltpu.SEMAPHORE)
_EFFECT = pltpu.SideEffectType.DATAFLOW_SIDE_EFFECTING


def _peers(me):
    return [tuple(1 - me[ax] if f[ax] else me[ax] for ax in range(3)) for f in _FLIPS]


def _alltoall_copy(srcs, lands, send_sems, recv_sems, a, k, me, peers, own):
    dst = lands[a].at[_slot(me if own else peers[k])]
    return pltpu.make_async_remote_copy(
        src_ref=srcs[a].at[_slot(peers[k])], dst_ref=dst, send_sem=send_sems.at[7 * a + k], recv_sem=recv_sems.at[7 * a + k],
        device_id=peers[k], device_id_type=MESH)


def _alltoall_start(arrays, lands, name):
    n = len(arrays)

    def body(*refs):
        srcs, lnds = refs[:n], refs[n:2 * n]
        send_sems, recv_sems = refs[2 * n:2 * n + 2]
        token = refs[-1]
        me = _place()
        peers = _peers(me)
        for a in range(n):
            for k in range(7):
                _alltoall_copy(srcs, lnds, send_sems, recv_sems, a, k, me, peers, True).start()
        token[...] = jnp.zeros_like(token)

    hbm = lambda v: pltpu.with_memory_space_constraint(v, pltpu.HBM)
    out = pl.pallas_call(
        body, name=name, in_specs=[_HBM] * (2 * n),
        out_specs=(_SEM, _SEM, *[_HBM] * (2 * n), pl.BlockSpec(memory_space=pltpu.VMEM)),
        out_shape=(pltpu.SemaphoreType.DMA((7 * n,)), pltpu.SemaphoreType.DMA((7 * n,)),
                   *[pltpu.HBM(v.shape, v.dtype) for v in (*arrays, *lands)], jax.ShapeDtypeStruct((8, LANES), F32)),
        input_output_aliases={i: 2 + i for i in range(2 * n)},
        compiler_params=pltpu.CompilerParams(has_side_effects=_EFFECT),
    )(*[hbm(v) for v in (*arrays, *lands)])
    return out[0], out[1], out[2:2 + 2 * n], out[-1]


def _alltoall_wait(send_sems, recv_sems, thru, after, name):
    n = len(thru) // 2

    def body(*refs):
        srcs, lnds = refs[:n], refs[n:2 * n]
        send_sems, recv_sems = refs[2 * n:2 * n + 2]
        me = _place()
        peers = _peers(me)
        for a in range(n):
            for k in range(7):
                _alltoall_copy(srcs, lnds, send_sems, recv_sems, a, k, me, peers, True).wait_send()
                _alltoall_copy(srcs, lnds, send_sems, recv_sems, a, k, me, peers, False).wait_recv()

    out = pl.pallas_call(
        body, name=name, in_specs=[*[_HBM] * (2 * n), _SEM, _SEM, ANY], out_specs=[_HBM] * (2 * n),
        out_shape=[pltpu.HBM(v.shape, v.dtype) for v in thru],
        input_output_aliases={i: i for i in range(2 * n)},
        compiler_params=pltpu.CompilerParams(has_side_effects=_EFFECT),
    )(*thru, send_sems, recv_sems, after)
    return out[n:]


def _sibling_exchange(arrays, name):
    n = len(arrays)

    def body(*refs):
        ins, outs = refs[:n], refs[n:2 * n]
        send_sems, recv_sems, local_sems = refs[2 * n:]
        x, y, c = _place()

        def copy(a, slot):
            return pltpu.make_async_remote_copy(
                src_ref=ins[a], dst_ref=outs[a].at[slot], send_sem=send_sems.at[a], recv_sem=recv_sems.at[a],
                device_id=(x, y, 1 - c), device_id_type=MESH)

        mine = [pltpu.make_async_copy(ins[a], outs[a].at[c], local_sems.at[a]) for a in range(n)]
        for cp in mine:
            cp.start()
        sent = [copy(a, c) for a in range(n)]
        for cp in sent:
            cp.start()
        for a in range(n):
            copy(a, 1 - c).wait_recv()
        for cp in sent:
            cp.wait_send()
        for cp in mine:
            cp.wait()

    vmem = pl.BlockSpec(memory_space=pltpu.VMEM)
    return pl.pallas_call(
        body, name=name, in_specs=[vmem] * n, out_specs=[vmem] * n,
        compiler_params=pltpu.CompilerParams(vmem_limit_bytes=VMEM_LIMIT),
        out_shape=[jax.ShapeDtypeStruct((2, *a.shape), a.dtype) for a in arrays],
        scratch_shapes=[pltpu.SemaphoreType.DMA((n,)), pltpu.SemaphoreType.DMA((n,)), pltpu.SemaphoreType.DMA((n,))],
    )(*arrays)


def _sum8(parts, name):
    _, rr, cc = parts.shape
    r = _pick(rr, (128, 64, 32, 16, 8))

    def body(p_ref, o_ref):
        acc = p_ref[0].astype(F32)
        for j in range(1, 8):
            acc = acc + p_ref[j].astype(F32)
        o_ref[...] = acc

    return pl.pallas_call(
        body, name=name, grid=(rr // r,), in_specs=[pl.BlockSpec((8, r, cc), lambda i: (0, i, 0))],
        out_specs=pl.BlockSpec((r, cc), lambda i: (i, 0)), out_shape=jax.ShapeDtypeStruct((rr, cc), F32),
        compiler_params=_cparams("parallel"),
    )(parts)


def _adamw(g, w, m, v, name):
    rr, cc = g.shape
    r = _pick(rr, (128, 64, 32, 16, 8))

    def body(g_ref, w_ref, m_ref, v_ref, d_ref, nm_ref, nv_ref):
        gv = g_ref[...]
        nm = ADAM_B1 * m_ref[...] + (1.0 - ADAM_B1) * gv
        nv = ADAM_B2 * v_ref[...] + (1.0 - ADAM_B2) * (gv * gv)
        m_hat = nm / (1.0 - ADAM_B1 ** ADAM_STEP)
        v_hat = nv / (1.0 - ADAM_B2 ** ADAM_STEP)
        d_ref[...] = -ADAM_LR * (m_hat / (jnp.sqrt(v_hat) + ADAM_EPS) + ADAM_WD * w_ref[...])
        nm_ref[...] = nm
        nv_ref[...] = nv

    blk = pl.BlockSpec((r, cc), lambda i: (i, 0))
    return pl.pallas_call(
        body, name=name, grid=(rr // r,), in_specs=[blk] * 4, out_specs=[blk] * 3,
        out_shape=[jax.ShapeDtypeStruct((rr, cc), F32)] * 3, compiler_params=_cparams("parallel"),
    )(g, w, m, v)


def _columns(sources, start, stop):
    out, at = [], 0
    for arr, off, width in sources:
        lo, hi = max(start, at), min(stop, at + width)
        if lo < hi:
            out.append(arr[:, off + lo - at:off + hi - at])
        at += width
    return out


def _flat_pack(parts, width):
    flat = jnp.concatenate([p.reshape(-1) for p in parts])
    total = -(-flat.shape[0] // (8 * width)) * (8 * width)
    return jnp.pad(flat, (0, total - flat.shape[0])).reshape(-1, width)


def _flat_unpack(pack, shapes):
    flat = pack.reshape(-1)
    out, at = [], 0
    for s in shapes:
        size = 1
        for e in s:
            size *= e
        out.append(flat[at:at + size].reshape(s))
        at += size
    return out


def kernel(x, meta, norm_w, w_in, conv_qkv_w, a_log, dt_bias, dn_norm_w, w_dn_out, dw_w, dw_b, ln_w, ln_b, w_cf_out, b_cf_out, w_o, final_norm_w, loss_target, m_meta, m_norm_w, m_w_in, m_conv_qkv_w, m_a_log, m_dt_bias, m_dn_norm_w, m_w_dn_out, m_dw_w, m_dw_b, m_ln_w, m_ln_b, m_w_cf_out, m_b_cf_out, m_w_o, m_final_norm_w, v_meta, v_norm_w, v_w_in, v_conv_qkv_w, v_a_log, v_dt_bias, v_dn_norm_w, v_w_dn_out, v_dw_w, v_dw_b, v_ln_w, v_ln_b, v_w_cf_out, v_b_cf_out, v_w_o, v_final_norm_w):
    d = x.shape[-1]
    heads = a_log.shape[-1]
    seq = x.shape[1]
    t = HDR + seq
    nc = t // CHUNK
    in_w = 9 * d + 2 * heads
    assert heads * DK == d and seq % LANES == 0 and w_in.shape[-1] * 4 == in_w
    xi, yi, ci = _place()
    shard = 2 * xi + yi
    ds = d // 4

    w_in_half = lax.dynamic_slice_in_dim(w_in[0].astype(MM_DTYPE), ci * (d // 2), d // 2, axis=0)
    w3_half = lax.dynamic_slice_in_dim(jnp.stack([w_dn_out[0], w_cf_out[0], w_o[0]]).astype(MM_DTYPE), ci * (d // 8), d // 8, axis=1)
    small = jnp.concatenate([
        jnp.pad(conv_qkv_w[0], ((0, 4), (0, 0))), jnp.pad(meta, ((0, 0), (0, 2 * ds))),
        jnp.pad(dw_w[0], ((0, 1), (0, 2 * ds)))], axis=0)
    g_in, g_w3, g_small = _allgather([w_in_half, w3_half, small], "gather_weights")
    sw = in_w // 4
    g_in = g_in.reshape(4, d, sw)
    w_cols = [(g_in[s], 0, sw) for s in range(4)]
    o_glu = 4 * d + 2 * heads
    w_main = jnp.concatenate(
        _columns(w_cols, 0, 4 * d) + _columns(w_cols, o_glu, o_glu + 2 * d) + _columns(w_cols, o_glu + 3 * d, o_glu + 5 * d)
        + _columns(w_cols, o_glu + 2 * d, o_glu + 3 * d), axis=1)
    w_ba = jnp.pad(jnp.concatenate(_columns(w_cols, 4 * d, o_glu), axis=1), ((0, 0), (0, LANES - 2 * heads)))
    w3_full = g_w3.transpose(1, 0, 2, 3).reshape(3, d, d)
    w_dn_f, w_cf_f, w_o_f = w3_full[0], w3_full[1], w3_full[2]
    small4 = g_small[0::2]
    conv_w_f = small4[:, 0:DN_TAPS, :].transpose(1, 0, 2).reshape(DN_TAPS, 3 * d)
    meta_f = small4[:, 8:8 + N_META, :ds].transpose(1, 0, 2).reshape(N_META, d)
    dw_w_f = small4[:, 24:56, :ds].transpose(1, 0, 2).reshape(32, d)

    xs = jnp.concatenate([jnp.pad(meta_f, ((PAD_ROWS, 0), (0, 0))), x[0]], axis=0)
    tgt = jnp.pad(loss_target[0], ((HDR, 0), (0, 0)))
    h = _rms_fwd(xs, norm_w, "rms_fwd")
    proj = _mm(h, w_main, name="proj_main")
    pba = _mm(h, w_ba, name="proj_ba")
    qkvn = _dn_prep(proj, conv_w_f, d, heads, "dn_prep")
    pvec = jnp.zeros((8, LANES), F32).at[0, heads:2 * heads].set(a_log[0]).at[1, heads:2 * heads].set(dt_bias[0])
    bg = _ba_fwd(pba, pvec, heads, "ba_fwd")
    g_row = bg[:, heads:2 * heads].reshape(nc, CHUNK, heads).transpose(0, 2, 1)
    o, sall, ainv = _dn_fwd(qkvn, bg, g_row, d, heads, "dn_fwd")
    ya_in = _dn_post(o, proj, dn_norm_w, d, heads, "dn_post")
    ya = _mm(ya_in, w_dn_f, name="ya")
    c1, c3 = _cf_fwd(proj, dw_w_f, dw_b, ln_w, ln_b, d, "cf_fwd")
    yb = _mm(c3, w_cf_f, name="yb")
    merged = _merge(proj, ya, yb, b_cf_out, d, "merge")
    mo = _mm(merged, w_o_f, name="mo")
    dxo, dxo_mm, loss_acc, g_fnw = _final(xs, mo, final_norm_w.reshape(1, d), tgt, "final")

    dmerged = _mm(dxo_mm, w_o_f, nt=True, name="d_merged")
    gw_o = _mm(merged, dxo_mm, ta=True, out_dtype=MM_DTYPE, name="gw_o")
    dya, dyb, dproj, g_bcf = _merge_bwd(proj, ya, yb, b_cf_out, dmerged, d, "merge_bwd")
    dc3 = _mm(dyb, w_cf_f, nt=True, name="d_c3")
    gw_cf = _mm(c3, dyb, ta=True, out_dtype=MM_DTYPE, name="gw_cf")
    dyain = _mm(dya, w_dn_f, nt=True, name="d_ya_in")
    gw_dn = _mm(ya_in, dya, ta=True, out_dtype=MM_DTYPE, name="gw_dn")
    dc1, dproj, g_ln = _cf_bwd1(c1, proj, ln_w, ln_b, dc3, dproj, d, "cf_bwd1")
    dproj, g_dww = _cf_bwd2(dc1, proj, dw_w_f, dproj, d, "cf_bwd2")
    do, dproj, g_dnw = _dn_post_bwd(o, proj, dn_norm_w, dyain, dproj, d, heads, "dn_post_bwd")
    dqkvn, dbg, dg_row = _dn_bwd(qkvn, bg, g_row, do, sall, ainv, d, heads, "dn_bwd")
    dconv = _dn_prep_bwd1(proj, conv_w_f, dqkvn, d, heads, "dn_prep_bwd1")
    dproj, g_convw = _dn_prep_bwd2(dconv, proj, conv_w_f, dproj, d, "dn_prep_bwd2")
    dbg = dbg + jnp.pad(dg_row.transpose(0, 2, 1).reshape(t, heads), ((0, 0), (heads, LANES - 2 * heads)))
    dpba, g_ba = _ba_bwd(pba, pvec, dbg, heads, "ba_bwd")
    gw_main = _mm(h, dproj, ta=True, out_dtype=MM_DTYPE, name="gw_main")
    gw_ba = _mm(h, dpba, ta=True, out_dtype=MM_DTYPE, name="gw_ba")

    g_cols = [(gw_main, 0, 4 * d), (gw_ba, 0, 2 * heads), (gw_main, 4 * d, 2 * d), (gw_main, 8 * d, d),
              (gw_main, 6 * d, 2 * d)]
    send_in = jnp.stack([jnp.concatenate(_columns(g_cols, s * sw, (s + 1) * sw), axis=1) for s in range(4)])
    send_in = send_in.reshape(8, d // 2, sw)
    send_w3 = jnp.stack([gw_dn, gw_cf, gw_o]).reshape(3, 8, d // 8, d).transpose(1, 0, 2, 3).reshape(8, 3 * d // 8, d)
    my_slot = _slot((xi, yi, ci))
    lands = [lax.dynamic_update_slice_in_dim(lax.empty(v.shape, v.dtype), lax.dynamic_slice_in_dim(v, my_slot, 1, axis=0),
                                             my_slot, axis=0) for v in (send_in, send_w3)]
    send_sems, recv_sems, in_flight, token = _alltoall_start([send_in, send_w3], lands, "exchange_start")
    dh1 = _mm(dproj, w_main, nt=True, tiles=(_pick(t, (640, 128)), None, _pick(9 * d, (4608, 768))), after=token,
              name="d_h_main")
    dh2 = _mm(dpba, w_ba, nt=True, after=token, name="d_h_ba")
    dxs, g_nw = _rms_bwd(xs, norm_w, dh1, dh2, dxo, "rms_bwd")
    grad_x = dxs[HDR:][None]
    got_in, got_w3 = _alltoall_wait(send_sems, recv_sems, in_flight, dxs, "exchange_wait")

    wide = lambda a: jnp.pad(a, ((0, 0), (0, d - a.shape[1])))
    small_g = jnp.concatenate([
        g_convw.reshape(24, d), dxs[PAD_ROWS:HDR], g_dww, g_nw, g_ln, g_bcf, g_fnw, wide(g_ba), wide(g_dnw), wide(loss_acc)],
        axis=0)
    red_in = _sum8(got_in, "sum_w_in")
    red_w3 = _sum8(got_w3, "sum_w3")
    pair_in, pair_w3 = _sibling_exchange([red_in, red_w3], "pair_grads")
    g_w_in = pair_in.reshape(d, in_w // 4)
    g_w3 = pair_w3.reshape(2, 3, d // 8, d).transpose(1, 0, 2, 3).reshape(3 * ds, d)
    all_small, = _allgather([small_g], "gather_small_grads")
    sg = _sum8(all_small, "sum_small")
    g_conv = lax.dynamic_slice_in_dim(sg[0:24].reshape(8, 3 * d)[0:DN_TAPS], shard * 3 * ds, 3 * ds, axis=1)
    g_meta = lax.dynamic_slice_in_dim(sg[24:40], shard * ds, ds, axis=1)
    g_dw_w = lax.dynamic_slice_in_dim(sg[40:40 + CF_TAPS], shard * ds, ds, axis=1)
    g_rep = {"norm_w": sg[72:73], "ln_w": sg[80:81], "ln_b": sg[81:82], "dw_b": sg[82:83], "b_cf_out": sg[88:89],
             "final_norm_w": sg[96], "a_log": sg[104:105, heads:2 * heads], "dt_bias": sg[105:106, heads:2 * heads],
             "dn_norm_w": sg[112:113, 0:DK]}
    loss = sg[120, 0]

    res = {}
    dl, nm, nv = _adamw(g_w_in, w_in[0], m_w_in[0], v_w_in[0], "adamw_w_in")
    res["w_in"] = (g_w_in[None], dl[None], nm[None], nv[None])
    stack3 = lambda a, b, c: jnp.concatenate([a[0], b[0], c[0]], axis=0)
    dl, nm, nv = _adamw(g_w3, stack3(w_dn_out, w_cf_out, w_o), stack3(m_w_dn_out, m_w_cf_out, m_w_o),
                        stack3(v_w_dn_out, v_w_cf_out, v_w_o), "adamw_w3")
    for j, nme in enumerate(("w_dn_out", "w_cf_out", "w_o")):
        res[nme] = tuple(a[j * ds:(j + 1) * ds][None] for a in (g_w3, dl, nm, nv))
    names = ["meta", "conv_qkv_w", "dw_w", "norm_w", "dw_b", "ln_w", "ln_b", "b_cf_out", "final_norm_w", "a_log", "dt_bias",
             "dn_norm_w"]
    grads = {"meta": g_meta, "conv_qkv_w": g_conv[None], "dw_w": g_dw_w[None], **g_rep}
    given = dict(meta=(meta, m_meta, v_meta), conv_qkv_w=(conv_qkv_w, m_conv_qkv_w, v_conv_qkv_w), dw_w=(dw_w, m_dw_w, v_dw_w),
                 norm_w=(norm_w, m_norm_w, v_norm_w), dw_b=(dw_b, m_dw_b, v_dw_b), ln_w=(ln_w, m_ln_w, v_ln_w),
                 ln_b=(ln_b, m_ln_b, v_ln_b), b_cf_out=(b_cf_out, m_b_cf_out, v_b_cf_out),
                 final_norm_w=(final_norm_w, m_final_norm_w, v_final_norm_w), a_log=(a_log, m_a_log, v_a_log),
                 dt_bias=(dt_bias, m_dt_bias, v_dt_bias), dn_norm_w=(dn_norm_w, m_dn_norm_w, v_dn_norm_w))
    shapes = [given[nme][0].shape for nme in names]
    packs = [_flat_pack([grads[nme] for nme in names], LANES)] + [_flat_pack([given[nme][j] for nme in names], LANES) for j in range(3)]
    outs = _adamw(*packs, "adamw_small")
    unpacked = [_flat_unpack(p, shapes) for p in (packs[0], *outs)]
    for j, nme in enumerate(names):
        res[nme] = tuple(u[j] for u in unpacked)

    order = ["meta", "norm_w", "w_in", "conv_qkv_w", "a_log", "dt_bias", "dn_norm_w", "w_dn_out", "dw_w", "dw_b", "ln_w", "ln_b",
             "w_cf_out", "b_cf_out", "w_o", "final_norm_w"]
    return (loss, grad_x, *[res[nme][0] for nme in order], *[res[nme][1] for nme in order],
            *[res[nme][2] for nme in order], *[res[nme][3] for nme in order])
```

```python
import jax
import jax.numpy as jnp
from jax import lax
from jax.experimental import pallas as pl
from jax.experimental.pallas import tpu as pltpu

F32 = jnp.float32
MM_DTYPE = jnp.bfloat16
EPS = 1e-6
N_META = 16
HDR = 128
PAD_ROWS = HDR - N_META
CHUNK = 64
DK = 128
CF_TAPS = 31
DN_TAPS = 4
HALO = 64
LANES = 128
ADAM_LR, ADAM_B1, ADAM_B2, ADAM_EPS, ADAM_WD, ADAM_STEP = 0.001, 0.9, 0.999, 1e-08, 0.01, 10
VMEM_LIMIT = 48 * 1024 * 1024
MESH = pl.DeviceIdType.MESH
HI = lax.Precision.HIGHEST
LO = lax.Precision.DEFAULT
NN = (((1,), (0,)), ((), ()))
NT = (((1,), (1,)), ((), ()))
TN = (((0,), (0,)), ((), ()))
ANY = pl.BlockSpec(memory_space=pl.ANY)


def _dot(a, b, dims=NN, prec=LO):
    if prec == LO:
        a, b = a.astype(MM_DTYPE), b.astype(MM_DTYPE)
    return lax.dot_general(a, b, dims, precision=prec, preferred_element_type=F32)


def _split(a):
    hi = a.astype(MM_DTYPE)
    return hi, (a - hi.astype(F32)).astype(MM_DTYPE)


def _dot3(a, b, dims=NN):
    ah, al = _split(a)
    bh, bl = _split(b)
    return _dot(ah, bh, dims) + (_dot(ah, bl, dims) + _dot(al, bh, dims))


def _pick(n, options):
    for o in options:
        if n % o == 0:
            return o
    return n


def _cparams(*sem):
    return pltpu.CompilerParams(dimension_semantics=sem, vmem_limit_bytes=VMEM_LIMIT)


def _sig(x):
    return jax.nn.sigmoid(x)


def _sum_all(a):
    return jnp.sum(jnp.sum(a, axis=-1, keepdims=True), axis=0, keepdims=True)


def _dsilu(x, s):
    return s + x * s * (1.0 - s)


def _shift(win, s):
    n = win.shape[0]
    s = s % n
    return win if s == 0 else pltpu.roll(win, s, 0)


def _row_tile(t):
    return _pick(t, (320, 128))


def _mm(a, b, *, nt=False, ta=False, out_dtype=F32, tiles=(None, None, None), after=None, name):
    k, m = a.shape if ta else a.shape[::-1]
    n = b.shape[0] if nt else b.shape[1]
    tm = tiles[0] or _pick(m, (1664, 1024, 640, 512, 384, 256, 128))
    tn = tiles[1] or _pick(n, (1024, 768, 512, 256, 128))
    tk = tiles[2] or _pick(k, (1664, 1024, 640, 512, 256, 128))
    assert m % tm == 0 and n % tn == 0 and k % tk == 0
    nk = k // tk
    dims = NT if nt else (TN if ta else NN)

    follows = [] if after is None else [after]

    def body(a_ref, b_ref, *rest):
        o_ref = rest[len(follows)]
        p = lax.dot_general(a_ref[...], b_ref[...], dims, preferred_element_type=F32)
        if nk == 1:
            o_ref[...] = p.astype(o_ref.dtype)
            return
        acc_ref = rest[-1]
        kk = pl.program_id(2)

        @pl.when(kk == 0)
        def _():
            acc_ref[...] = p

        @pl.when(kk > 0)
        def _():
            acc_ref[...] += p

        @pl.when(kk == nk - 1)
        def _():
            o_ref[...] = acc_ref[...].astype(o_ref.dtype)

    a_spec = pl.BlockSpec((tk, tm), lambda i, j, kk: (kk, i)) if ta else pl.BlockSpec((tm, tk), lambda i, j, kk: (i, kk))
    b_spec = pl.BlockSpec((tn, tk), lambda i, j, kk: (j, kk)) if nt else pl.BlockSpec((tk, tn), lambda i, j, kk: (kk, j))
    return pl.pallas_call(
        body, name=name, grid=(m // tm, n // tn, nk),
        in_specs=[a_spec, b_spec] + [ANY] * len(follows),
        out_specs=pl.BlockSpec((tm, tn), lambda i, j, kk: (i, j)),
        out_shape=jax.ShapeDtypeStruct((m, n), out_dtype),
        scratch_shapes=[] if nk == 1 else [pltpu.VMEM((tm, tn), F32)],
        compiler_params=_cparams("parallel", "parallel", "arbitrary"),
    )(a, b, *follows)


def _rms_fwd(xs, w, name):
    t, d = xs.shape
    r = _row_tile(t)

    def body(x_ref, w_ref, h_ref):
        x = x_ref[...]
        rs = lax.rsqrt(jnp.mean(x * x, axis=-1, keepdims=True) + EPS)
        h_ref[...] = (x * rs * w_ref[...]).astype(h_ref.dtype)

    return pl.pallas_call(
        body, name=name, grid=(t // r,),
        in_specs=[pl.BlockSpec((r, d), lambda i: (i, 0)), pl.BlockSpec((1, d), lambda i: (0, 0))],
        out_specs=pl.BlockSpec((r, d), lambda i: (i, 0)),
        out_shape=jax.ShapeDtypeStruct((t, d), MM_DTYPE), compiler_params=_cparams("parallel"),
    )(xs, w)


def _mm_rows(a, b, *, nt, tm, tk, extras, outs, epilogue, name):
    m, k = a.shape
    n = b.shape[0] if nt else b.shape[1]
    assert m % tm == 0 and k % tk == 0
    nk = k // tk
    ne = len(extras)

    def body(a_ref, b_ref, *rest):
        ex, ou = rest[:ne], rest[ne:ne + len(outs)]
        i, kk = pl.program_id(0), pl.program_id(1)
        p = lax.dot_general(a_ref[...], b_ref[...], NT if nt else NN, preferred_element_type=F32)
        if nk == 1:
            epilogue(p, i, ex, ou)
            return
        acc_ref = rest[-1]

        @pl.when(kk == 0)
        def _():
            acc_ref[...] = p

        @pl.when(jnp.logical_and(kk > 0, kk < nk - 1))
        def _():
            acc_ref[...] += p

        @pl.when(kk == nk - 1)
        def _():
            epilogue(acc_ref[...] + p, i, ex, ou)

    b_spec = pl.BlockSpec((n, tk), lambda i, kk: (0, kk)) if nt else pl.BlockSpec((tk, n), lambda i, kk: (kk, 0))
    return pl.pallas_call(
        body, name=name, grid=(m // tm, nk),
        in_specs=[pl.BlockSpec((tm, tk), lambda i, kk: (i, kk)), b_spec] + [s for _, s in extras],
        out_specs=[s for _, s in outs], out_shape=[s for s, _ in outs],
        scratch_shapes=[] if nk == 1 else [pltpu.VMEM((tm, n), F32)],
        compiler_params=_cparams("arbitrary", "arbitrary"),
    )(a, b, *[e for e, _ in extras])


def _dh_rms_bwd(dproj, w_main, dpba, w_ba, xs, w, dres, after, name):
    t, d = xs.shape
    tm = _pick(t, (640, 128))
    tk = _pick(w_main.shape[1], (2304, 768))

    def epilogue(dh, i, ex, ou):
        dpba_ref, wba_ref, x_ref, w_ref, dr_ref, _ = ex
        dx_ref, gw_ref = ou

        @pl.when(i == 0)
        def _():
            gw_ref[...] = jnp.zeros_like(gw_ref)

        dh = dh + lax.dot_general(dpba_ref[...], wba_ref[...], NT, preferred_element_type=F32)
        x = x_ref[...]
        rs = lax.rsqrt(jnp.mean(x * x, axis=-1, keepdims=True) + EPS)
        xh = x * rs
        dxh = dh * w_ref[...]
        dx_ref[...] = rs * (dxh - xh * jnp.mean(dxh * xh, axis=-1, keepdims=True)) + dr_ref[...]
        gw_ref[0:1, :] += jnp.sum(dh * xh, axis=0, keepdims=True)

    row = pl.BlockSpec((tm, d), lambda i, kk: (i, 0))
    const = lambda shape: pl.BlockSpec(shape, lambda i, kk: (0, 0))
    return _mm_rows(
        dproj, w_main, nt=True, tm=tm, tk=tk,
        extras=[(dpba, pl.BlockSpec((tm, LANES), lambda i, kk: (i, 0))), (w_ba, const(w_ba.shape)), (xs, row), (w, const((1, d))),
                (dres, row), (after, ANY)],
        outs=[(jax.ShapeDtypeStruct((t, d), F32), row), (jax.ShapeDtypeStruct((8, d), F32), const((8, d)))],
        epilogue=epilogue, name=name)


def _mo_final(merged, w_o, xs, w, tgt, name):
    t, d = xs.shape
    r = _row_tile(t)

    def epilogue(mo, i, ex, ou):
        x_ref, w_ref, t_ref = ex
        dx_ref, dxm_ref, loss_ref, gw_ref = ou

        @pl.when(i == 0)
        def _():
            loss_ref[...] = jnp.zeros_like(loss_ref)
            gw_ref[...] = jnp.zeros_like(gw_ref)

        xo = x_ref[...] + mo
        rs = lax.rsqrt(jnp.mean(xo * xo, axis=-1, keepdims=True) + EPS)
        xh = xo * rs
        y = xh * w_ref[...]
        rows = i * r + lax.broadcasted_iota(jnp.int32, (r, 1), 0)
        err = jnp.where(rows >= HDR, y - t_ref[...], 0.0)
        loss_ref[...] += 0.5 * _sum_all(err * err) / d
        dy = err / d
        gw_ref[0:1, :] += jnp.sum(dy * xh, axis=0, keepdims=True)
        dxh = dy * w_ref[...]
        dx = rs * (dxh - xh * jnp.mean(dxh * xh, axis=-1, keepdims=True))
        dx_ref[...] = dx
        dxm_ref[...] = dx.astype(dxm_ref.dtype)

    row = pl.BlockSpec((r, d), lambda i, kk: (i, 0))
    const = lambda shape: pl.BlockSpec(shape, lambda i, kk: (0, 0))
    return _mm_rows(
        merged, w_o, nt=False, tm=r, tk=d, extras=[(xs, row), (w, const((1, d))), (tgt, row)],
        outs=[(jax.ShapeDtypeStruct((t, d), F32), row), (jax.ShapeDtypeStruct((t, d), MM_DTYPE), row),
              (jax.ShapeDtypeStruct((8, LANES), F32), const((8, LANES))), (jax.ShapeDtypeStruct((8, d), F32), const((8, d)))],
        epilogue=epilogue, name=name)


def _halo_prev(r):
    return lambda i: (jnp.maximum(i * (r // HALO) - 1, 0), 0)


def _halo_next(r, t):
    return lambda i: (jnp.minimum((i + 1) * (r // HALO), t // HALO - 1), 0)


def _conv_rows(t):
    return _pick(t, (320, 128))


def _qkv_window(p_ref, ph_ref, i, cs):
    return jnp.concatenate([jnp.where(i > 0, ph_ref[HALO - 8:HALO, cs], 0.0), p_ref[:, cs]], axis=0)


def _qkv_conv(win, cw, r):
    y = None
    for s in range(DN_TAPS):
        term = _shift(win, s)[8:8 + r, :] * cw[DN_TAPS - 1 - s:DN_TAPS - s, :]
        y = term if y is None else y + term
    return y


def _dn_prep(proj, conv_w, d, heads, name):
    t = proj.shape[0]
    r = _conv_rows(t)
    w3 = 3 * d

    def body(p_ref, ph_ref, cw_ref, o_ref):
        i = pl.program_id(0)
        for cb in range(w3 // DK):
            cs = slice(cb * DK, (cb + 1) * DK)
            y = _qkv_conv(_qkv_window(p_ref, ph_ref, i, cs), cw_ref[:, cs], r)
            a = y * _sig(y)
            if cb < 2 * heads:
                sc = DK ** -0.5 if cb < heads else 1.0
                a = a * (lax.rsqrt(jnp.sum(a * a, axis=-1, keepdims=True) + EPS) * sc)
            o_ref[:, cs] = a

    return pl.pallas_call(
        body, name=name, grid=(t // r,),
        in_specs=[pl.BlockSpec((r, w3), lambda i: (i, 0)), pl.BlockSpec((HALO, w3), _halo_prev(r)),
                  pl.BlockSpec((DN_TAPS, w3), lambda i: (0, 0))],
        out_specs=pl.BlockSpec((r, w3), lambda i: (i, 0)),
        out_shape=jax.ShapeDtypeStruct((t, w3), F32), compiler_params=_cparams("parallel"),
    )(proj, proj, conv_w)


def _dn_prep_bwd1(proj, conv_w, dqkvn, d, heads, name):
    t = proj.shape[0]
    r = _conv_rows(t)
    w3 = 3 * d

    def body(p_ref, ph_ref, cw_ref, dn_ref, o_ref):
        i = pl.program_id(0)
        for cb in range(w3 // DK):
            cs = slice(cb * DK, (cb + 1) * DK)
            y = _qkv_conv(_qkv_window(p_ref, ph_ref, i, cs), cw_ref[:, cs], r)
            sy = _sig(y)
            da = dn_ref[:, cs]
            if cb < 2 * heads:
                a = y * sy
                rn = lax.rsqrt(jnp.sum(a * a, axis=-1, keepdims=True) + EPS)
                n0 = a * rn
                dn0 = da * (DK ** -0.5 if cb < heads else 1.0)
                da = rn * (dn0 - n0 * jnp.sum(dn0 * n0, axis=-1, keepdims=True))
            o_ref[:, cs] = da * _dsilu(y, sy)

    row = pl.BlockSpec((r, w3), lambda i: (i, 0))
    return pl.pallas_call(
        body, name=name, grid=(t // r,),
        in_specs=[row, pl.BlockSpec((HALO, w3), _halo_prev(r)), pl.BlockSpec((DN_TAPS, w3), lambda i: (0, 0)), row],
        out_specs=row, out_shape=jax.ShapeDtypeStruct((t, w3), F32), compiler_params=_cparams("parallel"),
    )(proj, proj, conv_w, dqkvn)


def _dn_prep_bwd2(dconv, proj, conv_w, dproj, d, name):
    t = proj.shape[0]
    r = _conv_rows(t)
    w3 = 3 * d
    last = t // r - 1

    def body(dc_ref, dcn_ref, p_ref, ph_ref, cw_ref, _, dp_ref, gw_ref):
        i = pl.program_id(0)

        @pl.when(i == 0)
        def _():
            gw_ref[...] = jnp.zeros_like(gw_ref)

        for cb in range(w3 // DK):
            cs = slice(cb * DK, (cb + 1) * DK)
            cw = cw_ref[:, cs]
            dcur = dc_ref[:, cs]
            dwin = jnp.concatenate([dcur, jnp.where(i < last, dcn_ref[0:8, cs], 0.0)], axis=0)
            win = _qkv_window(p_ref, ph_ref, i, cs)
            acc = None
            for j in range(DN_TAPS):
                s = DN_TAPS - 1 - j
                term = _shift(dwin, -s)[0:r, :] * cw[j:j + 1, :]
                acc = term if acc is None else acc + term
                gw_ref[j:j + 1, cs] += jnp.sum(dcur * _shift(win, s)[8:8 + r, :], axis=0, keepdims=True)
            dp_ref[:, cs] = acc.astype(dp_ref.dtype)

    row = pl.BlockSpec((r, w3), lambda i: (i, 0))
    return pl.pallas_call(
        body, name=name, grid=(t // r,),
        in_specs=[row, pl.BlockSpec((HALO, w3), _halo_next(r, t)), row, pl.BlockSpec((HALO, w3), _halo_prev(r)),
                  pl.BlockSpec((DN_TAPS, w3), lambda i: (0, 0)), ANY],
        out_specs=[row, pl.BlockSpec((8, w3), lambda i: (0, 0))],
        out_shape=[jax.ShapeDtypeStruct(dproj.shape, dproj.dtype), jax.ShapeDtypeStruct((8, w3), F32)],
        input_output_aliases={5: 0}, compiler_params=_cparams("arbitrary"),
    )(dconv, dconv, proj, proj, conv_w, dproj)


def _ba_terms(x, pv, heads):
    lane = lax.broadcasted_iota(jnp.int32, x.shape, 1)
    is_b = lane < heads
    is_a = jnp.logical_and(lane >= heads, lane < 2 * heads)
    beta = _sig(x)
    z = x + pv[1:2, :]
    nexp = -jnp.exp(pv[0:1, :])
    sp = jnp.maximum(z, 0.0) + jnp.log1p(jnp.exp(-jnp.abs(z)))
    return is_b, is_a, beta, z, nexp, nexp * sp


def _ba_fwd(pba, pvec, heads, name):
    t = pba.shape[0]
    r = _row_tile(t)

    def body(x_ref, pv_ref, o_ref):
        is_b, is_a, beta, _, _, g = _ba_terms(x_ref[...], pv_ref[...], heads)
        rows = pl.program_id(0) * r + lax.broadcasted_iota(jnp.int32, (r, 1), 0)
        o_ref[...] = jnp.where(rows >= PAD_ROWS, jnp.where(is_b, beta, jnp.where(is_a, g, 0.0)), 0.0)

    row = pl.BlockSpec((r, LANES), lambda i: (i, 0))
    return pl.pallas_call(
        body, name=name, grid=(t // r,), in_specs=[row, pl.BlockSpec((8, LANES), lambda i: (0, 0))],
        out_specs=row, out_shape=jax.ShapeDtypeStruct((t, LANES), F32), compiler_params=_cparams("parallel"),
    )(pba, pvec)


def _ba_bwd(pba, pvec, dbg, heads, name):
    t = pba.shape[0]
    r = _row_tile(t)

    def body(x_ref, pv_ref, d_ref, o_ref, g_ref):
        @pl.when(pl.program_id(0) == 0)
        def _():
            g_ref[...] = jnp.zeros_like(g_ref)

        is_b, is_a, beta, z, nexp, g = _ba_terms(x_ref[...], pv_ref[...], heads)
        rows = pl.program_id(0) * r + lax.broadcasted_iota(jnp.int32, (r, 1), 0)
        dd = jnp.where(rows >= PAD_ROWS, d_ref[...], 0.0)
        dz = dd * nexp * _sig(z)
        o_ref[...] = jnp.where(is_b, dd * beta * (1.0 - beta), jnp.where(is_a, dz, 0.0)).astype(o_ref.dtype)
        g_ref[0:1, :] += jnp.sum(jnp.where(is_a, dd * g, 0.0), axis=0, keepdims=True)
        g_ref[1:2, :] += jnp.sum(jnp.where(is_a, dz, 0.0), axis=0, keepdims=True)

    row = pl.BlockSpec((r, LANES), lambda i: (i, 0))
    par = pl.BlockSpec((8, LANES), lambda i: (0, 0))
    return pl.pallas_call(
        body, name=name, grid=(t // r,), in_specs=[row, par, row], out_specs=[row, par],
        out_shape=[jax.ShapeDtypeStruct((t, LANES), MM_DTYPE), jax.ShapeDtypeStruct((8, LANES), F32)],
        compiler_params=_cparams("arbitrary"),
    )(pba, pvec, dbg)


def _chunk_consts():
    ri = lax.broadcasted_iota(jnp.int32, (CHUNK, CHUNK), 0)
    ci = lax.broadcasted_iota(jnp.int32, (CHUNK, CHUNK), 1)
    return ri >= ci, ri > ci, (ri == ci).astype(F32), (ri >= ci).astype(F32), (ri <= ci).astype(F32)


def _chunk_decay(gc, gr, incl):
    return jnp.where(incl, jnp.exp(jnp.where(incl, gc - gr, 0.0)), 0.0)


def _unit_lower_inverse(n, eye):
    x = [eye + a for a in n]
    p = list(n)
    for _ in range(5):
        p = [_dot(a, a) for a in p]
        x = [a + _dot(a, b) for a, b in zip(x, p)]
    r = [eye - a + _dot3(b, a) for a, b in zip(x, n)]
    return [a + _dot(a, b) for a, b in zip(x, r)]


def _dn_fwd(qkvn, bg, g_row, d, heads, name):
    t = qkvn.shape[0]
    nc = t // CHUNK

    def body(qkv_ref, bg_ref, gr_ref, o_ref, sall_ref, ainv_ref, s_ref):
        @pl.when(pl.program_id(0) == 0)
        def _():
            s_ref[...] = jnp.zeros_like(s_ref)

        incl, strict, eye, tril, triu = _chunk_consts()
        bgv = bg_ref[...]
        gc_all = _dot(tril, bgv, NN, HI)
        gr_all = _dot(gr_ref[0], triu, NN, HI)
        hs = range(heads)
        q = [qkv_ref[:, h * DK:(h + 1) * DK] for h in hs]
        k = [qkv_ref[:, d + h * DK:d + (h + 1) * DK] for h in hs]
        v = [qkv_ref[:, 2 * d + h * DK:2 * d + (h + 1) * DK] for h in hs]
        kb = [a.astype(MM_DTYPE) for a in k]
        beta = [bgv[:, h:h + 1] for h in hs]
        gc = [gc_all[:, heads + h:heads + h + 1] for h in hs]
        decay = [_chunk_decay(gc[h], gr_all[h:h + 1, :], incl) for h in hs]
        eg = [jnp.exp(a) for a in gc]
        n = [jnp.where(strict, -(beta[h] * _dot(kb[h], kb[h], NT) * decay[h]), 0.0) for h in hs]
        x = _unit_lower_inverse(n, eye)
        sol = [_dot3(x[h], jnp.concatenate([v[h] * beta[h], k[h] * (beta[h] * eg[h])], axis=1)) for h in hs]
        attn = [_dot(q[h], kb[h], NT) * decay[h] for h in hs]
        s = [s_ref[h] for h in hs]
        sb = [a.astype(MM_DTYPE) for a in s]
        wv = [sol[h][:, :DK] - _dot(sol[h][:, DK:], sb[h]) for h in hs]
        o = [_dot(q[h] * eg[h], sb[h]) + _dot(attn[h], wv[h]) for h in hs]
        glast = [a[CHUNK - 1:CHUNK, :] for a in gc]
        s_new = [s[h] * jnp.exp(glast[h]) + _dot(k[h] * jnp.exp(glast[h] - gc[h]), wv[h], TN) for h in hs]
        for h in hs:
            o_ref[:, h * DK:(h + 1) * DK] = o[h]
            sall_ref[0, h] = s[h]
            ainv_ref[0, h] = x[h]
            s_ref[h] = s_new[h]

    return pl.pallas_call(
        body, name=name, grid=(nc,),
        in_specs=[pl.BlockSpec((CHUNK, 3 * d), lambda i: (i, 0)), pl.BlockSpec((CHUNK, LANES), lambda i: (i, 0)),
                  pl.BlockSpec((1, heads, CHUNK), lambda i: (i, 0, 0))],
        out_specs=[pl.BlockSpec((CHUNK, d), lambda i: (i, 0)), pl.BlockSpec((1, heads, DK, DK), lambda i: (i, 0, 0, 0)),
                   pl.BlockSpec((1, heads, CHUNK, CHUNK), lambda i: (i, 0, 0, 0))],
        out_shape=[jax.ShapeDtypeStruct((t, d), F32), jax.ShapeDtypeStruct((nc, heads, DK, DK), F32),
                   jax.ShapeDtypeStruct((nc, heads, CHUNK, CHUNK), F32)],
        scratch_shapes=[pltpu.VMEM((heads, DK, DK), F32)], compiler_params=_cparams("arbitrary"),
    )(qkvn, bg, g_row)


def _dn_bwd(qkvn, bg, g_row, do, sall, ainv, d, heads, name):
    t = qkvn.shape[0]
    nc = t // CHUNK
    rev = lambda i: nc - 1 - i

    def body(qkv_ref, bg_ref, gr_ref, do_ref, sall_ref, ainv_ref, dqkv_ref, dbg_ref, dgr_ref, ds_ref):
        @pl.when(pl.program_id(0) == 0)
        def _():
            ds_ref[...] = jnp.zeros_like(ds_ref)

        incl, strict, eye, tril, triu = _chunk_consts()
        bgv = bg_ref[...]
        gc_all = _dot(tril, bgv, NN, HI)
        gr_all = _dot(gr_ref[0], triu, NN, HI)
        lane = lax.broadcasted_iota(jnp.int32, (CHUNK, LANES), 1)
        hrow = lax.broadcasted_iota(jnp.int32, (heads, CHUNK), 0)
        last_row = lax.broadcasted_iota(jnp.int32, (CHUNK, 1), 0) == CHUNK - 1
        dbeta_slab = jnp.zeros((CHUNK, LANES), F32)
        dgc_slab = jnp.zeros((CHUNK, LANES), F32)
        dgr_slab = jnp.zeros((heads, CHUNK), F32)
        hs = range(heads)
        cols = lambda ref, off: [ref[:, off + h * DK:off + (h + 1) * DK] for h in hs]
        qs, ks, vs, douts = cols(qkv_ref, 0), cols(qkv_ref, d), cols(qkv_ref, 2 * d), cols(do_ref, 0)
        ss, xs_, dsns = [sall_ref[0, h] for h in hs], [ainv_ref[0, h] for h in hs], [ds_ref[h] for h in hs]

        def recompute(h, z):
            z.q, z.k, z.v, z.dout, z.s, z.x, z.dsn = qs[h], ks[h], vs[h], douts[h], ss[h], xs_[h], dsns[h]
            z.kb, z.qb, z.sb, z.dsnb, z.doutb = (a.astype(MM_DTYPE) for a in (z.k, z.q, z.s, z.dsn, z.dout))
            z.beta = bgv[:, h:h + 1]
            gc = gc_all[:, heads + h:heads + h + 1]
            z.decay = _chunk_decay(gc, gr_all[h:h + 1, :], incl)
            z.eg = jnp.exp(gc)
            glast = gc[CHUNK - 1:CHUNK, :]
            z.eglast = jnp.exp(glast)
            z.ek = jnp.exp(glast - gc)
            z.kk = _dot(z.kb, z.kb, NT)
            z.qk = _dot(z.qb, z.kb, NT)
            sol = _dot3(z.x, jnp.concatenate([z.v * z.beta, z.k * (z.beta * z.eg)], axis=1))
            z.u, z.wb = sol[:, :DK], sol[:, DK:].astype(MM_DTYPE)
            z.attn = z.qk * z.decay
            z.qg = z.q * z.eg
            z.kend = z.k * z.ek

        def pseudo_values(h, z):
            z.wvb = (z.u - _dot(z.wb, z.sb)).astype(MM_DTYPE)

        def scan_step(h, z):
            z.d_wv = _dot(z.attn, z.doutb, TN) + _dot(z.kend, z.dsnb)
            z.d_attn = _dot(z.doutb, z.wvb, NT)
            z.d_qg = _dot(z.doutb, z.sb, NT)
            z.d_kend = _dot(z.wvb, z.dsnb, NT)
            z.ds_new = _dot(z.qg, z.doutb, TN) + z.eglast * z.dsn
            z.d_glast = z.eglast * _sum_all(z.dsn * z.s) + _sum_all(z.d_kend * z.kend)

        def state_terms(h, z):
            z.ds_new = z.ds_new - _dot(z.wb, z.d_wv, TN)
            z.d_w = -_dot(z.d_wv, z.sb, NT)

        def solve_transpose(h, z):
            d_rhs = _dot3(z.x, jnp.concatenate([z.d_wv, z.d_w], axis=1), TN)
            z.d_ru, z.d_rw = d_rhs[:, :DK], d_rhs[:, DK:]

        def lower_terms(h, z):
            z.d_low = jnp.where(strict, -(_dot(z.d_ru, z.u, NT) + _dot(z.d_rw, z.wb, NT)), 0.0)

        def outputs(h, z):
            rw_k = jnp.sum(z.d_rw * z.k, axis=-1, keepdims=True)
            z.dbeta = (jnp.sum(z.d_ru * z.v, axis=-1, keepdims=True) + rw_k * z.eg
                       + jnp.sum(z.d_low * z.kk * z.decay, axis=-1, keepdims=True))
            z.dv = z.d_ru * z.beta
            d_kk = (z.d_low * z.beta * z.decay).astype(MM_DTYPE)
            d_qk = (z.d_attn * z.decay).astype(MM_DTYPE)
            z.dk = (z.d_rw * (z.beta * z.eg) + _dot(d_kk, z.kb) + _dot(d_kk, z.kb, TN) + _dot(d_qk, z.qb, TN)
                    + z.d_kend * z.ek)
            z.dq = _dot(d_qk, z.kb) + z.d_qg * z.eg
            e = (z.d_low * z.beta * z.kk + z.d_attn * z.qk) * z.decay
            z.dgc = (rw_k * z.beta * z.eg + jnp.sum(e, axis=-1, keepdims=True)
                     + jnp.sum(z.d_qg * z.qg, axis=-1, keepdims=True) - jnp.sum(z.d_kend * z.kend, axis=-1, keepdims=True)
                     + jnp.where(last_row, z.d_glast, 0.0))
            z.dgr = -jnp.sum(e, axis=0, keepdims=True)

        class _Head:
            pass

        st = [_Head() for _ in hs]
        for phase in (recompute, pseudo_values, scan_step, state_terms, solve_transpose, lower_terms, outputs):
            for h in hs:
                phase(h, st[h])
        res = [(z.dq, z.dk, z.dv, z.ds_new, z.dbeta, z.dgc, z.dgr) for z in st]
        for h, (dq, dk, dv, ds_new, dbeta, dgc, dgr) in enumerate(res):
            dqkv_ref[:, h * DK:(h + 1) * DK] = dq
            dqkv_ref[:, d + h * DK:d + (h + 1) * DK] = dk
            dqkv_ref[:, 2 * d + h * DK:2 * d + (h + 1) * DK] = dv
            ds_ref[h] = ds_new
            dbeta_slab = jnp.where(lane == h, dbeta, dbeta_slab)
            dgc_slab = jnp.where(lane == heads + h, dgc, dgc_slab)
            dgr_slab = jnp.where(hrow == h, dgr, dgr_slab)
        dbg_ref[...] = dbeta_slab + _dot(triu, dgc_slab, NN, HI)
        dgr_ref[0] = _dot(dgr_slab, tril, NN, HI)

    return pl.pallas_call(
        body, name=name, grid=(nc,),
        in_specs=[pl.BlockSpec((CHUNK, 3 * d), lambda i: (rev(i), 0)), pl.BlockSpec((CHUNK, LANES), lambda i: (rev(i), 0)),
                  pl.BlockSpec((1, heads, CHUNK), lambda i: (rev(i), 0, 0)), pl.BlockSpec((CHUNK, d), lambda i: (rev(i), 0)),
                  pl.BlockSpec((1, heads, DK, DK), lambda i: (rev(i), 0, 0, 0)),
                  pl.BlockSpec((1, heads, CHUNK, CHUNK), lambda i: (rev(i), 0, 0, 0))],
        out_specs=[pl.BlockSpec((CHUNK, 3 * d), lambda i: (rev(i), 0)), pl.BlockSpec((CHUNK, LANES), lambda i: (rev(i), 0)),
                   pl.BlockSpec((1, heads, CHUNK), lambda i: (rev(i), 0, 0))],
        out_shape=[jax.ShapeDtypeStruct((t, 3 * d), F32), jax.ShapeDtypeStruct((t, LANES), F32),
                   jax.ShapeDtypeStruct((nc, heads, CHUNK), F32)],
        scratch_shapes=[pltpu.VMEM((heads, DK, DK), F32)], compiler_params=_cparams("arbitrary"),
    )(qkvn, bg, g_row, do, sall, ainv)


def _dn_post(o, proj, w, d, heads, name):
    t = o.shape[0]
    r = _row_tile(t)

    def body(o_ref, z_ref, w_ref, y_ref):
        for h in range(heads):
            sl = slice(h * DK, (h + 1) * DK)
            oh, z = o_ref[:, sl], z_ref[:, sl]
            rs = lax.rsqrt(jnp.mean(oh * oh, axis=-1, keepdims=True) + EPS)
            y_ref[:, sl] = (oh * rs * w_ref[...] * (z * _sig(z))).astype(y_ref.dtype)

    row = pl.BlockSpec((r, d), lambda i: (i, 0))
    return pl.pallas_call(
        body, name=name, grid=(t // r,),
        in_specs=[row, pl.BlockSpec((r, d), lambda i: (i, 3)), pl.BlockSpec((1, DK), lambda i: (0, 0))],
        out_specs=row, out_shape=jax.ShapeDtypeStruct((t, d), MM_DTYPE), compiler_params=_cparams("parallel"),
    )(o, proj, w)


def _dn_post_bwd(o, proj, w, dy, dproj, d, heads, name):
    t = o.shape[0]
    r = _row_tile(t)

    def body(o_ref, z_ref, w_ref, dy_ref, _, do_ref, dz_ref, gw_ref):
        @pl.when(pl.program_id(0) == 0)
        def _():
            gw_ref[...] = jnp.zeros_like(gw_ref)

        gw = jnp.zeros((1, DK), F32)
        for h in range(heads):
            sl = slice(h * DK, (h + 1) * DK)
            oh, z, dyh = o_ref[:, sl], z_ref[:, sl], dy_ref[:, sl]
            sz = _sig(z)
            rs = lax.rsqrt(jnp.mean(oh * oh, axis=-1, keepdims=True) + EPS)
            xh = oh * rs
            dn = dyh * (z * sz)
            dz_ref[:, sl] = (dyh * (xh * w_ref[...]) * _dsilu(z, sz)).astype(dz_ref.dtype)
            dxh = dn * w_ref[...]
            do_ref[:, sl] = rs * (dxh - xh * jnp.mean(dxh * xh, axis=-1, keepdims=True))
            gw = gw + jnp.sum(dn * xh, axis=0, keepdims=True)
        gw_ref[0:1, :] += gw

    row = pl.BlockSpec((r, d), lambda i: (i, 0))
    za = pl.BlockSpec((r, d), lambda i: (i, 3))
    return pl.pallas_call(
        body, name=name, grid=(t // r,),
        in_specs=[row, za, pl.BlockSpec((1, DK), lambda i: (0, 0)), row, ANY],
        out_specs=[row, za, pl.BlockSpec((8, DK), lambda i: (0, 0))],
        out_shape=[jax.ShapeDtypeStruct((t, d), F32), jax.ShapeDtypeStruct(dproj.shape, dproj.dtype),
                   jax.ShapeDtypeStruct((8, DK), F32)],
        input_output_aliases={4: 1}, compiler_params=_cparams("arbitrary"),
    )(o, proj, w, dy, dproj)


def _glu_window(g_ref, gh_ref, i, d, cs):
    cs2 = slice(d + cs.start, d + cs.stop)
    cur = g_ref[:, cs] * _sig(g_ref[:, cs2])
    prev = gh_ref[HALO - 32:HALO, cs] * _sig(gh_ref[HALO - 32:HALO, cs2])
    return jnp.concatenate([jnp.where(i > 0, prev, 0.0), cur], axis=0)


def _ln_stats(c1):
    mu = jnp.mean(c1, axis=-1, keepdims=True)
    cc = c1 - mu
    return cc * lax.rsqrt(jnp.mean(cc * cc, axis=-1, keepdims=True) + EPS), lax.rsqrt(jnp.mean(cc * cc, axis=-1, keepdims=True) + EPS)


def _cf_fwd(proj, dw_w, dw_b, ln_w, ln_b, d, name):
    t = proj.shape[0]
    r = _conv_rows(t)

    def body(g_ref, gh_ref, zb_ref, w_ref, b_ref, lw_ref, lb_ref, c1_ref, c3_ref):
        for cb in range(d // LANES):
            cs = slice(cb * LANES, (cb + 1) * LANES)
            win = _glu_window(g_ref, gh_ref, pl.program_id(0), d, cs)
            acc = jnp.broadcast_to(b_ref[:, cs], (r, LANES))
            for sub in range(8):
                rot = _shift(win, sub)
                for s in range(sub, CF_TAPS, 8):
                    acc = acc + rot[32 - (s - sub):32 - (s - sub) + r, :] * w_ref[CF_TAPS - 1 - s:CF_TAPS - s, cs]
            c1_ref[:, cs] = acc
        xh, _ = _ln_stats(c1_ref[...])
        ln = xh * lw_ref[...] + lb_ref[...]
        zb = zb_ref[...]
        c3_ref[...] = ((ln * _sig(ln)) * (zb * _sig(zb))).astype(c3_ref.dtype)

    row = pl.BlockSpec((r, d), lambda i: (i, 0))
    par = pl.BlockSpec((1, d), lambda i: (0, 0))
    return pl.pallas_call(
        body, name=name, grid=(t // r,),
        in_specs=[pl.BlockSpec((r, 2 * d), lambda i: (i, 2)), pl.BlockSpec((HALO, 2 * d), lambda i: (_halo_prev(r)(i)[0], 2)),
                  pl.BlockSpec((r, d), lambda i: (i, 8)), pl.BlockSpec((32, d), lambda i: (0, 0)), par, par, par],
        out_specs=[row, row],
        out_shape=[jax.ShapeDtypeStruct((t, d), F32), jax.ShapeDtypeStruct((t, d), MM_DTYPE)],
        compiler_params=_cparams("parallel"),
    )(proj, proj, proj, dw_w, dw_b, ln_w, ln_b)


def _cf_bwd1(c1, proj, ln_w, ln_b, dc3, dproj, d, name):
    t = c1.shape[0]
    r = _row_tile(t)

    def body(c1_ref, zb_ref, lw_ref, lb_ref, dc3_ref, _, dc1_ref, dzb_ref, g_ref):
        @pl.when(pl.program_id(0) == 0)
        def _():
            g_ref[...] = jnp.zeros_like(g_ref)

        xh, rs = _ln_stats(c1_ref[...])
        ln = xh * lw_ref[...] + lb_ref[...]
        sl, zb, dc3v = _sig(ln), zb_ref[...], dc3_ref[...]
        szb = _sig(zb)
        dzb_ref[...] = (dc3v * (ln * sl) * _dsilu(zb, szb)).astype(dzb_ref.dtype)
        dln = dc3v * (zb * szb) * _dsilu(ln, sl)
        dxh = dln * lw_ref[...]
        dc1 = rs * (dxh - jnp.mean(dxh, axis=-1, keepdims=True) - xh * jnp.mean(dxh * xh, axis=-1, keepdims=True))
        dc1_ref[...] = dc1
        g_ref[0:1, :] += jnp.sum(dln * xh, axis=0, keepdims=True)
        g_ref[1:2, :] += jnp.sum(dln, axis=0, keepdims=True)
        g_ref[2:3, :] += jnp.sum(dc1, axis=0, keepdims=True)

    row = pl.BlockSpec((r, d), lambda i: (i, 0))
    zbs = pl.BlockSpec((r, d), lambda i: (i, 8))
    par = pl.BlockSpec((1, d), lambda i: (0, 0))
    return pl.pallas_call(
        body, name=name, grid=(t // r,), in_specs=[row, zbs, par, par, row, ANY],
        out_specs=[row, zbs, pl.BlockSpec((8, d), lambda i: (0, 0))],
        out_shape=[jax.ShapeDtypeStruct((t, d), F32), jax.ShapeDtypeStruct(dproj.shape, dproj.dtype),
                   jax.ShapeDtypeStruct((8, d), F32)],
        input_output_aliases={5: 1}, compiler_params=_cparams("arbitrary"),
    )(c1, proj, ln_w, ln_b, dc3, dproj)


def _cf_bwd2(dc1, proj, dw_w, dproj, d, name):
    t = dc1.shape[0]
    r = _conv_rows(t)
    last = t // r - 1

    def body(dc_ref, dcn_ref, g_ref, gh_ref, w_ref, _, dg_ref, gw_ref):
        i = pl.program_id(0)

        @pl.when(i == 0)
        def _():
            gw_ref[...] = jnp.zeros_like(gw_ref)

        for cb in range(d // LANES):
            cs = slice(cb * LANES, (cb + 1) * LANES)
            cs2 = slice(d + cs.start, d + cs.stop)
            dcur = dc_ref[:, cs]
            dwin = jnp.concatenate([dcur, jnp.where(i < last, dcn_ref[0:32, cs], 0.0)], axis=0)
            win = _glu_window(g_ref, gh_ref, i, d, cs)
            acc = None
            for sub in range(8):
                drot = _shift(dwin, -sub)
                rot = _shift(win, sub)
                for s in range(sub, CF_TAPS, 8):
                    j = CF_TAPS - 1 - s
                    term = drot[s - sub:s - sub + r, :] * w_ref[j:j + 1, cs]
                    acc = term if acc is None else acc + term
                    gw_ref[j:j + 1, cs] += jnp.sum(dcur * rot[32 - (s - sub):32 - (s - sub) + r, :], axis=0, keepdims=True)
            ga, sb = g_ref[:, cs], _sig(g_ref[:, cs2])
            dg_ref[:, cs] = (acc * sb).astype(dg_ref.dtype)
            dg_ref[:, cs2] = (acc * ga * sb * (1.0 - sb)).astype(dg_ref.dtype)

    row = pl.BlockSpec((r, d), lambda i: (i, 0))
    glu = pl.BlockSpec((r, 2 * d), lambda i: (i, 2))
    return pl.pallas_call(
        body, name=name, grid=(t // r,),
        in_specs=[row, pl.BlockSpec((HALO, d), _halo_next(r, t)), glu,
                  pl.BlockSpec((HALO, 2 * d), lambda i: (_halo_prev(r)(i)[0], 2)), pl.BlockSpec((32, d), lambda i: (0, 0)), ANY],
        out_specs=[glu, pl.BlockSpec((32, d), lambda i: (0, 0))],
        out_shape=[jax.ShapeDtypeStruct(dproj.shape, dproj.dtype), jax.ShapeDtypeStruct((32, d), F32)],
        input_output_aliases={5: 0}, compiler_params=_cparams("arbitrary"),
    )(dc1, dc1, proj, proj, dw_w, dproj)


def _merge(proj, ya, yb, b, d, name):
    t = ya.shape[0]
    r = _row_tile(t)

    def body(g_ref, ya_ref, yb_ref, b_ref, m_ref):
        m_ref[...] = (_sig(g_ref[:, :d]) * ya_ref[...] + _sig(g_ref[:, d:]) * (yb_ref[...] + b_ref[...])).astype(m_ref.dtype)

    row = pl.BlockSpec((r, d), lambda i: (i, 0))
    return pl.pallas_call(
        body, name=name, grid=(t // r,),
        in_specs=[pl.BlockSpec((r, 2 * d), lambda i: (i, 3)), row, row, pl.BlockSpec((1, d), lambda i: (0, 0))],
        out_specs=row, out_shape=jax.ShapeDtypeStruct((t, d), MM_DTYPE), compiler_params=_cparams("parallel"),
    )(proj, ya, yb, b)


def _merge_bwd(dxo_mm, w_o, proj, ya, yb, b, d, name):
    t = ya.shape[0]
    r = _row_tile(t)

    def epilogue(dmv, i, ex, ou):
        g_ref, ya_ref, yb_ref, b_ref = ex
        dya_ref, dyb_ref, dg_ref, gb_ref = ou

        @pl.when(i == 0)
        def _():
            gb_ref[...] = jnp.zeros_like(gb_ref)

        sa, sb = _sig(g_ref[:, :d]), _sig(g_ref[:, d:])
        dyb = dmv * sb
        dya_ref[...] = (dmv * sa).astype(dya_ref.dtype)
        dyb_ref[...] = dyb.astype(dyb_ref.dtype)
        dg_ref[:, :d] = (dmv * ya_ref[...] * sa * (1.0 - sa)).astype(dg_ref.dtype)
        dg_ref[:, d:] = (dmv * (yb_ref[...] + b_ref[...]) * sb * (1.0 - sb)).astype(dg_ref.dtype)
        gb_ref[0:1, :] += jnp.sum(dyb, axis=0, keepdims=True)

    row = pl.BlockSpec((r, d), lambda i, kk: (i, 0))
    gate = pl.BlockSpec((r, 2 * d), lambda i, kk: (i, 3))
    const = lambda shape: pl.BlockSpec(shape, lambda i, kk: (0, 0))
    return _mm_rows(
        dxo_mm, w_o, nt=True, tm=r, tk=d, extras=[(proj, gate), (ya, row), (yb, row), (b, const((1, d)))],
        outs=[(jax.ShapeDtypeStruct((t, d), MM_DTYPE), row), (jax.ShapeDtypeStruct((t, d), MM_DTYPE), row),
              (jax.ShapeDtypeStruct((t, 9 * d), MM_DTYPE), gate), (jax.ShapeDtypeStruct((8, d), F32), const((8, d)))],
        epilogue=epilogue, name=name)


def _place():
    return lax.axis_index("x"), lax.axis_index("y"), lax.axis_index("c")


def _slot(p):
    return 4 * p[0] + 2 * p[1] + p[2]


def _allgather(blocks, name):
    n = len(blocks)

    def body(*refs):
        ins, outs = refs[:n], refs[n:2 * n]
        send_sems, recv_sems, local_sems = refs[2 * n:]
        x, y, c = _place()
        me, sibling = (x, y, c), (x, y, 1 - c)
        chips = [(1 - x, y), (x, 1 - y), (1 - x, 1 - y)]

        def copy(a, k, block, to, src=None):
            dst = outs[a].at[_slot(block)]
            return pltpu.make_async_remote_copy(
                src_ref=dst if src is None else src, dst_ref=dst, send_sem=send_sems.at[7 * a + k],
                recv_sem=recv_sems.at[7 * a + k], device_id=to, device_id_type=MESH)

        mine = [pltpu.make_async_copy(ins[a], outs[a].at[_slot(me)], local_sems.at[a]) for a in range(n)]
        for cp in mine:
            cp.start()
        first = []
        for a in range(n):
            first.append(copy(a, 0, me, sibling, src=ins[a]))
            first += [copy(a, 1 + j, me, (*chip, c), src=ins[a]) for j, chip in enumerate(chips)]
        for cp in first:
            cp.start()
        passed = []
        for j, chip in enumerate(chips):
            for a in range(n):
                copy(a, 1 + j, (*chip, c), me).wait_recv()
                cp = copy(a, 4 + j, (*chip, c), sibling)
                cp.start()
                passed.append(cp)
        for a in range(n):
            copy(a, 0, sibling, me).wait_recv()
            for j, chip in enumerate(chips):
                copy(a, 4 + j, (*chip, 1 - c), me).wait_recv()
        for cp in first + passed:
            cp.wait_send()
        for cp in mine:
            cp.wait()

    return pl.pallas_call(
        body, name=name, in_specs=[ANY] * n, out_specs=[ANY] * n,
        out_shape=[jax.ShapeDtypeStruct((8, *b.shape), b.dtype) for b in blocks],
        scratch_shapes=[pltpu.SemaphoreType.DMA((7 * n,)), pltpu.SemaphoreType.DMA((7 * n,)), pltpu.SemaphoreType.DMA((n,))],
    )(*blocks)


_FLIPS = [(0, 0, 1), (0, 1, 0), (0, 1, 1), (1, 0, 0), (1, 0, 1), (1, 1, 0), (1, 1, 1)]


_HBM = pl.BlockSpec(memory_space=pltpu.HBM)
_SEM = pl.BlockSpec(memory_space=pltpu.SEMAPHORE)
_EFFECT = pltpu.SideEffectType.DATAFLOW_SIDE_EFFECTING


def _peers(me):
    return [tuple(1 - me[ax] if f[ax] else me[ax] for ax in range(3)) for f in _FLIPS]


def _alltoall_copy(srcs, lands, send_sems, recv_sems, a, k, me, peers, own):
    dst = lands[a].at[_slot(me if own else peers[k])]
    src = srcs[a].at[0 if srcs[a].shape[0] == 1 else _slot(peers[k])]
    return pltpu.make_async_remote_copy(
        src_ref=src, dst_ref=dst, send_sem=send_sems.at[7 * a + k], recv_sem=recv_sems.at[7 * a + k],
        device_id=peers[k], device_id_type=MESH)


def _alltoall_start(arrays, lands, name):
    n = len(arrays)

    def body(*refs):
        srcs, lnds = refs[:n], refs[n:2 * n]
        send_sems, recv_sems = refs[2 * n:2 * n + 2]
        token = refs[-1]
        me = _place()
        peers = _peers(me)
        for a in range(n):
            for k in range(7):
                _alltoall_copy(srcs, lnds, send_sems, recv_sems, a, k, me, peers, True).start()
        token[...] = jnp.zeros_like(token)

    hbm = lambda v: pltpu.with_memory_space_constraint(v, pltpu.HBM)
    out = pl.pallas_call(
        body, name=name, in_specs=[_HBM] * (2 * n),
        out_specs=(_SEM, _SEM, *[_HBM] * (2 * n), pl.BlockSpec(memory_space=pltpu.VMEM)),
        out_shape=(pltpu.SemaphoreType.DMA((7 * n,)), pltpu.SemaphoreType.DMA((7 * n,)),
                   *[pltpu.HBM(v.shape, v.dtype) for v in (*arrays, *lands)], jax.ShapeDtypeStruct((8, LANES), F32)),
        input_output_aliases={i: 2 + i for i in range(2 * n)},
        compiler_params=pltpu.CompilerParams(has_side_effects=_EFFECT),
    )(*[hbm(v) for v in (*arrays, *lands)])
    return out[0], out[1], out[2:2 + 2 * n], out[-1]


def _alltoall_wait(send_sems, recv_sems, thru, after, name):
    n = len(thru) // 2

    def body(*refs):
        srcs, lnds = refs[:n], refs[n:2 * n]
        send_sems, recv_sems = refs[2 * n:2 * n + 2]
        me = _place()
        peers = _peers(me)
        for a in range(n):
            for k in range(7):
                _alltoall_copy(srcs, lnds, send_sems, recv_sems, a, k, me, peers, True).wait_send()
                _alltoall_copy(srcs, lnds, send_sems, recv_sems, a, k, me, peers, False).wait_recv()

    out = pl.pallas_call(
        body, name=name, in_specs=[*[_HBM] * (2 * n), _SEM, _SEM, ANY], out_specs=[_HBM] * (2 * n),
        out_shape=[pltpu.HBM(v.shape, v.dtype) for v in thru],
        input_output_aliases={i: i for i in range(2 * n)},
        compiler_params=pltpu.CompilerParams(has_side_effects=_EFFECT),
    )(*thru, send_sems, recv_sems, after)
    return out[n:]


def _sibling_exchange(arrays, name):
    n = len(arrays)

    def body(*refs):
        ins, outs = refs[:n], refs[n:2 * n]
        send_sems, recv_sems, local_sems = refs[2 * n:]
        x, y, c = _place()

        def copy(a, slot):
            return pltpu.make_async_remote_copy(
                src_ref=ins[a], dst_ref=outs[a].at[slot], send_sem=send_sems.at[a], recv_sem=recv_sems.at[a],
                device_id=(x, y, 1 - c), device_id_type=MESH)

        mine = [pltpu.make_async_copy(ins[a], outs[a].at[c], local_sems.at[a]) for a in range(n)]
        for cp in mine:
            cp.start()
        sent = [copy(a, c) for a in range(n)]
        for cp in sent:
            cp.start()
        for a in range(n):
            copy(a, 1 - c).wait_recv()
        for cp in sent:
            cp.wait_send()
        for cp in mine:
            cp.wait()

    vmem = pl.BlockSpec(memory_space=pltpu.VMEM)
    return pl.pallas_call(
        body, name=name, in_specs=[vmem] * n, out_specs=[vmem] * n,
        compiler_params=pltpu.CompilerParams(vmem_limit_bytes=VMEM_LIMIT),
        out_shape=[jax.ShapeDtypeStruct((2, *a.shape), a.dtype) for a in arrays],
        scratch_shapes=[pltpu.SemaphoreType.DMA((n,)), pltpu.SemaphoreType.DMA((n,)), pltpu.SemaphoreType.DMA((n,))],
    )(*arrays)


def _sum8(parts, name):
    _, rr, cc = parts.shape
    r = _pick(rr, (128, 64, 32, 16, 8))

    def body(p_ref, o_ref):
        acc = p_ref[0].astype(F32)
        for j in range(1, 8):
            acc = acc + p_ref[j].astype(F32)
        o_ref[...] = acc

    return pl.pallas_call(
        body, name=name, grid=(rr // r,), in_specs=[pl.BlockSpec((8, r, cc), lambda i: (0, i, 0))],
        out_specs=pl.BlockSpec((r, cc), lambda i: (i, 0)), out_shape=jax.ShapeDtypeStruct((rr, cc), F32),
        compiler_params=_cparams("parallel"),
    )(parts)


def _adamw(g, w, m, v, name):
    rr, cc = g.shape
    r = _pick(rr, (128, 64, 32, 16, 8))

    def body(g_ref, w_ref, m_ref, v_ref, d_ref, nm_ref, nv_ref):
        gv = g_ref[...]
        nm = ADAM_B1 * m_ref[...] + (1.0 - ADAM_B1) * gv
        nv = ADAM_B2 * v_ref[...] + (1.0 - ADAM_B2) * (gv * gv)
        m_hat = nm / (1.0 - ADAM_B1 ** ADAM_STEP)
        v_hat = nv / (1.0 - ADAM_B2 ** ADAM_STEP)
        d_ref[...] = -ADAM_LR * (m_hat / (jnp.sqrt(v_hat) + ADAM_EPS) + ADAM_WD * w_ref[...])
        nm_ref[...] = nm
        nv_ref[...] = nv

    blk = pl.BlockSpec((r, cc), lambda i: (i, 0))
    return pl.pallas_call(
        body, name=name, grid=(rr // r,), in_specs=[blk] * 4, out_specs=[blk] * 3,
        out_shape=[jax.ShapeDtypeStruct((rr, cc), F32)] * 3, compiler_params=_cparams("parallel"),
    )(g, w, m, v)


def _columns(sources, start, stop):
    out, at = [], 0
    for arr, off, width in sources:
        lo, hi = max(start, at), min(stop, at + width)
        if lo < hi:
            out.append(arr[:, off + lo - at:off + hi - at])
        at += width
    return out


def _flat_pack(parts, width):
    flat = jnp.concatenate([p.reshape(-1) for p in parts])
    total = -(-flat.shape[0] // (8 * width)) * (8 * width)
    return jnp.pad(flat, (0, total - flat.shape[0])).reshape(-1, width)


def _flat_unpack(pack, shapes):
    flat = pack.reshape(-1)
    out, at = [], 0
    for s in shapes:
        size = 1
        for e in s:
            size *= e
        out.append(flat[at:at + size].reshape(s))
        at += size
    return out


def kernel(x, meta, norm_w, w_in, conv_qkv_w, a_log, dt_bias, dn_norm_w, w_dn_out, dw_w, dw_b, ln_w, ln_b, w_cf_out, b_cf_out, w_o, final_norm_w, loss_target, m_meta, m_norm_w, m_w_in, m_conv_qkv_w, m_a_log, m_dt_bias, m_dn_norm_w, m_w_dn_out, m_dw_w, m_dw_b, m_ln_w, m_ln_b, m_w_cf_out, m_b_cf_out, m_w_o, m_final_norm_w, v_meta, v_norm_w, v_w_in, v_conv_qkv_w, v_a_log, v_dt_bias, v_dn_norm_w, v_w_dn_out, v_dw_w, v_dw_b, v_ln_w, v_ln_b, v_w_cf_out, v_b_cf_out, v_w_o, v_final_norm_w):
    d = x.shape[-1]
    heads = a_log.shape[-1]
    seq = x.shape[1]
    t = HDR + seq
    nc = t // CHUNK
    in_w = 9 * d + 2 * heads
    assert heads * DK == d and seq % LANES == 0 and w_in.shape[-1] * 4 == in_w
    xi, yi, ci = _place()
    shard = 2 * xi + yi
    ds = d // 4

    w_in_half = lax.dynamic_slice_in_dim(w_in[0].astype(MM_DTYPE), ci * (d // 2), d // 2, axis=0)
    w3_half = lax.dynamic_slice_in_dim(jnp.stack([w_dn_out[0], w_cf_out[0], w_o[0]]).astype(MM_DTYPE), ci * (d // 8), d // 8, axis=1)
    small = jnp.concatenate([
        jnp.pad(conv_qkv_w[0], ((0, 4), (0, 0))), jnp.pad(meta, ((0, 0), (0, 2 * ds))),
        jnp.pad(dw_w[0], ((0, 1), (0, 2 * ds)))], axis=0)
    g_in, g_small = _allgather([w_in_half, small], "gather_weights")
    my_slot = _slot((xi, yi, ci))
    w3_land = lax.dynamic_update_slice_in_dim(lax.empty((8, *w3_half.shape), w3_half.dtype), w3_half[None], my_slot, axis=0)
    w3_send, w3_recv, w3_in_flight, _ = _alltoall_start([w3_half[None]], [w3_land], "w3_start")
    sw = in_w // 4
    g_in = g_in.reshape(4, d, sw)
    w_cols = [(g_in[s], 0, sw) for s in range(4)]
    o_glu = 4 * d + 2 * heads
    w_main = jnp.concatenate(
        _columns(w_cols, 0, 4 * d) + _columns(w_cols, o_glu, o_glu + 2 * d) + _columns(w_cols, o_glu + 3 * d, o_glu + 5 * d)
        + _columns(w_cols, o_glu + 2 * d, o_glu + 3 * d), axis=1)
    w_ba = jnp.pad(jnp.concatenate(_columns(w_cols, 4 * d, o_glu), axis=1), ((0, 0), (0, LANES - 2 * heads)))
    small4 = g_small[0::2]
    conv_w_f = small4[:, 0:DN_TAPS, :].transpose(1, 0, 2).reshape(DN_TAPS, 3 * d)
    meta_f = small4[:, 8:8 + N_META, :ds].transpose(1, 0, 2).reshape(N_META, d)
    dw_w_f = small4[:, 24:56, :ds].transpose(1, 0, 2).reshape(32, d)

    xs = jnp.concatenate([jnp.pad(meta_f, ((PAD_ROWS, 0), (0, 0))), x[0]], axis=0)
    tgt = jnp.pad(loss_target[0], ((HDR, 0), (0, 0)))
    h = _rms_fwd(xs, norm_w, "rms_fwd")
    proj = _mm(h, w_main, name="proj_main")
    pba = _mm(h, w_ba, name="proj_ba")
    qkvn = _dn_prep(proj, conv_w_f, d, heads, "dn_prep")
    pvec = jnp.zeros((8, LANES), F32).at[0, heads:2 * heads].set(a_log[0]).at[1, heads:2 * heads].set(dt_bias[0])
    bg = _ba_fwd(pba, pvec, heads, "ba_fwd")
    g_row = bg[:, heads:2 * heads].reshape(nc, CHUNK, heads).transpose(0, 2, 1)
    o, sall, ainv = _dn_fwd(qkvn, bg, g_row, d, heads, "dn_fwd")
    ya_in = _dn_post(o, proj, dn_norm_w, d, heads, "dn_post")
    g_w3, = _alltoall_wait(w3_send, w3_recv, w3_in_flight, ya_in, "w3_wait")
    w3_full = g_w3.transpose(1, 0, 2, 3).reshape(3, d, d)
    w_dn_f, w_cf_f, w_o_f = w3_full[0], w3_full[1], w3_full[2]
    ya = _mm(ya_in, w_dn_f, name="ya")
    c1, c3 = _cf_fwd(proj, dw_w_f, dw_b, ln_w, ln_b, d, "cf_fwd")
    yb = _mm(c3, w_cf_f, name="yb")
    merged = _merge(proj, ya, yb, b_cf_out, d, "merge")
    dxo, dxo_mm, loss_acc, g_fnw = _mo_final(merged, w_o_f, xs, final_norm_w.reshape(1, d), tgt, "mo_final")

    gw_o = _mm(merged, dxo_mm, ta=True, out_dtype=MM_DTYPE, name="gw_o")
    dya, dyb, dproj, g_bcf = _merge_bwd(dxo_mm, w_o_f, proj, ya, yb, b_cf_out, d, "merge_bwd")
    dc3 = _mm(dyb, w_cf_f, nt=True, name="d_c3")
    gw_cf = _mm(c3, dyb, ta=True, out_dtype=MM_DTYPE, name="gw_cf")
    dyain = _mm(dya, w_dn_f, nt=True, name="d_ya_in")
    gw_dn = _mm(ya_in, dya, ta=True, out_dtype=MM_DTYPE, name="gw_dn")
    dc1, dproj, g_ln = _cf_bwd1(c1, proj, ln_w, ln_b, dc3, dproj, d, "cf_bwd1")
    dproj, g_dww = _cf_bwd2(dc1, proj, dw_w_f, dproj, d, "cf_bwd2")
    do, dproj, g_dnw = _dn_post_bwd(o, proj, dn_norm_w, dyain, dproj, d, heads, "dn_post_bwd")
    dqkvn, dbg, dg_row = _dn_bwd(qkvn, bg, g_row, do, sall, ainv, d, heads, "dn_bwd")
    dconv = _dn_prep_bwd1(proj, conv_w_f, dqkvn, d, heads, "dn_prep_bwd1")
    dproj, g_convw = _dn_prep_bwd2(dconv, proj, conv_w_f, dproj, d, "dn_prep_bwd2")
    dbg = dbg + jnp.pad(dg_row.transpose(0, 2, 1).reshape(t, heads), ((0, 0), (heads, LANES - 2 * heads)))
    dpba, g_ba = _ba_bwd(pba, pvec, dbg, heads, "ba_bwd")
    gw_main = _mm(h, dproj, ta=True, out_dtype=MM_DTYPE, name="gw_main")
    gw_ba = _mm(h, dpba, ta=True, out_dtype=MM_DTYPE, name="gw_ba")

    g_cols = [(gw_main, 0, 4 * d), (gw_ba, 0, 2 * heads), (gw_main, 4 * d, 2 * d), (gw_main, 8 * d, d),
              (gw_main, 6 * d, 2 * d)]
    send_in = jnp.stack([jnp.concatenate(_columns(g_cols, s * sw, (s + 1) * sw), axis=1) for s in range(4)])
    send_in = send_in.reshape(8, d // 2, sw)
    send_w3 = jnp.stack([gw_dn, gw_cf, gw_o]).reshape(3, 8, d // 8, d).transpose(1, 0, 2, 3).reshape(8, 3 * d // 8, d)
    my_slot = _slot((xi, yi, ci))
    lands = [lax.dynamic_update_slice_in_dim(lax.empty(v.shape, v.dtype), lax.dynamic_slice_in_dim(v, my_slot, 1, axis=0),
                                             my_slot, axis=0) for v in (send_in, send_w3)]
    send_sems, recv_sems, in_flight, token = _alltoall_start([send_in, send_w3], lands, "exchange_start")
    dxs, g_nw = _dh_rms_bwd(dproj, w_main, dpba, w_ba, xs, norm_w, dxo, token, "dh_rms_bwd")
    grad_x = dxs[HDR:][None]
    got_in, got_w3 = _alltoall_wait(send_sems, recv_sems, in_flight, dxs, "exchange_wait")

    wide = lambda a: jnp.pad(a, ((0, 0), (0, d - a.shape[1])))
    small_g = jnp.concatenate([
        g_convw.reshape(24, d), dxs[PAD_ROWS:HDR], g_dww, g_nw, g_ln, g_bcf, g_fnw, wide(g_ba), wide(g_dnw), wide(loss_acc)],
        axis=0)
    red_in = _sum8(got_in, "sum_w_in")
    red_w3 = _sum8(got_w3, "sum_w3")
    pair_in, pair_w3 = _sibling_exchange([red_in, red_w3], "pair_grads")
    g_w_in = pair_in.reshape(d, in_w // 4)
    g_w3 = pair_w3.reshape(2, 3, d // 8, d).transpose(1, 0, 2, 3).reshape(3 * ds, d)
    all_small, = _allgather([small_g], "gather_small_grads")
    sg = _sum8(all_small, "sum_small")
    g_conv = lax.dynamic_slice_in_dim(sg[0:24].reshape(8, 3 * d)[0:DN_TAPS], shard * 3 * ds, 3 * ds, axis=1)
    g_meta = lax.dynamic_slice_in_dim(sg[24:40], shard * ds, ds, axis=1)
    g_dw_w = lax.dynamic_slice_in_dim(sg[40:40 + CF_TAPS], shard * ds, ds, axis=1)
    g_rep = {"norm_w": sg[72:73], "ln_w": sg[80:81], "ln_b": sg[81:82], "dw_b": sg[82:83], "b_cf_out": sg[88:89],
             "final_norm_w": sg[96], "a_log": sg[104:105, heads:2 * heads], "dt_bias": sg[105:106, heads:2 * heads],
             "dn_norm_w": sg[112:113, 0:DK]}
    loss = sg[120, 0]

    res = {}
    dl, nm, nv = _adamw(g_w_in, w_in[0], m_w_in[0], v_w_in[0], "adamw_w_in")
    res["w_in"] = (g_w_in[None], dl[None], nm[None], nv[None])
    stack3 = lambda a, b, c: jnp.concatenate([a[0], b[0], c[0]], axis=0)
    dl, nm, nv = _adamw(g_w3, stack3(w_dn_out, w_cf_out, w_o), stack3(m_w_dn_out, m_w_cf_out, m_w_o),
                        stack3(v_w_dn_out, v_w_cf_out, v_w_o), "adamw_w3")
    for j, nme in enumerate(("w_dn_out", "w_cf_out", "w_o")):
        res[nme] = tuple(a[j * ds:(j + 1) * ds][None] for a in (g_w3, dl, nm, nv))
    names = ["meta", "conv_qkv_w", "dw_w", "norm_w", "dw_b", "ln_w", "ln_b", "b_cf_out", "final_norm_w", "a_log", "dt_bias",
             "dn_norm_w"]
    grads = {"meta": g_meta, "conv_qkv_w": g_conv[None], "dw_w": g_dw_w[None], **g_rep}
    given = dict(meta=(meta, m_meta, v_meta), conv_qkv_w=(conv_qkv_w, m_conv_qkv_w, v_conv_qkv_w), dw_w=(dw_w, m_dw_w, v_dw_w),
                 norm_w=(norm_w, m_norm_w, v_norm_w), dw_b=(dw_b, m_dw_b, v_dw_b), ln_w=(ln_w, m_ln_w, v_ln_w),
                 ln_b=(ln_b, m_ln_b, v_ln_b), b_cf_out=(b_cf_out, m_b_cf_out, v_b_cf_out),
                 final_norm_w=(final_norm_w, m_final_norm_w, v_final_norm_w), a_log=(a_log, m_a_log, v_a_log),
                 dt_bias=(dt_bias, m_dt_bias, v_dt_bias), dn_norm_w=(dn_norm_w, m_dn_norm_w, v_dn_norm_w))
    shapes = [given[nme][0].shape for nme in names]
    packs = [_flat_pack([grads[nme] for nme in names], LANES)] + [_flat_pack([given[nme][j] for nme in names], LANES) for j in range(3)]
    outs = _adamw(*packs, "adamw_small")
    unpacked = [_flat_unpack(p, shapes) for p in (packs[0], *outs)]
    for j, nme in enumerate(names):
        res[nme] = tuple(u[j] for u in unpacked)

    order = ["meta", "norm_w", "w_in", "conv_qkv_w", "a_log", "dt_bias", "dn_norm_w", "w_dn_out", "dw_w", "dw_b", "ln_w", "ln_b",
             "w_cf_out", "b_cf_out", "w_o", "final_norm_w"]
    return (loss, grad_x, *[res[nme][0] for nme in order], *[res[nme][1] for nme in order],
            *[res[nme][2] for nme in order], *[res[nme][3] for nme in order])
```

```python
import jax
import jax.numpy as jnp
from jax import lax
from jax.experimental import pallas as pl
from jax.experimental.pallas import tpu as pltpu

F32 = jnp.float32
MM_DTYPE = jnp.bfloat16
EPS = 1e-6
N_META = 16
HDR = 128
PAD_ROWS = HDR - N_META
CHUNK = 64
DK = 128
CF_TAPS = 31
DN_TAPS = 4
HALO = 64
LANES = 128
ADAM_LR, ADAM_B1, ADAM_B2, ADAM_EPS, ADAM_WD, ADAM_STEP = 0.001, 0.9, 0.999, 1e-08, 0.01, 10
VMEM_LIMIT = 48 * 1024 * 1024
MESH = pl.DeviceIdType.MESH
HI = lax.Precision.HIGHEST
LO = lax.Precision.DEFAULT
NN = (((1,), (0,)), ((), ()))
NT = (((1,), (1,)), ((), ()))
TN = (((0,), (0,)), ((), ()))
ANY = pl.BlockSpec(memory_space=pl.ANY)


def _dot(a, b, dims=NN, prec=LO):
    if prec == LO:
        a, b = a.astype(MM_DTYPE), b.astype(MM_DTYPE)
    return lax.dot_general(a, b, dims, precision=prec, preferred_element_type=F32)


def _split(a):
    hi = a.astype(MM_DTYPE)
    return hi, (a - hi.astype(F32)).astype(MM_DTYPE)


def _dot3(a, b, dims=NN):
    ah, al = _split(a)
    bh, bl = _split(b)
    return _dot(ah, bh, dims) + (_dot(ah, bl, dims) + _dot(al, bh, dims))


def _pick(n, options):
    for o in options:
        if n % o == 0:
            return o
    return n


def _cparams(*sem):
    return pltpu.CompilerParams(dimension_semantics=sem, vmem_limit_bytes=VMEM_LIMIT)


def _sig(x):
    return jax.nn.sigmoid(x)


def _sum_all(a):
    return jnp.sum(jnp.sum(a, axis=-1, keepdims=True), axis=0, keepdims=True)


def _dsilu(x, s):
    return s + x * s * (1.0 - s)


def _shift(win, s):
    n = win.shape[0]
    s = s % n
    return win if s == 0 else pltpu.roll(win, s, 0)


def _row_tile(t):
    return _pick(t, (320, 128))


def _mm(a, b, *, nt=False, ta=False, out_dtype=F32, name):
    k, m = a.shape if ta else a.shape[::-1]
    n = b.shape[0] if nt else b.shape[1]
    tm = _pick(m, (1664, 1024, 640, 512, 384, 256, 128))
    tn = _pick(n, (1024, 768, 512, 256, 128))
    tk = _pick(k, (1664, 1024, 640, 512, 256, 128))
    nk = k // tk
    dims = NT if nt else (TN if ta else NN)

    def body(a_ref, b_ref, o_ref, *acc):
        p = lax.dot_general(a_ref[...], b_ref[...], dims, preferred_element_type=F32)
        if nk == 1:
            o_ref[...] = p.astype(o_ref.dtype)
            return
        acc_ref, = acc
        kk = pl.program_id(2)

        @pl.when(kk == 0)
        def _():
            acc_ref[...] = p

        @pl.when(kk > 0)
        def _():
            acc_ref[...] += p

        @pl.when(kk == nk - 1)
        def _():
            o_ref[...] = acc_ref[...].astype(o_ref.dtype)

    a_spec = pl.BlockSpec((tk, tm), lambda i, j, kk: (kk, i)) if ta else pl.BlockSpec((tm, tk), lambda i, j, kk: (i, kk))
    b_spec = pl.BlockSpec((tn, tk), lambda i, j, kk: (j, kk)) if nt else pl.BlockSpec((tk, tn), lambda i, j, kk: (kk, j))
    return pl.pallas_call(
        body, name=name, grid=(m // tm, n // tn, nk),
        in_specs=[a_spec, b_spec],
        out_specs=pl.BlockSpec((tm, tn), lambda i, j, kk: (i, j)),
        out_shape=jax.ShapeDtypeStruct((m, n), out_dtype),
        scratch_shapes=[] if nk == 1 else [pltpu.VMEM((tm, tn), F32)],
        compiler_params=_cparams("parallel", "parallel", "arbitrary"),
    )(a, b)


def _rms_fwd(xs, w, name):
    t, d = xs.shape
    r = _row_tile(t)

    def body(x_ref, w_ref, h_ref):
        x = x_ref[...]
        rs = lax.rsqrt(jnp.mean(x * x, axis=-1, keepdims=True) + EPS)
        h_ref[...] = (x * rs * w_ref[...]).astype(h_ref.dtype)

    return pl.pallas_call(
        body, name=name, grid=(t // r,),
        in_specs=[pl.BlockSpec((r, d), lambda i: (i, 0)), pl.BlockSpec((1, d), lambda i: (0, 0))],
        out_specs=pl.BlockSpec((r, d), lambda i: (i, 0)),
        out_shape=jax.ShapeDtypeStruct((t, d), MM_DTYPE), compiler_params=_cparams("parallel"),
    )(xs, w)


def _mm_rows(a, b, *, nt, tm, tk, extras, outs, epilogue, name):
    m, k = a.shape
    n = b.shape[0] if nt else b.shape[1]
    assert m % tm == 0 and k % tk == 0
    nk = k // tk
    ne = len(extras)

    def body(a_ref, b_ref, *rest):
        ex, ou = rest[:ne], rest[ne:ne + len(outs)]
        i, kk = pl.program_id(0), pl.program_id(1)
        p = lax.dot_general(a_ref[...], b_ref[...], NT if nt else NN, preferred_element_type=F32)
        if nk == 1:
            epilogue(p, i, ex, ou)
            return
        acc_ref = rest[-1]

        @pl.when(kk == 0)
        def _():
            acc_ref[...] = p

        @pl.when(jnp.logical_and(kk > 0, kk < nk - 1))
        def _():
            acc_ref[...] += p

        @pl.when(kk == nk - 1)
        def _():
            epilogue(acc_ref[...] + p, i, ex, ou)

    b_spec = pl.BlockSpec((n, tk), lambda i, kk: (0, kk)) if nt else pl.BlockSpec((tk, n), lambda i, kk: (kk, 0))
    return pl.pallas_call(
        body, name=name, grid=(m // tm, nk),
        in_specs=[pl.BlockSpec((tm, tk), lambda i, kk: (i, kk)), b_spec] + [s for _, s in extras],
        out_specs=[s for _, s in outs], out_shape=[s for s, _ in outs],
        scratch_shapes=[] if nk == 1 else [pltpu.VMEM((tm, n), F32)],
        compiler_params=_cparams("arbitrary", "arbitrary"),
    )(a, b, *[e for e, _ in extras])


def _dh_rms_bwd(dproj, w_main, dpba, w_ba, xs, w, dres, after, name):
    t, d = xs.shape
    tm = _pick(t, (640, 128))
    tk = _pick(w_main.shape[1], (2304, 768))

    def epilogue(dh, i, ex, ou):
        dpba_ref, wba_ref, x_ref, w_ref, dr_ref, _ = ex
        dx_ref, gw_ref = ou

        @pl.when(i == 0)
        def _():
            gw_ref[...] = jnp.zeros_like(gw_ref)

        dh = dh + lax.dot_general(dpba_ref[...], wba_ref[...], NT, preferred_element_type=F32)
        x = x_ref[...]
        rs = lax.rsqrt(jnp.mean(x * x, axis=-1, keepdims=True) + EPS)
        xh = x * rs
        dxh = dh * w_ref[...]
        dx_ref[...] = rs * (dxh - xh * jnp.mean(dxh * xh, axis=-1, keepdims=True)) + dr_ref[...]
        gw_ref[0:1, :] += jnp.sum(dh * xh, axis=0, keepdims=True)

    row = pl.BlockSpec((tm, d), lambda i, kk: (i, 0))
    const = lambda shape: pl.BlockSpec(shape, lambda i, kk: (0, 0))
    return _mm_rows(
        dproj, w_main, nt=True, tm=tm, tk=tk,
        extras=[(dpba, pl.BlockSpec((tm, LANES), lambda i, kk: (i, 0))), (w_ba, const(w_ba.shape)), (xs, row), (w, const((1, d))),
                (dres, row), (after, ANY)],
        outs=[(jax.ShapeDtypeStruct((t, d), F32), row), (jax.ShapeDtypeStruct((8, d), F32), const((8, d)))],
        epilogue=epilogue, name=name)


def _mo_final(merged, w_o, xs, w, tgt, name):
    t, d = xs.shape
    r = _row_tile(t)

    def epilogue(mo, i, ex, ou):
        x_ref, w_ref, t_ref = ex
        dx_ref, dxm_ref, loss_ref, gw_ref = ou

        @pl.when(i == 0)
        def _():
            loss_ref[...] = jnp.zeros_like(loss_ref)
            gw_ref[...] = jnp.zeros_like(gw_ref)

        xo = x_ref[...] + mo
        rs = lax.rsqrt(jnp.mean(xo * xo, axis=-1, keepdims=True) + EPS)
        xh = xo * rs
        y = xh * w_ref[...]
        rows = i * r + lax.broadcasted_iota(jnp.int32, (r, 1), 0)
        err = jnp.where(rows >= HDR, y - t_ref[...], 0.0)
        loss_ref[...] += 0.5 * _sum_all(err * err) / d
        dy = err / d
        gw_ref[0:1, :] += jnp.sum(dy * xh, axis=0, keepdims=True)
        dxh = dy * w_ref[...]
        dx = rs * (dxh - xh * jnp.mean(dxh * xh, axis=-1, keepdims=True))
        dx_ref[...] = dx
        dxm_ref[...] = dx.astype(dxm_ref.dtype)

    row = pl.BlockSpec((r, d), lambda i, kk: (i, 0))
    const = lambda shape: pl.BlockSpec(shape, lambda i, kk: (0, 0))
    return _mm_rows(
        merged, w_o, nt=False, tm=r, tk=d, extras=[(xs, row), (w, const((1, d))), (tgt, row)],
        outs=[(jax.ShapeDtypeStruct((t, d), F32), row), (jax.ShapeDtypeStruct((t, d), MM_DTYPE), row),
              (jax.ShapeDtypeStruct((8, LANES), F32), const((8, LANES))), (jax.ShapeDtypeStruct((8, d), F32), const((8, d)))],
        epilogue=epilogue, name=name)


def _halo_prev(r):
    return lambda i: (jnp.maximum(i * (r // HALO) - 1, 0), 0)


def _halo_next(r, t):
    return lambda i: (jnp.minimum((i + 1) * (r // HALO), t // HALO - 1), 0)


def _conv_rows(t):
    return _pick(t, (320, 128))


def _qkv_window(p_ref, ph_ref, i, cs):
    return jnp.concatenate([jnp.where(i > 0, ph_ref[HALO - 8:HALO, cs], 0.0), p_ref[:, cs]], axis=0)


def _qkv_conv(win, cw, r):
    y = None
    for s in range(DN_TAPS):
        term = _shift(win, s)[8:8 + r, :] * cw[DN_TAPS - 1 - s:DN_TAPS - s, :]
        y = term if y is None else y + term
    return y


def _dn_prep(proj, conv_w, d, heads, name):
    t = proj.shape[0]
    r = _conv_rows(t)
    w3 = 3 * d

    def body(p_ref, ph_ref, cw_ref, o_ref):
        i = pl.program_id(0)
        for cb in range(w3 // DK):
            cs = slice(cb * DK, (cb + 1) * DK)
            y = _qkv_conv(_qkv_window(p_ref, ph_ref, i, cs), cw_ref[:, cs], r)
            a = y * _sig(y)
            if cb < 2 * heads:
                sc = DK ** -0.5 if cb < heads else 1.0
                a = a * (lax.rsqrt(jnp.sum(a * a, axis=-1, keepdims=True) + EPS) * sc)
            o_ref[:, cs] = a

    return pl.pallas_call(
        body, name=name, grid=(t // r,),
        in_specs=[pl.BlockSpec((r, w3), lambda i: (i, 0)), pl.BlockSpec((HALO, w3), _halo_prev(r)),
                  pl.BlockSpec((DN_TAPS, w3), lambda i: (0, 0))],
        out_specs=pl.BlockSpec((r, w3), lambda i: (i, 0)),
        out_shape=jax.ShapeDtypeStruct((t, w3), F32), compiler_params=_cparams("parallel"),
    )(proj, proj, conv_w)


def _dn_prep_bwd(proj, conv_w, dqkvn, dproj, d, heads, name):
    t = proj.shape[0]
    r = _conv_rows(t)
    w3 = 3 * d
    last = t // r - 1

    def body(p_ref, ph_ref, pn_ref, cw_ref, dn_ref, dnn_ref, _, dp_ref, gw_ref):
        i = pl.program_id(0)

        @pl.when(i == 0)
        def _():
            gw_ref[...] = jnp.zeros_like(gw_ref)

        for cb in range(w3 // DK):
            cs = slice(cb * DK, (cb + 1) * DK)
            cw = cw_ref[:, cs]
            win = jnp.concatenate([_qkv_window(p_ref, ph_ref, i, cs), jnp.where(i < last, pn_ref[0:8, cs], 0.0)], axis=0)
            y = _qkv_conv(win, cw, r + 8)
            sy = _sig(y)
            da = jnp.concatenate([dn_ref[:, cs], jnp.where(i < last, dnn_ref[0:8, cs], 0.0)], axis=0)
            if cb < 2 * heads:
                a = y * sy
                rn = lax.rsqrt(jnp.sum(a * a, axis=-1, keepdims=True) + EPS)
                n0 = a * rn
                dn0 = da * (DK ** -0.5 if cb < heads else 1.0)
                da = rn * (dn0 - n0 * jnp.sum(dn0 * n0, axis=-1, keepdims=True))
            dconv = da * _dsilu(y, sy)
            acc = None
            for j in range(DN_TAPS):
                s = DN_TAPS - 1 - j
                term = _shift(dconv, -s)[0:r, :] * cw[j:j + 1, :]
                acc = term if acc is None else acc + term
                gw_ref[j:j + 1, cs] += jnp.sum(dconv[0:r, :] * _shift(win, s)[8:8 + r, :], axis=0, keepdims=True)
            dp_ref[:, cs] = acc.astype(dp_ref.dtype)

    row = pl.BlockSpec((r, w3), lambda i: (i, 0))
    nxt = pl.BlockSpec((HALO, w3), _halo_next(r, t))
    return pl.pallas_call(
        body, name=name, grid=(t // r,),
        in_specs=[row, pl.BlockSpec((HALO, w3), _halo_prev(r)), nxt, pl.BlockSpec((DN_TAPS, w3), lambda i: (0, 0)), row, nxt, ANY],
        out_specs=[row, pl.BlockSpec((8, w3), lambda i: (0, 0))],
        out_shape=[jax.ShapeDtypeStruct(dproj.shape, dproj.dtype), jax.ShapeDtypeStruct((8, w3), F32)],
        input_output_aliases={6: 0}, compiler_params=_cparams("arbitrary"),
    )(proj, proj, proj, conv_w, dqkvn, dqkvn, dproj)


def _ba_terms(x, pv, heads):
    lane = lax.broadcasted_iota(jnp.int32, x.shape, 1)
    is_b = lane < heads
    is_a = jnp.logical_and(lane >= heads, lane < 2 * heads)
    beta = _sig(x)
    z = x + pv[1:2, :]
    nexp = -jnp.exp(pv[0:1, :])
    sp = jnp.maximum(z, 0.0) + jnp.log1p(jnp.exp(-jnp.abs(z)))
    return is_b, is_a, beta, z, nexp, nexp * sp


def _ba_fwd(pba, pvec, heads, name):
    t = pba.shape[0]
    r = _row_tile(t)

    def body(x_ref, pv_ref, o_ref):
        is_b, is_a, beta, _, _, g = _ba_terms(x_ref[...], pv_ref[...], heads)
        rows = pl.program_id(0) * r + lax.broadcasted_iota(jnp.int32, (r, 1), 0)
        o_ref[...] = jnp.where(rows >= PAD_ROWS, jnp.where(is_b, beta, jnp.where(is_a, g, 0.0)), 0.0)

    row = pl.BlockSpec((r, LANES), lambda i: (i, 0))
    return pl.pallas_call(
        body, name=name, grid=(t // r,), in_specs=[row, pl.BlockSpec((8, LANES), lambda i: (0, 0))],
        out_specs=row, out_shape=jax.ShapeDtypeStruct((t, LANES), F32), compiler_params=_cparams("parallel"),
    )(pba, pvec)


def _ba_bwd(pba, pvec, dbg, heads, name):
    t = pba.shape[0]
    r = _row_tile(t)

    def body(x_ref, pv_ref, d_ref, o_ref, g_ref):
        @pl.when(pl.program_id(0) == 0)
        def _():
            g_ref[...] = jnp.zeros_like(g_ref)

        is_b, is_a, beta, z, nexp, g = _ba_terms(x_ref[...], pv_ref[...], heads)
        rows = pl.program_id(0) * r + lax.broadcasted_iota(jnp.int32, (r, 1), 0)
        dd = jnp.where(rows >= PAD_ROWS, d_ref[...], 0.0)
        dz = dd * nexp * _sig(z)
        o_ref[...] = jnp.where(is_b, dd * beta * (1.0 - beta), jnp.where(is_a, dz, 0.0)).astype(o_ref.dtype)
        g_ref[0:1, :] += jnp.sum(jnp.where(is_a, dd * g, 0.0), axis=0, keepdims=True)
        g_ref[1:2, :] += jnp.sum(jnp.where(is_a, dz, 0.0), axis=0, keepdims=True)

    row = pl.BlockSpec((r, LANES), lambda i: (i, 0))
    par = pl.BlockSpec((8, LANES), lambda i: (0, 0))
    return pl.pallas_call(
        body, name=name, grid=(t // r,), in_specs=[row, par, row], out_specs=[row, par],
        out_shape=[jax.ShapeDtypeStruct((t, LANES), MM_DTYPE), jax.ShapeDtypeStruct((8, LANES), F32)],
        compiler_params=_cparams("arbitrary"),
    )(pba, pvec, dbg)


def _chunk_consts():
    ri = lax.broadcasted_iota(jnp.int32, (CHUNK, CHUNK), 0)
    ci = lax.broadcasted_iota(jnp.int32, (CHUNK, CHUNK), 1)
    return ri >= ci, ri > ci, (ri == ci).astype(F32), (ri >= ci).astype(F32), (ri <= ci).astype(F32)


def _chunk_decay(gc, gr, incl):
    return jnp.where(incl, jnp.exp(jnp.where(incl, gc - gr, 0.0)), 0.0)


def _unit_lower_inverse(n, eye):
    x = [eye + a for a in n]
    p = list(n)
    for _ in range(5):
        p = [_dot(a, a) for a in p]
        x = [a + _dot(a, b) for a, b in zip(x, p)]
    r = [eye - a + _dot3(b, a) for a, b in zip(x, n)]
    return [a + _dot(a, b) for a, b in zip(x, r)]


def _dn_fwd(qkvn, bg, g_row, d, heads, name):
    t = qkvn.shape[0]
    nc = t // CHUNK

    def body(qkv_ref, bg_ref, gr_ref, o_ref, sall_ref, ainv_ref, s_ref):
        @pl.when(pl.program_id(0) == 0)
        def _():
            s_ref[...] = jnp.zeros_like(s_ref)

        incl, strict, eye, tril, triu = _chunk_consts()
        bgv = bg_ref[...]
        gc_all = _dot(tril, bgv, NN, HI)
        gr_all = _dot(gr_ref[0], triu, NN, HI)
        hs = range(heads)
        q = [qkv_ref[:, h * DK:(h + 1) * DK] for h in hs]
        k = [qkv_ref[:, d + h * DK:d + (h + 1) * DK] for h in hs]
        v = [qkv_ref[:, 2 * d + h * DK:2 * d + (h + 1) * DK] for h in hs]
        kb = [a.astype(MM_DTYPE) for a in k]
        beta = [bgv[:, h:h + 1] for h in hs]
        gc = [gc_all[:, heads + h:heads + h + 1] for h in hs]
        decay = [_chunk_decay(gc[h], gr_all[h:h + 1, :], incl) for h in hs]
        eg = [jnp.exp(a) for a in gc]
        n = [jnp.where(strict, -(beta[h] * _dot(kb[h], kb[h], NT) * decay[h]), 0.0) for h in hs]
        x = _unit_lower_inverse(n, eye)
        sol = [_dot3(x[h], jnp.concatenate([v[h] * beta[h], k[h] * (beta[h] * eg[h])], axis=1)) for h in hs]
        attn = [_dot(q[h], kb[h], NT) * decay[h] for h in hs]
        s = [s_ref[h] for h in hs]
        sb = [a.astype(MM_DTYPE) for a in s]
        wv = [sol[h][:, :DK] - _dot(sol[h][:, DK:], sb[h]) for h in hs]
        o = [_dot(q[h] * eg[h], sb[h]) + _dot(attn[h], wv[h]) for h in hs]
        glast = [a[CHUNK - 1:CHUNK, :] for a in gc]
        s_new = [s[h] * jnp.exp(glast[h]) + _dot(k[h] * jnp.exp(glast[h] - gc[h]), wv[h], TN) for h in hs]
        for h in hs:
            o_ref[:, h * DK:(h + 1) * DK] = o[h]
            sall_ref[0, h] = s[h]
            ainv_ref[0, h] = x[h]
            s_ref[h] = s_new[h]

    return pl.pallas_call(
        body, name=name, grid=(nc,),
        in_specs=[pl.BlockSpec((CHUNK, 3 * d), lambda i: (i, 0)), pl.BlockSpec((CHUNK, LANES), lambda i: (i, 0)),
                  pl.BlockSpec((1, heads, CHUNK), lambda i: (i, 0, 0))],
        out_specs=[pl.BlockSpec((CHUNK, d), lambda i: (i, 0)), pl.BlockSpec((1, heads, DK, DK), lambda i: (i, 0, 0, 0)),
                   pl.BlockSpec((1, heads, CHUNK, CHUNK), lambda i: (i, 0, 0, 0))],
        out_shape=[jax.ShapeDtypeStruct((t, d), F32), jax.ShapeDtypeStruct((nc, heads, DK, DK), F32),
                   jax.ShapeDtypeStruct((nc, heads, CHUNK, CHUNK), F32)],
        scratch_shapes=[pltpu.VMEM((heads, DK, DK), F32)], compiler_params=_cparams("arbitrary"),
    )(qkvn, bg, g_row)


def _dn_bwd(qkvn, bg, g_row, do, sall, ainv, d, heads, name):
    t = qkvn.shape[0]
    nc = t // CHUNK
    rev = lambda i: nc - 1 - i

    def body(qkv_ref, bg_ref, gr_ref, do_ref, sall_ref, ainv_ref, dqkv_ref, dbg_ref, dgr_ref, ds_ref):
        @pl.when(pl.program_id(0) == 0)
        def _():
            ds_ref[...] = jnp.zeros_like(ds_ref)

        incl, strict, eye, tril, triu = _chunk_consts()
        bgv = bg_ref[...]
        gc_all = _dot(tril, bgv, NN, HI)
        gr_all = _dot(gr_ref[0], triu, NN, HI)
        lane = lax.broadcasted_iota(jnp.int32, (CHUNK, LANES), 1)
        hrow = lax.broadcasted_iota(jnp.int32, (heads, CHUNK), 0)
        last_row = lax.broadcasted_iota(jnp.int32, (CHUNK, 1), 0) == CHUNK - 1
        dbeta_slab = jnp.zeros((CHUNK, LANES), F32)
        dgc_slab = jnp.zeros((CHUNK, LANES), F32)
        dgr_slab = jnp.zeros((heads, CHUNK), F32)
        hs = range(heads)
        cols = lambda ref, off: [ref[:, off + h * DK:off + (h + 1) * DK] for h in hs]
        qs, ks, vs, douts = cols(qkv_ref, 0), cols(qkv_ref, d), cols(qkv_ref, 2 * d), cols(do_ref, 0)
        ss, xs_, dsns = [sall_ref[0, h] for h in hs], [ainv_ref[0, h] for h in hs], [ds_ref[h] for h in hs]

        def recompute(h, z):
            z.q, z.k, z.v, z.dout, z.s, z.x, z.dsn = qs[h], ks[h], vs[h], douts[h], ss[h], xs_[h], dsns[h]
            z.kb, z.qb, z.sb, z.dsnb, z.doutb = (a.astype(MM_DTYPE) for a in (z.k, z.q, z.s, z.dsn, z.dout))
            z.beta = bgv[:, h:h + 1]
            gc = gc_all[:, heads + h:heads + h + 1]
            z.decay = _chunk_decay(gc, gr_all[h:h + 1, :], incl)
            z.eg = jnp.exp(gc)
            glast = gc[CHUNK - 1:CHUNK, :]
            z.eglast = jnp.exp(glast)
            z.ek = jnp.exp(glast - gc)
            z.kk = _dot(z.kb, z.kb, NT)
            z.qk = _dot(z.qb, z.kb, NT)
            sol = _dot3(z.x, jnp.concatenate([z.v * z.beta, z.k * (z.beta * z.eg)], axis=1))
            z.u, z.wb = sol[:, :DK], sol[:, DK:].astype(MM_DTYPE)
            z.attn = z.qk * z.decay
            z.qg = z.q * z.eg
            z.kend = z.k * z.ek

        def pseudo_values(h, z):
            z.wvb = (z.u - _dot(z.wb, z.sb)).astype(MM_DTYPE)

        def scan_step(h, z):
            z.d_wv = _dot(z.attn, z.doutb, TN) + _dot(z.kend, z.dsnb)
            z.d_attn = _dot(z.doutb, z.wvb, NT)
            z.d_qg = _dot(z.doutb, z.sb, NT)
            z.d_kend = _dot(z.wvb, z.dsnb, NT)
            z.ds_new = _dot(z.qg, z.doutb, TN) + z.eglast * z.dsn
            z.d_glast = z.eglast * _sum_all(z.dsn * z.s) + _sum_all(z.d_kend * z.kend)

        def state_terms(h, z):
            z.ds_new = z.ds_new - _dot(z.wb, z.d_wv, TN)
            z.d_w = -_dot(z.d_wv, z.sb, NT)

        def solve_transpose(h, z):
            d_rhs = _dot3(z.x, jnp.concatenate([z.d_wv, z.d_w], axis=1), TN)
            z.d_ru, z.d_rw = d_rhs[:, :DK], d_rhs[:, DK:]

        def lower_terms(h, z):
            z.d_low = jnp.where(strict, -(_dot(z.d_ru, z.u, NT) + _dot(z.d_rw, z.wb, NT)), 0.0)

        def outputs(h, z):
            rw_k = jnp.sum(z.d_rw * z.k, axis=-1, keepdims=True)
            z.dbeta = (jnp.sum(z.d_ru * z.v, axis=-1, keepdims=True) + rw_k * z.eg
                       + jnp.sum(z.d_low * z.kk * z.decay, axis=-1, keepdims=True))
            z.dv = z.d_ru * z.beta
            d_kk = (z.d_low * z.beta * z.decay).astype(MM_DTYPE)
            d_qk = (z.d_attn * z.decay).astype(MM_DTYPE)
            z.dk = (z.d_rw * (z.beta * z.eg) + _dot(d_kk, z.kb) + _dot(d_kk, z.kb, TN) + _dot(d_qk, z.qb, TN)
                    + z.d_kend * z.ek)
            z.dq = _dot(d_qk, z.kb) + z.d_qg * z.eg
            e = (z.d_low * z.beta * z.kk + z.d_attn * z.qk) * z.decay
            z.dgc = (rw_k * z.beta * z.eg + jnp.sum(e, axis=-1, keepdims=True)
                     + jnp.sum(z.d_qg * z.qg, axis=-1, keepdims=True) - jnp.sum(z.d_kend * z.kend, axis=-1, keepdims=True)
                     + jnp.where(last_row, z.d_glast, 0.0))
            z.dgr = -jnp.sum(e, axis=0, keepdims=True)

        class _Head:
            pass

        st = [_Head() for _ in hs]
        for phase in (recompute, pseudo_values, scan_step, state_terms, solve_transpose, lower_terms, outputs):
            for h in hs:
                phase(h, st[h])
        res = [(z.dq, z.dk, z.dv, z.ds_new, z.dbeta, z.dgc, z.dgr) for z in st]
        for h, (dq, dk, dv, ds_new, dbeta, dgc, dgr) in enumerate(res):
            dqkv_ref[:, h * DK:(h + 1) * DK] = dq
            dqkv_ref[:, d + h * DK:d + (h + 1) * DK] = dk
            dqkv_ref[:, 2 * d + h * DK:2 * d + (h + 1) * DK] = dv
            ds_ref[h] = ds_new
            dbeta_slab = jnp.where(lane == h, dbeta, dbeta_slab)
            dgc_slab = jnp.where(lane == heads + h, dgc, dgc_slab)
            dgr_slab = jnp.where(hrow == h, dgr, dgr_slab)
        dbg_ref[...] = dbeta_slab + _dot(triu, dgc_slab, NN, HI)
        dgr_ref[0] = _dot(dgr_slab, tril, NN, HI)

    return pl.pallas_call(
        body, name=name, grid=(nc,),
        in_specs=[pl.BlockSpec((CHUNK, 3 * d), lambda i: (rev(i), 0)), pl.BlockSpec((CHUNK, LANES), lambda i: (rev(i), 0)),
                  pl.BlockSpec((1, heads, CHUNK), lambda i: (rev(i), 0, 0)), pl.BlockSpec((CHUNK, d), lambda i: (rev(i), 0)),
                  pl.BlockSpec((1, heads, DK, DK), lambda i: (rev(i), 0, 0, 0)),
                  pl.BlockSpec((1, heads, CHUNK, CHUNK), lambda i: (rev(i), 0, 0, 0))],
        out_specs=[pl.BlockSpec((CHUNK, 3 * d), lambda i: (rev(i), 0)), pl.BlockSpec((CHUNK, LANES), lambda i: (rev(i), 0)),
                   pl.BlockSpec((1, heads, CHUNK), lambda i: (rev(i), 0, 0))],
        out_shape=[jax.ShapeDtypeStruct((t, 3 * d), F32), jax.ShapeDtypeStruct((t, LANES), F32),
                   jax.ShapeDtypeStruct((nc, heads, CHUNK), F32)],
        scratch_shapes=[pltpu.VMEM((heads, DK, DK), F32)], compiler_params=_cparams("arbitrary"),
    )(qkvn, bg, g_row, do, sall, ainv)


def _dn_post(o, proj, w, d, heads, name):
    t = o.shape[0]
    r = _row_tile(t)

    def body(o_ref, z_ref, w_ref, y_ref):
        for h in range(heads):
            sl = slice(h * DK, (h + 1) * DK)
            oh, z = o_ref[:, sl], z_ref[:, sl]
            rs = lax.rsqrt(jnp.mean(oh * oh, axis=-1, keepdims=True) + EPS)
            y_ref[:, sl] = (oh * rs * w_ref[...] * (z * _sig(z))).astype(y_ref.dtype)

    row = pl.BlockSpec((r, d), lambda i: (i, 0))
    return pl.pallas_call(
        body, name=name, grid=(t // r,),
        in_specs=[row, pl.BlockSpec((r, d), lambda i: (i, 3)), pl.BlockSpec((1, DK), lambda i: (0, 0))],
        out_specs=row, out_shape=jax.ShapeDtypeStruct((t, d), MM_DTYPE), compiler_params=_cparams("parallel"),
    )(o, proj, w)


def _dn_post_bwd(o, proj, w, dy, dproj, d, heads, name):
    t = o.shape[0]
    r = _row_tile(t)

    def body(o_ref, z_ref, w_ref, dy_ref, _, do_ref, dz_ref, gw_ref):
        @pl.when(pl.program_id(0) == 0)
        def _():
            gw_ref[...] = jnp.zeros_like(gw_ref)

        gw = jnp.zeros((1, DK), F32)
        for h in range(heads):
            sl = slice(h * DK, (h + 1) * DK)
            oh, z, dyh = o_ref[:, sl], z_ref[:, sl], dy_ref[:, sl]
            sz = _sig(z)
            rs = lax.rsqrt(jnp.mean(oh * oh, axis=-1, keepdims=True) + EPS)
            xh = oh * rs
            dn = dyh * (z * sz)
            dz_ref[:, sl] = (dyh * (xh * w_ref[...]) * _dsilu(z, sz)).astype(dz_ref.dtype)
            dxh = dn * w_ref[...]
            do_ref[:, sl] = rs * (dxh - xh * jnp.mean(dxh * xh, axis=-1, keepdims=True))
            gw = gw + jnp.sum(dn * xh, axis=0, keepdims=True)
        gw_ref[0:1, :] += gw

    row = pl.BlockSpec((r, d), lambda i: (i, 0))
    za = pl.BlockSpec((r, d), lambda i: (i, 3))
    return pl.pallas_call(
        body, name=name, grid=(t // r,),
        in_specs=[row, za, pl.BlockSpec((1, DK), lambda i: (0, 0)), row, ANY],
        out_specs=[row, za, pl.BlockSpec((8, DK), lambda i: (0, 0))],
        out_shape=[jax.ShapeDtypeStruct((t, d), F32), jax.ShapeDtypeStruct(dproj.shape, dproj.dtype),
                   jax.ShapeDtypeStruct((8, DK), F32)],
        input_output_aliases={4: 1}, compiler_params=_cparams("arbitrary"),
    )(o, proj, w, dy, dproj)


def _glu_window(g_ref, gh_ref, i, d, cs):
    cs2 = slice(d + cs.start, d + cs.stop)
    cur = g_ref[:, cs] * _sig(g_ref[:, cs2])
    prev = gh_ref[HALO - 32:HALO, cs] * _sig(gh_ref[HALO - 32:HALO, cs2])
    return jnp.concatenate([jnp.where(i > 0, prev, 0.0), cur], axis=0)


def _ln_stats(c1):
    mu = jnp.mean(c1, axis=-1, keepdims=True)
    cc = c1 - mu
    return cc * lax.rsqrt(jnp.mean(cc * cc, axis=-1, keepdims=True) + EPS), lax.rsqrt(jnp.mean(cc * cc, axis=-1, keepdims=True) + EPS)


def _cf_fwd(proj, dw_w, dw_b, ln_w, ln_b, d, name):
    t = proj.shape[0]
    r = _conv_rows(t)

    def body(g_ref, gh_ref, zb_ref, w_ref, b_ref, lw_ref, lb_ref, c1_ref, c3_ref):
        for cb in range(d // LANES):
            cs = slice(cb * LANES, (cb + 1) * LANES)
            win = _glu_window(g_ref, gh_ref, pl.program_id(0), d, cs)
            acc = jnp.broadcast_to(b_ref[:, cs], (r, LANES))
            for sub in range(8):
                rot = _shift(win, sub)
                for s in range(sub, CF_TAPS, 8):
                    acc = acc + rot[32 - (s - sub):32 - (s - sub) + r, :] * w_ref[CF_TAPS - 1 - s:CF_TAPS - s, cs]
            c1_ref[:, cs] = acc
        xh, _ = _ln_stats(c1_ref[...])
        ln = xh * lw_ref[...] + lb_ref[...]
        zb = zb_ref[...]
        c3_ref[...] = ((ln * _sig(ln)) * (zb * _sig(zb))).astype(c3_ref.dtype)

    row = pl.BlockSpec((r, d), lambda i: (i, 0))
    par = pl.BlockSpec((1, d), lambda i: (0, 0))
    return pl.pallas_call(
        body, name=name, grid=(t // r,),
        in_specs=[pl.BlockSpec((r, 2 * d), lambda i: (i, 2)), pl.BlockSpec((HALO, 2 * d), lambda i: (_halo_prev(r)(i)[0], 2)),
                  pl.BlockSpec((r, d), lambda i: (i, 8)), pl.BlockSpec((32, d), lambda i: (0, 0)), par, par, par],
        out_specs=[row, row],
        out_shape=[jax.ShapeDtypeStruct((t, d), F32), jax.ShapeDtypeStruct((t, d), MM_DTYPE)],
        compiler_params=_cparams("parallel"),
    )(proj, proj, proj, dw_w, dw_b, ln_w, ln_b)


def _cf_bwd1(c1, proj, ln_w, ln_b, dc3, dproj, d, name):
    t = c1.shape[0]
    r = _row_tile(t)

    def body(c1_ref, zb_ref, lw_ref, lb_ref, dc3_ref, _, dc1_ref, dzb_ref, g_ref):
        @pl.when(pl.program_id(0) == 0)
        def _():
            g_ref[...] = jnp.zeros_like(g_ref)

        xh, rs = _ln_stats(c1_ref[...])
        ln = xh * lw_ref[...] + lb_ref[...]
        sl, zb, dc3v = _sig(ln), zb_ref[...], dc3_ref[...]
        szb = _sig(zb)
        dzb_ref[...] = (dc3v * (ln * sl) * _dsilu(zb, szb)).astype(dzb_ref.dtype)
        dln = dc3v * (zb * szb) * _dsilu(ln, sl)
        dxh = dln * lw_ref[...]
        dc1 = rs * (dxh - jnp.mean(dxh, axis=-1, keepdims=True) - xh * jnp.mean(dxh * xh, axis=-1, keepdims=True))
        dc1_ref[...] = dc1
        g_ref[0:1, :] += jnp.sum(dln * xh, axis=0, keepdims=True)
        g_ref[1:2, :] += jnp.sum(dln, axis=0, keepdims=True)
        g_ref[2:3, :] += jnp.sum(dc1, axis=0, keepdims=True)

    row = pl.BlockSpec((r, d), lambda i: (i, 0))
    zbs = pl.BlockSpec((r, d), lambda i: (i, 8))
    par = pl.BlockSpec((1, d), lambda i: (0, 0))
    return pl.pallas_call(
        body, name=name, grid=(t // r,), in_specs=[row, zbs, par, par, row, ANY],
        out_specs=[row, zbs, pl.BlockSpec((8, d), lambda i: (0, 0))],
        out_shape=[jax.ShapeDtypeStruct((t, d), F32), jax.ShapeDtypeStruct(dproj.shape, dproj.dtype),
                   jax.ShapeDtypeStruct((8, d), F32)],
        input_output_aliases={5: 1}, compiler_params=_cparams("arbitrary"),
    )(c1, proj, ln_w, ln_b, dc3, dproj)


def _cf_bwd2(dc1, proj, dw_w, dproj, d, name):
    t = dc1.shape[0]
    r = _conv_rows(t)
    last = t // r - 1

    def body(dc_ref, dcn_ref, g_ref, gh_ref, w_ref, _, dg_ref, gw_ref):
        i = pl.program_id(0)

        @pl.when(i == 0)
        def _():
            gw_ref[...] = jnp.zeros_like(gw_ref)

        for cb in range(d // LANES):
            cs = slice(cb * LANES, (cb + 1) * LANES)
            cs2 = slice(d + cs.start, d + cs.stop)
            dcur = dc_ref[:, cs]
            dwin = jnp.concatenate([dcur, jnp.where(i < last, dcn_ref[0:32, cs], 0.0)], axis=0)
            win = _glu_window(g_ref, gh_ref, i, d, cs)
            acc = None
            for sub in range(8):
                drot = _shift(dwin, -sub)
                rot = _shift(win, sub)
                for s in range(sub, CF_TAPS, 8):
                    j = CF_TAPS - 1 - s
                    term = drot[s - sub:s - sub + r, :] * w_ref[j:j + 1, cs]
                    acc = term if acc is None else acc + term
                    gw_ref[j:j + 1, cs] += jnp.sum(dcur * rot[32 - (s - sub):32 - (s - sub) + r, :], axis=0, keepdims=True)
            ga, sb = g_ref[:, cs], _sig(g_ref[:, cs2])
            dg_ref[:, cs] = (acc * sb).astype(dg_ref.dtype)
            dg_ref[:, cs2] = (acc * ga * sb * (1.0 - sb)).astype(dg_ref.dtype)

    row = pl.BlockSpec((r, d), lambda i: (i, 0))
    glu = pl.BlockSpec((r, 2 * d), lambda i: (i, 2))
    return pl.pallas_call(
        body, name=name, grid=(t // r,),
        in_specs=[row, pl.BlockSpec((HALO, d), _halo_next(r, t)), glu,
                  pl.BlockSpec((HALO, 2 * d), lambda i: (_halo_prev(r)(i)[0], 2)), pl.BlockSpec((32, d), lambda i: (0, 0)), ANY],
        out_specs=[glu, pl.BlockSpec((32, d), lambda i: (0, 0))],
        out_shape=[jax.ShapeDtypeStruct(dproj.shape, dproj.dtype), jax.ShapeDtypeStruct((32, d), F32)],
        input_output_aliases={5: 0}, compiler_params=_cparams("arbitrary"),
    )(dc1, dc1, proj, proj, dw_w, dproj)


def _merge(proj, ya, yb, b, d, name):
    t = ya.shape[0]
    r = _row_tile(t)

    def body(g_ref, ya_ref, yb_ref, b_ref, m_ref):
        m_ref[...] = (_sig(g_ref[:, :d]) * ya_ref[...] + _sig(g_ref[:, d:]) * (yb_ref[...] + b_ref[...])).astype(m_ref.dtype)

    row = pl.BlockSpec((r, d), lambda i: (i, 0))
    return pl.pallas_call(
        body, name=name, grid=(t // r,),
        in_specs=[pl.BlockSpec((r, 2 * d), lambda i: (i, 3)), row, row, pl.BlockSpec((1, d), lambda i: (0, 0))],
        out_specs=row, out_shape=jax.ShapeDtypeStruct((t, d), MM_DTYPE), compiler_params=_cparams("parallel"),
    )(proj, ya, yb, b)


def _merge_bwd(dxo_mm, w_o, proj, ya, yb, b, d, name):
    t = ya.shape[0]
    r = _row_tile(t)

    def epilogue(dmv, i, ex, ou):
        g_ref, ya_ref, yb_ref, b_ref = ex
        dya_ref, dyb_ref, dg_ref, gb_ref = ou

        @pl.when(i == 0)
        def _():
            gb_ref[...] = jnp.zeros_like(gb_ref)

        sa, sb = _sig(g_ref[:, :d]), _sig(g_ref[:, d:])
        dyb = dmv * sb
        dya_ref[...] = (dmv * sa).astype(dya_ref.dtype)
        dyb_ref[...] = dyb.astype(dyb_ref.dtype)
        dg_ref[:, :d] = (dmv * ya_ref[...] * sa * (1.0 - sa)).astype(dg_ref.dtype)
        dg_ref[:, d:] = (dmv * (yb_ref[...] + b_ref[...]) * sb * (1.0 - sb)).astype(dg_ref.dtype)
        gb_ref[0:1, :] += jnp.sum(dyb, axis=0, keepdims=True)

    row = pl.BlockSpec((r, d), lambda i, kk: (i, 0))
    gate = pl.BlockSpec((r, 2 * d), lambda i, kk: (i, 3))
    const = lambda shape: pl.BlockSpec(shape, lambda i, kk: (0, 0))
    return _mm_rows(
        dxo_mm, w_o, nt=True, tm=r, tk=d, extras=[(proj, gate), (ya, row), (yb, row), (b, const((1, d)))],
        outs=[(jax.ShapeDtypeStruct((t, d), MM_DTYPE), row), (jax.ShapeDtypeStruct((t, d), MM_DTYPE), row),
              (jax.ShapeDtypeStruct((t, 9 * d), MM_DTYPE), gate), (jax.ShapeDtypeStruct((8, d), F32), const((8, d)))],
        epilogue=epilogue, name=name)


def _place():
    return lax.axis_index("x"), lax.axis_index("y"), lax.axis_index("c")


def _slot(p):
    return 4 * p[0] + 2 * p[1] + p[2]


def _allgather(blocks, name):
    n = len(blocks)

    def body(*refs):
        ins, outs = refs[:n], refs[n:2 * n]
        send_sems, recv_sems, local_sems = refs[2 * n:]
        x, y, c = _place()
        me, sibling = (x, y, c), (x, y, 1 - c)
        chips = [(1 - x, y), (x, 1 - y), (1 - x, 1 - y)]

        def copy(a, k, block, to, src=None):
            dst = outs[a].at[_slot(block)]
            return pltpu.make_async_remote_copy(
                src_ref=dst if src is None else src, dst_ref=dst, send_sem=send_sems.at[7 * a + k],
                recv_sem=recv_sems.at[7 * a + k], device_id=to, device_id_type=MESH)

        mine = [pltpu.make_async_copy(ins[a], outs[a].at[_slot(me)], local_sems.at[a]) for a in range(n)]
        for cp in mine:
            cp.start()
        first = []
        for a in range(n):
            first.append(copy(a, 0, me, sibling, src=ins[a]))
            first += [copy(a, 1 + j, me, (*chip, c), src=ins[a]) for j, chip in enumerate(chips)]
        for cp in first:
            cp.start()
        passed = []
        for j, chip in enumerate(chips):
            for a in range(n):
                copy(a, 1 + j, (*chip, c), me).wait_recv()
                cp = copy(a, 4 + j, (*chip, c), sibling)
                cp.start()
                passed.append(cp)
        for a in range(n):
            copy(a, 0, sibling, me).wait_recv()
            for j, chip in enumerate(chips):
                copy(a, 4 + j, (*chip, 1 - c), me).wait_recv()
        for cp in first + passed:
            cp.wait_send()
        for cp in mine:
            cp.wait()

    return pl.pallas_call(
        body, name=name, in_specs=[ANY] * n, out_specs=[ANY] * n,
        out_shape=[jax.ShapeDtypeStruct((8, *b.shape), b.dtype) for b in blocks],
        scratch_shapes=[pltpu.SemaphoreType.DMA((7 * n,)), pltpu.SemaphoreType.DMA((7 * n,)), pltpu.SemaphoreType.DMA((n,))],
    )(*blocks)


_FLIPS = [(0, 0, 1), (0, 1, 0), (0, 1, 1), (1, 0, 0), (1, 0, 1), (1, 1, 0), (1, 1, 1)]


_HBM = pl.BlockSpec(memory_space=pltpu.HBM)
_SEM = pl.BlockSpec(memory_space=pltpu.SEMAPHORE)
_EFFECT = pltpu.SideEffectType.DATAFLOW_SIDE_EFFECTING


def _peers(me):
    return [tuple(1 - me[ax] if f[ax] else me[ax] for ax in range(3)) for f in _FLIPS]


def _alltoall_copy(srcs, lands, send_sems, recv_sems, a, k, me, peers, own):
    dst = lands[a].at[_slot(me if own else peers[k])]
    src = srcs[a].at[0 if srcs[a].shape[0] == 1 else _slot(peers[k])]
    return pltpu.make_async_remote_copy(
        src_ref=src, dst_ref=dst, send_sem=send_sems.at[7 * a + k], recv_sem=recv_sems.at[7 * a + k],
        device_id=peers[k], device_id_type=MESH)


def _alltoall_start(arrays, lands, name):
    n = len(arrays)

    def body(*refs):
        srcs, lnds = refs[:n], refs[n:2 * n]
        send_sems, recv_sems = refs[2 * n:2 * n + 2]
        token = refs[-1]
        me = _place()
        peers = _peers(me)
        for a in range(n):
            for k in range(7):
                _alltoall_copy(srcs, lnds, send_sems, recv_sems, a, k, me, peers, True).start()
        token[...] = jnp.zeros_like(token)

    hbm = lambda v: pltpu.with_memory_space_constraint(v, pltpu.HBM)
    out = pl.pallas_call(
        body, name=name, in_specs=[_HBM] * (2 * n),
        out_specs=(_SEM, _SEM, *[_HBM] * (2 * n), pl.BlockSpec(memory_space=pltpu.VMEM)),
        out_shape=(pltpu.SemaphoreType.DMA((7 * n,)), pltpu.SemaphoreType.DMA((7 * n,)),
                   *[pltpu.HBM(v.shape, v.dtype) for v in (*arrays, *lands)], jax.ShapeDtypeStruct((8, LANES), F32)),
        input_output_aliases={i: 2 + i for i in range(2 * n)},
        compiler_params=pltpu.CompilerParams(has_side_effects=_EFFECT),
    )(*[hbm(v) for v in (*arrays, *lands)])
    return out[0], out[1], out[2:2 + 2 * n], out[-1]


def _alltoall_wait(send_sems, recv_sems, thru, after, name):
    n = len(thru) // 2

    def body(*refs):
        srcs, lnds = refs[:n], refs[n:2 * n]
        send_sems, recv_sems = refs[2 * n:2 * n + 2]
        me = _place()
        peers = _peers(me)
        for a in range(n):
            for k in range(7):
                _alltoall_copy(srcs, lnds, send_sems, recv_sems, a, k, me, peers, True).wait_send()
                _alltoall_copy(srcs, lnds, send_sems, recv_sems, a, k, me, peers, False).wait_recv()

    out = pl.pallas_call(
        body, name=name, in_specs=[*[_HBM] * (2 * n), _SEM, _SEM, ANY], out_specs=[_HBM] * (2 * n),
        out_shape=[pltpu.HBM(v.shape, v.dtype) for v in thru],
        input_output_aliases={i: i for i in range(2 * n)},
        compiler_params=pltpu.CompilerParams(has_side_effects=_EFFECT),
    )(*thru, send_sems, recv_sems, after)
    return out[n:]


def _sibling_exchange(arrays, name):
    n = len(arrays)

    def body(*refs):
        ins, outs = refs[:n], refs[n:2 * n]
        send_sems, recv_sems, local_sems = refs[2 * n:]
        x, y, c = _place()

        def copy(a, slot):
            return pltpu.make_async_remote_copy(
                src_ref=ins[a], dst_ref=outs[a].at[slot], send_sem=send_sems.at[a], recv_sem=recv_sems.at[a],
                device_id=(x, y, 1 - c), device_id_type=MESH)

        mine = [pltpu.make_async_copy(ins[a], outs[a].at[c], local_sems.at[a]) for a in range(n)]
        for cp in mine:
            cp.start()
        sent = [copy(a, c) for a in range(n)]
        for cp in sent:
            cp.start()
        for a in range(n):
            copy(a, 1 - c).wait_recv()
        for cp in sent:
            cp.wait_send()
        for cp in mine:
            cp.wait()

    vmem = pl.BlockSpec(memory_space=pltpu.VMEM)
    return pl.pallas_call(
        body, name=name, in_specs=[vmem] * n, out_specs=[vmem] * n,
        compiler_params=pltpu.CompilerParams(vmem_limit_bytes=VMEM_LIMIT),
        out_shape=[jax.ShapeDtypeStruct((2, *a.shape), a.dtype) for a in arrays],
        scratch_shapes=[pltpu.SemaphoreType.DMA((n,)), pltpu.SemaphoreType.DMA((n,)), pltpu.SemaphoreType.DMA((n,))],
    )(*arrays)


def _sum8(parts, name):
    _, rr, cc = parts.shape
    r = _pick(rr, (128, 64, 32, 16, 8))

    def body(p_ref, o_ref):
        acc = p_ref[0].astype(F32)
        for j in range(1, 8):
            acc = acc + p_ref[j].astype(F32)
        o_ref[...] = acc

    return pl.pallas_call(
        body, name=name, grid=(rr // r,), in_specs=[pl.BlockSpec((8, r, cc), lambda i: (0, i, 0))],
        out_specs=pl.BlockSpec((r, cc), lambda i: (i, 0)), out_shape=jax.ShapeDtypeStruct((rr, cc), F32),
        compiler_params=_cparams("parallel"),
    )(parts)


def _adamw(g, w, m, v, name):
    rr, cc = g.shape
    r = _pick(rr, (128, 64, 32, 16, 8))

    def body(g_ref, w_ref, m_ref, v_ref, d_ref, nm_ref, nv_ref):
        gv = g_ref[...]
        nm = ADAM_B1 * m_ref[...] + (1.0 - ADAM_B1) * gv
        nv = ADAM_B2 * v_ref[...] + (1.0 - ADAM_B2) * (gv * gv)
        m_hat = nm / (1.0 - ADAM_B1 ** ADAM_STEP)
        v_hat = nv / (1.0 - ADAM_B2 ** ADAM_STEP)
        d_ref[...] = -ADAM_LR * (m_hat / (jnp.sqrt(v_hat) + ADAM_EPS) + ADAM_WD * w_ref[...])
        nm_ref[...] = nm
        nv_ref[...] = nv

    blk = pl.BlockSpec((r, cc), lambda i: (i, 0))
    return pl.pallas_call(
        body, name=name, grid=(rr // r,), in_specs=[blk] * 4, out_specs=[blk] * 3,
        out_shape=[jax.ShapeDtypeStruct((rr, cc), F32)] * 3, compiler_params=_cparams("parallel"),
    )(g, w, m, v)


def _columns(sources, start, stop):
    out, at = [], 0
    for arr, off, width in sources:
        lo, hi = max(start, at), min(stop, at + width)
        if lo < hi:
            out.append(arr[:, off + lo - at:off + hi - at])
        at += width
    return out


def _flat_pack(parts, width):
    flat = jnp.concatenate([p.reshape(-1) for p in parts])
    total = -(-flat.shape[0] // (8 * width)) * (8 * width)
    return jnp.pad(flat, (0, total - flat.shape[0])).reshape(-1, width)


def _flat_unpack(pack, shapes):
    flat = pack.reshape(-1)
    out, at = [], 0
    for s in shapes:
        size = 1
        for e in s:
            size *= e
        out.append(flat[at:at + size].reshape(s))
        at += size
    return out


def kernel(x, meta, norm_w, w_in, conv_qkv_w, a_log, dt_bias, dn_norm_w, w_dn_out, dw_w, dw_b, ln_w, ln_b, w_cf_out, b_cf_out, w_o, final_norm_w, loss_target, m_meta, m_norm_w, m_w_in, m_conv_qkv_w, m_a_log, m_dt_bias, m_dn_norm_w, m_w_dn_out, m_dw_w, m_dw_b, m_ln_w, m_ln_b, m_w_cf_out, m_b_cf_out, m_w_o, m_final_norm_w, v_meta, v_norm_w, v_w_in, v_conv_qkv_w, v_a_log, v_dt_bias, v_dn_norm_w, v_w_dn_out, v_dw_w, v_dw_b, v_ln_w, v_ln_b, v_w_cf_out, v_b_cf_out, v_w_o, v_final_norm_w):
    d = x.shape[-1]
    heads = a_log.shape[-1]
    seq = x.shape[1]
    t = HDR + seq
    nc = t // CHUNK
    in_w = 9 * d + 2 * heads
    assert heads * DK == d and seq % LANES == 0 and w_in.shape[-1] * 4 == in_w
    xi, yi, ci = _place()
    shard = 2 * xi + yi
    ds = d // 4

    w_in_half = lax.dynamic_slice_in_dim(w_in[0].astype(MM_DTYPE), ci * (d // 2), d // 2, axis=0)
    w3_half = lax.dynamic_slice_in_dim(jnp.stack([w_dn_out[0], w_cf_out[0], w_o[0]]).astype(MM_DTYPE), ci * (d // 8), d // 8, axis=1)
    small = jnp.concatenate([
        jnp.pad(conv_qkv_w[0], ((0, 4), (0, 0))), jnp.pad(meta, ((0, 0), (0, 2 * ds))),
        jnp.pad(dw_w[0], ((0, 1), (0, 2 * ds)))], axis=0)
    g_in, g_small = _allgather([w_in_half, small], "gather_weights")
    my_slot = _slot((xi, yi, ci))
    w3_land = lax.dynamic_update_slice_in_dim(lax.empty((8, *w3_half.shape), w3_half.dtype), w3_half[None], my_slot, axis=0)
    w3_send, w3_recv, w3_in_flight, _ = _alltoall_start([w3_half[None]], [w3_land], "w3_start")
    sw = in_w // 4
    g_in = g_in.reshape(4, d, sw)
    w_cols = [(g_in[s], 0, sw) for s in range(4)]
    o_glu = 4 * d + 2 * heads
    w_main = jnp.concatenate(
        _columns(w_cols, 0, 4 * d) + _columns(w_cols, o_glu, o_glu + 2 * d) + _columns(w_cols, o_glu + 3 * d, o_glu + 5 * d)
        + _columns(w_cols, o_glu + 2 * d, o_glu + 3 * d), axis=1)
    w_ba = jnp.pad(jnp.concatenate(_columns(w_cols, 4 * d, o_glu), axis=1), ((0, 0), (0, LANES - 2 * heads)))
    small4 = g_small[0::2]
    conv_w_f = small4[:, 0:DN_TAPS, :].transpose(1, 0, 2).reshape(DN_TAPS, 3 * d)
    meta_f = small4[:, 8:8 + N_META, :ds].transpose(1, 0, 2).reshape(N_META, d)
    dw_w_f = small4[:, 24:56, :ds].transpose(1, 0, 2).reshape(32, d)

    xs = jnp.concatenate([jnp.pad(meta_f, ((PAD_ROWS, 0), (0, 0))), x[0]], axis=0)
    tgt = jnp.pad(loss_target[0], ((HDR, 0), (0, 0)))
    h = _rms_fwd(xs, norm_w, "rms_fwd")
    proj = _mm(h, w_main, name="proj_main")
    pba = _mm(h, w_ba, name="proj_ba")
    qkvn = _dn_prep(proj, conv_w_f, d, heads, "dn_prep")
    pvec = jnp.zeros((8, LANES), F32).at[0, heads:2 * heads].set(a_log[0]).at[1, heads:2 * heads].set(dt_bias[0])
    bg = _ba_fwd(pba, pvec, heads, "ba_fwd")
    g_row = bg[:, heads:2 * heads].reshape(nc, CHUNK, heads).transpose(0, 2, 1)
    o, sall, ainv = _dn_fwd(qkvn, bg, g_row, d, heads, "dn_fwd")
    ya_in = _dn_post(o, proj, dn_norm_w, d, heads, "dn_post")
    g_w3, = _alltoall_wait(w3_send, w3_recv, w3_in_flight, ya_in, "w3_wait")
    w3_full = g_w3.transpose(1, 0, 2, 3).reshape(3, d, d)
    w_dn_f, w_cf_f, w_o_f = w3_full[0], w3_full[1], w3_full[2]
    ya = _mm(ya_in, w_dn_f, name="ya")
    c1, c3 = _cf_fwd(proj, dw_w_f, dw_b, ln_w, ln_b, d, "cf_fwd")
    yb = _mm(c3, w_cf_f, name="yb")
    merged = _merge(proj, ya, yb, b_cf_out, d, "merge")
    dxo, dxo_mm, loss_acc, g_fnw = _mo_final(merged, w_o_f, xs, final_norm_w.reshape(1, d), tgt, "mo_final")

    gw_o = _mm(merged, dxo_mm, ta=True, out_dtype=MM_DTYPE, name="gw_o")
    dya, dyb, dproj, g_bcf = _merge_bwd(dxo_mm, w_o_f, proj, ya, yb, b_cf_out, d, "merge_bwd")
    dc3 = _mm(dyb, w_cf_f, nt=True, name="d_c3")
    gw_cf = _mm(c3, dyb, ta=True, out_dtype=MM_DTYPE, name="gw_cf")
    dyain = _mm(dya, w_dn_f, nt=True, name="d_ya_in")
    gw_dn = _mm(ya_in, dya, ta=True, out_dtype=MM_DTYPE, name="gw_dn")
    own_home = lambda v: lax.dynamic_update_slice_in_dim(
        lax.empty(v.shape, v.dtype), lax.dynamic_slice_in_dim(v, my_slot, 1, axis=0), my_slot, axis=0)
    send_w3 = jnp.stack([gw_dn, gw_cf, gw_o]).reshape(3, 8, d // 8, d).transpose(1, 0, 2, 3).reshape(8, 3 * d // 8, d)
    gw3_send, gw3_recv, gw3_in_flight, _ = _alltoall_start([send_w3], [own_home(send_w3)], "exchange_w3_start")
    dc1, dproj, g_ln = _cf_bwd1(c1, proj, ln_w, ln_b, dc3, dproj, d, "cf_bwd1")
    dproj, g_dww = _cf_bwd2(dc1, proj, dw_w_f, dproj, d, "cf_bwd2")
    do, dproj, g_dnw = _dn_post_bwd(o, proj, dn_norm_w, dyain, dproj, d, heads, "dn_post_bwd")
    dqkvn, dbg, dg_row = _dn_bwd(qkvn, bg, g_row, do, sall, ainv, d, heads, "dn_bwd")
    dproj, g_convw = _dn_prep_bwd(proj, conv_w_f, dqkvn, dproj, d, heads, "dn_prep_bwd")
    dbg = dbg + jnp.pad(dg_row.transpose(0, 2, 1).reshape(t, heads), ((0, 0), (heads, LANES - 2 * heads)))
    dpba, g_ba = _ba_bwd(pba, pvec, dbg, heads, "ba_bwd")
    gw_main = _mm(h, dproj, ta=True, out_dtype=MM_DTYPE, name="gw_main")
    gw_ba = _mm(h, dpba, ta=True, out_dtype=MM_DTYPE, name="gw_ba")

    g_cols = [(gw_main, 0, 4 * d), (gw_ba, 0, 2 * heads), (gw_main, 4 * d, 2 * d), (gw_main, 8 * d, d),
              (gw_main, 6 * d, 2 * d)]
    send_in = jnp.stack([jnp.concatenate(_columns(g_cols, s * sw, (s + 1) * sw), axis=1) for s in range(4)])
    send_in = send_in.reshape(8, d // 2, sw)
    send_sems, recv_sems, in_flight, token = _alltoall_start([send_in], [own_home(send_in)], "exchange_start")
    dxs, g_nw = _dh_rms_bwd(dproj, w_main, dpba, w_ba, xs, norm_w, dxo, token, "dh_rms_bwd")
    grad_x = dxs[HDR:][None]
    got_w3, = _alltoall_wait(gw3_send, gw3_recv, gw3_in_flight, dxs, "exchange_w3_wait")
    got_in, = _alltoall_wait(send_sems, recv_sems, in_flight, dxs, "exchange_wait")

    wide = lambda a: jnp.pad(a, ((0, 0), (0, d - a.shape[1])))
    small_g = jnp.concatenate([
        g_convw.reshape(24, d), dxs[PAD_ROWS:HDR], g_dww, g_nw, g_ln, g_bcf, g_fnw, wide(g_ba), wide(g_dnw), wide(loss_acc)],
        axis=0)
    red_in = _sum8(got_in, "sum_w_in")
    red_w3 = _sum8(got_w3, "sum_w3")
    pair_in, pair_w3 = _sibling_exchange([red_in, red_w3], "pair_grads")
    g_w_in = pair_in.reshape(d, in_w // 4)
    g_w3 = pair_w3.reshape(2, 3, d // 8, d).transpose(1, 0, 2, 3).reshape(3 * ds, d)
    all_small, = _allgather([small_g], "gather_small_grads")
    sg = _sum8(all_small, "sum_small")
    g_conv = lax.dynamic_slice_in_dim(sg[0:24].reshape(8, 3 * d)[0:DN_TAPS], shard * 3 * ds, 3 * ds, axis=1)
    g_meta = lax.dynamic_slice_in_dim(sg[24:40], shard * ds, ds, axis=1)
    g_dw_w = lax.dynamic_slice_in_dim(sg[40:40 + CF_TAPS], shard * ds, ds, axis=1)
    g_rep = {"norm_w": sg[72:73], "ln_w": sg[80:81], "ln_b": sg[81:82], "dw_b": sg[82:83], "b_cf_out": sg[88:89],
             "final_norm_w": sg[96], "a_log": sg[104:105, heads:2 * heads], "dt_bias": sg[105:106, heads:2 * heads],
             "dn_norm_w": sg[112:113, 0:DK]}
    loss = sg[120, 0]

    res = {}
    dl, nm, nv = _adamw(g_w_in, w_in[0], m_w_in[0], v_w_in[0], "adamw_w_in")
    res["w_in"] = (g_w_in[None], dl[None], nm[None], nv[None])
    stack3 = lambda a, b, c: jnp.concatenate([a[0], b[0], c[0]], axis=0)
    dl, nm, nv = _adamw(g_w3, stack3(w_dn_out, w_cf_out, w_o), stack3(m_w_dn_out, m_w_cf_out, m_w_o),
                        stack3(v_w_dn_out, v_w_cf_out, v_w_o), "adamw_w3")
    for j, nme in enumerate(("w_dn_out", "w_cf_out", "w_o")):
        res[nme] = tuple(a[j * ds:(j + 1) * ds][None] for a in (g_w3, dl, nm, nv))
    names = ["meta", "conv_qkv_w", "dw_w", "norm_w", "dw_b", "ln_w", "ln_b", "b_cf_out", "final_norm_w", "a_log", "dt_bias",
             "dn_norm_w"]
    grads = {"meta": g_meta, "conv_qkv_w": g_conv[None], "dw_w": g_dw_w[None], **g_rep}
    given = dict(meta=(meta, m_meta, v_meta), conv_qkv_w=(conv_qkv_w, m_conv_qkv_w, v_conv_qkv_w), dw_w=(dw_w, m_dw_w, v_dw_w),
                 norm_w=(norm_w, m_norm_w, v_norm_w), dw_b=(dw_b, m_dw_b, v_dw_b), ln_w=(ln_w, m_ln_w, v_ln_w),
                 ln_b=(ln_b, m_ln_b, v_ln_b), b_cf_out=(b_cf_out, m_b_cf_out, v_b_cf_out),
                 final_norm_w=(final_norm_w, m_final_norm_w, v_final_norm_w), a_log=(a_log, m_a_log, v_a_log),
                 dt_bias=(dt_bias, m_dt_bias, v_dt_bias), dn_norm_w=(dn_norm_w, m_dn_norm_w, v_dn_norm_w))
    shapes = [given[nme][0].shape for nme in names]
    packs = [_flat_pack([grads[nme] for nme in names], LANES)] + [_flat_pack([given[nme][j] for nme in names], LANES) for j in range(3)]
    outs = _adamw(*packs, "adamw_small")
    unpacked = [_flat_unpack(p, shapes) for p in (packs[0], *outs)]
    for j, nme in enumerate(names):
        res[nme] = tuple(u[j] for u in unpacked)

    order = ["meta", "norm_w", "w_in", "conv_qkv_w", "a_log", "dt_bias", "dn_norm_w", "w_dn_out", "dw_w", "dw_b", "ln_w", "ln_b",
             "w_cf_out", "b_cf_out", "w_o", "final_norm_w"]
    return (loss, grad_x, *[res[nme][0] for nme in order], *[res[nme][1] for nme in order],
            *[res[nme][2] for nme in order], *[res[nme][3] for nme in order])
```

```python
import jax
import jax.numpy as jnp
from jax import lax
from jax.experimental import pallas as pl
from jax.experimental.pallas import tpu as pltpu

F32 = jnp.float32
MM_DTYPE = jnp.bfloat16
EPS = 1e-6
N_META = 16
HDR = 128
PAD_ROWS = HDR - N_META
CHUNK = 64
DK = 128
CF_TAPS = 31
DN_TAPS = 4
HALO = 64
LANES = 128
ADAM_LR, ADAM_B1, ADAM_B2, ADAM_EPS, ADAM_WD, ADAM_STEP = 0.001, 0.9, 0.999, 1e-08, 0.01, 10
VMEM_LIMIT = 48 * 1024 * 1024
MESH = pl.DeviceIdType.MESH
HI = lax.Precision.HIGHEST
LO = lax.Precision.DEFAULT
NN = (((1,), (0,)), ((), ()))
NT = (((1,), (1,)), ((), ()))
TN = (((0,), (0,)), ((), ()))
ANY = pl.BlockSpec(memory_space=pl.ANY)


def _dot(a, b, dims=NN, prec=LO):
    if prec == LO:
        a, b = a.astype(MM_DTYPE), b.astype(MM_DTYPE)
    return lax.dot_general(a, b, dims, precision=prec, preferred_element_type=F32)


def _split(a):
    hi = a.astype(MM_DTYPE)
    return hi, (a - hi.astype(F32)).astype(MM_DTYPE)


def _dot3(a, b, dims=NN):
    ah, al = _split(a)
    bh, bl = _split(b)
    return _dot(ah, bh, dims) + (_dot(ah, bl, dims) + _dot(al, bh, dims))


def _pick(n, options):
    for o in options:
        if n % o == 0:
            return o
    return n


def _cparams(*sem):
    return pltpu.CompilerParams(dimension_semantics=sem, vmem_limit_bytes=VMEM_LIMIT)


def _sig(x):
    return jax.nn.sigmoid(x)


def _sum_all(a):
    return jnp.sum(jnp.sum(a, axis=-1, keepdims=True), axis=0, keepdims=True)


def _dsilu(x, s):
    return s + x * s * (1.0 - s)


def _shift(win, s):
    n = win.shape[0]
    s = s % n
    return win if s == 0 else pltpu.roll(win, s, 0)


def _row_tile(t):
    return _pick(t, (320, 128))


def _mm(a, b, *, nt=False, ta=False, out_dtype=F32, name):
    k, m = a.shape if ta else a.shape[::-1]
    n = b.shape[0] if nt else b.shape[1]
    tm = _pick(m, (1664, 1024, 640, 512, 384, 256, 128))
    tn = _pick(n, (1024, 768, 512, 256, 128))
    tk = _pick(k, (1664, 1024, 640, 512, 256, 128))
    nk = k // tk
    dims = NT if nt else (TN if ta else NN)

    def body(a_ref, b_ref, o_ref, *acc):
        p = lax.dot_general(a_ref[...], b_ref[...], dims, preferred_element_type=F32)
        if nk == 1:
            o_ref[...] = p.astype(o_ref.dtype)
            return
        acc_ref, = acc
        kk = pl.program_id(2)

        @pl.when(kk == 0)
        def _():
            acc_ref[...] = p

        @pl.when(kk > 0)
        def _():
            acc_ref[...] += p

        @pl.when(kk == nk - 1)
        def _():
            o_ref[...] = acc_ref[...].astype(o_ref.dtype)

    a_spec = pl.BlockSpec((tk, tm), lambda i, j, kk: (kk, i)) if ta else pl.BlockSpec((tm, tk), lambda i, j, kk: (i, kk))
    b_spec = pl.BlockSpec((tn, tk), lambda i, j, kk: (j, kk)) if nt else pl.BlockSpec((tk, tn), lambda i, j, kk: (kk, j))
    return pl.pallas_call(
        body, name=name, grid=(m // tm, n // tn, nk),
        in_specs=[a_spec, b_spec],
        out_specs=pl.BlockSpec((tm, tn), lambda i, j, kk: (i, j)),
        out_shape=jax.ShapeDtypeStruct((m, n), out_dtype),
        scratch_shapes=[] if nk == 1 else [pltpu.VMEM((tm, tn), F32)],
        compiler_params=_cparams("parallel", "parallel", "arbitrary"),
    )(a, b)


def _rms_fwd(xs, w, name):
    t, d = xs.shape
    r = _row_tile(t)

    def body(x_ref, w_ref, h_ref):
        x = x_ref[...]
        rs = lax.rsqrt(jnp.mean(x * x, axis=-1, keepdims=True) + EPS)
        h_ref[...] = (x * rs * w_ref[...]).astype(h_ref.dtype)

    return pl.pallas_call(
        body, name=name, grid=(t // r,),
        in_specs=[pl.BlockSpec((r, d), lambda i: (i, 0)), pl.BlockSpec((1, d), lambda i: (0, 0))],
        out_specs=pl.BlockSpec((r, d), lambda i: (i, 0)),
        out_shape=jax.ShapeDtypeStruct((t, d), MM_DTYPE), compiler_params=_cparams("parallel"),
    )(xs, w)


def _mm_rows(a, b, *, nt, tm, tk, extras, outs, epilogue, name, prologue=None, aliases=None):
    m, k = a.shape
    n = b.shape[0] if nt else b.shape[1]
    assert m % tm == 0 and k % tk == 0
    nk = k // tk
    ne = len(extras)

    def body(a_ref, b_ref, *rest):
        ex, ou = rest[:ne], rest[ne:ne + len(outs)]
        i, kk = pl.program_id(0), pl.program_id(1)
        lhs = a_ref[...] if prologue is None else prologue(a_ref, ex, ou)
        p = lax.dot_general(lhs, b_ref[...], NT if nt else NN, preferred_element_type=F32)
        if nk == 1:
            epilogue(p, i, ex, ou)
            return
        acc_ref = rest[-1]

        @pl.when(kk == 0)
        def _():
            acc_ref[...] = p

        @pl.when(jnp.logical_and(kk > 0, kk < nk - 1))
        def _():
            acc_ref[...] += p

        @pl.when(kk == nk - 1)
        def _():
            epilogue(acc_ref[...] + p, i, ex, ou)

    b_spec = pl.BlockSpec((n, tk), lambda i, kk: (0, kk)) if nt else pl.BlockSpec((tk, n), lambda i, kk: (kk, 0))
    return pl.pallas_call(
        body, name=name, grid=(m // tm, nk),
        in_specs=[pl.BlockSpec((tm, tk), lambda i, kk: (i, kk)), b_spec] + [s for _, s in extras],
        out_specs=[s for _, s in outs], out_shape=[s for s, _ in outs],
        scratch_shapes=[] if nk == 1 else [pltpu.VMEM((tm, n), F32)],
        input_output_aliases={2 + e: o for e, o in (aliases or {}).items()},
        compiler_params=_cparams("arbitrary", "arbitrary"),
    )(a, b, *[e for e, _ in extras])


def _dh_rms_bwd(dproj, w_main, dpba, w_ba, xs, w, dres, after, name):
    t, d = xs.shape
    tm = _pick(t, (640, 128))
    tk = _pick(w_main.shape[1], (2304, 768))

    def epilogue(dh, i, ex, ou):
        dpba_ref, wba_ref, x_ref, w_ref, dr_ref, _ = ex
        dx_ref, gw_ref = ou

        @pl.when(i == 0)
        def _():
            gw_ref[...] = jnp.zeros_like(gw_ref)

        dh = dh + lax.dot_general(dpba_ref[...], wba_ref[...], NT, preferred_element_type=F32)
        x = x_ref[...]
        rs = lax.rsqrt(jnp.mean(x * x, axis=-1, keepdims=True) + EPS)
        xh = x * rs
        dxh = dh * w_ref[...]
        dx_ref[...] = rs * (dxh - xh * jnp.mean(dxh * xh, axis=-1, keepdims=True)) + dr_ref[...]
        gw_ref[0:1, :] += jnp.sum(dh * xh, axis=0, keepdims=True)

    row = pl.BlockSpec((tm, d), lambda i, kk: (i, 0))
    const = lambda shape: pl.BlockSpec(shape, lambda i, kk: (0, 0))
    return _mm_rows(
        dproj, w_main, nt=True, tm=tm, tk=tk,
        extras=[(dpba, pl.BlockSpec((tm, LANES), lambda i, kk: (i, 0))), (w_ba, const(w_ba.shape)), (xs, row), (w, const((1, d))),
                (dres, row), (after, ANY)],
        outs=[(jax.ShapeDtypeStruct((t, d), F32), row), (jax.ShapeDtypeStruct((8, d), F32), const((8, d)))],
        epilogue=epilogue, name=name)


def _merge_final(ya, yb, b, proj, w_o, xs, w, tgt, name):
    t, d = xs.shape
    r = _row_tile(t)

    def prologue(ya_ref, ex, ou):
        yb_ref, b_ref, g_ref = ex[3:]
        m_ref = ou[4]
        m_ref[...] = (_sig(g_ref[:, :d]) * ya_ref[...] + _sig(g_ref[:, d:]) * (yb_ref[...] + b_ref[...])).astype(m_ref.dtype)
        return m_ref[...]

    def epilogue(mo, i, ex, ou):
        x_ref, w_ref, t_ref = ex[:3]
        dx_ref, dxm_ref, loss_ref, gw_ref = ou[:4]

        @pl.when(i == 0)
        def _():
            loss_ref[...] = jnp.zeros_like(loss_ref)
            gw_ref[...] = jnp.zeros_like(gw_ref)

        xo = x_ref[...] + mo
        rs = lax.rsqrt(jnp.mean(xo * xo, axis=-1, keepdims=True) + EPS)
        xh = xo * rs
        y = xh * w_ref[...]
        rows = i * r + lax.broadcasted_iota(jnp.int32, (r, 1), 0)
        err = jnp.where(rows >= HDR, y - t_ref[...], 0.0)
        loss_ref[...] += 0.5 * _sum_all(err * err) / d
        dy = err / d
        gw_ref[0:1, :] += jnp.sum(dy * xh, axis=0, keepdims=True)
        dxh = dy * w_ref[...]
        dx = rs * (dxh - xh * jnp.mean(dxh * xh, axis=-1, keepdims=True))
        dx_ref[...] = dx
        dxm_ref[...] = dx.astype(dxm_ref.dtype)

    row = pl.BlockSpec((r, d), lambda i, kk: (i, 0))
    const = lambda shape: pl.BlockSpec(shape, lambda i, kk: (0, 0))
    return _mm_rows(
        ya, w_o, nt=False, tm=r, tk=d,
        extras=[(xs, row), (w, const((1, d))), (tgt, row), (yb, row), (b, const((1, d))),
                (proj, pl.BlockSpec((r, 2 * d), lambda i, kk: (i, 3)))],
        outs=[(jax.ShapeDtypeStruct((t, d), F32), row), (jax.ShapeDtypeStruct((t, d), MM_DTYPE), row),
              (jax.ShapeDtypeStruct((8, LANES), F32), const((8, LANES))), (jax.ShapeDtypeStruct((8, d), F32), const((8, d))),
              (jax.ShapeDtypeStruct((t, d), MM_DTYPE), row)],
        prologue=prologue, epilogue=epilogue, name=name)


def _halo_prev(r):
    return lambda i: (jnp.maximum(i * (r // HALO) - 1, 0), 0)


def _halo_next(r, t):
    return lambda i: (jnp.minimum((i + 1) * (r // HALO), t // HALO - 1), 0)


def _conv_rows(t):
    return _pick(t, (320, 128))


def _qkv_window(p_ref, ph_ref, i, cs):
    return jnp.concatenate([jnp.where(i > 0, ph_ref[HALO - 8:HALO, cs], 0.0), p_ref[:, cs]], axis=0)


def _qkv_conv(win, cw, r):
    y = None
    for s in range(DN_TAPS):
        term = _shift(win, s)[8:8 + r, :] * cw[DN_TAPS - 1 - s:DN_TAPS - s, :]
        y = term if y is None else y + term
    return y


def _dn_prep(proj, conv_w, d, heads, name):
    t = proj.shape[0]
    r = _conv_rows(t)
    w3 = 3 * d

    def body(p_ref, ph_ref, cw_ref, o_ref):
        i = pl.program_id(0)
        for cb in range(w3 // DK):
            cs = slice(cb * DK, (cb + 1) * DK)
            y = _qkv_conv(_qkv_window(p_ref, ph_ref, i, cs), cw_ref[:, cs], r)
            a = y * _sig(y)
            if cb < 2 * heads:
                sc = DK ** -0.5 if cb < heads else 1.0
                a = a * (lax.rsqrt(jnp.sum(a * a, axis=-1, keepdims=True) + EPS) * sc)
            o_ref[:, cs] = a

    return pl.pallas_call(
        body, name=name, grid=(t // r,),
        in_specs=[pl.BlockSpec((r, w3), lambda i: (i, 0)), pl.BlockSpec((HALO, w3), _halo_prev(r)),
                  pl.BlockSpec((DN_TAPS, w3), lambda i: (0, 0))],
        out_specs=pl.BlockSpec((r, w3), lambda i: (i, 0)),
        out_shape=jax.ShapeDtypeStruct((t, w3), F32), compiler_params=_cparams("parallel"),
    )(proj, proj, conv_w)


def _dn_prep_bwd(proj, conv_w, dqkvn, dproj, d, heads, name):
    t = proj.shape[0]
    r = _conv_rows(t)
    w3 = 3 * d
    last = t // r - 1

    def body(p_ref, ph_ref, pn_ref, cw_ref, dn_ref, dnn_ref, _, dp_ref, gw_ref):
        i = pl.program_id(0)

        @pl.when(i == 0)
        def _():
            gw_ref[...] = jnp.zeros_like(gw_ref)

        for cb in range(w3 // DK):
            cs = slice(cb * DK, (cb + 1) * DK)
            cw = cw_ref[:, cs]
            win = jnp.concatenate([_qkv_window(p_ref, ph_ref, i, cs), jnp.where(i < last, pn_ref[0:8, cs], 0.0)], axis=0)
            y = _qkv_conv(win, cw, r + 8)
            sy = _sig(y)
            da = jnp.concatenate([dn_ref[:, cs], jnp.where(i < last, dnn_ref[0:8, cs], 0.0)], axis=0)
            if cb < 2 * heads:
                a = y * sy
                rn = lax.rsqrt(jnp.sum(a * a, axis=-1, keepdims=True) + EPS)
                n0 = a * rn
                dn0 = da * (DK ** -0.5 if cb < heads else 1.0)
                da = rn * (dn0 - n0 * jnp.sum(dn0 * n0, axis=-1, keepdims=True))
            dconv = da * _dsilu(y, sy)
            acc = None
            for j in range(DN_TAPS):
                s = DN_TAPS - 1 - j
                term = _shift(dconv, -s)[0:r, :] * cw[j:j + 1, :]
                acc = term if acc is None else acc + term
                gw_ref[j:j + 1, cs] += jnp.sum(dconv[0:r, :] * _shift(win, s)[8:8 + r, :], axis=0, keepdims=True)
            dp_ref[:, cs] = acc.astype(dp_ref.dtype)

    row = pl.BlockSpec((r, w3), lambda i: (i, 0))
    nxt = pl.BlockSpec((HALO, w3), _halo_next(r, t))
    return pl.pallas_call(
        body, name=name, grid=(t // r,),
        in_specs=[row, pl.BlockSpec((HALO, w3), _halo_prev(r)), nxt, pl.BlockSpec((DN_TAPS, w3), lambda i: (0, 0)), row, nxt, ANY],
        out_specs=[row, pl.BlockSpec((8, w3), lambda i: (0, 0))],
        out_shape=[jax.ShapeDtypeStruct(dproj.shape, dproj.dtype), jax.ShapeDtypeStruct((8, w3), F32)],
        input_output_aliases={6: 0}, compiler_params=_cparams("arbitrary"),
    )(proj, proj, proj, conv_w, dqkvn, dqkvn, dproj)


def _ba_terms(x, pv, heads):
    lane = lax.broadcasted_iota(jnp.int32, x.shape, 1)
    is_b = lane < heads
    is_a = jnp.logical_and(lane >= heads, lane < 2 * heads)
    beta = _sig(x)
    z = x + pv[1:2, :]
    nexp = -jnp.exp(pv[0:1, :])
    sp = jnp.maximum(z, 0.0) + jnp.log1p(jnp.exp(-jnp.abs(z)))
    return is_b, is_a, beta, z, nexp, nexp * sp


def _ba_fwd(pba, pvec, heads, name):
    t = pba.shape[0]
    r = _row_tile(t)

    def body(x_ref, pv_ref, o_ref):
        is_b, is_a, beta, _, _, g = _ba_terms(x_ref[...], pv_ref[...], heads)
        rows = pl.program_id(0) * r + lax.broadcasted_iota(jnp.int32, (r, 1), 0)
        o_ref[...] = jnp.where(rows >= PAD_ROWS, jnp.where(is_b, beta, jnp.where(is_a, g, 0.0)), 0.0)

    row = pl.BlockSpec((r, LANES), lambda i: (i, 0))
    return pl.pallas_call(
        body, name=name, grid=(t // r,), in_specs=[row, pl.BlockSpec((8, LANES), lambda i: (0, 0))],
        out_specs=row, out_shape=jax.ShapeDtypeStruct((t, LANES), F32), compiler_params=_cparams("parallel"),
    )(pba, pvec)


def _ba_bwd(pba, pvec, dbg, heads, name):
    t = pba.shape[0]
    r = _row_tile(t)

    def body(x_ref, pv_ref, d_ref, o_ref, g_ref):
        @pl.when(pl.program_id(0) == 0)
        def _():
            g_ref[...] = jnp.zeros_like(g_ref)

        is_b, is_a, beta, z, nexp, g = _ba_terms(x_ref[...], pv_ref[...], heads)
        rows = pl.program_id(0) * r + lax.broadcasted_iota(jnp.int32, (r, 1), 0)
        dd = jnp.where(rows >= PAD_ROWS, d_ref[...], 0.0)
        dz = dd * nexp * _sig(z)
        o_ref[...] = jnp.where(is_b, dd * beta * (1.0 - beta), jnp.where(is_a, dz, 0.0)).astype(o_ref.dtype)
        g_ref[0:1, :] += jnp.sum(jnp.where(is_a, dd * g, 0.0), axis=0, keepdims=True)
        g_ref[1:2, :] += jnp.sum(jnp.where(is_a, dz, 0.0), axis=0, keepdims=True)

    row = pl.BlockSpec((r, LANES), lambda i: (i, 0))
    par = pl.BlockSpec((8, LANES), lambda i: (0, 0))
    return pl.pallas_call(
        body, name=name, grid=(t // r,), in_specs=[row, par, row], out_specs=[row, par],
        out_shape=[jax.ShapeDtypeStruct((t, LANES), MM_DTYPE), jax.ShapeDtypeStruct((8, LANES), F32)],
        compiler_params=_cparams("arbitrary"),
    )(pba, pvec, dbg)


def _chunk_consts():
    ri = lax.broadcasted_iota(jnp.int32, (CHUNK, CHUNK), 0)
    ci = lax.broadcasted_iota(jnp.int32, (CHUNK, CHUNK), 1)
    return ri >= ci, ri > ci, (ri == ci).astype(F32), (ri >= ci).astype(F32), (ri <= ci).astype(F32)


def _chunk_decay(gc, gr, incl):
    return jnp.where(incl, jnp.exp(jnp.where(incl, gc - gr, 0.0)), 0.0)


def _unit_lower_inverse(n, eye):
    x = [eye + a for a in n]
    p = list(n)
    for _ in range(5):
        p = [_dot(a, a) for a in p]
        x = [a + _dot(a, b) for a, b in zip(x, p)]
    r = [eye - a + _dot3(b, a) for a, b in zip(x, n)]
    return [a + _dot(a, b) for a, b in zip(x, r)]


def _dn_fwd(qkvn, bg, g_row, d, heads, name):
    t = qkvn.shape[0]
    nc = t // CHUNK

    def body(qkv_ref, bg_ref, gr_ref, o_ref, sall_ref, ainv_ref, s_ref):
        @pl.when(pl.program_id(0) == 0)
        def _():
            s_ref[...] = jnp.zeros_like(s_ref)

        incl, strict, eye, tril, triu = _chunk_consts()
        bgv = bg_ref[...]
        gc_all = _dot(tril, bgv, NN, HI)
        gr_all = _dot(gr_ref[0], triu, NN, HI)
        hs = range(heads)
        q = [qkv_ref[:, h * DK:(h + 1) * DK] for h in hs]
        k = [qkv_ref[:, d + h * DK:d + (h + 1) * DK] for h in hs]
        v = [qkv_ref[:, 2 * d + h * DK:2 * d + (h + 1) * DK] for h in hs]
        kb = [a.astype(MM_DTYPE) for a in k]
        beta = [bgv[:, h:h + 1] for h in hs]
        gc = [gc_all[:, heads + h:heads + h + 1] for h in hs]
        decay = [_chunk_decay(gc[h], gr_all[h:h + 1, :], incl) for h in hs]
        eg = [jnp.exp(a) for a in gc]
        n = [jnp.where(strict, -(beta[h] * _dot(kb[h], kb[h], NT) * decay[h]), 0.0) for h in hs]
        x = _unit_lower_inverse(n, eye)
        sol = [_dot3(x[h], jnp.concatenate([v[h] * beta[h], k[h] * (beta[h] * eg[h])], axis=1)) for h in hs]
        attn = [_dot(q[h], kb[h], NT) * decay[h] for h in hs]
        s = [s_ref[h] for h in hs]
        sb = [a.astype(MM_DTYPE) for a in s]
        wv = [sol[h][:, :DK] - _dot(sol[h][:, DK:], sb[h]) for h in hs]
        o = [_dot(q[h] * eg[h], sb[h]) + _dot(attn[h], wv[h]) for h in hs]
        glast = [a[CHUNK - 1:CHUNK, :] for a in gc]
        s_new = [s[h] * jnp.exp(glast[h]) + _dot(k[h] * jnp.exp(glast[h] - gc[h]), wv[h], TN) for h in hs]
        for h in hs:
            o_ref[:, h * DK:(h + 1) * DK] = o[h]
            sall_ref[0, h] = s[h]
            ainv_ref[0, h] = x[h]
            s_ref[h] = s_new[h]

    return pl.pallas_call(
        body, name=name, grid=(nc,),
        in_specs=[pl.BlockSpec((CHUNK, 3 * d), lambda i: (i, 0)), pl.BlockSpec((CHUNK, LANES), lambda i: (i, 0)),
                  pl.BlockSpec((1, heads, CHUNK), lambda i: (i, 0, 0))],
        out_specs=[pl.BlockSpec((CHUNK, d), lambda i: (i, 0)), pl.BlockSpec((1, heads, DK, DK), lambda i: (i, 0, 0, 0)),
                   pl.BlockSpec((1, heads, CHUNK, CHUNK), lambda i: (i, 0, 0, 0))],
        out_shape=[jax.ShapeDtypeStruct((t, d), F32), jax.ShapeDtypeStruct((nc, heads, DK, DK), F32),
                   jax.ShapeDtypeStruct((nc, heads, CHUNK, CHUNK), F32)],
        scratch_shapes=[pltpu.VMEM((heads, DK, DK), F32)], compiler_params=_cparams("arbitrary"),
    )(qkvn, bg, g_row)


def _dn_bwd(qkvn, bg, g_row, do, sall, ainv, d, heads, name):
    t = qkvn.shape[0]
    nc = t // CHUNK
    rev = lambda i: nc - 1 - i

    def body(qkv_ref, bg_ref, gr_ref, do_ref, sall_ref, ainv_ref, dqkv_ref, dbg_ref, dgr_ref, ds_ref):
        @pl.when(pl.program_id(0) == 0)
        def _():
            ds_ref[...] = jnp.zeros_like(ds_ref)

        incl, strict, eye, tril, triu = _chunk_consts()
        bgv = bg_ref[...]
        gc_all = _dot(tril, bgv, NN, HI)
        gr_all = _dot(gr_ref[0], triu, NN, HI)
        lane = lax.broadcasted_iota(jnp.int32, (CHUNK, LANES), 1)
        hrow = lax.broadcasted_iota(jnp.int32, (heads, CHUNK), 0)
        last_row = lax.broadcasted_iota(jnp.int32, (CHUNK, 1), 0) == CHUNK - 1
        dbeta_slab = jnp.zeros((CHUNK, LANES), F32)
        dgc_slab = jnp.zeros((CHUNK, LANES), F32)
        dgr_slab = jnp.zeros((heads, CHUNK), F32)
        hs = range(heads)
        cols = lambda ref, off: [ref[:, off + h * DK:off + (h + 1) * DK] for h in hs]
        qs, ks, vs, douts = cols(qkv_ref, 0), cols(qkv_ref, d), cols(qkv_ref, 2 * d), cols(do_ref, 0)
        ss, xs_, dsns = [sall_ref[0, h] for h in hs], [ainv_ref[0, h] for h in hs], [ds_ref[h] for h in hs]

        def recompute(h, z):
            z.q, z.k, z.v, z.dout, z.s, z.x, z.dsn = qs[h], ks[h], vs[h], douts[h], ss[h], xs_[h], dsns[h]
            z.kb, z.qb, z.sb, z.dsnb, z.doutb = (a.astype(MM_DTYPE) for a in (z.k, z.q, z.s, z.dsn, z.dout))
            z.beta = bgv[:, h:h + 1]
            gc = gc_all[:, heads + h:heads + h + 1]
            z.decay = _chunk_decay(gc, gr_all[h:h + 1, :], incl)
            z.eg = jnp.exp(gc)
            glast = gc[CHUNK - 1:CHUNK, :]
            z.eglast = jnp.exp(glast)
            z.ek = jnp.exp(glast - gc)
            z.kk = _dot(z.kb, z.kb, NT)
            z.qk = _dot(z.qb, z.kb, NT)
            sol = _dot3(z.x, jnp.concatenate([z.v * z.beta, z.k * (z.beta * z.eg)], axis=1))
            z.u, z.wb = sol[:, :DK], sol[:, DK:].astype(MM_DTYPE)
            z.attn = z.qk * z.decay
            z.qg = z.q * z.eg
            z.kend = z.k * z.ek

        def pseudo_values(h, z):
            z.wvb = (z.u - _dot(z.wb, z.sb)).astype(MM_DTYPE)

        def scan_step(h, z):
            z.d_wv = _dot(z.attn, z.doutb, TN) + _dot(z.kend, z.dsnb)
            z.d_attn = _dot(z.doutb, z.wvb, NT)
            z.d_qg = _dot(z.doutb, z.sb, NT)
            z.d_kend = _dot(z.wvb, z.dsnb, NT)
            z.ds_new = _dot(z.qg, z.doutb, TN) + z.eglast * z.dsn
            z.d_glast = z.eglast * _sum_all(z.dsn * z.s) + _sum_all(z.d_kend * z.kend)

        def state_terms(h, z):
            z.ds_new = z.ds_new - _dot(z.wb, z.d_wv, TN)
            z.d_w = -_dot(z.d_wv, z.sb, NT)

        def solve_transpose(h, z):
            d_rhs = _dot3(z.x, jnp.concatenate([z.d_wv, z.d_w], axis=1), TN)
            z.d_ru, z.d_rw = d_rhs[:, :DK], d_rhs[:, DK:]

        def lower_terms(h, z):
            z.d_low = jnp.where(strict, -(_dot(z.d_ru, z.u, NT) + _dot(z.d_rw, z.wb, NT)), 0.0)

        def outputs(h, z):
            rw_k = jnp.sum(z.d_rw * z.k, axis=-1, keepdims=True)
            z.dbeta = (jnp.sum(z.d_ru * z.v, axis=-1, keepdims=True) + rw_k * z.eg
                       + jnp.sum(z.d_low * z.kk * z.decay, axis=-1, keepdims=True))
            z.dv = z.d_ru * z.beta
            d_kk = (z.d_low * z.beta * z.decay).astype(MM_DTYPE)
            d_qk = (z.d_attn * z.decay).astype(MM_DTYPE)
            z.dk = (z.d_rw * (z.beta * z.eg) + _dot(d_kk, z.kb) + _dot(d_kk, z.kb, TN) + _dot(d_qk, z.qb, TN)
                    + z.d_kend * z.ek)
            z.dq = _dot(d_qk, z.kb) + z.d_qg * z.eg
            e = (z.d_low * z.beta * z.kk + z.d_attn * z.qk) * z.decay
            z.dgc = (rw_k * z.beta * z.eg + jnp.sum(e, axis=-1, keepdims=True)
                     + jnp.sum(z.d_qg * z.qg, axis=-1, keepdims=True) - jnp.sum(z.d_kend * z.kend, axis=-1, keepdims=True)
                     + jnp.where(last_row, z.d_glast, 0.0))
            z.dgr = -jnp.sum(e, axis=0, keepdims=True)

        class _Head:
            pass

        st = [_Head() for _ in hs]
        for phase in (recompute, pseudo_values, scan_step, state_terms, solve_transpose, lower_terms, outputs):
            for h in hs:
                phase(h, st[h])
        res = [(z.dq, z.dk, z.dv, z.ds_new, z.dbeta, z.dgc, z.dgr) for z in st]
        for h, (dq, dk, dv, ds_new, dbeta, dgc, dgr) in enumerate(res):
            dqkv_ref[:, h * DK:(h + 1) * DK] = dq
            dqkv_ref[:, d + h * DK:d + (h + 1) * DK] = dk
            dqkv_ref[:, 2 * d + h * DK:2 * d + (h + 1) * DK] = dv
            ds_ref[h] = ds_new
            dbeta_slab = jnp.where(lane == h, dbeta, dbeta_slab)
            dgc_slab = jnp.where(lane == heads + h, dgc, dgc_slab)
            dgr_slab = jnp.where(hrow == h, dgr, dgr_slab)
        dbg_ref[...] = dbeta_slab + _dot(triu, dgc_slab, NN, HI)
        dgr_ref[0] = _dot(dgr_slab, tril, NN, HI)

    return pl.pallas_call(
        body, name=name, grid=(nc,),
        in_specs=[pl.BlockSpec((CHUNK, 3 * d), lambda i: (rev(i), 0)), pl.BlockSpec((CHUNK, LANES), lambda i: (rev(i), 0)),
                  pl.BlockSpec((1, heads, CHUNK), lambda i: (rev(i), 0, 0)), pl.BlockSpec((CHUNK, d), lambda i: (rev(i), 0)),
                  pl.BlockSpec((1, heads, DK, DK), lambda i: (rev(i), 0, 0, 0)),
                  pl.BlockSpec((1, heads, CHUNK, CHUNK), lambda i: (rev(i), 0, 0, 0))],
        out_specs=[pl.BlockSpec((CHUNK, 3 * d), lambda i: (rev(i), 0)), pl.BlockSpec((CHUNK, LANES), lambda i: (rev(i), 0)),
                   pl.BlockSpec((1, heads, CHUNK), lambda i: (rev(i), 0, 0))],
        out_shape=[jax.ShapeDtypeStruct((t, 3 * d), F32), jax.ShapeDtypeStruct((t, LANES), F32),
                   jax.ShapeDtypeStruct((nc, heads, CHUNK), F32)],
        scratch_shapes=[pltpu.VMEM((heads, DK, DK), F32)], compiler_params=_cparams("arbitrary"),
    )(qkvn, bg, g_row, do, sall, ainv)


def _dn_post(o, proj, w, w_dn, d, heads, name):
    t = o.shape[0]
    r = _row_tile(t)

    def prologue(o_ref, ex, ou):
        z_ref, w_ref = ex
        y_ref = ou[0]
        for h in range(heads):
            sl = slice(h * DK, (h + 1) * DK)
            oh, z = o_ref[:, sl], z_ref[:, sl]
            rs = lax.rsqrt(jnp.mean(oh * oh, axis=-1, keepdims=True) + EPS)
            y_ref[:, sl] = (oh * rs * w_ref[...] * (z * _sig(z))).astype(y_ref.dtype)
        return y_ref[...]

    def epilogue(p, i, ex, ou):
        ou[1][...] = p

    row = pl.BlockSpec((r, d), lambda i, kk: (i, 0))
    return _mm_rows(
        o, w_dn, nt=False, tm=r, tk=d,
        extras=[(proj, pl.BlockSpec((r, d), lambda i, kk: (i, 3))), (w, pl.BlockSpec((1, DK), lambda i, kk: (0, 0)))],
        outs=[(jax.ShapeDtypeStruct((t, d), MM_DTYPE), row), (jax.ShapeDtypeStruct((t, d), F32), row)],
        prologue=prologue, epilogue=epilogue, name=name)


def _dn_post_bwd(dya, w_dn, o, proj, w, dproj, d, heads, name):
    t = o.shape[0]
    r = _row_tile(t)

    def epilogue(dy, i, ex, ou):
        o_ref, z_ref, w_ref, _ = ex
        do_ref, dz_ref, gw_ref = ou

        @pl.when(i == 0)
        def _():
            gw_ref[...] = jnp.zeros_like(gw_ref)

        gw = jnp.zeros((1, DK), F32)
        for h in range(heads):
            sl = slice(h * DK, (h + 1) * DK)
            oh, z, dyh = o_ref[:, sl], z_ref[:, sl], dy[:, sl]
            sz = _sig(z)
            rs = lax.rsqrt(jnp.mean(oh * oh, axis=-1, keepdims=True) + EPS)
            xh = oh * rs
            dn = dyh * (z * sz)
            dz_ref[:, sl] = (dyh * (xh * w_ref[...]) * _dsilu(z, sz)).astype(dz_ref.dtype)
            dxh = dn * w_ref[...]
            do_ref[:, sl] = rs * (dxh - xh * jnp.mean(dxh * xh, axis=-1, keepdims=True))
            gw = gw + jnp.sum(dn * xh, axis=0, keepdims=True)
        gw_ref[0:1, :] += gw

    row = pl.BlockSpec((r, d), lambda i, kk: (i, 0))
    za = pl.BlockSpec((r, d), lambda i, kk: (i, 3))
    return _mm_rows(
        dya, w_dn, nt=True, tm=r, tk=d,
        extras=[(o, row), (proj, za), (w, pl.BlockSpec((1, DK), lambda i, kk: (0, 0))), (dproj, ANY)],
        outs=[(jax.ShapeDtypeStruct((t, d), F32), row), (jax.ShapeDtypeStruct(dproj.shape, dproj.dtype), za),
              (jax.ShapeDtypeStruct((8, DK), F32), pl.BlockSpec((8, DK), lambda i, kk: (0, 0)))],
        aliases={3: 1}, epilogue=epilogue, name=name)


def _glu_window(g_ref, gh_ref, i, d, cs):
    cs2 = slice(d + cs.start, d + cs.stop)
    cur = g_ref[:, cs] * _sig(g_ref[:, cs2])
    prev = gh_ref[HALO - 32:HALO, cs] * _sig(gh_ref[HALO - 32:HALO, cs2])
    return jnp.concatenate([jnp.where(i > 0, prev, 0.0), cur], axis=0)


def _ln_stats(c1):
    mu = jnp.mean(c1, axis=-1, keepdims=True)
    cc = c1 - mu
    return cc * lax.rsqrt(jnp.mean(cc * cc, axis=-1, keepdims=True) + EPS), lax.rsqrt(jnp.mean(cc * cc, axis=-1, keepdims=True) + EPS)


def _cf_fwd(proj, dw_w, dw_b, ln_w, ln_b, d, name):
    t = proj.shape[0]
    r = _conv_rows(t)

    def body(g_ref, gh_ref, zb_ref, w_ref, b_ref, lw_ref, lb_ref, c1_ref, c3_ref):
        for cb in range(d // LANES):
            cs = slice(cb * LANES, (cb + 1) * LANES)
            win = _glu_window(g_ref, gh_ref, pl.program_id(0), d, cs)
            acc = jnp.broadcast_to(b_ref[:, cs], (r, LANES))
            for sub in range(8):
                rot = _shift(win, sub)
                for s in range(sub, CF_TAPS, 8):
                    acc = acc + rot[32 - (s - sub):32 - (s - sub) + r, :] * w_ref[CF_TAPS - 1 - s:CF_TAPS - s, cs]
            c1_ref[:, cs] = acc
        xh, _ = _ln_stats(c1_ref[...])
        ln = xh * lw_ref[...] + lb_ref[...]
        zb = zb_ref[...]
        c3_ref[...] = ((ln * _sig(ln)) * (zb * _sig(zb))).astype(c3_ref.dtype)

    row = pl.BlockSpec((r, d), lambda i: (i, 0))
    par = pl.BlockSpec((1, d), lambda i: (0, 0))
    return pl.pallas_call(
        body, name=name, grid=(t // r,),
        in_specs=[pl.BlockSpec((r, 2 * d), lambda i: (i, 2)), pl.BlockSpec((HALO, 2 * d), lambda i: (_halo_prev(r)(i)[0], 2)),
                  pl.BlockSpec((r, d), lambda i: (i, 8)), pl.BlockSpec((32, d), lambda i: (0, 0)), par, par, par],
        out_specs=[row, row],
        out_shape=[jax.ShapeDtypeStruct((t, d), F32), jax.ShapeDtypeStruct((t, d), MM_DTYPE)],
        compiler_params=_cparams("parallel"),
    )(proj, proj, proj, dw_w, dw_b, ln_w, ln_b)


def _cf_bwd1(dyb, w_cf, c1, proj, ln_w, ln_b, dproj, d, name):
    t = c1.shape[0]
    r = _row_tile(t)

    def epilogue(dc3v, i, ex, ou):
        c1_ref, zb_ref, lw_ref, lb_ref, _ = ex
        dc1_ref, dzb_ref, g_ref = ou

        @pl.when(i == 0)
        def _():
            g_ref[...] = jnp.zeros_like(g_ref)

        xh, rs = _ln_stats(c1_ref[...])
        ln = xh * lw_ref[...] + lb_ref[...]
        sl, zb = _sig(ln), zb_ref[...]
        szb = _sig(zb)
        dzb_ref[...] = (dc3v * (ln * sl) * _dsilu(zb, szb)).astype(dzb_ref.dtype)
        dln = dc3v * (zb * szb) * _dsilu(ln, sl)
        dxh = dln * lw_ref[...]
        dc1 = rs * (dxh - jnp.mean(dxh, axis=-1, keepdims=True) - xh * jnp.mean(dxh * xh, axis=-1, keepdims=True))
        dc1_ref[...] = dc1
        g_ref[0:1, :] += jnp.sum(dln * xh, axis=0, keepdims=True)
        g_ref[1:2, :] += jnp.sum(dln, axis=0, keepdims=True)
        g_ref[2:3, :] += jnp.sum(dc1, axis=0, keepdims=True)

    row = pl.BlockSpec((r, d), lambda i, kk: (i, 0))
    zbs = pl.BlockSpec((r, d), lambda i, kk: (i, 8))
    par = pl.BlockSpec((1, d), lambda i, kk: (0, 0))
    return _mm_rows(
        dyb, w_cf, nt=True, tm=r, tk=d, extras=[(c1, row), (proj, zbs), (ln_w, par), (ln_b, par), (dproj, ANY)],
        outs=[(jax.ShapeDtypeStruct((t, d), F32), row), (jax.ShapeDtypeStruct(dproj.shape, dproj.dtype), zbs),
              (jax.ShapeDtypeStruct((8, d), F32), pl.BlockSpec((8, d), lambda i, kk: (0, 0)))],
        aliases={4: 1}, epilogue=epilogue, name=name)


def _cf_bwd2(dc1, proj, dw_w, dproj, d, name):
    t = dc1.shape[0]
    r = _conv_rows(t)
    last = t // r - 1

    def body(dc_ref, dcn_ref, g_ref, gh_ref, w_ref, _, dg_ref, gw_ref):
        i = pl.program_id(0)

        @pl.when(i == 0)
        def _():
            gw_ref[...] = jnp.zeros_like(gw_ref)

        for cb in range(d // LANES):
            cs = slice(cb * LANES, (cb + 1) * LANES)
            cs2 = slice(d + cs.start, d + cs.stop)
            dcur = dc_ref[:, cs]
            dwin = jnp.concatenate([dcur, jnp.where(i < last, dcn_ref[0:32, cs], 0.0)], axis=0)
            win = _glu_window(g_ref, gh_ref, i, d, cs)
            acc = None
            for sub in range(8):
                drot = _shift(dwin, -sub)
                rot = _shift(win, sub)
                for s in range(sub, CF_TAPS, 8):
                    j = CF_TAPS - 1 - s
                    term = drot[s - sub:s - sub + r, :] * w_ref[j:j + 1, cs]
                    acc = term if acc is None else acc + term
                    gw_ref[j:j + 1, cs] += jnp.sum(dcur * rot[32 - (s - sub):32 - (s - sub) + r, :], axis=0, keepdims=True)
            ga, sb = g_ref[:, cs], _sig(g_ref[:, cs2])
            dg_ref[:, cs] = (acc * sb).astype(dg_ref.dtype)
            dg_ref[:, cs2] = (acc * ga * sb * (1.0 - sb)).astype(dg_ref.dtype)

    row = pl.BlockSpec((r, d), lambda i: (i, 0))
    glu = pl.BlockSpec((r, 2 * d), lambda i: (i, 2))
    return pl.pallas_call(
        body, name=name, grid=(t // r,),
        in_specs=[row, pl.BlockSpec((HALO, d), _halo_next(r, t)), glu,
                  pl.BlockSpec((HALO, 2 * d), lambda i: (_halo_prev(r)(i)[0], 2)), pl.BlockSpec((32, d), lambda i: (0, 0)), ANY],
        out_specs=[glu, pl.BlockSpec((32, d), lambda i: (0, 0))],
        out_shape=[jax.ShapeDtypeStruct(dproj.shape, dproj.dtype), jax.ShapeDtypeStruct((32, d), F32)],
        input_output_aliases={5: 0}, compiler_params=_cparams("arbitrary"),
    )(dc1, dc1, proj, proj, dw_w, dproj)


def _merge_bwd(dxo_mm, w_o, proj, ya, yb, b, d, name):
    t = ya.shape[0]
    r = _row_tile(t)

    def epilogue(dmv, i, ex, ou):
        g_ref, ya_ref, yb_ref, b_ref = ex
        dya_ref, dyb_ref, dg_ref, gb_ref = ou

        @pl.when(i == 0)
        def _():
            gb_ref[...] = jnp.zeros_like(gb_ref)

        sa, sb = _sig(g_ref[:, :d]), _sig(g_ref[:, d:])
        dyb = dmv * sb
        dya_ref[...] = (dmv * sa).astype(dya_ref.dtype)
        dyb_ref[...] = dyb.astype(dyb_ref.dtype)
        dg_ref[:, :d] = (dmv * ya_ref[...] * sa * (1.0 - sa)).astype(dg_ref.dtype)
        dg_ref[:, d:] = (dmv * (yb_ref[...] + b_ref[...]) * sb * (1.0 - sb)).astype(dg_ref.dtype)
        gb_ref[0:1, :] += jnp.sum(dyb, axis=0, keepdims=True)

    row = pl.BlockSpec((r, d), lambda i, kk: (i, 0))
    gate = pl.BlockSpec((r, 2 * d), lambda i, kk: (i, 3))
    const = lambda shape: pl.BlockSpec(shape, lambda i, kk: (0, 0))
    return _mm_rows(
        dxo_mm, w_o, nt=True, tm=r, tk=d, extras=[(proj, gate), (ya, row), (yb, row), (b, const((1, d)))],
        outs=[(jax.ShapeDtypeStruct((t, d), MM_DTYPE), row), (jax.ShapeDtypeStruct((t, d), MM_DTYPE), row),
              (jax.ShapeDtypeStruct((t, 9 * d), MM_DTYPE), gate), (jax.ShapeDtypeStruct((8, d), F32), const((8, d)))],
        epilogue=epilogue, name=name)


def _place():
    return lax.axis_index("x"), lax.axis_index("y"), lax.axis_index("c")


def _slot(p):
    return 4 * p[0] + 2 * p[1] + p[2]


def _allgather(blocks, name):
    n = len(blocks)

    def body(*refs):
        ins, outs = refs[:n], refs[n:2 * n]
        send_sems, recv_sems, local_sems = refs[2 * n:]
        x, y, c = _place()
        me, sibling = (x, y, c), (x, y, 1 - c)
        chips = [(1 - x, y), (x, 1 - y), (1 - x, 1 - y)]

        def copy(a, k, block, to, src=None):
            dst = outs[a].at[_slot(block)]
            return pltpu.make_async_remote_copy(
                src_ref=dst if src is None else src, dst_ref=dst, send_sem=send_sems.at[7 * a + k],
                recv_sem=recv_sems.at[7 * a + k], device_id=to, device_id_type=MESH)

        mine = [pltpu.make_async_copy(ins[a], outs[a].at[_slot(me)], local_sems.at[a]) for a in range(n)]
        for cp in mine:
            cp.start()
        first = []
        for a in range(n):
            first.append(copy(a, 0, me, sibling, src=ins[a]))
            first += [copy(a, 1 + j, me, (*chip, c), src=ins[a]) for j, chip in enumerate(chips)]
        for cp in first:
            cp.start()
        passed = []
        for j, chip in enumerate(chips):
            for a in range(n):
                copy(a, 1 + j, (*chip, c), me).wait_recv()
                cp = copy(a, 4 + j, (*chip, c), sibling)
                cp.start()
                passed.append(cp)
        for a in range(n):
            copy(a, 0, sibling, me).wait_recv()
            for j, chip in enumerate(chips):
                copy(a, 4 + j, (*chip, 1 - c), me).wait_recv()
        for cp in first + passed:
            cp.wait_send()
        for cp in mine:
            cp.wait()

    return pl.pallas_call(
        body, name=name, in_specs=[ANY] * n, out_specs=[ANY] * n,
        out_shape=[jax.ShapeDtypeStruct((8, *b.shape), b.dtype) for b in blocks],
        scratch_shapes=[pltpu.SemaphoreType.DMA((7 * n,)), pltpu.SemaphoreType.DMA((7 * n,)), pltpu.SemaphoreType.DMA((n,))],
    )(*blocks)


_FLIPS = [(0, 0, 1), (0, 1, 0), (0, 1, 1), (1, 0, 0), (1, 0, 1), (1, 1, 0), (1, 1, 1)]


_HBM = pl.BlockSpec(memory_space=pltpu.HBM)
_SEM = pl.BlockSpec(memory_space=pltpu.SEMAPHORE)
_EFFECT = pltpu.SideEffectType.DATAFLOW_SIDE_EFFECTING


def _peers(me):
    return [tuple(1 - me[ax] if f[ax] else me[ax] for ax in range(3)) for f in _FLIPS]


def _alltoall_copy(srcs, lands, send_sems, recv_sems, a, k, me, peers, own):
    dst = lands[a].at[_slot(me if own else peers[k])]
    src = srcs[a].at[0 if srcs[a].shape[0] == 1 else _slot(peers[k])]
    return pltpu.make_async_remote_copy(
        src_ref=src, dst_ref=dst, send_sem=send_sems.at[7 * a + k], recv_sem=recv_sems.at[7 * a + k],
        device_id=peers[k], device_id_type=MESH)


def _alltoall_start(arrays, lands, name):
    n = len(arrays)

    def body(*refs):
        srcs, lnds = refs[:n], refs[n:2 * n]
        send_sems, recv_sems = refs[2 * n:2 * n + 2]
        token = refs[-1]
        me = _place()
        peers = _peers(me)
        for a in range(n):
            for k in range(7):
                _alltoall_copy(srcs, lnds, send_sems, recv_sems, a, k, me, peers, True).start()
        token[...] = jnp.zeros_like(token)

    hbm = lambda v: pltpu.with_memory_space_constraint(v, pltpu.HBM)
    out = pl.pallas_call(
        body, name=name, in_specs=[_HBM] * (2 * n),
        out_specs=(_SEM, _SEM, *[_HBM] * (2 * n), pl.BlockSpec(memory_space=pltpu.VMEM)),
        out_shape=(pltpu.SemaphoreType.DMA((7 * n,)), pltpu.SemaphoreType.DMA((7 * n,)),
                   *[pltpu.HBM(v.shape, v.dtype) for v in (*arrays, *lands)], jax.ShapeDtypeStruct((8, LANES), F32)),
        input_output_aliases={i: 2 + i for i in range(2 * n)},
        compiler_params=pltpu.CompilerParams(has_side_effects=_EFFECT),
    )(*[hbm(v) for v in (*arrays, *lands)])
    return out[0], out[1], out[2:2 + 2 * n], out[-1]


def _alltoall_wait(send_sems, recv_sems, thru, after, name):
    n = len(thru) // 2

    def body(*refs):
        srcs, lnds = refs[:n], refs[n:2 * n]
        send_sems, recv_sems = refs[2 * n:2 * n + 2]
        me = _place()
        peers = _peers(me)
        for a in range(n):
            for k in range(7):
                _alltoall_copy(srcs, lnds, send_sems, recv_sems, a, k, me, peers, True).wait_send()
                _alltoall_copy(srcs, lnds, send_sems, recv_sems, a, k, me, peers, False).wait_recv()

    out = pl.pallas_call(
        body, name=name, in_specs=[*[_HBM] * (2 * n), _SEM, _SEM, ANY], out_specs=[_HBM] * (2 * n),
        out_shape=[pltpu.HBM(v.shape, v.dtype) for v in thru],
        input_output_aliases={i: i for i in range(2 * n)},
        compiler_params=pltpu.CompilerParams(has_side_effects=_EFFECT),
    )(*thru, send_sems, recv_sems, after)
    return out[n:]


def _sibling_exchange(arrays, name):
    n = len(arrays)

    def body(*refs):
        ins, outs = refs[:n], refs[n:2 * n]
        send_sems, recv_sems, local_sems = refs[2 * n:]
        x, y, c = _place()

        def copy(a, slot):
            return pltpu.make_async_remote_copy(
                src_ref=ins[a], dst_ref=outs[a].at[slot], send_sem=send_sems.at[a], recv_sem=recv_sems.at[a],
                device_id=(x, y, 1 - c), device_id_type=MESH)

        mine = [pltpu.make_async_copy(ins[a], outs[a].at[c], local_sems.at[a]) for a in range(n)]
        for cp in mine:
            cp.start()
        sent = [copy(a, c) for a in range(n)]
        for cp in sent:
            cp.start()
        for a in range(n):
            copy(a, 1 - c).wait_recv()
        for cp in sent:
            cp.wait_send()
        for cp in mine:
            cp.wait()

    vmem = pl.BlockSpec(memory_space=pltpu.VMEM)
    return pl.pallas_call(
        body, name=name, in_specs=[vmem] * n, out_specs=[vmem] * n,
        compiler_params=pltpu.CompilerParams(vmem_limit_bytes=VMEM_LIMIT),
        out_shape=[jax.ShapeDtypeStruct((2, *a.shape), a.dtype) for a in arrays],
        scratch_shapes=[pltpu.SemaphoreType.DMA((n,)), pltpu.SemaphoreType.DMA((n,)), pltpu.SemaphoreType.DMA((n,))],
    )(*arrays)


def _sum8(parts, name):
    _, rr, cc = parts.shape
    r = _pick(rr, (128, 64, 32, 16, 8))

    def body(p_ref, o_ref):
        acc = p_ref[0].astype(F32)
        for j in range(1, 8):
            acc = acc + p_ref[j].astype(F32)
        o_ref[...] = acc

    return pl.pallas_call(
        body, name=name, grid=(rr // r,), in_specs=[pl.BlockSpec((8, r, cc), lambda i: (0, i, 0))],
        out_specs=pl.BlockSpec((r, cc), lambda i: (i, 0)), out_shape=jax.ShapeDtypeStruct((rr, cc), F32),
        compiler_params=_cparams("parallel"),
    )(parts)


def _adamw(g, w, m, v, name):
    rr, cc = g.shape
    r = _pick(rr, (128, 64, 32, 16, 8))

    def body(g_ref, w_ref, m_ref, v_ref, d_ref, nm_ref, nv_ref):
        gv = g_ref[...]
        nm = ADAM_B1 * m_ref[...] + (1.0 - ADAM_B1) * gv
        nv = ADAM_B2 * v_ref[...] + (1.0 - ADAM_B2) * (gv * gv)
        m_hat = nm / (1.0 - ADAM_B1 ** ADAM_STEP)
        v_hat = nv / (1.0 - ADAM_B2 ** ADAM_STEP)
        d_ref[...] = -ADAM_LR * (m_hat / (jnp.sqrt(v_hat) + ADAM_EPS) + ADAM_WD * w_ref[...])
        nm_ref[...] = nm
        nv_ref[...] = nv

    blk = pl.BlockSpec((r, cc), lambda i: (i, 0))
    return pl.pallas_call(
        body, name=name, grid=(rr // r,), in_specs=[blk] * 4, out_specs=[blk] * 3,
        out_shape=[jax.ShapeDtypeStruct((rr, cc), F32)] * 3, compiler_params=_cparams("parallel"),
    )(g, w, m, v)


def _columns(sources, start, stop):
    out, at = [], 0
    for arr, off, width in sources:
        lo, hi = max(start, at), min(stop, at + width)
        if lo < hi:
            out.append(arr[:, off + lo - at:off + hi - at])
        at += width
    return out


def _flat_pack(parts, width):
    flat = jnp.concatenate([p.reshape(-1) for p in parts])
    total = -(-flat.shape[0] // (8 * width)) * (8 * width)
    return jnp.pad(flat, (0, total - flat.shape[0])).reshape(-1, width)


def _flat_unpack(pack, shapes):
    flat = pack.reshape(-1)
    out, at = [], 0
    for s in shapes:
        size = 1
        for e in s:
            size *= e
        out.append(flat[at:at + size].reshape(s))
        at += size
    return out


def kernel(x, meta, norm_w, w_in, conv_qkv_w, a_log, dt_bias, dn_norm_w, w_dn_out, dw_w, dw_b, ln_w, ln_b, w_cf_out, b_cf_out, w_o, final_norm_w, loss_target, m_meta, m_norm_w, m_w_in, m_conv_qkv_w, m_a_log, m_dt_bias, m_dn_norm_w, m_w_dn_out, m_dw_w, m_dw_b, m_ln_w, m_ln_b, m_w_cf_out, m_b_cf_out, m_w_o, m_final_norm_w, v_meta, v_norm_w, v_w_in, v_conv_qkv_w, v_a_log, v_dt_bias, v_dn_norm_w, v_w_dn_out, v_dw_w, v_dw_b, v_ln_w, v_ln_b, v_w_cf_out, v_b_cf_out, v_w_o, v_final_norm_w):
    d = x.shape[-1]
    heads = a_log.shape[-1]
    seq = x.shape[1]
    t = HDR + seq
    nc = t // CHUNK
    in_w = 9 * d + 2 * heads
    assert heads * DK == d and seq % LANES == 0 and w_in.shape[-1] * 4 == in_w
    xi, yi, ci = _place()
    shard = 2 * xi + yi
    ds = d // 4

    w_in_half = lax.dynamic_slice_in_dim(w_in[0].astype(MM_DTYPE), ci * (d // 2), d // 2, axis=0)
    w3_half = lax.dynamic_slice_in_dim(jnp.stack([w_dn_out[0], w_cf_out[0], w_o[0]]).astype(MM_DTYPE), ci * (d // 8), d // 8, axis=1)
    small = jnp.concatenate([
        jnp.pad(conv_qkv_w[0], ((0, 4), (0, 0))), jnp.pad(meta, ((0, 0), (0, 2 * ds))),
        jnp.pad(dw_w[0], ((0, 1), (0, 2 * ds)))], axis=0)
    g_in, g_small = _allgather([w_in_half, small], "gather_weights")
    my_slot = _slot((xi, yi, ci))
    w3_land = lax.dynamic_update_slice_in_dim(lax.empty((8, *w3_half.shape), w3_half.dtype), w3_half[None], my_slot, axis=0)
    w3_send, w3_recv, w3_in_flight, _ = _alltoall_start([w3_half[None]], [w3_land], "w3_start")
    sw = in_w // 4
    g_in = g_in.reshape(4, d, sw)
    w_cols = [(g_in[s], 0, sw) for s in range(4)]
    o_glu = 4 * d + 2 * heads
    w_main = jnp.concatenate(
        _columns(w_cols, 0, 4 * d) + _columns(w_cols, o_glu, o_glu + 2 * d) + _columns(w_cols, o_glu + 3 * d, o_glu + 5 * d)
        + _columns(w_cols, o_glu + 2 * d, o_glu + 3 * d), axis=1)
    w_ba = jnp.pad(jnp.concatenate(_columns(w_cols, 4 * d, o_glu), axis=1), ((0, 0), (0, LANES - 2 * heads)))
    small4 = g_small[0::2]
    conv_w_f = small4[:, 0:DN_TAPS, :].transpose(1, 0, 2).reshape(DN_TAPS, 3 * d)
    meta_f = small4[:, 8:8 + N_META, :ds].transpose(1, 0, 2).reshape(N_META, d)
    dw_w_f = small4[:, 24:56, :ds].transpose(1, 0, 2).reshape(32, d)

    xs = jnp.concatenate([jnp.pad(meta_f, ((PAD_ROWS, 0), (0, 0))), x[0]], axis=0)
    tgt = jnp.pad(loss_target[0], ((HDR, 0), (0, 0)))
    h = _rms_fwd(xs, norm_w, "rms_fwd")
    proj = _mm(h, w_main, name="proj_main")
    pba = _mm(h, w_ba, name="proj_ba")
    qkvn = _dn_prep(proj, conv_w_f, d, heads, "dn_prep")
    pvec = jnp.zeros((8, LANES), F32).at[0, heads:2 * heads].set(a_log[0]).at[1, heads:2 * heads].set(dt_bias[0])
    bg = _ba_fwd(pba, pvec, heads, "ba_fwd")
    g_row = bg[:, heads:2 * heads].reshape(nc, CHUNK, heads).transpose(0, 2, 1)
    o, sall, ainv = _dn_fwd(qkvn, bg, g_row, d, heads, "dn_fwd")
    g_w3, = _alltoall_wait(w3_send, w3_recv, w3_in_flight, o, "w3_wait")
    w3_full = g_w3.transpose(1, 0, 2, 3).reshape(3, d, d)
    w_dn_f, w_cf_f, w_o_f = w3_full[0], w3_full[1], w3_full[2]
    ya_in, ya = _dn_post(o, proj, dn_norm_w, w_dn_f, d, heads, "dn_post")
    c1, c3 = _cf_fwd(proj, dw_w_f, dw_b, ln_w, ln_b, d, "cf_fwd")
    yb = _mm(c3, w_cf_f, name="yb")
    dxo, dxo_mm, loss_acc, g_fnw, merged = _merge_final(ya, yb, b_cf_out, proj, w_o_f, xs, final_norm_w.reshape(1, d), tgt,
                                                        "merge_final")

    gw_o = _mm(merged, dxo_mm, ta=True, out_dtype=MM_DTYPE, name="gw_o")
    dya, dyb, dproj, g_bcf = _merge_bwd(dxo_mm, w_o_f, proj, ya, yb, b_cf_out, d, "merge_bwd")
    gw_cf = _mm(c3, dyb, ta=True, out_dtype=MM_DTYPE, name="gw_cf")
    gw_dn = _mm(ya_in, dya, ta=True, out_dtype=MM_DTYPE, name="gw_dn")
    own_home = lambda v: lax.dynamic_update_slice_in_dim(
        lax.empty(v.shape, v.dtype), lax.dynamic_slice_in_dim(v, my_slot, 1, axis=0), my_slot, axis=0)
    send_w3 = jnp.stack([gw_dn, gw_cf, gw_o]).reshape(3, 8, d // 8, d).transpose(1, 0, 2, 3).reshape(8, 3 * d // 8, d)
    gw3_send, gw3_recv, gw3_in_flight, _ = _alltoall_start([send_w3], [own_home(send_w3)], "exchange_w3_start")
    dc1, dproj, g_ln = _cf_bwd1(dyb, w_cf_f, c1, proj, ln_w, ln_b, dproj, d, "cf_bwd1")
    dproj, g_dww = _cf_bwd2(dc1, proj, dw_w_f, dproj, d, "cf_bwd2")
    do, dproj, g_dnw = _dn_post_bwd(dya, w_dn_f, o, proj, dn_norm_w, dproj, d, heads, "dn_post_bwd")
    dqkvn, dbg, dg_row = _dn_bwd(qkvn, bg, g_row, do, sall, ainv, d, heads, "dn_bwd")
    dproj, g_convw = _dn_prep_bwd(proj, conv_w_f, dqkvn, dproj, d, heads, "dn_prep_bwd")
    dbg = dbg + jnp.pad(dg_row.transpose(0, 2, 1).reshape(t, heads), ((0, 0), (heads, LANES - 2 * heads)))
    dpba, g_ba = _ba_bwd(pba, pvec, dbg, heads, "ba_bwd")
    gw_main = _mm(h, dproj, ta=True, out_dtype=MM_DTYPE, name="gw_main")
    gw_ba = _mm(h, dpba, ta=True, out_dtype=MM_DTYPE, name="gw_ba")

    g_cols = [(gw_main, 0, 4 * d), (gw_ba, 0, 2 * heads), (gw_main, 4 * d, 2 * d), (gw_main, 8 * d, d),
              (gw_main, 6 * d, 2 * d)]
    send_in = jnp.stack([jnp.concatenate(_columns(g_cols, s * sw, (s + 1) * sw), axis=1) for s in range(4)])
    send_in = send_in.reshape(8, d // 2, sw)
    send_sems, recv_sems, in_flight, token = _alltoall_start([send_in], [own_home(send_in)], "exchange_start")
    dxs, g_nw = _dh_rms_bwd(dproj, w_main, dpba, w_ba, xs, norm_w, dxo, token, "dh_rms_bwd")
    grad_x = dxs[HDR:][None]
    got_w3, = _alltoall_wait(gw3_send, gw3_recv, gw3_in_flight, dxs, "exchange_w3_wait")
    got_in, = _alltoall_wait(send_sems, recv_sems, in_flight, dxs, "exchange_wait")

    wide = lambda a: jnp.pad(a, ((0, 0), (0, d - a.shape[1])))
    small_g = jnp.concatenate([
        g_convw.reshape(24, d), dxs[PAD_ROWS:HDR], g_dww, g_nw, g_ln, g_bcf, g_fnw, wide(g_ba), wide(g_dnw), wide(loss_acc)],
        axis=0)
    red_in = _sum8(got_in, "sum_w_in")
    red_w3 = _sum8(got_w3, "sum_w3")
    pair_in, pair_w3 = _sibling_exchange([red_in, red_w3], "pair_grads")
    g_w_in = pair_in.reshape(d, in_w // 4)
    g_w3 = pair_w3.reshape(2, 3, d // 8, d).transpose(1, 0, 2, 3).reshape(3 * ds, d)
    all_small, = _allgather([small_g], "gather_small_grads")
    sg = _sum8(all_small, "sum_small")
    g_conv = lax.dynamic_slice_in_dim(sg[0:24].reshape(8, 3 * d)[0:DN_TAPS], shard * 3 * ds, 3 * ds, axis=1)
    g_meta = lax.dynamic_slice_in_dim(sg[24:40], shard * ds, ds, axis=1)
    g_dw_w = lax.dynamic_slice_in_dim(sg[40:40 + CF_TAPS], shard * ds, ds, axis=1)
    g_rep = {"norm_w": sg[72:73], "ln_w": sg[80:81], "ln_b": sg[81:82], "dw_b": sg[82:83], "b_cf_out": sg[88:89],
             "final_norm_w": sg[96], "a_log": sg[104:105, heads:2 * heads], "dt_bias": sg[105:106, heads:2 * heads],
             "dn_norm_w": sg[112:113, 0:DK]}
    loss = sg[120, 0]

    res = {}
    dl, nm, nv = _adamw(g_w_in, w_in[0], m_w_in[0], v_w_in[0], "adamw_w_in")
    res["w_in"] = (g_w_in[None], dl[None], nm[None], nv[None])
    stack3 = lambda a, b, c: jnp.concatenate([a[0], b[0], c[0]], axis=0)
    dl, nm, nv = _adamw(g_w3, stack3(w_dn_out, w_cf_out, w_o), stack3(m_w_dn_out, m_w_cf_out, m_w_o),
                        stack3(v_w_dn_out, v_w_cf_out, v_w_o), "adamw_w3")
    for j, nme in enumerate(("w_dn_out", "w_cf_out", "w_o")):
        res[nme] = tuple(a[j * ds:(j + 1) * ds][None] for a in (g_w3, dl, nm, nv))
    names = ["meta", "conv_qkv_w", "dw_w", "norm_w", "dw_b", "ln_w", "ln_b", "b_cf_out", "final_norm_w", "a_log", "dt_bias",
             "dn_norm_w"]
    grads = {"meta": g_meta, "conv_qkv_w": g_conv[None], "dw_w": g_dw_w[None], **g_rep}
    given = dict(meta=(meta, m_meta, v_meta), conv_qkv_w=(conv_qkv_w, m_conv_qkv_w, v_conv_qkv_w), dw_w=(dw_w, m_dw_w, v_dw_w),
                 norm_w=(norm_w, m_norm_w, v_norm_w), dw_b=(dw_b, m_dw_b, v_dw_b), ln_w=(ln_w, m_ln_w, v_ln_w),
                 ln_b=(ln_b, m_ln_b, v_ln_b), b_cf_out=(b_cf_out, m_b_cf_out, v_b_cf_out),
                 final_norm_w=(final_norm_w, m_final_norm_w, v_final_norm_w), a_log=(a_log, m_a_log, v_a_log),
                 dt_bias=(dt_bias, m_dt_bias, v_dt_bias), dn_norm_w=(dn_norm_w, m_dn_norm_w, v_dn_norm_w))
    shapes = [given[nme][0].shape for nme in names]
    packs = [_flat_pack([grads[nme] for nme in names], LANES)] + [_flat_pack([given[nme][j] for nme in names], LANES) for j in range(3)]
    outs = _adamw(*packs, "adamw_small")
    unpacked = [_flat_unpack(p, shapes) for p in (packs[0], *outs)]
    for j, nme in enumerate(names):
        res[nme] = tuple(u[j] for u in unpacked)

    order = ["meta", "norm_w", "w_in", "conv_qkv_w", "a_log", "dt_bias", "dn_norm_w", "w_dn_out", "dw_w", "dw_b", "ln_w", "ln_b",
             "w_cf_out", "b_cf_out", "w_o", "final_norm_w"]
    return (loss, grad_x, *[res[nme][0] for nme in order], *[res[nme][1] for nme in order],
            *[res[nme][2] for nme in order], *[res[nme][3] for nme in order])
```

```python
import jax
import jax.numpy as jnp
from jax import lax
from jax.experimental import pallas as pl
from jax.experimental.pallas import tpu as pltpu

F32 = jnp.float32
MM_DTYPE = jnp.bfloat16
EPS = 1e-6
N_META = 16
HDR = 128
PAD_ROWS = HDR - N_META
CHUNK = 64
DK = 128
CF_TAPS = 31
DN_TAPS = 4
HALO = 64
LANES = 128
ADAM_LR, ADAM_B1, ADAM_B2, ADAM_EPS, ADAM_WD, ADAM_STEP = 0.001, 0.9, 0.999, 1e-08, 0.01, 10
VMEM_LIMIT = 48 * 1024 * 1024
MESH = pl.DeviceIdType.MESH
HI = lax.Precision.HIGHEST
LO = lax.Precision.DEFAULT
NN = (((1,), (0,)), ((), ()))
NT = (((1,), (1,)), ((), ()))
TN = (((0,), (0,)), ((), ()))
ANY = pl.BlockSpec(memory_space=pl.ANY)


def _dot(a, b, dims=NN, prec=LO):
    if prec == LO:
        a, b = a.astype(MM_DTYPE), b.astype(MM_DTYPE)
    return lax.dot_general(a, b, dims, precision=prec, preferred_element_type=F32)


def _split(a):
    hi = a.astype(MM_DTYPE)
    return hi, (a - hi.astype(F32)).astype(MM_DTYPE)


def _dot3(a, b, dims=NN):
    ah, al = _split(a)
    bh, bl = _split(b)
    return _dot(ah, bh, dims) + (_dot(ah, bl, dims) + _dot(al, bh, dims))


def _pick(n, options):
    for o in options:
        if n % o == 0:
            return o
    return n


def _cparams(*sem):
    return pltpu.CompilerParams(dimension_semantics=sem, vmem_limit_bytes=VMEM_LIMIT)


def _sig(x):
    return jax.nn.sigmoid(x)


def _sum_all(a):
    return jnp.sum(jnp.sum(a, axis=-1, keepdims=True), axis=0, keepdims=True)


def _dsilu(x, s):
    return s + x * s * (1.0 - s)


def _shift(win, s):
    n = win.shape[0]
    s = s % n
    return win if s == 0 else pltpu.roll(win, s, 0)


def _row_tile(t):
    return _pick(t, (320, 128))


def _mm(a, b, *, nt=False, ta=False, out_dtype=F32, name):
    k, m = a.shape if ta else a.shape[::-1]
    n = b.shape[0] if nt else b.shape[1]
    tm = _pick(m, (1664, 1024, 640, 512, 384, 256, 128))
    tn = _pick(n, (1024, 768, 512, 256, 128))
    tk = _pick(k, (1664, 1024, 640, 512, 256, 128))
    nk = k // tk
    dims = NT if nt else (TN if ta else NN)

    def body(a_ref, b_ref, o_ref, *acc):
        p = lax.dot_general(a_ref[...], b_ref[...], dims, preferred_element_type=F32)
        if nk == 1:
            o_ref[...] = p.astype(o_ref.dtype)
            return
        acc_ref, = acc
        kk = pl.program_id(2)

        @pl.when(kk == 0)
        def _():
            acc_ref[...] = p

        @pl.when(kk > 0)
        def _():
            acc_ref[...] += p

        @pl.when(kk == nk - 1)
        def _():
            o_ref[...] = acc_ref[...].astype(o_ref.dtype)

    a_spec = pl.BlockSpec((tk, tm), lambda i, j, kk: (kk, i)) if ta else pl.BlockSpec((tm, tk), lambda i, j, kk: (i, kk))
    b_spec = pl.BlockSpec((tn, tk), lambda i, j, kk: (j, kk)) if nt else pl.BlockSpec((tk, tn), lambda i, j, kk: (kk, j))
    return pl.pallas_call(
        body, name=name, grid=(m // tm, n // tn, nk),
        in_specs=[a_spec, b_spec],
        out_specs=pl.BlockSpec((tm, tn), lambda i, j, kk: (i, j)),
        out_shape=jax.ShapeDtypeStruct((m, n), out_dtype),
        scratch_shapes=[] if nk == 1 else [pltpu.VMEM((tm, tn), F32)],
        compiler_params=_cparams("parallel", "parallel", "arbitrary"),
    )(a, b)


def _rms_fwd(xs, w, name):
    t, d = xs.shape
    r = _row_tile(t)

    def body(x_ref, w_ref, h_ref):
        x = x_ref[...]
        rs = lax.rsqrt(jnp.mean(x * x, axis=-1, keepdims=True) + EPS)
        h_ref[...] = (x * rs * w_ref[...]).astype(h_ref.dtype)

    return pl.pallas_call(
        body, name=name, grid=(t // r,),
        in_specs=[pl.BlockSpec((r, d), lambda i: (i, 0)), pl.BlockSpec((1, d), lambda i: (0, 0))],
        out_specs=pl.BlockSpec((r, d), lambda i: (i, 0)),
        out_shape=jax.ShapeDtypeStruct((t, d), MM_DTYPE), compiler_params=_cparams("parallel"),
    )(xs, w)


def _mm_rows(a, b, *, nt, tm, tk, extras, outs, epilogue, name, prologue=None, aliases=None):
    m, k = a.shape
    n = b.shape[0] if nt else b.shape[1]
    assert m % tm == 0 and k % tk == 0
    nk = k // tk
    ne = len(extras)

    def body(a_ref, b_ref, *rest):
        ex, ou = rest[:ne], rest[ne:ne + len(outs)]
        i, kk = pl.program_id(0), pl.program_id(1)
        lhs = a_ref[...] if prologue is None else prologue(a_ref, ex, ou)
        p = lax.dot_general(lhs, b_ref[...], NT if nt else NN, preferred_element_type=F32)
        if nk == 1:
            epilogue(p, i, ex, ou)
            return
        acc_ref = rest[-1]

        @pl.when(kk == 0)
        def _():
            acc_ref[...] = p

        @pl.when(jnp.logical_and(kk > 0, kk < nk - 1))
        def _():
            acc_ref[...] += p

        @pl.when(kk == nk - 1)
        def _():
            epilogue(acc_ref[...] + p, i, ex, ou)

    b_spec = pl.BlockSpec((n, tk), lambda i, kk: (0, kk)) if nt else pl.BlockSpec((tk, n), lambda i, kk: (kk, 0))
    return pl.pallas_call(
        body, name=name, grid=(m // tm, nk),
        in_specs=[pl.BlockSpec((tm, tk), lambda i, kk: (i, kk)), b_spec] + [s for _, s in extras],
        out_specs=[s for _, s in outs], out_shape=[s for s, _ in outs],
        scratch_shapes=[] if nk == 1 else [pltpu.VMEM((tm, n), F32)],
        input_output_aliases={2 + e: o for e, o in (aliases or {}).items()},
        compiler_params=_cparams("arbitrary", "arbitrary"),
    )(a, b, *[e for e, _ in extras])


def _dh_rms_bwd(dproj, w_main, dpba, w_ba, xs, w, dres, after, name):
    t, d = xs.shape
    tm = _pick(t, (640, 128))
    tk = _pick(w_main.shape[1], (2304, 768))

    def epilogue(dh, i, ex, ou):
        dpba_ref, wba_ref, x_ref, w_ref, dr_ref, _ = ex
        dx_ref, gw_ref = ou

        @pl.when(i == 0)
        def _():
            gw_ref[...] = jnp.zeros_like(gw_ref)

        dh = dh + lax.dot_general(dpba_ref[...], wba_ref[...], NT, preferred_element_type=F32)
        x = x_ref[...]
        rs = lax.rsqrt(jnp.mean(x * x, axis=-1, keepdims=True) + EPS)
        xh = x * rs
        dxh = dh * w_ref[...]
        dx_ref[...] = rs * (dxh - xh * jnp.mean(dxh * xh, axis=-1, keepdims=True)) + dr_ref[...]
        gw_ref[0:1, :] += jnp.sum(dh * xh, axis=0, keepdims=True)

    row = pl.BlockSpec((tm, d), lambda i, kk: (i, 0))
    const = lambda shape: pl.BlockSpec(shape, lambda i, kk: (0, 0))
    return _mm_rows(
        dproj, w_main, nt=True, tm=tm, tk=tk,
        extras=[(dpba, pl.BlockSpec((tm, LANES), lambda i, kk: (i, 0))), (w_ba, const(w_ba.shape)), (xs, row), (w, const((1, d))),
                (dres, row), (after, ANY)],
        outs=[(jax.ShapeDtypeStruct((t, d), F32), row), (jax.ShapeDtypeStruct((8, d), F32), const((8, d)))],
        epilogue=epilogue, name=name)


def _merge_final(ya, yb, b, proj, w_o, xs, w, tgt, name):
    t, d = xs.shape
    r = _row_tile(t)

    def prologue(ya_ref, ex, ou):
        yb_ref, b_ref, g_ref = ex[3:]
        m_ref = ou[4]
        m_ref[...] = (_sig(g_ref[:, :d]) * ya_ref[...] + _sig(g_ref[:, d:]) * (yb_ref[...] + b_ref[...])).astype(m_ref.dtype)
        return m_ref[...]

    def epilogue(mo, i, ex, ou):
        x_ref, w_ref, t_ref = ex[:3]
        dx_ref, dxm_ref, loss_ref, gw_ref = ou[:4]

        @pl.when(i == 0)
        def _():
            loss_ref[...] = jnp.zeros_like(loss_ref)
            gw_ref[...] = jnp.zeros_like(gw_ref)

        xo = x_ref[...] + mo
        rs = lax.rsqrt(jnp.mean(xo * xo, axis=-1, keepdims=True) + EPS)
        xh = xo * rs
        y = xh * w_ref[...]
        rows = i * r + lax.broadcasted_iota(jnp.int32, (r, 1), 0)
        err = jnp.where(rows >= HDR, y - t_ref[...], 0.0)
        loss_ref[...] += 0.5 * _sum_all(err * err) / d
        dy = err / d
        gw_ref[0:1, :] += jnp.sum(dy * xh, axis=0, keepdims=True)
        dxh = dy * w_ref[...]
        dx = rs * (dxh - xh * jnp.mean(dxh * xh, axis=-1, keepdims=True))
        dx_ref[...] = dx
        dxm_ref[...] = dx.astype(dxm_ref.dtype)

    row = pl.BlockSpec((r, d), lambda i, kk: (i, 0))
    const = lambda shape: pl.BlockSpec(shape, lambda i, kk: (0, 0))
    return _mm_rows(
        ya, w_o, nt=False, tm=r, tk=d,
        extras=[(xs, row), (w, const((1, d))), (tgt, row), (yb, row), (b, const((1, d))),
                (proj, pl.BlockSpec((r, 2 * d), lambda i, kk: (i, 3)))],
        outs=[(jax.ShapeDtypeStruct((t, d), F32), row), (jax.ShapeDtypeStruct((t, d), MM_DTYPE), row),
              (jax.ShapeDtypeStruct((8, LANES), F32), const((8, LANES))), (jax.ShapeDtypeStruct((8, d), F32), const((8, d))),
              (jax.ShapeDtypeStruct((t, d), MM_DTYPE), row)],
        prologue=prologue, epilogue=epilogue, name=name)


def _halo_prev(r):
    return lambda i: (jnp.maximum(i * (r // HALO) - 1, 0), 0)


def _halo_next(r, t):
    return lambda i: (jnp.minimum((i + 1) * (r // HALO), t // HALO - 1), 0)


def _conv_rows(t):
    return _pick(t, (320, 128))


def _qkv_window(p_ref, ph_ref, i, cs):
    return jnp.concatenate([jnp.where(i > 0, ph_ref[HALO - 8:HALO, cs], 0.0), p_ref[:, cs]], axis=0)


def _qkv_conv(win, cw, r):
    y = None
    for s in range(DN_TAPS):
        term = _shift(win, s)[8:8 + r, :] * cw[DN_TAPS - 1 - s:DN_TAPS - s, :]
        y = term if y is None else y + term
    return y


def _dn_prep(proj, conv_w, d, heads, name):
    t = proj.shape[0]
    r = _conv_rows(t)
    w3 = 3 * d

    def body(p_ref, ph_ref, cw_ref, o_ref):
        i = pl.program_id(0)
        for cb in range(w3 // DK):
            cs = slice(cb * DK, (cb + 1) * DK)
            y = _qkv_conv(_qkv_window(p_ref, ph_ref, i, cs), cw_ref[:, cs], r)
            a = y * _sig(y)
            if cb < 2 * heads:
                sc = DK ** -0.5 if cb < heads else 1.0
                a = a * (lax.rsqrt(jnp.sum(a * a, axis=-1, keepdims=True) + EPS) * sc)
            o_ref[:, cs] = a

    return pl.pallas_call(
        body, name=name, grid=(t // r,),
        in_specs=[pl.BlockSpec((r, w3), lambda i: (i, 0)), pl.BlockSpec((HALO, w3), _halo_prev(r)),
                  pl.BlockSpec((DN_TAPS, w3), lambda i: (0, 0))],
        out_specs=pl.BlockSpec((r, w3), lambda i: (i, 0)),
        out_shape=jax.ShapeDtypeStruct((t, w3), F32), compiler_params=_cparams("parallel"),
    )(proj, proj, conv_w)


def _dn_prep_bwd(proj, conv_w, dqkvn, dproj, d, heads, name):
    t = proj.shape[0]
    r = _conv_rows(t)
    w3 = 3 * d
    last = t // r - 1

    def body(p_ref, ph_ref, pn_ref, cw_ref, dn_ref, dnn_ref, _, dp_ref, gw_ref):
        i = pl.program_id(0)

        @pl.when(i == 0)
        def _():
            gw_ref[...] = jnp.zeros_like(gw_ref)

        for cb in range(w3 // DK):
            cs = slice(cb * DK, (cb + 1) * DK)
            cw = cw_ref[:, cs]
            win = jnp.concatenate([_qkv_window(p_ref, ph_ref, i, cs), jnp.where(i < last, pn_ref[0:8, cs], 0.0)], axis=0)
            y = _qkv_conv(win, cw, r + 8)
            sy = _sig(y)
            da = jnp.concatenate([dn_ref[:, cs], jnp.where(i < last, dnn_ref[0:8, cs], 0.0)], axis=0)
            if cb < 2 * heads:
                a = y * sy
                rn = lax.rsqrt(jnp.sum(a * a, axis=-1, keepdims=True) + EPS)
                n0 = a * rn
                dn0 = da * (DK ** -0.5 if cb < heads else 1.0)
                da = rn * (dn0 - n0 * jnp.sum(dn0 * n0, axis=-1, keepdims=True))
            dconv = da * _dsilu(y, sy)
            acc = None
            for j in range(DN_TAPS):
                s = DN_TAPS - 1 - j
                term = _shift(dconv, -s)[0:r, :] * cw[j:j + 1, :]
                acc = term if acc is None else acc + term
                gw_ref[j:j + 1, cs] += jnp.sum(dconv[0:r, :] * _shift(win, s)[8:8 + r, :], axis=0, keepdims=True)
            dp_ref[:, cs] = acc.astype(dp_ref.dtype)

    row = pl.BlockSpec((r, w3), lambda i: (i, 0))
    nxt = pl.BlockSpec((HALO, w3), _halo_next(r, t))
    return pl.pallas_call(
        body, name=name, grid=(t // r,),
        in_specs=[row, pl.BlockSpec((HALO, w3), _halo_prev(r)), nxt, pl.BlockSpec((DN_TAPS, w3), lambda i: (0, 0)), row, nxt, ANY],
        out_specs=[row, pl.BlockSpec((8, w3), lambda i: (0, 0))],
        out_shape=[jax.ShapeDtypeStruct(dproj.shape, dproj.dtype), jax.ShapeDtypeStruct((8, w3), F32)],
        input_output_aliases={6: 0}, compiler_params=_cparams("arbitrary"),
    )(proj, proj, proj, conv_w, dqkvn, dqkvn, dproj)


def _ba_terms(x, pv, heads):
    lane = lax.broadcasted_iota(jnp.int32, x.shape, 1)
    is_b = lane < heads
    is_a = jnp.logical_and(lane >= heads, lane < 2 * heads)
    beta = _sig(x)
    z = x + pv[1:2, :]
    nexp = -jnp.exp(pv[0:1, :])
    sp = jnp.maximum(z, 0.0) + jnp.log1p(jnp.exp(-jnp.abs(z)))
    return is_b, is_a, beta, z, nexp, nexp * sp


def _ba_fwd(pba, pvec, heads, name):
    t = pba.shape[0]
    r = _row_tile(t)

    def body(x_ref, pv_ref, o_ref):
        is_b, is_a, beta, _, _, g = _ba_terms(x_ref[...], pv_ref[...], heads)
        rows = pl.program_id(0) * r + lax.broadcasted_iota(jnp.int32, (r, 1), 0)
        o_ref[...] = jnp.where(rows >= PAD_ROWS, jnp.where(is_b, beta, jnp.where(is_a, g, 0.0)), 0.0)

    row = pl.BlockSpec((r, LANES), lambda i: (i, 0))
    return pl.pallas_call(
        body, name=name, grid=(t // r,), in_specs=[row, pl.BlockSpec((8, LANES), lambda i: (0, 0))],
        out_specs=row, out_shape=jax.ShapeDtypeStruct((t, LANES), F32), compiler_params=_cparams("parallel"),
    )(pba, pvec)


def _ba_bwd(pba, pvec, dbg, heads, name):
    t = pba.shape[0]
    r = _row_tile(t)

    def body(x_ref, pv_ref, d_ref, o_ref, g_ref):
        @pl.when(pl.program_id(0) == 0)
        def _():
            g_ref[...] = jnp.zeros_like(g_ref)

        is_b, is_a, beta, z, nexp, g = _ba_terms(x_ref[...], pv_ref[...], heads)
        rows = pl.program_id(0) * r + lax.broadcasted_iota(jnp.int32, (r, 1), 0)
        dd = jnp.where(rows >= PAD_ROWS, d_ref[...], 0.0)
        dz = dd * nexp * _sig(z)
        o_ref[...] = jnp.where(is_b, dd * beta * (1.0 - beta), jnp.where(is_a, dz, 0.0)).astype(o_ref.dtype)
        g_ref[0:1, :] += jnp.sum(jnp.where(is_a, dd * g, 0.0), axis=0, keepdims=True)
        g_ref[1:2, :] += jnp.sum(jnp.where(is_a, dz, 0.0), axis=0, keepdims=True)

    row = pl.BlockSpec((r, LANES), lambda i: (i, 0))
    par = pl.BlockSpec((8, LANES), lambda i: (0, 0))
    return pl.pallas_call(
        body, name=name, grid=(t // r,), in_specs=[row, par, row], out_specs=[row, par],
        out_shape=[jax.ShapeDtypeStruct((t, LANES), MM_DTYPE), jax.ShapeDtypeStruct((8, LANES), F32)],
        compiler_params=_cparams("arbitrary"),
    )(pba, pvec, dbg)


def _chunk_consts():
    ri = lax.broadcasted_iota(jnp.int32, (CHUNK, CHUNK), 0)
    ci = lax.broadcasted_iota(jnp.int32, (CHUNK, CHUNK), 1)
    return ri >= ci, ri > ci, (ri == ci).astype(F32), (ri >= ci).astype(F32), (ri <= ci).astype(F32)


def _chunk_decay(gc, gr, incl):
    return jnp.where(incl, jnp.exp(jnp.where(incl, gc - gr, 0.0)), 0.0)


def _unit_lower_inverse(n, eye):
    x = [eye + a for a in n]
    p = list(n)
    for _ in range(5):
        p = [_dot(a, a) for a in p]
        x = [a + _dot(a, b) for a, b in zip(x, p)]
    r = [eye - a + _dot3(b, a) for a, b in zip(x, n)]
    return [a + _dot(a, b) for a, b in zip(x, r)]


def _dn_fwd(qkvn, bg, g_row, d, heads, name):
    t = qkvn.shape[0]
    nc = t // CHUNK

    def body(qkv_ref, bg_ref, gr_ref, o_ref, sall_ref, ainv_ref, s_ref):
        @pl.when(pl.program_id(0) == 0)
        def _():
            s_ref[...] = jnp.zeros_like(s_ref)

        incl, strict, eye, tril, triu = _chunk_consts()
        bgv = bg_ref[...]
        gc_all = _dot(tril, bgv, NN, HI)
        gr_all = _dot(gr_ref[0], triu, NN, HI)
        hs = range(heads)
        q = [qkv_ref[:, h * DK:(h + 1) * DK] for h in hs]
        k = [qkv_ref[:, d + h * DK:d + (h + 1) * DK] for h in hs]
        v = [qkv_ref[:, 2 * d + h * DK:2 * d + (h + 1) * DK] for h in hs]
        kb = [a.astype(MM_DTYPE) for a in k]
        beta = [bgv[:, h:h + 1] for h in hs]
        gc = [gc_all[:, heads + h:heads + h + 1] for h in hs]
        decay = [_chunk_decay(gc[h], gr_all[h:h + 1, :], incl) for h in hs]
        eg = [jnp.exp(a) for a in gc]
        n = [jnp.where(strict, -(beta[h] * _dot(kb[h], kb[h], NT) * decay[h]), 0.0) for h in hs]
        x = _unit_lower_inverse(n, eye)
        sol = [_dot3(x[h], jnp.concatenate([v[h] * beta[h], k[h] * (beta[h] * eg[h])], axis=1)) for h in hs]
        attn = [_dot(q[h], kb[h], NT) * decay[h] for h in hs]
        s = [s_ref[h] for h in hs]
        sb = [a.astype(MM_DTYPE) for a in s]
        wv = [sol[h][:, :DK] - _dot(sol[h][:, DK:], sb[h]) for h in hs]
        o = [_dot(q[h] * eg[h], sb[h]) + _dot(attn[h], wv[h]) for h in hs]
        glast = [a[CHUNK - 1:CHUNK, :] for a in gc]
        s_new = [s[h] * jnp.exp(glast[h]) + _dot(k[h] * jnp.exp(glast[h] - gc[h]), wv[h], TN) for h in hs]
        for h in hs:
            o_ref[:, h * DK:(h + 1) * DK] = o[h]
            sall_ref[0, h] = s[h]
            ainv_ref[0, h] = x[h]
            s_ref[h] = s_new[h]

    return pl.pallas_call(
        body, name=name, grid=(nc,),
        in_specs=[pl.BlockSpec((CHUNK, 3 * d), lambda i: (i, 0)), pl.BlockSpec((CHUNK, LANES), lambda i: (i, 0)),
                  pl.BlockSpec((1, heads, CHUNK), lambda i: (i, 0, 0))],
        out_specs=[pl.BlockSpec((CHUNK, d), lambda i: (i, 0)), pl.BlockSpec((1, heads, DK, DK), lambda i: (i, 0, 0, 0)),
                   pl.BlockSpec((1, heads, CHUNK, CHUNK), lambda i: (i, 0, 0, 0))],
        out_shape=[jax.ShapeDtypeStruct((t, d), F32), jax.ShapeDtypeStruct((nc, heads, DK, DK), F32),
                   jax.ShapeDtypeStruct((nc, heads, CHUNK, CHUNK), F32)],
        scratch_shapes=[pltpu.VMEM((heads, DK, DK), F32)], compiler_params=_cparams("arbitrary"),
    )(qkvn, bg, g_row)


def _dn_bwd(qkvn, bg, g_row, do, sall, ainv, d, heads, name):
    t = qkvn.shape[0]
    nc = t // CHUNK
    rev = lambda i: nc - 1 - i

    def body(qkv_ref, bg_ref, gr_ref, do_ref, sall_ref, ainv_ref, dqkv_ref, dbg_ref, dgr_ref, ds_ref):
        @pl.when(pl.program_id(0) == 0)
        def _():
            ds_ref[...] = jnp.zeros_like(ds_ref)

        incl, strict, eye, tril, triu = _chunk_consts()
        bgv = bg_ref[...]
        gc_all = _dot(tril, bgv, NN, HI)
        gr_all = _dot(gr_ref[0], triu, NN, HI)
        lane = lax.broadcasted_iota(jnp.int32, (CHUNK, LANES), 1)
        hrow = lax.broadcasted_iota(jnp.int32, (heads, CHUNK), 0)
        last_row = lax.broadcasted_iota(jnp.int32, (CHUNK, 1), 0) == CHUNK - 1
        dbeta_slab = jnp.zeros((CHUNK, LANES), F32)
        dgc_slab = jnp.zeros((CHUNK, LANES), F32)
        dgr_slab = jnp.zeros((heads, CHUNK), F32)
        hs = range(heads)
        cols = lambda ref, off: [ref[:, off + h * DK:off + (h + 1) * DK] for h in hs]
        qs, ks, vs, douts = cols(qkv_ref, 0), cols(qkv_ref, d), cols(qkv_ref, 2 * d), cols(do_ref, 0)
        ss, xs_, dsns = [sall_ref[0, h] for h in hs], [ainv_ref[0, h] for h in hs], [ds_ref[h] for h in hs]

        def recompute(h, z):
            z.q, z.k, z.v, z.dout, z.s, z.x, z.dsn = qs[h], ks[h], vs[h], douts[h], ss[h], xs_[h], dsns[h]
            z.kb, z.qb, z.sb, z.dsnb, z.doutb = (a.astype(MM_DTYPE) for a in (z.k, z.q, z.s, z.dsn, z.dout))
            z.beta = bgv[:, h:h + 1]
            gc = gc_all[:, heads + h:heads + h + 1]
            z.decay = _chunk_decay(gc, gr_all[h:h + 1, :], incl)
            z.eg = jnp.exp(gc)
            glast = gc[CHUNK - 1:CHUNK, :]
            z.eglast = jnp.exp(glast)
            z.ek = jnp.exp(glast - gc)
            z.kk = _dot(z.kb, z.kb, NT)
            z.qk = _dot(z.qb, z.kb, NT)
            sol = _dot3(z.x, jnp.concatenate([z.v * z.beta, z.k * (z.beta * z.eg)], axis=1))
            z.u, z.wb = sol[:, :DK], sol[:, DK:].astype(MM_DTYPE)
            z.attn = z.qk * z.decay
            z.qg = z.q * z.eg
            z.kend = z.k * z.ek

        def pseudo_values(h, z):
            z.wvb = (z.u - _dot(z.wb, z.sb)).astype(MM_DTYPE)

        def scan_step(h, z):
            z.d_wv = _dot(z.attn, z.doutb, TN) + _dot(z.kend, z.dsnb)
            z.d_attn = _dot(z.doutb, z.wvb, NT)
            z.d_qg = _dot(z.doutb, z.sb, NT)
            z.d_kend = _dot(z.wvb, z.dsnb, NT)
            z.ds_new = _dot(z.qg, z.doutb, TN) + z.eglast * z.dsn
            z.d_glast = z.eglast * _sum_all(z.dsn * z.s) + _sum_all(z.d_kend * z.kend)

        def state_terms(h, z):
            z.ds_new = z.ds_new - _dot(z.wb, z.d_wv, TN)
            z.d_w = -_dot(z.d_wv, z.sb, NT)

        def solve_transpose(h, z):
            d_rhs = _dot3(z.x, jnp.concatenate([z.d_wv, z.d_w], axis=1), TN)
            z.d_ru, z.d_rw = d_rhs[:, :DK], d_rhs[:, DK:]

        def lower_terms(h, z):
            z.d_low = jnp.where(strict, -(_dot(z.d_ru, z.u, NT) + _dot(z.d_rw, z.wb, NT)), 0.0)

        def outputs(h, z):
            rw_k = jnp.sum(z.d_rw * z.k, axis=-1, keepdims=True)
            z.dbeta = (jnp.sum(z.d_ru * z.v, axis=-1, keepdims=True) + rw_k * z.eg
                       + jnp.sum(z.d_low * z.kk * z.decay, axis=-1, keepdims=True))
            z.dv = z.d_ru * z.beta
            d_kk = (z.d_low * z.beta * z.decay).astype(MM_DTYPE)
            d_qk = (z.d_attn * z.decay).astype(MM_DTYPE)
            z.dk = (z.d_rw * (z.beta * z.eg) + _dot(d_kk, z.kb) + _dot(d_kk, z.kb, TN) + _dot(d_qk, z.qb, TN)
                    + z.d_kend * z.ek)
            z.dq = _dot(d_qk, z.kb) + z.d_qg * z.eg
            e = (z.d_low * z.beta * z.kk + z.d_attn * z.qk) * z.decay
            z.dgc = (rw_k * z.beta * z.eg + jnp.sum(e, axis=-1, keepdims=True)
                     + jnp.sum(z.d_qg * z.qg, axis=-1, keepdims=True) - jnp.sum(z.d_kend * z.kend, axis=-1, keepdims=True)
                     + jnp.where(last_row, z.d_glast, 0.0))
            z.dgr = -jnp.sum(e, axis=0, keepdims=True)

        class _Head:
            pass

        st = [_Head() for _ in hs]
        for phase in (recompute, pseudo_values, scan_step, state_terms, solve_transpose, lower_terms, outputs):
            for h in hs:
                phase(h, st[h])
        res = [(z.dq, z.dk, z.dv, z.ds_new, z.dbeta, z.dgc, z.dgr) for z in st]
        for h, (dq, dk, dv, ds_new, dbeta, dgc, dgr) in enumerate(res):
            dqkv_ref[:, h * DK:(h + 1) * DK] = dq
            dqkv_ref[:, d + h * DK:d + (h + 1) * DK] = dk
            dqkv_ref[:, 2 * d + h * DK:2 * d + (h + 1) * DK] = dv
            ds_ref[h] = ds_new
            dbeta_slab = jnp.where(lane == h, dbeta, dbeta_slab)
            dgc_slab = jnp.where(lane == heads + h, dgc, dgc_slab)
            dgr_slab = jnp.where(hrow == h, dgr, dgr_slab)
        dbg_ref[...] = dbeta_slab + _dot(triu, dgc_slab, NN, HI)
        dgr_ref[0] = _dot(dgr_slab, tril, NN, HI)

    return pl.pallas_call(
        body, name=name, grid=(nc,),
        in_specs=[pl.BlockSpec((CHUNK, 3 * d), lambda i: (rev(i), 0)), pl.BlockSpec((CHUNK, LANES), lambda i: (rev(i), 0)),
                  pl.BlockSpec((1, heads, CHUNK), lambda i: (rev(i), 0, 0)), pl.BlockSpec((CHUNK, d), lambda i: (rev(i), 0)),
                  pl.BlockSpec((1, heads, DK, DK), lambda i: (rev(i), 0, 0, 0)),
                  pl.BlockSpec((1, heads, CHUNK, CHUNK), lambda i: (rev(i), 0, 0, 0))],
        out_specs=[pl.BlockSpec((CHUNK, 3 * d), lambda i: (rev(i), 0)), pl.BlockSpec((CHUNK, LANES), lambda i: (rev(i), 0)),
                   pl.BlockSpec((1, heads, CHUNK), lambda i: (rev(i), 0, 0))],
        out_shape=[jax.ShapeDtypeStruct((t, 3 * d), F32), jax.ShapeDtypeStruct((t, LANES), F32),
                   jax.ShapeDtypeStruct((nc, heads, CHUNK), F32)],
        scratch_shapes=[pltpu.VMEM((heads, DK, DK), F32)], compiler_params=_cparams("arbitrary"),
    )(qkvn, bg, g_row, do, sall, ainv)


def _dn_post(o, proj, w, w_dn, d, heads, name):
    t = o.shape[0]
    r = _row_tile(t)

    def prologue(o_ref, ex, ou):
        z_ref, w_ref = ex
        y_ref = ou[0]
        for h in range(heads):
            sl = slice(h * DK, (h + 1) * DK)
            oh, z = o_ref[:, sl], z_ref[:, sl]
            rs = lax.rsqrt(jnp.mean(oh * oh, axis=-1, keepdims=True) + EPS)
            y_ref[:, sl] = (oh * rs * w_ref[...] * (z * _sig(z))).astype(y_ref.dtype)
        return y_ref[...]

    def epilogue(p, i, ex, ou):
        ou[1][...] = p

    row = pl.BlockSpec((r, d), lambda i, kk: (i, 0))
    return _mm_rows(
        o, w_dn, nt=False, tm=r, tk=d,
        extras=[(proj, pl.BlockSpec((r, d), lambda i, kk: (i, 3))), (w, pl.BlockSpec((1, DK), lambda i, kk: (0, 0)))],
        outs=[(jax.ShapeDtypeStruct((t, d), MM_DTYPE), row), (jax.ShapeDtypeStruct((t, d), F32), row)],
        prologue=prologue, epilogue=epilogue, name=name)


def _dn_post_bwd(dya, w_dn, o, proj, w, dproj, d, heads, name):
    t = o.shape[0]
    r = _row_tile(t)

    def epilogue(dy, i, ex, ou):
        o_ref, z_ref, w_ref, _ = ex
        do_ref, dz_ref, gw_ref = ou

        @pl.when(i == 0)
        def _():
            gw_ref[...] = jnp.zeros_like(gw_ref)

        gw = jnp.zeros((1, DK), F32)
        for h in range(heads):
            sl = slice(h * DK, (h + 1) * DK)
            oh, z, dyh = o_ref[:, sl], z_ref[:, sl], dy[:, sl]
            sz = _sig(z)
            rs = lax.rsqrt(jnp.mean(oh * oh, axis=-1, keepdims=True) + EPS)
            xh = oh * rs
            dn = dyh * (z * sz)
            dz_ref[:, sl] = (dyh * (xh * w_ref[...]) * _dsilu(z, sz)).astype(dz_ref.dtype)
            dxh = dn * w_ref[...]
            do_ref[:, sl] = rs * (dxh - xh * jnp.mean(dxh * xh, axis=-1, keepdims=True))
            gw = gw + jnp.sum(dn * xh, axis=0, keepdims=True)
        gw_ref[0:1, :] += gw

    row = pl.BlockSpec((r, d), lambda i, kk: (i, 0))
    za = pl.BlockSpec((r, d), lambda i, kk: (i, 3))
    return _mm_rows(
        dya, w_dn, nt=True, tm=r, tk=d,
        extras=[(o, row), (proj, za), (w, pl.BlockSpec((1, DK), lambda i, kk: (0, 0))), (dproj, ANY)],
        outs=[(jax.ShapeDtypeStruct((t, d), F32), row), (jax.ShapeDtypeStruct(dproj.shape, dproj.dtype), za),
              (jax.ShapeDtypeStruct((8, DK), F32), pl.BlockSpec((8, DK), lambda i, kk: (0, 0)))],
        aliases={3: 1}, epilogue=epilogue, name=name)


def _glu_window(g_ref, gh_ref, i, d, cs):
    cs2 = slice(d + cs.start, d + cs.stop)
    cur = g_ref[:, cs] * _sig(g_ref[:, cs2])
    prev = gh_ref[HALO - 32:HALO, cs] * _sig(gh_ref[HALO - 32:HALO, cs2])
    return jnp.concatenate([jnp.where(i > 0, prev, 0.0), cur], axis=0)


def _ln_stats(c1):
    mu = jnp.mean(c1, axis=-1, keepdims=True)
    cc = c1 - mu
    return cc * lax.rsqrt(jnp.mean(cc * cc, axis=-1, keepdims=True) + EPS), lax.rsqrt(jnp.mean(cc * cc, axis=-1, keepdims=True) + EPS)


def _cf_fwd(proj, dw_w, dw_b, ln_w, ln_b, d, name):
    t = proj.shape[0]
    r = _conv_rows(t)

    def body(g_ref, gh_ref, zb_ref, w_ref, b_ref, lw_ref, lb_ref, c1_ref, c3_ref):
        for cb in range(d // LANES):
            cs = slice(cb * LANES, (cb + 1) * LANES)
            win = _glu_window(g_ref, gh_ref, pl.program_id(0), d, cs)
            acc = jnp.broadcast_to(b_ref[:, cs], (r, LANES))
            for sub in range(8):
                rot = _shift(win, sub)
                for s in range(sub, CF_TAPS, 8):
                    acc = acc + rot[32 - (s - sub):32 - (s - sub) + r, :] * w_ref[CF_TAPS - 1 - s:CF_TAPS - s, cs]
            c1_ref[:, cs] = acc
        xh, _ = _ln_stats(c1_ref[...])
        ln = xh * lw_ref[...] + lb_ref[...]
        zb = zb_ref[...]
        c3_ref[...] = ((ln * _sig(ln)) * (zb * _sig(zb))).astype(c3_ref.dtype)

    row = pl.BlockSpec((r, d), lambda i: (i, 0))
    par = pl.BlockSpec((1, d), lambda i: (0, 0))
    return pl.pallas_call(
        body, name=name, grid=(t // r,),
        in_specs=[pl.BlockSpec((r, 2 * d), lambda i: (i, 2)), pl.BlockSpec((HALO, 2 * d), lambda i: (_halo_prev(r)(i)[0], 2)),
                  pl.BlockSpec((r, d), lambda i: (i, 8)), pl.BlockSpec((32, d), lambda i: (0, 0)), par, par, par],
        out_specs=[row, row],
        out_shape=[jax.ShapeDtypeStruct((t, d), F32), jax.ShapeDtypeStruct((t, d), MM_DTYPE)],
        compiler_params=_cparams("parallel"),
    )(proj, proj, proj, dw_w, dw_b, ln_w, ln_b)


def _cf_bwd1(dyb, w_cf, c1, proj, ln_w, ln_b, dproj, d, name):
    t = c1.shape[0]
    r = _row_tile(t)

    def epilogue(dc3v, i, ex, ou):
        c1_ref, zb_ref, lw_ref, lb_ref, _ = ex
        dc1_ref, dzb_ref, g_ref = ou

        @pl.when(i == 0)
        def _():
            g_ref[...] = jnp.zeros_like(g_ref)

        xh, rs = _ln_stats(c1_ref[...])
        ln = xh * lw_ref[...] + lb_ref[...]
        sl, zb = _sig(ln), zb_ref[...]
        szb = _sig(zb)
        dzb_ref[...] = (dc3v * (ln * sl) * _dsilu(zb, szb)).astype(dzb_ref.dtype)
        dln = dc3v * (zb * szb) * _dsilu(ln, sl)
        dxh = dln * lw_ref[...]
        dc1 = rs * (dxh - jnp.mean(dxh, axis=-1, keepdims=True) - xh * jnp.mean(dxh * xh, axis=-1, keepdims=True))
        dc1_ref[...] = dc1
        g_ref[0:1, :] += jnp.sum(dln * xh, axis=0, keepdims=True)
        g_ref[1:2, :] += jnp.sum(dln, axis=0, keepdims=True)
        g_ref[2:3, :] += jnp.sum(dc1, axis=0, keepdims=True)

    row = pl.BlockSpec((r, d), lambda i, kk: (i, 0))
    zbs = pl.BlockSpec((r, d), lambda i, kk: (i, 8))
    par = pl.BlockSpec((1, d), lambda i, kk: (0, 0))
    return _mm_rows(
        dyb, w_cf, nt=True, tm=r, tk=d, extras=[(c1, row), (proj, zbs), (ln_w, par), (ln_b, par), (dproj, ANY)],
        outs=[(jax.ShapeDtypeStruct((t, d), F32), row), (jax.ShapeDtypeStruct(dproj.shape, dproj.dtype), zbs),
              (jax.ShapeDtypeStruct((8, d), F32), pl.BlockSpec((8, d), lambda i, kk: (0, 0)))],
        aliases={4: 1}, epilogue=epilogue, name=name)


def _cf_bwd2(dc1, proj, dw_w, dproj, d, name):
    t = dc1.shape[0]
    r = _conv_rows(t)
    last = t // r - 1

    def body(dc_ref, dcn_ref, g_ref, gh_ref, w_ref, _, dg_ref, gw_ref):
        i = pl.program_id(0)

        @pl.when(i == 0)
        def _():
            gw_ref[...] = jnp.zeros_like(gw_ref)

        for cb in range(d // LANES):
            cs = slice(cb * LANES, (cb + 1) * LANES)
            cs2 = slice(d + cs.start, d + cs.stop)
            dcur = dc_ref[:, cs]
            dwin = jnp.concatenate([dcur, jnp.where(i < last, dcn_ref[0:32, cs], 0.0)], axis=0)
            win = _glu_window(g_ref, gh_ref, i, d, cs)
            acc = None
            for sub in range(8):
                drot = _shift(dwin, -sub)
                rot = _shift(win, sub)
                for s in range(sub, CF_TAPS, 8):
                    j = CF_TAPS - 1 - s
                    term = drot[s - sub:s - sub + r, :] * w_ref[j:j + 1, cs]
                    acc = term if acc is None else acc + term
                    gw_ref[j:j + 1, cs] += jnp.sum(dcur * rot[32 - (s - sub):32 - (s - sub) + r, :], axis=0, keepdims=True)
            ga, sb = g_ref[:, cs], _sig(g_ref[:, cs2])
            dg_ref[:, cs] = (acc * sb).astype(dg_ref.dtype)
            dg_ref[:, cs2] = (acc * ga * sb * (1.0 - sb)).astype(dg_ref.dtype)

    row = pl.BlockSpec((r, d), lambda i: (i, 0))
    glu = pl.BlockSpec((r, 2 * d), lambda i: (i, 2))
    return pl.pallas_call(
        body, name=name, grid=(t // r,),
        in_specs=[row, pl.BlockSpec((HALO, d), _halo_next(r, t)), glu,
                  pl.BlockSpec((HALO, 2 * d), lambda i: (_halo_prev(r)(i)[0], 2)), pl.BlockSpec((32, d), lambda i: (0, 0)), ANY],
        out_specs=[glu, pl.BlockSpec((32, d), lambda i: (0, 0))],
        out_shape=[jax.ShapeDtypeStruct(dproj.shape, dproj.dtype), jax.ShapeDtypeStruct((32, d), F32)],
        input_output_aliases={5: 0}, compiler_params=_cparams("arbitrary"),
    )(dc1, dc1, proj, proj, dw_w, dproj)


def _merge_bwd(dxo_mm, w_o, proj, ya, yb, b, d, name):
    t = ya.shape[0]
    r = _row_tile(t)

    def epilogue(dmv, i, ex, ou):
        g_ref, ya_ref, yb_ref, b_ref = ex
        dya_ref, dyb_ref, dg_ref, gb_ref = ou

        @pl.when(i == 0)
        def _():
            gb_ref[...] = jnp.zeros_like(gb_ref)

        sa, sb = _sig(g_ref[:, :d]), _sig(g_ref[:, d:])
        dyb = dmv * sb
        dya_ref[...] = (dmv * sa).astype(dya_ref.dtype)
        dyb_ref[...] = dyb.astype(dyb_ref.dtype)
        dg_ref[:, :d] = (dmv * ya_ref[...] * sa * (1.0 - sa)).astype(dg_ref.dtype)
        dg_ref[:, d:] = (dmv * (yb_ref[...] + b_ref[...]) * sb * (1.0 - sb)).astype(dg_ref.dtype)
        gb_ref[0:1, :] += jnp.sum(dyb, axis=0, keepdims=True)

    row = pl.BlockSpec((r, d), lambda i, kk: (i, 0))
    gate = pl.BlockSpec((r, 2 * d), lambda i, kk: (i, 3))
    const = lambda shape: pl.BlockSpec(shape, lambda i, kk: (0, 0))
    return _mm_rows(
        dxo_mm, w_o, nt=True, tm=r, tk=d, extras=[(proj, gate), (ya, row), (yb, row), (b, const((1, d)))],
        outs=[(jax.ShapeDtypeStruct((t, d), MM_DTYPE), row), (jax.ShapeDtypeStruct((t, d), MM_DTYPE), row),
              (jax.ShapeDtypeStruct((t, 9 * d), MM_DTYPE), gate), (jax.ShapeDtypeStruct((8, d), F32), const((8, d)))],
        epilogue=epilogue, name=name)


def _place():
    return lax.axis_index("x"), lax.axis_index("y"), lax.axis_index("c")


def _slot(p):
    return 4 * p[0] + 2 * p[1] + p[2]


def _allgather(blocks, name):
    n = len(blocks)

    def body(*refs):
        ins, outs = refs[:n], refs[n:2 * n]
        send_sems, recv_sems, local_sems = refs[2 * n:]
        x, y, c = _place()
        me, sibling = (x, y, c), (x, y, 1 - c)
        chips = [(1 - x, y), (x, 1 - y), (1 - x, 1 - y)]

        def copy(a, k, block, to, src=None):
            dst = outs[a].at[_slot(block)]
            return pltpu.make_async_remote_copy(
                src_ref=dst if src is None else src, dst_ref=dst, send_sem=send_sems.at[7 * a + k],
                recv_sem=recv_sems.at[7 * a + k], device_id=to, device_id_type=MESH)

        mine = [pltpu.make_async_copy(ins[a], outs[a].at[_slot(me)], local_sems.at[a]) for a in range(n)]
        for cp in mine:
            cp.start()
        first = []
        for a in range(n):
            first.append(copy(a, 0, me, sibling, src=ins[a]))
            first += [copy(a, 1 + j, me, (*chip, c), src=ins[a]) for j, chip in enumerate(chips)]
        for cp in first:
            cp.start()
        passed = []
        for j, chip in enumerate(chips):
            for a in range(n):
                copy(a, 1 + j, (*chip, c), me).wait_recv()
                cp = copy(a, 4 + j, (*chip, c), sibling)
                cp.start()
                passed.append(cp)
        for a in range(n):
            copy(a, 0, sibling, me).wait_recv()
            for j, chip in enumerate(chips):
                copy(a, 4 + j, (*chip, 1 - c), me).wait_recv()
        for cp in first + passed:
            cp.wait_send()
        for cp in mine:
            cp.wait()

    return pl.pallas_call(
        body, name=name, in_specs=[ANY] * n, out_specs=[ANY] * n,
        out_shape=[jax.ShapeDtypeStruct((8, *b.shape), b.dtype) for b in blocks],
        scratch_shapes=[pltpu.SemaphoreType.DMA((7 * n,)), pltpu.SemaphoreType.DMA((7 * n,)), pltpu.SemaphoreType.DMA((n,))],
    )(*blocks)


_FLIPS = [(0, 0, 1), (0, 1, 0), (0, 1, 1), (1, 0, 0), (1, 0, 1), (1, 1, 0), (1, 1, 1)]


_HBM = pl.BlockSpec(memory_space=pltpu.HBM)
_SEM = pl.BlockSpec(memory_space=pltpu.SEMAPHORE)
_EFFECT = pltpu.SideEffectType.DATAFLOW_SIDE_EFFECTING


def _peers(me):
    return [tuple(1 - me[ax] if f[ax] else me[ax] for ax in range(3)) for f in _FLIPS]


def _alltoall_copy(srcs, lands, send_sems, recv_sems, a, k, me, peers, own):
    dst = lands[a].at[_slot(me if own else peers[k])]
    src = srcs[a].at[0 if srcs[a].shape[0] == 1 else _slot(peers[k])]
    return pltpu.make_async_remote_copy(
        src_ref=src, dst_ref=dst, send_sem=send_sems.at[7 * a + k], recv_sem=recv_sems.at[7 * a + k],
        device_id=peers[k], device_id_type=MESH)


def _alltoall_start(arrays, lands, name):
    n = len(arrays)

    def body(*refs):
        srcs, lnds = refs[:n], refs[n:2 * n]
        send_sems, recv_sems = refs[2 * n:2 * n + 2]
        token = refs[-1]
        me = _place()
        peers = _peers(me)
        for a in range(n):
            for k in range(7):
                _alltoall_copy(srcs, lnds, send_sems, recv_sems, a, k, me, peers, True).start()
        token[...] = jnp.zeros_like(token)

    hbm = lambda v: pltpu.with_memory_space_constraint(v, pltpu.HBM)
    out = pl.pallas_call(
        body, name=name, in_specs=[_HBM] * (2 * n),
        out_specs=(_SEM, _SEM, *[_HBM] * (2 * n), pl.BlockSpec(memory_space=pltpu.VMEM)),
        out_shape=(pltpu.SemaphoreType.DMA((7 * n,)), pltpu.SemaphoreType.DMA((7 * n,)),
                   *[pltpu.HBM(v.shape, v.dtype) for v in (*arrays, *lands)], jax.ShapeDtypeStruct((8, LANES), F32)),
        input_output_aliases={i: 2 + i for i in range(2 * n)},
        compiler_params=pltpu.CompilerParams(has_side_effects=_EFFECT),
    )(*[hbm(v) for v in (*arrays, *lands)])
    return out[0], out[1], out[2:2 + 2 * n], out[-1]


def _alltoall_wait(send_sems, recv_sems, thru, after, name):
    n = len(thru) // 2

    def body(*refs):
        srcs, lnds = refs[:n], refs[n:2 * n]
        send_sems, recv_sems = refs[2 * n:2 * n + 2]
        me = _place()
        peers = _peers(me)
        for a in range(n):
            for k in range(7):
                _alltoall_copy(srcs, lnds, send_sems, recv_sems, a, k, me, peers, True).wait_send()
                _alltoall_copy(srcs, lnds, send_sems, recv_sems, a, k, me, peers, False).wait_recv()

    out = pl.pallas_call(
        body, name=name, in_specs=[*[_HBM] * (2 * n), _SEM, _SEM, ANY], out_specs=[_HBM] * (2 * n),
        out_shape=[pltpu.HBM(v.shape, v.dtype) for v in thru],
        input_output_aliases={i: i for i in range(2 * n)},
        compiler_params=pltpu.CompilerParams(has_side_effects=_EFFECT),
    )(*thru, send_sems, recv_sems, after)
    return out[n:]


def _sibling_exchange(arrays, name):
    n = len(arrays)

    def body(*refs):
        ins, outs = refs[:n], refs[n:2 * n]
        send_sems, recv_sems, local_sems = refs[2 * n:]
        x, y, c = _place()

        def copy(a, slot):
            return pltpu.make_async_remote_copy(
                src_ref=ins[a], dst_ref=outs[a].at[slot], send_sem=send_sems.at[a], recv_sem=recv_sems.at[a],
                device_id=(x, y, 1 - c), device_id_type=MESH)

        mine = [pltpu.make_async_copy(ins[a], outs[a].at[c], local_sems.at[a]) for a in range(n)]
        for cp in mine:
            cp.start()
        sent = [copy(a, c) for a in range(n)]
        for cp in sent:
            cp.start()
        for a in range(n):
            copy(a, 1 - c).wait_recv()
        for cp in sent:
            cp.wait_send()
        for cp in mine:
            cp.wait()

    vmem = pl.BlockSpec(memory_space=pltpu.VMEM)
    return pl.pallas_call(
        body, name=name, in_specs=[vmem] * n, out_specs=[vmem] * n,
        compiler_params=pltpu.CompilerParams(vmem_limit_bytes=VMEM_LIMIT),
        out_shape=[jax.ShapeDtypeStruct((2, *a.shape), a.dtype) for a in arrays],
        scratch_shapes=[pltpu.SemaphoreType.DMA((n,)), pltpu.SemaphoreType.DMA((n,)), pltpu.SemaphoreType.DMA((n,))],
    )(*arrays)


def _sum8(parts, name):
    _, rr, cc = parts.shape
    r = _pick(rr, (128, 64, 32, 16, 8))

    def body(p_ref, o_ref):
        acc = p_ref[0].astype(F32)
        for j in range(1, 8):
            acc = acc + p_ref[j].astype(F32)
        o_ref[...] = acc

    return pl.pallas_call(
        body, name=name, grid=(rr // r,), in_specs=[pl.BlockSpec((8, r, cc), lambda i: (0, i, 0))],
        out_specs=pl.BlockSpec((r, cc), lambda i: (i, 0)), out_shape=jax.ShapeDtypeStruct((rr, cc), F32),
        compiler_params=_cparams("parallel"),
    )(parts)


def _adamw(g, w, m, v, name):
    rr, cc = g.shape
    r = _pick(rr, (128, 64, 32, 16, 8))

    def body(g_ref, w_ref, m_ref, v_ref, d_ref, nm_ref, nv_ref):
        gv = g_ref[...]
        nm = ADAM_B1 * m_ref[...] + (1.0 - ADAM_B1) * gv
        nv = ADAM_B2 * v_ref[...] + (1.0 - ADAM_B2) * (gv * gv)
        m_hat = nm / (1.0 - ADAM_B1 ** ADAM_STEP)
        v_hat = nv / (1.0 - ADAM_B2 ** ADAM_STEP)
        d_ref[...] = -ADAM_LR * (m_hat / (jnp.sqrt(v_hat) + ADAM_EPS) + ADAM_WD * w_ref[...])
        nm_ref[...] = nm
        nv_ref[...] = nv

    blk = pl.BlockSpec((r, cc), lambda i: (i, 0))
    return pl.pallas_call(
        body, name=name, grid=(rr // r,), in_specs=[blk] * 4, out_specs=[blk] * 3,
        out_shape=[jax.ShapeDtypeStruct((rr, cc), F32)] * 3, compiler_params=_cparams("parallel"),
    )(g, w, m, v)


def _columns(sources, start, stop):
    out, at = [], 0
    for arr, off, width in sources:
        lo, hi = max(start, at), min(stop, at + width)
        if lo < hi:
            out.append(arr[:, off + lo - at:off + hi - at])
        at += width
    return out


def _flat_pack(parts, width):
    flat = jnp.concatenate([p.reshape(-1) for p in parts])
    total = -(-flat.shape[0] // (8 * width)) * (8 * width)
    return jnp.pad(flat, (0, total - flat.shape[0])).reshape(-1, width)


def _flat_unpack(pack, shapes):
    flat = pack.reshape(-1)
    out, at = [], 0
    for s in shapes:
        size = 1
        for e in s:
            size *= e
        out.append(flat[at:at + size].reshape(s))
        at += size
    return out


def kernel(x, meta, norm_w, w_in, conv_qkv_w, a_log, dt_bias, dn_norm_w, w_dn_out, dw_w, dw_b, ln_w, ln_b, w_cf_out, b_cf_out, w_o, final_norm_w, loss_target, m_meta, m_norm_w, m_w_in, m_conv_qkv_w, m_a_log, m_dt_bias, m_dn_norm_w, m_w_dn_out, m_dw_w, m_dw_b, m_ln_w, m_ln_b, m_w_cf_out, m_b_cf_out, m_w_o, m_final_norm_w, v_meta, v_norm_w, v_w_in, v_conv_qkv_w, v_a_log, v_dt_bias, v_dn_norm_w, v_w_dn_out, v_dw_w, v_dw_b, v_ln_w, v_ln_b, v_w_cf_out, v_b_cf_out, v_w_o, v_final_norm_w):
    d = x.shape[-1]
    heads = a_log.shape[-1]
    seq = x.shape[1]
    t = HDR + seq
    nc = t // CHUNK
    in_w = 9 * d + 2 * heads
    assert heads * DK == d and seq % LANES == 0 and w_in.shape[-1] * 4 == in_w
    xi, yi, ci = _place()
    shard = 2 * xi + yi
    ds = d // 4

    w_in_half = lax.dynamic_slice_in_dim(w_in[0].astype(MM_DTYPE), ci * (d // 2), d // 2, axis=0)
    w3_half = lax.dynamic_slice_in_dim(jnp.stack([w_dn_out[0], w_cf_out[0], w_o[0]]).astype(MM_DTYPE), ci * (d // 8), d // 8, axis=1)
    small = jnp.concatenate([
        jnp.pad(conv_qkv_w[0], ((0, 4), (0, 0))), jnp.pad(meta, ((0, 0), (0, 2 * ds))),
        jnp.pad(dw_w[0], ((0, 1), (0, 2 * ds)))], axis=0)
    g_in, g_small = _allgather([w_in_half, small], "gather_weights")
    my_slot = _slot((xi, yi, ci))
    w3_land = lax.dynamic_update_slice_in_dim(lax.empty((8, *w3_half.shape), w3_half.dtype), w3_half[None], my_slot, axis=0)
    w3_send, w3_recv, w3_in_flight, w3_token = _alltoall_start([w3_half[None]], [w3_land], "w3_start")
    sw = in_w // 4
    g_in = g_in.reshape(4, d, sw)
    w_cols = [(g_in[s], 0, sw) for s in range(4)]
    o_glu = 4 * d + 2 * heads
    w_main = jnp.concatenate(
        _columns(w_cols, 0, 4 * d) + _columns(w_cols, o_glu, o_glu + 2 * d) + _columns(w_cols, o_glu + 3 * d, o_glu + 5 * d)
        + _columns(w_cols, o_glu + 2 * d, o_glu + 3 * d), axis=1)
    w_ba = jnp.pad(jnp.concatenate(_columns(w_cols, 4 * d, o_glu), axis=1), ((0, 0), (0, LANES - 2 * heads)))
    small4 = g_small[0::2]
    conv_w_f = small4[:, 0:DN_TAPS, :].transpose(1, 0, 2).reshape(DN_TAPS, 3 * d)
    meta_f = small4[:, 8:8 + N_META, :ds].transpose(1, 0, 2).reshape(N_META, d)
    dw_w_f = small4[:, 24:56, :ds].transpose(1, 0, 2).reshape(32, d)

    xs = jnp.concatenate([jnp.pad(meta_f, ((PAD_ROWS, 0), (0, 0))), x[0]], axis=0)
    tgt = jnp.pad(loss_target[0], ((HDR, 0), (0, 0)))
    h = _rms_fwd(xs, norm_w + w3_token[0:1, 0:1], "rms_fwd")
    proj = _mm(h, w_main, name="proj_main")
    pba = _mm(h, w_ba, name="proj_ba")
    qkvn = _dn_prep(proj, conv_w_f, d, heads, "dn_prep")
    pvec = jnp.zeros((8, LANES), F32).at[0, heads:2 * heads].set(a_log[0]).at[1, heads:2 * heads].set(dt_bias[0])
    bg = _ba_fwd(pba, pvec, heads, "ba_fwd")
    g_row = bg[:, heads:2 * heads].reshape(nc, CHUNK, heads).transpose(0, 2, 1)
    o, sall, ainv = _dn_fwd(qkvn, bg, g_row, d, heads, "dn_fwd")
    g_w3, = _alltoall_wait(w3_send, w3_recv, w3_in_flight, o, "w3_wait")
    w3_full = g_w3.transpose(1, 0, 2, 3).reshape(3, d, d)
    w_dn_f, w_cf_f, w_o_f = w3_full[0], w3_full[1], w3_full[2]
    ya_in, ya = _dn_post(o, proj, dn_norm_w, w_dn_f, d, heads, "dn_post")
    c1, c3 = _cf_fwd(proj, dw_w_f, dw_b, ln_w, ln_b, d, "cf_fwd")
    yb = _mm(c3, w_cf_f, name="yb")
    dxo, dxo_mm, loss_acc, g_fnw, merged = _merge_final(ya, yb, b_cf_out, proj, w_o_f, xs, final_norm_w.reshape(1, d), tgt,
                                                        "merge_final")

    gw_o = _mm(merged, dxo_mm, ta=True, out_dtype=MM_DTYPE, name="gw_o")
    dya, dyb, dproj, g_bcf = _merge_bwd(dxo_mm, w_o_f, proj, ya, yb, b_cf_out, d, "merge_bwd")
    gw_cf = _mm(c3, dyb, ta=True, out_dtype=MM_DTYPE, name="gw_cf")
    gw_dn = _mm(ya_in, dya, ta=True, out_dtype=MM_DTYPE, name="gw_dn")
    own_home = lambda v: lax.dynamic_update_slice_in_dim(
        lax.empty(v.shape, v.dtype), lax.dynamic_slice_in_dim(v, my_slot, 1, axis=0), my_slot, axis=0)
    send_w3 = jnp.stack([gw_dn, gw_cf, gw_o]).reshape(3, 8, d // 8, d).transpose(1, 0, 2, 3).reshape(8, 3 * d // 8, d)
    gw3_send, gw3_recv, gw3_in_flight, _ = _alltoall_start([send_w3], [own_home(send_w3)], "exchange_w3_start")
    dc1, dproj, g_ln = _cf_bwd1(dyb, w_cf_f, c1, proj, ln_w, ln_b, dproj, d, "cf_bwd1")
    dproj, g_dww = _cf_bwd2(dc1, proj, dw_w_f, dproj, d, "cf_bwd2")
    do, dproj, g_dnw = _dn_post_bwd(dya, w_dn_f, o, proj, dn_norm_w, dproj, d, heads, "dn_post_bwd")
    dqkvn, dbg, dg_row = _dn_bwd(qkvn, bg, g_row, do, sall, ainv, d, heads, "dn_bwd")
    dproj, g_convw = _dn_prep_bwd(proj, conv_w_f, dqkvn, dproj, d, heads, "dn_prep_bwd")
    dbg = dbg + jnp.pad(dg_row.transpose(0, 2, 1).reshape(t, heads), ((0, 0), (heads, LANES - 2 * heads)))
    dpba, g_ba = _ba_bwd(pba, pvec, dbg, heads, "ba_bwd")
    gw_main = _mm(h, dproj, ta=True, out_dtype=MM_DTYPE, name="gw_main")
    gw_ba = _mm(h, dpba, ta=True, out_dtype=MM_DTYPE, name="gw_ba")

    g_cols = [(gw_main, 0, 4 * d), (gw_ba, 0, 2 * heads), (gw_main, 4 * d, 2 * d), (gw_main, 8 * d, d),
              (gw_main, 6 * d, 2 * d)]
    send_in = jnp.stack([jnp.concatenate(_columns(g_cols, s * sw, (s + 1) * sw), axis=1) for s in range(4)])
    send_in = send_in.reshape(8, d // 2, sw)
    send_sems, recv_sems, in_flight, token = _alltoall_start([send_in], [own_home(send_in)], "exchange_start")
    dxs, g_nw = _dh_rms_bwd(dproj, w_main, dpba, w_ba, xs, norm_w, dxo, token, "dh_rms_bwd")
    grad_x = dxs[HDR:][None]
    got_w3, = _alltoall_wait(gw3_send, gw3_recv, gw3_in_flight, dxs, "exchange_w3_wait")
    got_in, = _alltoall_wait(send_sems, recv_sems, in_flight, dxs, "exchange_wait")

    wide = lambda a: jnp.pad(a, ((0, 0), (0, d - a.shape[1])))
    small_g = jnp.concatenate([
        g_convw.reshape(24, d), dxs[PAD_ROWS:HDR], g_dww, g_nw, g_ln, g_bcf, g_fnw, wide(g_ba), wide(g_dnw), wide(loss_acc)],
        axis=0)
    red_in = _sum8(got_in, "sum_w_in")
    red_w3 = _sum8(got_w3, "sum_w3")
    pair_in, pair_w3 = _sibling_exchange([red_in, red_w3], "pair_grads")
    g_w_in = pair_in.reshape(d, in_w // 4)
    g_w3 = pair_w3.reshape(2, 3, d // 8, d).transpose(1, 0, 2, 3).reshape(3 * ds, d)
    all_small, = _allgather([small_g], "gather_small_grads")
    sg = _sum8(all_small, "sum_small")
    g_conv = lax.dynamic_slice_in_dim(sg[0:24].reshape(8, 3 * d)[0:DN_TAPS], shard * 3 * ds, 3 * ds, axis=1)
    g_meta = lax.dynamic_slice_in_dim(sg[24:40], shard * ds, ds, axis=1)
    g_dw_w = lax.dynamic_slice_in_dim(sg[40:40 + CF_TAPS], shard * ds, ds, axis=1)
    g_rep = {"norm_w": sg[72:73], "ln_w": sg[80:81], "ln_b": sg[81:82], "dw_b": sg[82:83], "b_cf_out": sg[88:89],
             "final_norm_w": sg[96], "a_log": sg[104:105, heads:2 * heads], "dt_bias": sg[105:106, heads:2 * heads],
             "dn_norm_w": sg[112:113, 0:DK]}
    loss = sg[120, 0]

    res = {}
    dl, nm, nv = _adamw(g_w_in, w_in[0], m_w_in[0], v_w_in[0], "adamw_w_in")
    res["w_in"] = (g_w_in[None], dl[None], nm[None], nv[None])
    stack3 = lambda a, b, c: jnp.concatenate([a[0], b[0], c[0]], axis=0)
    dl, nm, nv = _adamw(g_w3, stack3(w_dn_out, w_cf_out, w_o), stack3(m_w_dn_out, m_w_cf_out, m_w_o),
                        stack3(v_w_dn_out, v_w_cf_out, v_w_o), "adamw_w3")
    for j, nme in enumerate(("w_dn_out", "w_cf_out", "w_o")):
        res[nme] = tuple(a[j * ds:(j + 1) * ds][None] for a in (g_w3, dl, nm, nv))
    names = ["meta", "conv_qkv_w", "dw_w", "norm_w", "dw_b", "ln_w", "ln_b", "b_cf_out", "final_norm_w", "a_log", "dt_bias",
             "dn_norm_w"]
    grads = {"meta": g_meta, "conv_qkv_w": g_conv[None], "dw_w": g_dw_w[None], **g_rep}
    given = dict(meta=(meta, m_meta, v_meta), conv_qkv_w=(conv_qkv_w, m_conv_qkv_w, v_conv_qkv_w), dw_w=(dw_w, m_dw_w, v_dw_w),
                 norm_w=(norm_w, m_norm_w, v_norm_w), dw_b=(dw_b, m_dw_b, v_dw_b), ln_w=(ln_w, m_ln_w, v_ln_w),
                 ln_b=(ln_b, m_ln_b, v_ln_b), b_cf_out=(b_cf_out, m_b_cf_out, v_b_cf_out),
                 final_norm_w=(final_norm_w, m_final_norm_w, v_final_norm_w), a_log=(a_log, m_a_log, v_a_log),
                 dt_bias=(dt_bias, m_dt_bias, v_dt_bias), dn_norm_w=(dn_norm_w, m_dn_norm_w, v_dn_norm_w))
    shapes = [given[nme][0].shape for nme in names]
    packs = [_flat_pack([grads[nme] for nme in names], LANES)] + [_flat_pack([given[nme][j] for nme in names], LANES) for j in range(3)]
    outs = _adamw(*packs, "adamw_small")
    unpacked = [_flat_unpack(p, shapes) for p in (packs[0], *outs)]
    for j, nme in enumerate(names):
        res[nme] = tuple(u[j] for u in unpacked)

    order = ["meta", "norm_w", "w_in", "conv_qkv_w", "a_log", "dt_bias", "dn_norm_w", "w_dn_out", "dw_w", "dw_b", "ln_w", "ln_b",
             "w_cf_out", "b_cf_out", "w_o", "final_norm_w"]
    return (loss, grad_x, *[res[nme][0] for nme in order], *[res[nme][1] for nme in order],
            *[res[nme][2] for nme in order], *[res[nme][3] for nme in order])
```

```python
import jax
import jax.numpy as jnp
from jax import lax
from jax.experimental import pallas as pl
from jax.experimental.pallas import tpu as pltpu

F32 = jnp.float32
MM_DTYPE = jnp.bfloat16
EPS = 1e-6
N_META = 16
HDR = 128
PAD_ROWS = HDR - N_META
CHUNK = 64
DK = 128
CF_TAPS = 31
DN_TAPS = 4
HALO = 64
LANES = 128
ADAM_LR, ADAM_B1, ADAM_B2, ADAM_EPS, ADAM_WD, ADAM_STEP = 0.001, 0.9, 0.999, 1e-08, 0.01, 10
VMEM_LIMIT = 48 * 1024 * 1024
MESH = pl.DeviceIdType.MESH
HI = lax.Precision.HIGHEST
LO = lax.Precision.DEFAULT
NN = (((1,), (0,)), ((), ()))
NT = (((1,), (1,)), ((), ()))
TN = (((0,), (0,)), ((), ()))
ANY = pl.BlockSpec(memory_space=pl.ANY)


def _dot(a, b, dims=NN, prec=LO):
    if prec == LO:
        a, b = a.astype(MM_DTYPE), b.astype(MM_DTYPE)
    return lax.dot_general(a, b, dims, precision=prec, preferred_element_type=F32)


def _split(a):
    hi = a.astype(MM_DTYPE)
    return hi, (a - hi.astype(F32)).astype(MM_DTYPE)


def _dot3(a, b, dims=NN):
    ah, al = _split(a)
    bh, bl = _split(b)
    return _dot(ah, bh, dims) + (_dot(ah, bl, dims) + _dot(al, bh, dims))


def _pick(n, options):
    for o in options:
        if n % o == 0:
            return o
    return n


def _cparams(*sem):
    return pltpu.CompilerParams(dimension_semantics=sem, vmem_limit_bytes=VMEM_LIMIT)


def _sig(x):
    return jax.nn.sigmoid(x)


def _sum_all(a):
    return jnp.sum(jnp.sum(a, axis=-1, keepdims=True), axis=0, keepdims=True)


def _dsilu(x, s):
    return s + x * s * (1.0 - s)


def _shift(win, s):
    n = win.shape[0]
    s = s % n
    return win if s == 0 else pltpu.roll(win, s, 0)


def _row_tile(t):
    return _pick(t, (320, 128))


def _mm(a, b, *, nt=False, ta=False, out_dtype=F32, name):
    k, m = a.shape if ta else a.shape[::-1]
    n = b.shape[0] if nt else b.shape[1]
    tm = _pick(m, (1664, 1024, 640, 512, 384, 256, 128))
    tn = _pick(n, (1024, 768, 512, 256, 128))
    tk = _pick(k, (1664, 1024, 640, 512, 256, 128))
    nk = k // tk
    dims = NT if nt else (TN if ta else NN)

    def body(a_ref, b_ref, o_ref, *acc):
        p = lax.dot_general(a_ref[...], b_ref[...], dims, preferred_element_type=F32)
        if nk == 1:
            o_ref[...] = p.astype(o_ref.dtype)
            return
        acc_ref, = acc
        kk = pl.program_id(2)

        @pl.when(kk == 0)
        def _():
            acc_ref[...] = p

        @pl.when(kk > 0)
        def _():
            acc_ref[...] += p

        @pl.when(kk == nk - 1)
        def _():
            o_ref[...] = acc_ref[...].astype(o_ref.dtype)

    a_spec = pl.BlockSpec((tk, tm), lambda i, j, kk: (kk, i)) if ta else pl.BlockSpec((tm, tk), lambda i, j, kk: (i, kk))
    b_spec = pl.BlockSpec((tn, tk), lambda i, j, kk: (j, kk)) if nt else pl.BlockSpec((tk, tn), lambda i, j, kk: (kk, j))
    return pl.pallas_call(
        body, name=name, grid=(m // tm, n // tn, nk),
        in_specs=[a_spec, b_spec],
        out_specs=pl.BlockSpec((tm, tn), lambda i, j, kk: (i, j)),
        out_shape=jax.ShapeDtypeStruct((m, n), out_dtype),
        scratch_shapes=[] if nk == 1 else [pltpu.VMEM((tm, tn), F32)],
        compiler_params=_cparams("parallel", "parallel", "arbitrary"),
    )(a, b)


def _rms_fwd(xs, w, name):
    t, d = xs.shape
    r = _row_tile(t)

    def body(x_ref, w_ref, h_ref):
        x = x_ref[...]
        rs = lax.rsqrt(jnp.mean(x * x, axis=-1, keepdims=True) + EPS)
        h_ref[...] = (x * rs * w_ref[...]).astype(h_ref.dtype)

    return pl.pallas_call(
        body, name=name, grid=(t // r,),
        in_specs=[pl.BlockSpec((r, d), lambda i: (i, 0)), pl.BlockSpec((1, d), lambda i: (0, 0))],
        out_specs=pl.BlockSpec((r, d), lambda i: (i, 0)),
        out_shape=jax.ShapeDtypeStruct((t, d), MM_DTYPE), compiler_params=_cparams("parallel"),
    )(xs, w)


def _mm_rows(a, b, *, nt, tm, tk, extras, outs, epilogue, name, prologue=None, aliases=None):
    m, k = a.shape
    n = b.shape[0] if nt else b.shape[1]
    assert m % tm == 0 and k % tk == 0
    nk = k // tk
    ne = len(extras)

    def body(a_ref, b_ref, *rest):
        ex, ou = rest[:ne], rest[ne:ne + len(outs)]
        i, kk = pl.program_id(0), pl.program_id(1)
        lhs = a_ref[...] if prologue is None else prologue(a_ref, ex, ou)
        p = lax.dot_general(lhs, b_ref[...], NT if nt else NN, preferred_element_type=F32)
        if nk == 1:
            epilogue(p, i, ex, ou)
            return
        acc_ref = rest[-1]

        @pl.when(kk == 0)
        def _():
            acc_ref[...] = p

        @pl.when(jnp.logical_and(kk > 0, kk < nk - 1))
        def _():
            acc_ref[...] += p

        @pl.when(kk == nk - 1)
        def _():
            epilogue(acc_ref[...] + p, i, ex, ou)

    b_spec = pl.BlockSpec((n, tk), lambda i, kk: (0, kk)) if nt else pl.BlockSpec((tk, n), lambda i, kk: (kk, 0))
    return pl.pallas_call(
        body, name=name, grid=(m // tm, nk),
        in_specs=[pl.BlockSpec((tm, tk), lambda i, kk: (i, kk)), b_spec] + [s for _, s in extras],
        out_specs=[s for _, s in outs], out_shape=[s for s, _ in outs],
        scratch_shapes=[] if nk == 1 else [pltpu.VMEM((tm, n), F32)],
        input_output_aliases={2 + e: o for e, o in (aliases or {}).items()},
        compiler_params=_cparams("arbitrary", "arbitrary"),
    )(a, b, *[e for e, _ in extras])


def _dh_rms_bwd(dproj, w_main, dpba, w_ba, xs, w, dres, after, name):
    t, d = xs.shape
    tm = _pick(t, (640, 128))
    tk = _pick(w_main.shape[1], (2304, 768))

    def epilogue(dh, i, ex, ou):
        dpba_ref, wba_ref, x_ref, w_ref, dr_ref, _ = ex
        dx_ref, gw_ref = ou

        @pl.when(i == 0)
        def _():
            gw_ref[...] = jnp.zeros_like(gw_ref)

        dh = dh + lax.dot_general(dpba_ref[...], wba_ref[...], NT, preferred_element_type=F32)
        x = x_ref[...]
        rs = lax.rsqrt(jnp.mean(x * x, axis=-1, keepdims=True) + EPS)
        xh = x * rs
        dxh = dh * w_ref[...]
        dx_ref[...] = rs * (dxh - xh * jnp.mean(dxh * xh, axis=-1, keepdims=True)) + dr_ref[...]
        gw_ref[0:1, :] += jnp.sum(dh * xh, axis=0, keepdims=True)

    row = pl.BlockSpec((tm, d), lambda i, kk: (i, 0))
    const = lambda shape: pl.BlockSpec(shape, lambda i, kk: (0, 0))
    return _mm_rows(
        dproj, w_main, nt=True, tm=tm, tk=tk,
        extras=[(dpba, pl.BlockSpec((tm, LANES), lambda i, kk: (i, 0))), (w_ba, const(w_ba.shape)), (xs, row), (w, const((1, d))),
                (dres, row), (after, ANY)],
        outs=[(jax.ShapeDtypeStruct((t, d), F32), row), (jax.ShapeDtypeStruct((8, d), F32), const((8, d)))],
        epilogue=epilogue, name=name)


def _merge_final(ya, yb, b, proj, w_o, xs, w, tgt, name):
    t, d = xs.shape
    r = _row_tile(t)

    def prologue(ya_ref, ex, ou):
        yb_ref, b_ref, g_ref = ex[3:]
        m_ref = ou[4]
        m_ref[...] = (_sig(g_ref[:, :d]) * ya_ref[...] + _sig(g_ref[:, d:]) * (yb_ref[...] + b_ref[...])).astype(m_ref.dtype)
        return m_ref[...]

    def epilogue(mo, i, ex, ou):
        x_ref, w_ref, t_ref = ex[:3]
        dx_ref, dxm_ref, loss_ref, gw_ref = ou[:4]

        @pl.when(i == 0)
        def _():
            loss_ref[...] = jnp.zeros_like(loss_ref)
            gw_ref[...] = jnp.zeros_like(gw_ref)

        xo = x_ref[...] + mo
        rs = lax.rsqrt(jnp.mean(xo * xo, axis=-1, keepdims=True) + EPS)
        xh = xo * rs
        y = xh * w_ref[...]
        rows = i * r + lax.broadcasted_iota(jnp.int32, (r, 1), 0)
        err = jnp.where(rows >= HDR, y - t_ref[...], 0.0)
        loss_ref[...] += 0.5 * _sum_all(err * err) / d
        dy = err / d
        gw_ref[0:1, :] += jnp.sum(dy * xh, axis=0, keepdims=True)
        dxh = dy * w_ref[...]
        dx = rs * (dxh - xh * jnp.mean(dxh * xh, axis=-1, keepdims=True))
        dx_ref[...] = dx
        dxm_ref[...] = dx.astype(dxm_ref.dtype)

    row = pl.BlockSpec((r, d), lambda i, kk: (i, 0))
    const = lambda shape: pl.BlockSpec(shape, lambda i, kk: (0, 0))
    return _mm_rows(
        ya, w_o, nt=False, tm=r, tk=d,
        extras=[(xs, row), (w, const((1, d))), (tgt, row), (yb, row), (b, const((1, d))),
                (proj, pl.BlockSpec((r, 2 * d), lambda i, kk: (i, 3)))],
        outs=[(jax.ShapeDtypeStruct((t, d), F32), row), (jax.ShapeDtypeStruct((t, d), MM_DTYPE), row),
              (jax.ShapeDtypeStruct((8, LANES), F32), const((8, LANES))), (jax.ShapeDtypeStruct((8, d), F32), const((8, d))),
              (jax.ShapeDtypeStruct((t, d), MM_DTYPE), row)],
        prologue=prologue, epilogue=epilogue, name=name)


def _halo_prev(r):
    return lambda i: (jnp.maximum(i * (r // HALO) - 1, 0), 0)


def _halo_next(r, t):
    return lambda i: (jnp.minimum((i + 1) * (r // HALO), t // HALO - 1), 0)


def _conv_rows(t):
    return _pick(t, (320, 128))


def _qkv_window(p_ref, ph_ref, i, cs):
    return jnp.concatenate([jnp.where(i > 0, ph_ref[HALO - 8:HALO, cs], 0.0), p_ref[:, cs]], axis=0)


def _qkv_conv(win, cw, r):
    y = None
    for s in range(DN_TAPS):
        term = _shift(win, s)[8:8 + r, :] * cw[DN_TAPS - 1 - s:DN_TAPS - s, :]
        y = term if y is None else y + term
    return y


def _dn_prep(proj, conv_w, d, heads, name):
    t = proj.shape[0]
    r = _conv_rows(t)
    w3 = 3 * d

    def body(p_ref, ph_ref, cw_ref, o_ref):
        i = pl.program_id(0)
        for cb in range(w3 // DK):
            cs = slice(cb * DK, (cb + 1) * DK)
            y = _qkv_conv(_qkv_window(p_ref, ph_ref, i, cs), cw_ref[:, cs], r)
            a = y * _sig(y)
            if cb < 2 * heads:
                sc = DK ** -0.5 if cb < heads else 1.0
                a = a * (lax.rsqrt(jnp.sum(a * a, axis=-1, keepdims=True) + EPS) * sc)
            o_ref[:, cs] = a

    return pl.pallas_call(
        body, name=name, grid=(t // r,),
        in_specs=[pl.BlockSpec((r, w3), lambda i: (i, 0)), pl.BlockSpec((HALO, w3), _halo_prev(r)),
                  pl.BlockSpec((DN_TAPS, w3), lambda i: (0, 0))],
        out_specs=pl.BlockSpec((r, w3), lambda i: (i, 0)),
        out_shape=jax.ShapeDtypeStruct((t, w3), F32), compiler_params=_cparams("parallel"),
    )(proj, proj, conv_w)


def _dn_prep_bwd(proj, conv_w, dqkvn, dproj, d, heads, name):
    t = proj.shape[0]
    r = _conv_rows(t)
    w3 = 3 * d
    last = t // r - 1

    def body(p_ref, ph_ref, pn_ref, cw_ref, dn_ref, dnn_ref, _, dp_ref, gw_ref):
        i = pl.program_id(0)

        @pl.when(i == 0)
        def _():
            gw_ref[...] = jnp.zeros_like(gw_ref)

        for cb in range(w3 // DK):
            cs = slice(cb * DK, (cb + 1) * DK)
            cw = cw_ref[:, cs]
            win = jnp.concatenate([_qkv_window(p_ref, ph_ref, i, cs), jnp.where(i < last, pn_ref[0:8, cs], 0.0)], axis=0)
            y = _qkv_conv(win, cw, r + 8)
            sy = _sig(y)
            da = jnp.concatenate([dn_ref[:, cs], jnp.where(i < last, dnn_ref[0:8, cs], 0.0)], axis=0)
            if cb < 2 * heads:
                a = y * sy
                rn = lax.rsqrt(jnp.sum(a * a, axis=-1, keepdims=True) + EPS)
                n0 = a * rn
                dn0 = da * (DK ** -0.5 if cb < heads else 1.0)
                da = rn * (dn0 - n0 * jnp.sum(dn0 * n0, axis=-1, keepdims=True))
            dconv = da * _dsilu(y, sy)
            acc = None
            for j in range(DN_TAPS):
                s = DN_TAPS - 1 - j
                term = _shift(dconv, -s)[0:r, :] * cw[j:j + 1, :]
                acc = term if acc is None else acc + term
                gw_ref[j:j + 1, cs] += jnp.sum(dconv[0:r, :] * _shift(win, s)[8:8 + r, :], axis=0, keepdims=True)
            dp_ref[:, cs] = acc.astype(dp_ref.dtype)

    row = pl.BlockSpec((r, w3), lambda i: (i, 0))
    nxt = pl.BlockSpec((HALO, w3), _halo_next(r, t))
    return pl.pallas_call(
        body, name=name, grid=(t // r,),
        in_specs=[row, pl.BlockSpec((HALO, w3), _halo_prev(r)), nxt, pl.BlockSpec((DN_TAPS, w3), lambda i: (0, 0)), row, nxt, ANY],
        out_specs=[row, pl.BlockSpec((8, w3), lambda i: (0, 0))],
        out_shape=[jax.ShapeDtypeStruct(dproj.shape, dproj.dtype), jax.ShapeDtypeStruct((8, w3), F32)],
        input_output_aliases={6: 0}, compiler_params=_cparams("arbitrary"),
    )(proj, proj, proj, conv_w, dqkvn, dqkvn, dproj)


def _ba_terms(x, pv, heads):
    lane = lax.broadcasted_iota(jnp.int32, x.shape, 1)
    is_b = lane < heads
    is_a = jnp.logical_and(lane >= heads, lane < 2 * heads)
    beta = _sig(x)
    z = x + pv[1:2, :]
    nexp = -jnp.exp(pv[0:1, :])
    sp = jnp.maximum(z, 0.0) + jnp.log1p(jnp.exp(-jnp.abs(z)))
    return is_b, is_a, beta, z, nexp, nexp * sp


def _ba_fwd(pba, pvec, heads, name):
    t = pba.shape[0]
    r = _row_tile(t)

    def body(x_ref, pv_ref, o_ref):
        is_b, is_a, beta, _, _, g = _ba_terms(x_ref[...], pv_ref[...], heads)
        rows = pl.program_id(0) * r + lax.broadcasted_iota(jnp.int32, (r, 1), 0)
        o_ref[...] = jnp.where(rows >= PAD_ROWS, jnp.where(is_b, beta, jnp.where(is_a, g, 0.0)), 0.0)

    row = pl.BlockSpec((r, LANES), lambda i: (i, 0))
    return pl.pallas_call(
        body, name=name, grid=(t // r,), in_specs=[row, pl.BlockSpec((8, LANES), lambda i: (0, 0))],
        out_specs=row, out_shape=jax.ShapeDtypeStruct((t, LANES), F32), compiler_params=_cparams("parallel"),
    )(pba, pvec)


def _ba_bwd(pba, pvec, dbg, heads, name):
    t = pba.shape[0]
    r = _row_tile(t)

    def body(x_ref, pv_ref, d_ref, o_ref, g_ref):
        @pl.when(pl.program_id(0) == 0)
        def _():
            g_ref[...] = jnp.zeros_like(g_ref)

        is_b, is_a, beta, z, nexp, g = _ba_terms(x_ref[...], pv_ref[...], heads)
        rows = pl.program_id(0) * r + lax.broadcasted_iota(jnp.int32, (r, 1), 0)
        dd = jnp.where(rows >= PAD_ROWS, d_ref[...], 0.0)
        dz = dd * nexp * _sig(z)
        o_ref[...] = jnp.where(is_b, dd * beta * (1.0 - beta), jnp.where(is_a, dz, 0.0)).astype(o_ref.dtype)
        g_ref[0:1, :] += jnp.sum(jnp.where(is_a, dd * g, 0.0), axis=0, keepdims=True)
        g_ref[1:2, :] += jnp.sum(jnp.where(is_a, dz, 0.0), axis=0, keepdims=True)

    row = pl.BlockSpec((r, LANES), lambda i: (i, 0))
    par = pl.BlockSpec((8, LANES), lambda i: (0, 0))
    return pl.pallas_call(
        body, name=name, grid=(t // r,), in_specs=[row, par, row], out_specs=[row, par],
        out_shape=[jax.ShapeDtypeStruct((t, LANES), MM_DTYPE), jax.ShapeDtypeStruct((8, LANES), F32)],
        compiler_params=_cparams("arbitrary"),
    )(pba, pvec, dbg)


def _chunk_consts():
    ri = lax.broadcasted_iota(jnp.int32, (CHUNK, CHUNK), 0)
    ci = lax.broadcasted_iota(jnp.int32, (CHUNK, CHUNK), 1)
    return ri >= ci, ri > ci, (ri == ci).astype(F32), (ri >= ci).astype(F32), (ri <= ci).astype(F32)


def _chunk_decay(gc, gr, incl):
    return jnp.where(incl, jnp.exp(jnp.where(incl, gc - gr, 0.0)), 0.0)


def _unit_lower_inverse(n, eye):
    x = [eye + a for a in n]
    p = list(n)
    for _ in range(5):
        p = [_dot(a, a) for a in p]
        x = [a + _dot(a, b) for a, b in zip(x, p)]
    r = [eye - a + _dot3(b, a) for a, b in zip(x, n)]
    return [a + _dot(a, b) for a, b in zip(x, r)]


def _dn_fwd(qkvn, bg, g_row, d, heads, name):
    t = qkvn.shape[0]
    nc = t // CHUNK

    def body(qkv_ref, bg_ref, gr_ref, o_ref, sall_ref, ainv_ref, s_ref):
        @pl.when(pl.program_id(0) == 0)
        def _():
            s_ref[...] = jnp.zeros_like(s_ref)

        incl, strict, eye, tril, triu = _chunk_consts()
        bgv = bg_ref[...]
        gc_all = _dot(tril, bgv, NN, HI)
        gr_all = _dot(gr_ref[0], triu, NN, HI)
        hs = range(heads)
        q = [qkv_ref[:, h * DK:(h + 1) * DK] for h in hs]
        k = [qkv_ref[:, d + h * DK:d + (h + 1) * DK] for h in hs]
        v = [qkv_ref[:, 2 * d + h * DK:2 * d + (h + 1) * DK] for h in hs]
        kb = [a.astype(MM_DTYPE) for a in k]
        beta = [bgv[:, h:h + 1] for h in hs]
        gc = [gc_all[:, heads + h:heads + h + 1] for h in hs]
        decay = [_chunk_decay(gc[h], gr_all[h:h + 1, :], incl) for h in hs]
        eg = [jnp.exp(a) for a in gc]
        n = [jnp.where(strict, -(beta[h] * _dot(kb[h], kb[h], NT) * decay[h]), 0.0) for h in hs]
        x = _unit_lower_inverse(n, eye)
        sol = [_dot3(x[h], jnp.concatenate([v[h] * beta[h], k[h] * (beta[h] * eg[h])], axis=1)) for h in hs]
        attn = [_dot(q[h], kb[h], NT) * decay[h] for h in hs]
        s = [s_ref[h] for h in hs]
        sb = [a.astype(MM_DTYPE) for a in s]
        wv = [sol[h][:, :DK] - _dot(sol[h][:, DK:], sb[h]) for h in hs]
        o = [_dot(q[h] * eg[h], sb[h]) + _dot(attn[h], wv[h]) for h in hs]
        glast = [a[CHUNK - 1:CHUNK, :] for a in gc]
        s_new = [s[h] * jnp.exp(glast[h]) + _dot(k[h] * jnp.exp(glast[h] - gc[h]), wv[h], TN) for h in hs]
        for h in hs:
            o_ref[:, h * DK:(h + 1) * DK] = o[h]
            sall_ref[0, h] = s[h]
            ainv_ref[0, h] = x[h]
            s_ref[h] = s_new[h]

    return pl.pallas_call(
        body, name=name, grid=(nc,),
        in_specs=[pl.BlockSpec((CHUNK, 3 * d), lambda i: (i, 0)), pl.BlockSpec((CHUNK, LANES), lambda i: (i, 0)),
                  pl.BlockSpec((1, heads, CHUNK), lambda i: (i, 0, 0))],
        out_specs=[pl.BlockSpec((CHUNK, d), lambda i: (i, 0)), pl.BlockSpec((1, heads, DK, DK), lambda i: (i, 0, 0, 0)),
                   pl.BlockSpec((1, heads, CHUNK, CHUNK), lambda i: (i, 0, 0, 0))],
        out_shape=[jax.ShapeDtypeStruct((t, d), F32), jax.ShapeDtypeStruct((nc, heads, DK, DK), F32),
                   jax.ShapeDtypeStruct((nc, heads, CHUNK, CHUNK), F32)],
        scratch_shapes=[pltpu.VMEM((heads, DK, DK), F32)], compiler_params=_cparams("arbitrary"),
    )(qkvn, bg, g_row)


def _dn_bwd(qkvn, bg, g_row, do, sall, ainv, d, heads, name):
    t = qkvn.shape[0]
    nc = t // CHUNK
    rev = lambda i: nc - 1 - i

    def body(qkv_ref, bg_ref, gr_ref, do_ref, sall_ref, ainv_ref, dqkv_ref, dbg_ref, dgr_ref, ds_ref):
        @pl.when(pl.program_id(0) == 0)
        def _():
            ds_ref[...] = jnp.zeros_like(ds_ref)

        incl, strict, eye, tril, triu = _chunk_consts()
        bgv = bg_ref[...]
        gc_all = _dot(tril, bgv, NN, HI)
        gr_all = _dot(gr_ref[0], triu, NN, HI)
        lane = lax.broadcasted_iota(jnp.int32, (CHUNK, LANES), 1)
        hrow = lax.broadcasted_iota(jnp.int32, (heads, CHUNK), 0)
        last_row = lax.broadcasted_iota(jnp.int32, (CHUNK, 1), 0) == CHUNK - 1
        dbeta_slab = jnp.zeros((CHUNK, LANES), F32)
        dgc_slab = jnp.zeros((CHUNK, LANES), F32)
        dgr_slab = jnp.zeros((heads, CHUNK), F32)
        hs = range(heads)
        cols = lambda ref, off: [ref[:, off + h * DK:off + (h + 1) * DK] for h in hs]
        qs, ks, vs, douts = cols(qkv_ref, 0), cols(qkv_ref, d), cols(qkv_ref, 2 * d), cols(do_ref, 0)
        ss, xs_, dsns = [sall_ref[0, h] for h in hs], [ainv_ref[0, h] for h in hs], [ds_ref[h] for h in hs]

        def recompute(h, z):
            z.q, z.k, z.v, z.dout, z.s, z.x, z.dsn = qs[h], ks[h], vs[h], douts[h], ss[h], xs_[h], dsns[h]
            z.kb, z.qb, z.sb, z.dsnb, z.doutb = (a.astype(MM_DTYPE) for a in (z.k, z.q, z.s, z.dsn, z.dout))
            z.beta = bgv[:, h:h + 1]
            gc = gc_all[:, heads + h:heads + h + 1]
            z.decay = _chunk_decay(gc, gr_all[h:h + 1, :], incl)
            z.eg = jnp.exp(gc)
            glast = gc[CHUNK - 1:CHUNK, :]
            z.eglast = jnp.exp(glast)
            z.ek = jnp.exp(glast - gc)
            z.kk = _dot(z.kb, z.kb, NT)
            z.qk = _dot(z.qb, z.kb, NT)
            sol = _dot3(z.x, jnp.concatenate([z.v * z.beta, z.k * (z.beta * z.eg)], axis=1))
            z.u, z.wb = sol[:, :DK], sol[:, DK:].astype(MM_DTYPE)
            z.attn = z.qk * z.decay
            z.qg = z.q * z.eg
            z.kend = z.k * z.ek

        def pseudo_values(h, z):
            z.wvb = (z.u - _dot(z.wb, z.sb)).astype(MM_DTYPE)

        def scan_step(h, z):
            z.d_wv = _dot(z.attn, z.doutb, TN) + _dot(z.kend, z.dsnb)
            z.d_attn = _dot(z.doutb, z.wvb, NT)
            z.d_qg = _dot(z.doutb, z.sb, NT)
            z.d_kend = _dot(z.wvb, z.dsnb, NT)
            z.ds_new = _dot(z.qg, z.doutb, TN) + z.eglast * z.dsn
            z.d_glast = z.eglast * _sum_all(z.dsn * z.s) + _sum_all(z.d_kend * z.kend)

        def state_terms(h, z):
            z.ds_new = z.ds_new - _dot(z.wb, z.d_wv, TN)
            z.d_w = -_dot(z.d_wv, z.sb, NT)

        def solve_transpose(h, z):
            d_rhs = _dot3(z.x, jnp.concatenate([z.d_wv, z.d_w], axis=1), TN)
            z.d_ru, z.d_rw = d_rhs[:, :DK], d_rhs[:, DK:]

        def lower_terms(h, z):
            z.d_low = jnp.where(strict, -(_dot(z.d_ru, z.u, NT) + _dot(z.d_rw, z.wb, NT)), 0.0)

        def outputs(h, z):
            rw_k = jnp.sum(z.d_rw * z.k, axis=-1, keepdims=True)
            z.dbeta = (jnp.sum(z.d_ru * z.v, axis=-1, keepdims=True) + rw_k * z.eg
                       + jnp.sum(z.d_low * z.kk * z.decay, axis=-1, keepdims=True))
            z.dv = z.d_ru * z.beta
            d_kk = (z.d_low * z.beta * z.decay).astype(MM_DTYPE)
            d_qk = (z.d_attn * z.decay).astype(MM_DTYPE)
            z.dk = (z.d_rw * (z.beta * z.eg) + _dot(d_kk, z.kb) + _dot(d_kk, z.kb, TN) + _dot(d_qk, z.qb, TN)
                    + z.d_kend * z.ek)
            z.dq = _dot(d_qk, z.kb) + z.d_qg * z.eg
            e = (z.d_low * z.beta * z.kk + z.d_attn * z.qk) * z.decay
            z.dgc = (rw_k * z.beta * z.eg + jnp.sum(e, axis=-1, keepdims=True)
                     + jnp.sum(z.d_qg * z.qg, axis=-1, keepdims=True) - jnp.sum(z.d_kend * z.kend, axis=-1, keepdims=True)
                     + jnp.where(last_row, z.d_glast, 0.0))
            z.dgr = -jnp.sum(e, axis=0, keepdims=True)

        class _Head:
            pass

        st = [_Head() for _ in hs]
        for phase in (recompute, pseudo_values, scan_step, state_terms, solve_transpose, lower_terms, outputs):
            for h in hs:
                phase(h, st[h])
        res = [(z.dq, z.dk, z.dv, z.ds_new, z.dbeta, z.dgc, z.dgr) for z in st]
        for h, (dq, dk, dv, ds_new, dbeta, dgc, dgr) in enumerate(res):
            dqkv_ref[:, h * DK:(h + 1) * DK] = dq
            dqkv_ref[:, d + h * DK:d + (h + 1) * DK] = dk
            dqkv_ref[:, 2 * d + h * DK:2 * d + (h + 1) * DK] = dv
            ds_ref[h] = ds_new
            dbeta_slab = jnp.where(lane == h, dbeta, dbeta_slab)
            dgc_slab = jnp.where(lane == heads + h, dgc, dgc_slab)
            dgr_slab = jnp.where(hrow == h, dgr, dgr_slab)
        dbg_ref[...] = dbeta_slab + _dot(triu, dgc_slab, NN, HI)
        dgr_ref[0] = _dot(dgr_slab, tril, NN, HI)

    return pl.pallas_call(
        body, name=name, grid=(nc,),
        in_specs=[pl.BlockSpec((CHUNK, 3 * d), lambda i: (rev(i), 0)), pl.BlockSpec((CHUNK, LANES), lambda i: (rev(i), 0)),
                  pl.BlockSpec((1, heads, CHUNK), lambda i: (rev(i), 0, 0)), pl.BlockSpec((CHUNK, d), lambda i: (rev(i), 0)),
                  pl.BlockSpec((1, heads, DK, DK), lambda i: (rev(i), 0, 0, 0)),
                  pl.BlockSpec((1, heads, CHUNK, CHUNK), lambda i: (rev(i), 0, 0, 0))],
        out_specs=[pl.BlockSpec((CHUNK, 3 * d), lambda i: (rev(i), 0)), pl.BlockSpec((CHUNK, LANES), lambda i: (rev(i), 0)),
                   pl.BlockSpec((1, heads, CHUNK), lambda i: (rev(i), 0, 0))],
        out_shape=[jax.ShapeDtypeStruct((t, 3 * d), F32), jax.ShapeDtypeStruct((t, LANES), F32),
                   jax.ShapeDtypeStruct((nc, heads, CHUNK), F32)],
        scratch_shapes=[pltpu.VMEM((heads, DK, DK), F32)], compiler_params=_cparams("arbitrary"),
    )(qkvn, bg, g_row, do, sall, ainv)


def _dn_post(o, proj, w, w_dn, d, heads, name):
    t = o.shape[0]
    r = _row_tile(t)

    def prologue(o_ref, ex, ou):
        z_ref, w_ref = ex
        y_ref = ou[0]
        for h in range(heads):
            sl = slice(h * DK, (h + 1) * DK)
            oh, z = o_ref[:, sl], z_ref[:, sl]
            rs = lax.rsqrt(jnp.mean(oh * oh, axis=-1, keepdims=True) + EPS)
            y_ref[:, sl] = (oh * rs * w_ref[...] * (z * _sig(z))).astype(y_ref.dtype)
        return y_ref[...]

    def epilogue(p, i, ex, ou):
        ou[1][...] = p

    row = pl.BlockSpec((r, d), lambda i, kk: (i, 0))
    return _mm_rows(
        o, w_dn, nt=False, tm=r, tk=d,
        extras=[(proj, pl.BlockSpec((r, d), lambda i, kk: (i, 3))), (w, pl.BlockSpec((1, DK), lambda i, kk: (0, 0)))],
        outs=[(jax.ShapeDtypeStruct((t, d), MM_DTYPE), row), (jax.ShapeDtypeStruct((t, d), F32), row)],
        prologue=prologue, epilogue=epilogue, name=name)


def _dn_post_bwd(dya, w_dn, o, proj, w, dproj, d, heads, name):
    t = o.shape[0]
    r = _row_tile(t)

    def epilogue(dy, i, ex, ou):
        o_ref, z_ref, w_ref, _ = ex
        do_ref, dz_ref, gw_ref = ou

        @pl.when(i == 0)
        def _():
            gw_ref[...] = jnp.zeros_like(gw_ref)

        gw = jnp.zeros((1, DK), F32)
        for h in range(heads):
            sl = slice(h * DK, (h + 1) * DK)
            oh, z, dyh = o_ref[:, sl], z_ref[:, sl], dy[:, sl]
            sz = _sig(z)
            rs = lax.rsqrt(jnp.mean(oh * oh, axis=-1, keepdims=True) + EPS)
            xh = oh * rs
            dn = dyh * (z * sz)
            dz_ref[:, sl] = (dyh * (xh * w_ref[...]) * _dsilu(z, sz)).astype(dz_ref.dtype)
            dxh = dn * w_ref[...]
            do_ref[:, sl] = rs * (dxh - xh * jnp.mean(dxh * xh, axis=-1, keepdims=True))
            gw = gw + jnp.sum(dn * xh, axis=0, keepdims=True)
        gw_ref[0:1, :] += gw

    row = pl.BlockSpec((r, d), lambda i, kk: (i, 0))
    za = pl.BlockSpec((r, d), lambda i, kk: (i, 3))
    return _mm_rows(
        dya, w_dn, nt=True, tm=r, tk=d,
        extras=[(o, row), (proj, za), (w, pl.BlockSpec((1, DK), lambda i, kk: (0, 0))), (dproj, ANY)],
        outs=[(jax.ShapeDtypeStruct((t, d), F32), row), (jax.ShapeDtypeStruct(dproj.shape, dproj.dtype), za),
              (jax.ShapeDtypeStruct((8, DK), F32), pl.BlockSpec((8, DK), lambda i, kk: (0, 0)))],
        aliases={3: 1}, epilogue=epilogue, name=name)


def _glu_window(g_ref, gh_ref, i, d, cs):
    cs2 = slice(d + cs.start, d + cs.stop)
    cur = g_ref[:, cs] * _sig(g_ref[:, cs2])
    prev = gh_ref[HALO - 32:HALO, cs] * _sig(gh_ref[HALO - 32:HALO, cs2])
    return jnp.concatenate([jnp.where(i > 0, prev, 0.0), cur], axis=0)


def _ln_stats(c1):
    mu = jnp.mean(c1, axis=-1, keepdims=True)
    cc = c1 - mu
    return cc * lax.rsqrt(jnp.mean(cc * cc, axis=-1, keepdims=True) + EPS), lax.rsqrt(jnp.mean(cc * cc, axis=-1, keepdims=True) + EPS)


def _cf_fwd(proj, dw_w, dw_b, ln_w, ln_b, d, name):
    t = proj.shape[0]
    r = _conv_rows(t)

    def body(g_ref, gh_ref, zb_ref, w_ref, b_ref, lw_ref, lb_ref, c1_ref, c3_ref):
        for cb in range(d // LANES):
            cs = slice(cb * LANES, (cb + 1) * LANES)
            win = _glu_window(g_ref, gh_ref, pl.program_id(0), d, cs)
            acc = jnp.broadcast_to(b_ref[:, cs], (r, LANES))
            for sub in range(8):
                rot = _shift(win, sub)
                for s in range(sub, CF_TAPS, 8):
                    acc = acc + rot[32 - (s - sub):32 - (s - sub) + r, :] * w_ref[CF_TAPS - 1 - s:CF_TAPS - s, cs]
            c1_ref[:, cs] = acc
        xh, _ = _ln_stats(c1_ref[...])
        ln = xh * lw_ref[...] + lb_ref[...]
        zb = zb_ref[...]
        c3_ref[...] = ((ln * _sig(ln)) * (zb * _sig(zb))).astype(c3_ref.dtype)

    row = pl.BlockSpec((r, d), lambda i: (i, 0))
    par = pl.BlockSpec((1, d), lambda i: (0, 0))
    return pl.pallas_call(
        body, name=name, grid=(t // r,),
        in_specs=[pl.BlockSpec((r, 2 * d), lambda i: (i, 2)), pl.BlockSpec((HALO, 2 * d), lambda i: (_halo_prev(r)(i)[0], 2)),
                  pl.BlockSpec((r, d), lambda i: (i, 8)), pl.BlockSpec((32, d), lambda i: (0, 0)), par, par, par],
        out_specs=[row, row],
        out_shape=[jax.ShapeDtypeStruct((t, d), F32), jax.ShapeDtypeStruct((t, d), MM_DTYPE)],
        compiler_params=_cparams("parallel"),
    )(proj, proj, proj, dw_w, dw_b, ln_w, ln_b)


def _cf_bwd1(dyb, w_cf, c1, proj, ln_w, ln_b, dproj, d, name):
    t = c1.shape[0]
    r = _row_tile(t)

    def epilogue(dc3v, i, ex, ou):
        c1_ref, zb_ref, lw_ref, lb_ref, _ = ex
        dc1_ref, dzb_ref, g_ref = ou

        @pl.when(i == 0)
        def _():
            g_ref[...] = jnp.zeros_like(g_ref)

        xh, rs = _ln_stats(c1_ref[...])
        ln = xh * lw_ref[...] + lb_ref[...]
        sl, zb = _sig(ln), zb_ref[...]
        szb = _sig(zb)
        dzb_ref[...] = (dc3v * (ln * sl) * _dsilu(zb, szb)).astype(dzb_ref.dtype)
        dln = dc3v * (zb * szb) * _dsilu(ln, sl)
        dxh = dln * lw_ref[...]
        dc1 = rs * (dxh - jnp.mean(dxh, axis=-1, keepdims=True) - xh * jnp.mean(dxh * xh, axis=-1, keepdims=True))
        dc1_ref[...] = dc1
        g_ref[0:1, :] += jnp.sum(dln * xh, axis=0, keepdims=True)
        g_ref[1:2, :] += jnp.sum(dln, axis=0, keepdims=True)
        g_ref[2:3, :] += jnp.sum(dc1, axis=0, keepdims=True)

    row = pl.BlockSpec((r, d), lambda i, kk: (i, 0))
    zbs = pl.BlockSpec((r, d), lambda i, kk: (i, 8))
    par = pl.BlockSpec((1, d), lambda i, kk: (0, 0))
    return _mm_rows(
        dyb, w_cf, nt=True, tm=r, tk=d, extras=[(c1, row), (proj, zbs), (ln_w, par), (ln_b, par), (dproj, ANY)],
        outs=[(jax.ShapeDtypeStruct((t, d), F32), row), (jax.ShapeDtypeStruct(dproj.shape, dproj.dtype), zbs),
              (jax.ShapeDtypeStruct((8, d), F32), pl.BlockSpec((8, d), lambda i, kk: (0, 0)))],
        aliases={4: 1}, epilogue=epilogue, name=name)


def _cf_bwd2(dc1, proj, dw_w, dproj, d, name):
    t = dc1.shape[0]
    r = _conv_rows(t)
    last = t // r - 1

    def body(dc_ref, dcn_ref, g_ref, gh_ref, w_ref, _, dg_ref, gw_ref):
        i = pl.program_id(0)

        @pl.when(i == 0)
        def _():
            gw_ref[...] = jnp.zeros_like(gw_ref)

        for cb in range(d // LANES):
            cs = slice(cb * LANES, (cb + 1) * LANES)
            cs2 = slice(d + cs.start, d + cs.stop)
            dcur = dc_ref[:, cs]
            dwin = jnp.concatenate([dcur, jnp.where(i < last, dcn_ref[0:32, cs], 0.0)], axis=0)
            win = _glu_window(g_ref, gh_ref, i, d, cs)
            acc = None
            for sub in range(8):
                drot = _shift(dwin, -sub)
                rot = _shift(win, sub)
                for s in range(sub, CF_TAPS, 8):
                    j = CF_TAPS - 1 - s
                    term = drot[s - sub:s - sub + r, :] * w_ref[j:j + 1, cs]
                    acc = term if acc is None else acc + term
                    gw_ref[j:j + 1, cs] += jnp.sum(dcur * rot[32 - (s - sub):32 - (s - sub) + r, :], axis=0, keepdims=True)
            ga, sb = g_ref[:, cs], _sig(g_ref[:, cs2])
            dg_ref[:, cs] = (acc * sb).astype(dg_ref.dtype)
            dg_ref[:, cs2] = (acc * ga * sb * (1.0 - sb)).astype(dg_ref.dtype)

    row = pl.BlockSpec((r, d), lambda i: (i, 0))
    glu = pl.BlockSpec((r, 2 * d), lambda i: (i, 2))
    return pl.pallas_call(
        body, name=name, grid=(t // r,),
        in_specs=[row, pl.BlockSpec((HALO, d), _halo_next(r, t)), glu,
                  pl.BlockSpec((HALO, 2 * d), lambda i: (_halo_prev(r)(i)[0], 2)), pl.BlockSpec((32, d), lambda i: (0, 0)), ANY],
        out_specs=[glu, pl.BlockSpec((32, d), lambda i: (0, 0))],
        out_shape=[jax.ShapeDtypeStruct(dproj.shape, dproj.dtype), jax.ShapeDtypeStruct((32, d), F32)],
        input_output_aliases={5: 0}, compiler_params=_cparams("arbitrary"),
    )(dc1, dc1, proj, proj, dw_w, dproj)


def _merge_bwd(dxo_mm, w_o, proj, ya, yb, b, d, name):
    t = ya.shape[0]
    r = _row_tile(t)

    def epilogue(dmv, i, ex, ou):
        g_ref, ya_ref, yb_ref, b_ref = ex
        dya_ref, dyb_ref, dg_ref, gb_ref = ou

        @pl.when(i == 0)
        def _():
            gb_ref[...] = jnp.zeros_like(gb_ref)

        sa, sb = _sig(g_ref[:, :d]), _sig(g_ref[:, d:])
        dyb = dmv * sb
        dya_ref[...] = (dmv * sa).astype(dya_ref.dtype)
        dyb_ref[...] = dyb.astype(dyb_ref.dtype)
        dg_ref[:, :d] = (dmv * ya_ref[...] * sa * (1.0 - sa)).astype(dg_ref.dtype)
        dg_ref[:, d:] = (dmv * (yb_ref[...] + b_ref[...]) * sb * (1.0 - sb)).astype(dg_ref.dtype)
        gb_ref[0:1, :] += jnp.sum(dyb, axis=0, keepdims=True)

    row = pl.BlockSpec((r, d), lambda i, kk: (i, 0))
    gate = pl.BlockSpec((r, 2 * d), lambda i, kk: (i, 3))
    const = lambda shape: pl.BlockSpec(shape, lambda i, kk: (0, 0))
    return _mm_rows(
        dxo_mm, w_o, nt=True, tm=r, tk=d, extras=[(proj, gate), (ya, row), (yb, row), (b, const((1, d)))],
        outs=[(jax.ShapeDtypeStruct((t, d), MM_DTYPE), row), (jax.ShapeDtypeStruct((t, d), MM_DTYPE), row),
              (jax.ShapeDtypeStruct((t, 9 * d), MM_DTYPE), gate), (jax.ShapeDtypeStruct((8, d), F32), const((8, d)))],
        epilogue=epilogue, name=name)


def _place():
    return lax.axis_index("x"), lax.axis_index("y"), lax.axis_index("c")


def _slot(p):
    return 4 * p[0] + 2 * p[1] + p[2]


def _allgather(blocks, name):
    n = len(blocks)

    def body(*refs):
        ins, outs = refs[:n], refs[n:2 * n]
        send_sems, recv_sems, local_sems = refs[2 * n:]
        x, y, c = _place()
        me, sibling = (x, y, c), (x, y, 1 - c)
        chips = [(1 - x, y), (x, 1 - y), (1 - x, 1 - y)]

        def copy(a, k, block, to, src=None):
            dst = outs[a].at[_slot(block)]
            return pltpu.make_async_remote_copy(
                src_ref=dst if src is None else src, dst_ref=dst, send_sem=send_sems.at[7 * a + k],
                recv_sem=recv_sems.at[7 * a + k], device_id=to, device_id_type=MESH)

        mine = [pltpu.make_async_copy(ins[a], outs[a].at[_slot(me)], local_sems.at[a]) for a in range(n)]
        for cp in mine:
            cp.start()
        first = []
        for a in range(n):
            first.append(copy(a, 0, me, sibling, src=ins[a]))
            first += [copy(a, 1 + j, me, (*chip, c), src=ins[a]) for j, chip in enumerate(chips)]
        for cp in first:
            cp.start()
        passed = []
        for j, chip in enumerate(chips):
            for a in range(n):
                copy(a, 1 + j, (*chip, c), me).wait_recv()
                cp = copy(a, 4 + j, (*chip, c), sibling)
                cp.start()
                passed.append(cp)
        for a in range(n):
            copy(a, 0, sibling, me).wait_recv()
            for j, chip in enumerate(chips):
                copy(a, 4 + j, (*chip, 1 - c), me).wait_recv()
        for cp in first + passed:
            cp.wait_send()
        for cp in mine:
            cp.wait()

    vmem = pl.BlockSpec(memory_space=pltpu.VMEM)
    return pl.pallas_call(
        body, name=name, in_specs=[vmem] * n, out_specs=[vmem] * n,
        out_shape=[jax.ShapeDtypeStruct((8, *b.shape), b.dtype) for b in blocks],
        scratch_shapes=[pltpu.SemaphoreType.DMA((7 * n,)), pltpu.SemaphoreType.DMA((7 * n,)), pltpu.SemaphoreType.DMA((n,))],
        compiler_params=pltpu.CompilerParams(vmem_limit_bytes=VMEM_LIMIT),
    )(*blocks)


_FLIPS = [(0, 0, 1), (0, 1, 0), (0, 1, 1), (1, 0, 0), (1, 0, 1), (1, 1, 0), (1, 1, 1)]


_HBM = pl.BlockSpec(memory_space=pltpu.HBM)
_SEM = pl.BlockSpec(memory_space=pltpu.SEMAPHORE)
_EFFECT = pltpu.SideEffectType.DATAFLOW_SIDE_EFFECTING


def _peers(me):
    return [tuple(1 - me[ax] if f[ax] else me[ax] for ax in range(3)) for f in _FLIPS]


def _alltoall_copy(srcs, lands, send_sems, recv_sems, a, k, me, peers, own):
    dst = lands[a].at[_slot(me if own else peers[k])]
    src = srcs[a].at[0 if srcs[a].shape[0] == 1 else _slot(peers[k])]
    return pltpu.make_async_remote_copy(
        src_ref=src, dst_ref=dst, send_sem=send_sems.at[7 * a + k], recv_sem=recv_sems.at[7 * a + k],
        device_id=peers[k], device_id_type=MESH)


def _alltoall_start(arrays, lands, name):
    n = len(arrays)

    def body(*refs):
        srcs, lnds = refs[:n], refs[n:2 * n]
        send_sems, recv_sems = refs[2 * n:2 * n + 2]
        token = refs[-1]
        me = _place()
        peers = _peers(me)
        for a in range(n):
            for k in range(7):
                _alltoall_copy(srcs, lnds, send_sems, recv_sems, a, k, me, peers, True).start()
        token[...] = jnp.zeros_like(token)

    hbm = lambda v: pltpu.with_memory_space_constraint(v, pltpu.HBM)
    out = pl.pallas_call(
        body, name=name, in_specs=[_HBM] * (2 * n),
        out_specs=(_SEM, _SEM, *[_HBM] * (2 * n), pl.BlockSpec(memory_space=pltpu.VMEM)),
        out_shape=(pltpu.SemaphoreType.DMA((7 * n,)), pltpu.SemaphoreType.DMA((7 * n,)),
                   *[pltpu.HBM(v.shape, v.dtype) for v in (*arrays, *lands)], jax.ShapeDtypeStruct((8, LANES), F32)),
        input_output_aliases={i: 2 + i for i in range(2 * n)},
        compiler_params=pltpu.CompilerParams(has_side_effects=_EFFECT),
    )(*[hbm(v) for v in (*arrays, *lands)])
    return out[0], out[1], out[2:2 + 2 * n], out[-1]


def _alltoall_wait(send_sems, recv_sems, thru, after, name):
    n = len(thru) // 2

    def body(*refs):
        srcs, lnds = refs[:n], refs[n:2 * n]
        send_sems, recv_sems = refs[2 * n:2 * n + 2]
        me = _place()
        peers = _peers(me)
        for a in range(n):
            for k in range(7):
                _alltoall_copy(srcs, lnds, send_sems, recv_sems, a, k, me, peers, True).wait_send()
                _alltoall_copy(srcs, lnds, send_sems, recv_sems, a, k, me, peers, False).wait_recv()

    out = pl.pallas_call(
        body, name=name, in_specs=[*[_HBM] * (2 * n), _SEM, _SEM, ANY], out_specs=[_HBM] * (2 * n),
        out_shape=[pltpu.HBM(v.shape, v.dtype) for v in thru],
        input_output_aliases={i: i for i in range(2 * n)},
        compiler_params=pltpu.CompilerParams(has_side_effects=_EFFECT),
    )(*thru, send_sems, recv_sems, after)
    return out[n:]


def _sibling_exchange(arrays, name):
    n = len(arrays)

    def body(*refs):
        ins, outs = refs[:n], refs[n:2 * n]
        send_sems, recv_sems, local_sems = refs[2 * n:]
        x, y, c = _place()

        def copy(a, slot):
            return pltpu.make_async_remote_copy(
                src_ref=ins[a], dst_ref=outs[a].at[slot], send_sem=send_sems.at[a], recv_sem=recv_sems.at[a],
                device_id=(x, y, 1 - c), device_id_type=MESH)

        mine = [pltpu.make_async_copy(ins[a], outs[a].at[c], local_sems.at[a]) for a in range(n)]
        for cp in mine:
            cp.start()
        sent = [copy(a, c) for a in range(n)]
        for cp in sent:
            cp.start()
        for a in range(n):
            copy(a, 1 - c).wait_recv()
        for cp in sent:
            cp.wait_send()
        for cp in mine:
            cp.wait()

    vmem = pl.BlockSpec(memory_space=pltpu.VMEM)
    return pl.pallas_call(
        body, name=name, in_specs=[vmem] * n, out_specs=[vmem] * n,
        compiler_params=pltpu.CompilerParams(vmem_limit_bytes=VMEM_LIMIT),
        out_shape=[jax.ShapeDtypeStruct((2, *a.shape), a.dtype) for a in arrays],
        scratch_shapes=[pltpu.SemaphoreType.DMA((n,)), pltpu.SemaphoreType.DMA((n,)), pltpu.SemaphoreType.DMA((n,))],
    )(*arrays)


def _sum8(parts, name):
    _, rr, cc = parts.shape
    r = _pick(rr, (128, 64, 32, 16, 8))

    def body(p_ref, o_ref):
        acc = p_ref[0].astype(F32)
        for j in range(1, 8):
            acc = acc + p_ref[j].astype(F32)
        o_ref[...] = acc

    return pl.pallas_call(
        body, name=name, grid=(rr // r,), in_specs=[pl.BlockSpec((8, r, cc), lambda i: (0, i, 0))],
        out_specs=pl.BlockSpec((r, cc), lambda i: (i, 0)), out_shape=jax.ShapeDtypeStruct((rr, cc), F32),
        compiler_params=_cparams("parallel"),
    )(parts)


def _adamw(g, w, m, v, name):
    rr, cc = g.shape
    r = _pick(rr, (128, 64, 32, 16, 8))

    def body(g_ref, w_ref, m_ref, v_ref, d_ref, nm_ref, nv_ref):
        gv = g_ref[...]
        nm = ADAM_B1 * m_ref[...] + (1.0 - ADAM_B1) * gv
        nv = ADAM_B2 * v_ref[...] + (1.0 - ADAM_B2) * (gv * gv)
        m_hat = nm / (1.0 - ADAM_B1 ** ADAM_STEP)
        v_hat = nv / (1.0 - ADAM_B2 ** ADAM_STEP)
        d_ref[...] = -ADAM_LR * (m_hat / (jnp.sqrt(v_hat) + ADAM_EPS) + ADAM_WD * w_ref[...])
        nm_ref[...] = nm
        nv_ref[...] = nv

    blk = pl.BlockSpec((r, cc), lambda i: (i, 0))
    return pl.pallas_call(
        body, name=name, grid=(rr // r,), in_specs=[blk] * 4, out_specs=[blk] * 3,
        out_shape=[jax.ShapeDtypeStruct((rr, cc), F32)] * 3, compiler_params=_cparams("parallel"),
    )(g, w, m, v)


def _columns(sources, start, stop):
    out, at = [], 0
    for arr, off, width in sources:
        lo, hi = max(start, at), min(stop, at + width)
        if lo < hi:
            out.append(arr[:, off + lo - at:off + hi - at])
        at += width
    return out


def _flat_pack(parts, width):
    flat = jnp.concatenate([p.reshape(-1) for p in parts])
    total = -(-flat.shape[0] // (8 * width)) * (8 * width)
    return jnp.pad(flat, (0, total - flat.shape[0])).reshape(-1, width)


def _flat_unpack(pack, shapes):
    flat = pack.reshape(-1)
    out, at = [], 0
    for s in shapes:
        size = 1
        for e in s:
            size *= e
        out.append(flat[at:at + size].reshape(s))
        at += size
    return out


def kernel(x, meta, norm_w, w_in, conv_qkv_w, a_log, dt_bias, dn_norm_w, w_dn_out, dw_w, dw_b, ln_w, ln_b, w_cf_out, b_cf_out, w_o, final_norm_w, loss_target, m_meta, m_norm_w, m_w_in, m_conv_qkv_w, m_a_log, m_dt_bias, m_dn_norm_w, m_w_dn_out, m_dw_w, m_dw_b, m_ln_w, m_ln_b, m_w_cf_out, m_b_cf_out, m_w_o, m_final_norm_w, v_meta, v_norm_w, v_w_in, v_conv_qkv_w, v_a_log, v_dt_bias, v_dn_norm_w, v_w_dn_out, v_dw_w, v_dw_b, v_ln_w, v_ln_b, v_w_cf_out, v_b_cf_out, v_w_o, v_final_norm_w):
    d = x.shape[-1]
    heads = a_log.shape[-1]
    seq = x.shape[1]
    t = HDR + seq
    nc = t // CHUNK
    in_w = 9 * d + 2 * heads
    assert heads * DK == d and seq % LANES == 0 and w_in.shape[-1] * 4 == in_w
    xi, yi, ci = _place()
    shard = 2 * xi + yi
    ds = d // 4

    w_in_half = lax.dynamic_slice_in_dim(w_in[0].astype(MM_DTYPE), ci * (d // 2), d // 2, axis=0)
    w3_half = lax.dynamic_slice_in_dim(jnp.stack([w_dn_out[0], w_cf_out[0], w_o[0]]).astype(MM_DTYPE), ci * (d // 8), d // 8, axis=1)
    small = jnp.concatenate([
        jnp.pad(conv_qkv_w[0], ((0, 4), (0, 0))), jnp.pad(meta, ((0, 0), (0, 2 * ds))),
        jnp.pad(dw_w[0], ((0, 1), (0, 2 * ds)))], axis=0)
    g_in, g_small = _allgather([w_in_half, small], "gather_weights")
    my_slot = _slot((xi, yi, ci))
    w3_land = lax.dynamic_update_slice_in_dim(lax.empty((8, *w3_half.shape), w3_half.dtype), w3_half[None], my_slot, axis=0)
    w3_send, w3_recv, w3_in_flight, w3_token = _alltoall_start([w3_half[None]], [w3_land], "w3_start")
    sw = in_w // 4
    g_in = g_in.reshape(4, d, sw)
    w_cols = [(g_in[s], 0, sw) for s in range(4)]
    o_glu = 4 * d + 2 * heads
    w_main = jnp.concatenate(
        _columns(w_cols, 0, 4 * d) + _columns(w_cols, o_glu, o_glu + 2 * d) + _columns(w_cols, o_glu + 3 * d, o_glu + 5 * d)
        + _columns(w_cols, o_glu + 2 * d, o_glu + 3 * d), axis=1)
    w_ba = jnp.pad(jnp.concatenate(_columns(w_cols, 4 * d, o_glu), axis=1), ((0, 0), (0, LANES - 2 * heads)))
    small4 = g_small[0::2]
    conv_w_f = small4[:, 0:DN_TAPS, :].transpose(1, 0, 2).reshape(DN_TAPS, 3 * d)
    meta_f = small4[:, 8:8 + N_META, :ds].transpose(1, 0, 2).reshape(N_META, d)
    dw_w_f = small4[:, 24:56, :ds].transpose(1, 0, 2).reshape(32, d)

    xs = jnp.concatenate([jnp.pad(meta_f, ((PAD_ROWS, 0), (0, 0))), x[0]], axis=0)
    tgt = jnp.pad(loss_target[0], ((HDR, 0), (0, 0)))
    h = _rms_fwd(xs, norm_w + w3_token[0:1, 0:1], "rms_fwd")
    proj = _mm(h, w_main, name="proj_main")
    pba = _mm(h, w_ba, name="proj_ba")
    qkvn = _dn_prep(proj, conv_w_f, d, heads, "dn_prep")
    pvec = jnp.zeros((8, LANES), F32).at[0, heads:2 * heads].set(a_log[0]).at[1, heads:2 * heads].set(dt_bias[0])
    bg = _ba_fwd(pba, pvec, heads, "ba_fwd")
    g_row = bg[:, heads:2 * heads].reshape(nc, CHUNK, heads).transpose(0, 2, 1)
    o, sall, ainv = _dn_fwd(qkvn, bg, g_row, d, heads, "dn_fwd")
    g_w3, = _alltoall_wait(w3_send, w3_recv, w3_in_flight, o, "w3_wait")
    w3_full = g_w3.transpose(1, 0, 2, 3).reshape(3, d, d)
    w_dn_f, w_cf_f, w_o_f = w3_full[0], w3_full[1], w3_full[2]
    ya_in, ya = _dn_post(o, proj, dn_norm_w, w_dn_f, d, heads, "dn_post")
    c1, c3 = _cf_fwd(proj, dw_w_f, dw_b, ln_w, ln_b, d, "cf_fwd")
    yb = _mm(c3, w_cf_f, name="yb")
    dxo, dxo_mm, loss_acc, g_fnw, merged = _merge_final(ya, yb, b_cf_out, proj, w_o_f, xs, final_norm_w.reshape(1, d), tgt,
                                                        "merge_final")

    gw_o = _mm(merged, dxo_mm, ta=True, out_dtype=MM_DTYPE, name="gw_o")
    dya, dyb, dproj, g_bcf = _merge_bwd(dxo_mm, w_o_f, proj, ya, yb, b_cf_out, d, "merge_bwd")
    gw_cf = _mm(c3, dyb, ta=True, out_dtype=MM_DTYPE, name="gw_cf")
    gw_dn = _mm(ya_in, dya, ta=True, out_dtype=MM_DTYPE, name="gw_dn")
    own_home = lambda v: lax.dynamic_update_slice_in_dim(
        lax.empty(v.shape, v.dtype), lax.dynamic_slice_in_dim(v, my_slot, 1, axis=0), my_slot, axis=0)
    send_w3 = jnp.stack([gw_dn, gw_cf, gw_o]).reshape(3, 8, d // 8, d).transpose(1, 0, 2, 3).reshape(8, 3 * d // 8, d)
    gw3_send, gw3_recv, gw3_in_flight, _ = _alltoall_start([send_w3], [own_home(send_w3)], "exchange_w3_start")
    dc1, dproj, g_ln = _cf_bwd1(dyb, w_cf_f, c1, proj, ln_w, ln_b, dproj, d, "cf_bwd1")
    dproj, g_dww = _cf_bwd2(dc1, proj, dw_w_f, dproj, d, "cf_bwd2")
    do, dproj, g_dnw = _dn_post_bwd(dya, w_dn_f, o, proj, dn_norm_w, dproj, d, heads, "dn_post_bwd")
    dqkvn, dbg, dg_row = _dn_bwd(qkvn, bg, g_row, do, sall, ainv, d, heads, "dn_bwd")
    dproj, g_convw = _dn_prep_bwd(proj, conv_w_f, dqkvn, dproj, d, heads, "dn_prep_bwd")
    dbg = dbg + jnp.pad(dg_row.transpose(0, 2, 1).reshape(t, heads), ((0, 0), (heads, LANES - 2 * heads)))
    dpba, g_ba = _ba_bwd(pba, pvec, dbg, heads, "ba_bwd")
    gw_main = _mm(h, dproj, ta=True, out_dtype=MM_DTYPE, name="gw_main")
    gw_ba = _mm(h, dpba, ta=True, out_dtype=MM_DTYPE, name="gw_ba")

    g_cols = [(gw_main, 0, 4 * d), (gw_ba, 0, 2 * heads), (gw_main, 4 * d, 2 * d), (gw_main, 8 * d, d),
              (gw_main, 6 * d, 2 * d)]
    send_in = jnp.stack([jnp.concatenate(_columns(g_cols, s * sw, (s + 1) * sw), axis=1) for s in range(4)])
    send_in = send_in.reshape(8, d // 2, sw)
    send_sems, recv_sems, in_flight, token = _alltoall_start([send_in], [own_home(send_in)], "exchange_start")
    dxs, g_nw = _dh_rms_bwd(dproj, w_main, dpba, w_ba, xs, norm_w, dxo, token, "dh_rms_bwd")
    grad_x = dxs[HDR:][None]
    got_w3, = _alltoall_wait(gw3_send, gw3_recv, gw3_in_flight, dxs, "exchange_w3_wait")
    got_in, = _alltoall_wait(send_sems, recv_sems, in_flight, dxs, "exchange_wait")

    wide = lambda a: jnp.pad(a, ((0, 0), (0, d - a.shape[1])))
    small_g = jnp.concatenate([
        g_convw.reshape(24, d), dxs[PAD_ROWS:HDR], g_dww, g_nw, g_ln, g_bcf, g_fnw, wide(g_ba), wide(g_dnw), wide(loss_acc)],
        axis=0)
    red_in = _sum8(got_in, "sum_w_in")
    red_w3 = _sum8(got_w3, "sum_w3")
    pair_in, pair_w3 = _sibling_exchange([red_in, red_w3], "pair_grads")
    g_w_in = pair_in.reshape(d, in_w // 4)
    g_w3 = pair_w3.reshape(2, 3, d // 8, d).transpose(1, 0, 2, 3).reshape(3 * ds, d)
    all_small, = _allgather([small_g], "gather_small_grads")
    sg = _sum8(all_small, "sum_small")
    g_conv = lax.dynamic_slice_in_dim(sg[0:24].reshape(8, 3 * d)[0:DN_TAPS], shard * 3 * ds, 3 * ds, axis=1)
    g_meta = lax.dynamic_slice_in_dim(sg[24:40], shard * ds, ds, axis=1)
    g_dw_w = lax.dynamic_slice_in_dim(sg[40:40 + CF_TAPS], shard * ds, ds, axis=1)
    g_rep = {"norm_w": sg[72:73], "ln_w": sg[80:81], "ln_b": sg[81:82], "dw_b": sg[82:83], "b_cf_out": sg[88:89],
             "final_norm_w": sg[96], "a_log": sg[104:105, heads:2 * heads], "dt_bias": sg[105:106, heads:2 * heads],
             "dn_norm_w": sg[112:113, 0:DK]}
    loss = sg[120, 0]

    res = {}
    dl, nm, nv = _adamw(g_w_in, w_in[0], m_w_in[0], v_w_in[0], "adamw_w_in")
    res["w_in"] = (g_w_in[None], dl[None], nm[None], nv[None])
    stack3 = lambda a, b, c: jnp.concatenate([a[0], b[0], c[0]], axis=0)
    dl, nm, nv = _adamw(g_w3, stack3(w_dn_out, w_cf_out, w_o), stack3(m_w_dn_out, m_w_cf_out, m_w_o),
                        stack3(v_w_dn_out, v_w_cf_out, v_w_o), "adamw_w3")
    for j, nme in enumerate(("w_dn_out", "w_cf_out", "w_o")):
        res[nme] = tuple(a[j * ds:(j + 1) * ds][None] for a in (g_w3, dl, nm, nv))
    names = ["meta", "conv_qkv_w", "dw_w", "norm_w", "dw_b", "ln_w", "ln_b", "b_cf_out", "final_norm_w", "a_log", "dt_bias",
             "dn_norm_w"]
    grads = {"meta": g_meta, "conv_qkv_w": g_conv[None], "dw_w": g_dw_w[None], **g_rep}
    given = dict(meta=(meta, m_meta, v_meta), conv_qkv_w=(conv_qkv_w, m_conv_qkv_w, v_conv_qkv_w), dw_w=(dw_w, m_dw_w, v_dw_w),
                 norm_w=(norm_w, m_norm_w, v_norm_w), dw_b=(dw_b, m_dw_b, v_dw_b), ln_w=(ln_w, m_ln_w, v_ln_w),
                 ln_b=(ln_b, m_ln_b, v_ln_b), b_cf_out=(b_cf_out, m_b_cf_out, v_b_cf_out),
                 final_norm_w=(final_norm_w, m_final_norm_w, v_final_norm_w), a_log=(a_log, m_a_log, v_a_log),
                 dt_bias=(dt_bias, m_dt_bias, v_dt_bias), dn_norm_w=(dn_norm_w, m_dn_norm_w, v_dn_norm_w))
    shapes = [given[nme][0].shape for nme in names]
    packs = [_flat_pack([grads[nme] for nme in names], LANES)] + [_flat_pack([given[nme][j] for nme in names], LANES) for j in range(3)]
    outs = _adamw(*packs, "adamw_small")
    unpacked = [_flat_unpack(p, shapes) for p in (packs[0], *outs)]
    for j, nme in enumerate(names):
        res[nme] = tuple(u[j] for u in unpacked)

    order = ["meta", "norm_w", "w_in", "conv_qkv_w", "a_log", "dt_bias", "dn_norm_w", "w_dn_out", "dw_w", "dw_b", "ln_w", "ln_b",
             "w_cf_out", "b_cf_out", "w_o", "final_norm_w"]
    return (loss, grad_x, *[res[nme][0] for nme in order], *[res[nme][1] for nme in order],
            *[res[nme][2] for nme in order], *[res[nme][3] for nme in order])
```

```python
import jax
import jax.numpy as jnp
from jax import lax
from jax.experimental import pallas as pl
from jax.experimental.pallas import tpu as pltpu

F32 = jnp.float32
MM_DTYPE = jnp.bfloat16
EPS = 1e-6
N_META = 16
HDR = 128
PAD_ROWS = HDR - N_META
CHUNK = 64
DK = 128
CF_TAPS = 31
DN_TAPS = 4
HALO = 64
LANES = 128
ADAM_LR, ADAM_B1, ADAM_B2, ADAM_EPS, ADAM_WD, ADAM_STEP = 0.001, 0.9, 0.999, 1e-08, 0.01, 10
VMEM_LIMIT = 48 * 1024 * 1024
MESH = pl.DeviceIdType.MESH
HI = lax.Precision.HIGHEST
LO = lax.Precision.DEFAULT
NN = (((1,), (0,)), ((), ()))
NT = (((1,), (1,)), ((), ()))
TN = (((0,), (0,)), ((), ()))
ANY = pl.BlockSpec(memory_space=pl.ANY)


def _dot(a, b, dims=NN, prec=LO):
    if prec == LO:
        a, b = a.astype(MM_DTYPE), b.astype(MM_DTYPE)
    return lax.dot_general(a, b, dims, precision=prec, preferred_element_type=F32)


def _split(a):
    hi = a.astype(MM_DTYPE)
    return hi, (a - hi.astype(F32)).astype(MM_DTYPE)


def _dot3(a, b, dims=NN):
    ah, al = _split(a)
    bh, bl = _split(b)
    return _dot(ah, bh, dims) + (_dot(ah, bl, dims) + _dot(al, bh, dims))


def _pick(n, options):
    for o in options:
        if n % o == 0:
            return o
    return n


def _cparams(*sem):
    return pltpu.CompilerParams(dimension_semantics=sem, vmem_limit_bytes=VMEM_LIMIT)


def _sig(x):
    return jax.nn.sigmoid(x)


def _sum_all(a):
    return jnp.sum(jnp.sum(a, axis=-1, keepdims=True), axis=0, keepdims=True)


def _dsilu(x, s):
    return s + x * s * (1.0 - s)


def _shift(win, s):
    n = win.shape[0]
    s = s % n
    return win if s == 0 else pltpu.roll(win, s, 0)


def _row_tile(t):
    return _pick(t, (320, 128))


def _mm(a, b, *, nt=False, ta=False, out_dtype=F32, name):
    k, m = a.shape if ta else a.shape[::-1]
    n = b.shape[0] if nt else b.shape[1]
    tm = _pick(m, (1664, 1024, 640, 512, 384, 256, 128))
    tn = _pick(n, (1024, 768, 512, 256, 128))
    tk = _pick(k, (1664, 1024, 640, 512, 256, 128))
    nk = k // tk
    dims = NT if nt else (TN if ta else NN)

    def body(a_ref, b_ref, o_ref, *acc):
        p = lax.dot_general(a_ref[...], b_ref[...], dims, preferred_element_type=F32)
        if nk == 1:
            o_ref[...] = p.astype(o_ref.dtype)
            return
        acc_ref, = acc
        kk = pl.program_id(2)

        @pl.when(kk == 0)
        def _():
            acc_ref[...] = p

        @pl.when(kk > 0)
        def _():
            acc_ref[...] += p

        @pl.when(kk == nk - 1)
        def _():
            o_ref[...] = acc_ref[...].astype(o_ref.dtype)

    a_spec = pl.BlockSpec((tk, tm), lambda i, j, kk: (kk, i)) if ta else pl.BlockSpec((tm, tk), lambda i, j, kk: (i, kk))
    b_spec = pl.BlockSpec((tn, tk), lambda i, j, kk: (j, kk)) if nt else pl.BlockSpec((tk, tn), lambda i, j, kk: (kk, j))
    return pl.pallas_call(
        body, name=name, grid=(m // tm, n // tn, nk),
        in_specs=[a_spec, b_spec],
        out_specs=pl.BlockSpec((tm, tn), lambda i, j, kk: (i, j)),
        out_shape=jax.ShapeDtypeStruct((m, n), out_dtype),
        scratch_shapes=[] if nk == 1 else [pltpu.VMEM((tm, tn), F32)],
        compiler_params=_cparams("parallel", "parallel", "arbitrary"),
    )(a, b)


def _rms_fwd(xs, w, name):
    t, d = xs.shape
    r = _row_tile(t)

    def body(x_ref, w_ref, h_ref):
        x = x_ref[...]
        rs = lax.rsqrt(jnp.mean(x * x, axis=-1, keepdims=True) + EPS)
        h_ref[...] = (x * rs * w_ref[...]).astype(h_ref.dtype)

    return pl.pallas_call(
        body, name=name, grid=(t // r,),
        in_specs=[pl.BlockSpec((r, d), lambda i: (i, 0)), pl.BlockSpec((1, d), lambda i: (0, 0))],
        out_specs=pl.BlockSpec((r, d), lambda i: (i, 0)),
        out_shape=jax.ShapeDtypeStruct((t, d), MM_DTYPE), compiler_params=_cparams("parallel"),
    )(xs, w)


def _mm_rows(a, b, *, nt, tm, tk, extras, outs, epilogue, name, prologue=None, aliases=None):
    m, k = a.shape
    n = b.shape[0] if nt else b.shape[1]
    assert m % tm == 0 and k % tk == 0
    nk = k // tk
    ne = len(extras)

    def body(a_ref, b_ref, *rest):
        ex, ou = rest[:ne], rest[ne:ne + len(outs)]
        i, kk = pl.program_id(0), pl.program_id(1)
        lhs = a_ref[...] if prologue is None else prologue(a_ref, ex, ou)
        p = lax.dot_general(lhs, b_ref[...], NT if nt else NN, preferred_element_type=F32)
        if nk == 1:
            epilogue(p, i, ex, ou)
            return
        acc_ref = rest[-1]

        @pl.when(kk == 0)
        def _():
            acc_ref[...] = p

        @pl.when(jnp.logical_and(kk > 0, kk < nk - 1))
        def _():
            acc_ref[...] += p

        @pl.when(kk == nk - 1)
        def _():
            epilogue(acc_ref[...] + p, i, ex, ou)

    b_spec = pl.BlockSpec((n, tk), lambda i, kk: (0, kk)) if nt else pl.BlockSpec((tk, n), lambda i, kk: (kk, 0))
    return pl.pallas_call(
        body, name=name, grid=(m // tm, nk),
        in_specs=[pl.BlockSpec((tm, tk), lambda i, kk: (i, kk)), b_spec] + [s for _, s in extras],
        out_specs=[s for _, s in outs], out_shape=[s for s, _ in outs],
        scratch_shapes=[] if nk == 1 else [pltpu.VMEM((tm, n), F32)],
        input_output_aliases={2 + e: o for e, o in (aliases or {}).items()},
        compiler_params=_cparams("arbitrary", "arbitrary"),
    )(a, b, *[e for e, _ in extras])


def _dh_rms_bwd(dproj, w_main, dpba, w_ba, xs, w, dres, after, name):
    t, d = xs.shape
    tm = _pick(t, (640, 128))
    tk = _pick(w_main.shape[1], (2304, 768))

    def epilogue(dh, i, ex, ou):
        dpba_ref, wba_ref, x_ref, w_ref, dr_ref, _ = ex
        dx_ref, gw_ref = ou

        @pl.when(i == 0)
        def _():
            gw_ref[...] = jnp.zeros_like(gw_ref)

        dh = dh + lax.dot_general(dpba_ref[...], wba_ref[...], NT, preferred_element_type=F32)
        x = x_ref[...]
        rs = lax.rsqrt(jnp.mean(x * x, axis=-1, keepdims=True) + EPS)
        xh = x * rs
        dxh = dh * w_ref[...]
        dx_ref[...] = rs * (dxh - xh * jnp.mean(dxh * xh, axis=-1, keepdims=True)) + dr_ref[...]
        gw_ref[0:1, :] += jnp.sum(dh * xh, axis=0, keepdims=True)

    row = pl.BlockSpec((tm, d), lambda i, kk: (i, 0))
    const = lambda shape: pl.BlockSpec(shape, lambda i, kk: (0, 0))
    return _mm_rows(
        dproj, w_main, nt=True, tm=tm, tk=tk,
        extras=[(dpba, pl.BlockSpec((tm, LANES), lambda i, kk: (i, 0))), (w_ba, const(w_ba.shape)), (xs, row), (w, const((1, d))),
                (dres, row), (after, ANY)],
        outs=[(jax.ShapeDtypeStruct((t, d), F32), row), (jax.ShapeDtypeStruct((8, d), F32), const((8, d)))],
        epilogue=epilogue, name=name)


def _merge_final(ya, yb, b, proj, w_o, xs, w, tgt, name):
    t, d = xs.shape
    r = _row_tile(t)
    pieces = r // HALO

    def prologue(ya_ref, ex, ou):
        yb_ref, b_ref, g_ref = ex[2:5]
        m_ref = ou[4]
        m_ref[...] = (_sig(g_ref[:, :d]) * ya_ref[...] + _sig(g_ref[:, d:]) * (yb_ref[...] + b_ref[...])).astype(m_ref.dtype)
        return m_ref[...]

    def epilogue(mo, i, ex, ou):
        x_ref, w_ref = ex[:2]
        target = jnp.concatenate([t_ref[...] for t_ref in ex[5:]], axis=0)
        dx_ref, dxm_ref, loss_ref, gw_ref = ou[:4]

        @pl.when(i == 0)
        def _():
            loss_ref[...] = jnp.zeros_like(loss_ref)
            gw_ref[...] = jnp.zeros_like(gw_ref)

        xo = x_ref[...] + mo
        rs = lax.rsqrt(jnp.mean(xo * xo, axis=-1, keepdims=True) + EPS)
        xh = xo * rs
        y = xh * w_ref[...]
        rows = i * r + lax.broadcasted_iota(jnp.int32, (r, 1), 0)
        err = jnp.where(rows >= HDR, y - target, 0.0)
        loss_ref[...] += 0.5 * _sum_all(err * err) / d
        dy = err / d
        gw_ref[0:1, :] += jnp.sum(dy * xh, axis=0, keepdims=True)
        dxh = dy * w_ref[...]
        dx = rs * (dxh - xh * jnp.mean(dxh * xh, axis=-1, keepdims=True))
        dx_ref[...] = dx
        dxm_ref[...] = dx.astype(dxm_ref.dtype)

    row = pl.BlockSpec((r, d), lambda i, kk: (i, 0))
    const = lambda shape: pl.BlockSpec(shape, lambda i, kk: (0, 0))
    return _mm_rows(
        ya, w_o, nt=False, tm=r, tk=d,
        extras=[(xs, row), (w, const((1, d))), (yb, row), (b, const((1, d))),
                (proj, pl.BlockSpec((r, 2 * d), lambda i, kk: (i, 3)))]
        + [(tgt, pl.BlockSpec((HALO, d), lambda i, kk, j=j: (jnp.maximum(i * pieces - HDR // HALO + j, 0), 0)))
           for j in range(pieces)],
        outs=[(jax.ShapeDtypeStruct((t, d), F32), row), (jax.ShapeDtypeStruct((t, d), MM_DTYPE), row),
              (jax.ShapeDtypeStruct((8, LANES), F32), const((8, LANES))), (jax.ShapeDtypeStruct((8, d), F32), const((8, d))),
              (jax.ShapeDtypeStruct((t, d), MM_DTYPE), row)],
        prologue=prologue, epilogue=epilogue, name=name)


def _halo_prev(r):
    return lambda i: (jnp.maximum(i * (r // HALO) - 1, 0), 0)


def _halo_next(r, t):
    return lambda i: (jnp.minimum((i + 1) * (r // HALO), t // HALO - 1), 0)


def _conv_rows(t):
    return _pick(t, (320, 128))


def _qkv_window(p_ref, ph_ref, i, cs):
    return jnp.concatenate([jnp.where(i > 0, ph_ref[HALO - 8:HALO, cs], 0.0), p_ref[:, cs]], axis=0)


def _qkv_conv(win, cw, r):
    y = None
    for s in range(DN_TAPS):
        term = _shift(win, s)[8:8 + r, :] * cw[DN_TAPS - 1 - s:DN_TAPS - s, :]
        y = term if y is None else y + term
    return y


def _dn_prep(proj, conv_w, d, heads, name):
    t = proj.shape[0]
    r = _conv_rows(t)
    w3 = 3 * d

    def body(p_ref, ph_ref, cw_ref, o_ref):
        i = pl.program_id(0)
        for cb in range(w3 // DK):
            cs = slice(cb * DK, (cb + 1) * DK)
            y = _qkv_conv(_qkv_window(p_ref, ph_ref, i, cs), cw_ref[:, cs], r)
            a = y * _sig(y)
            if cb < 2 * heads:
                sc = DK ** -0.5 if cb < heads else 1.0
                a = a * (lax.rsqrt(jnp.sum(a * a, axis=-1, keepdims=True) + EPS) * sc)
            o_ref[:, cs] = a

    return pl.pallas_call(
        body, name=name, grid=(t // r,),
        in_specs=[pl.BlockSpec((r, w3), lambda i: (i, 0)), pl.BlockSpec((HALO, w3), _halo_prev(r)),
                  pl.BlockSpec((DN_TAPS, w3), lambda i: (0, 0))],
        out_specs=pl.BlockSpec((r, w3), lambda i: (i, 0)),
        out_shape=jax.ShapeDtypeStruct((t, w3), F32), compiler_params=_cparams("parallel"),
    )(proj, proj, conv_w)


def _dn_prep_bwd(proj, conv_w, dqkvn, dproj, d, heads, name):
    t = proj.shape[0]
    r = _conv_rows(t)
    w3 = 3 * d
    last = t // r - 1

    def body(p_ref, ph_ref, pn_ref, cw_ref, dn_ref, dnn_ref, _, dp_ref, gw_ref):
        i = pl.program_id(0)

        @pl.when(i == 0)
        def _():
            gw_ref[...] = jnp.zeros_like(gw_ref)

        for cb in range(w3 // DK):
            cs = slice(cb * DK, (cb + 1) * DK)
            cw = cw_ref[:, cs]
            win = jnp.concatenate([_qkv_window(p_ref, ph_ref, i, cs), jnp.where(i < last, pn_ref[0:8, cs], 0.0)], axis=0)
            y = _qkv_conv(win, cw, r + 8)
            sy = _sig(y)
            da = jnp.concatenate([dn_ref[:, cs], jnp.where(i < last, dnn_ref[0:8, cs], 0.0)], axis=0)
            if cb < 2 * heads:
                a = y * sy
                rn = lax.rsqrt(jnp.sum(a * a, axis=-1, keepdims=True) + EPS)
                n0 = a * rn
                dn0 = da * (DK ** -0.5 if cb < heads else 1.0)
                da = rn * (dn0 - n0 * jnp.sum(dn0 * n0, axis=-1, keepdims=True))
            dconv = da * _dsilu(y, sy)
            acc = None
            for j in range(DN_TAPS):
                s = DN_TAPS - 1 - j
                term = _shift(dconv, -s)[0:r, :] * cw[j:j + 1, :]
                acc = term if acc is None else acc + term
                gw_ref[j:j + 1, cs] += jnp.sum(dconv[0:r, :] * _shift(win, s)[8:8 + r, :], axis=0, keepdims=True)
            dp_ref[:, cs] = acc.astype(dp_ref.dtype)

    row = pl.BlockSpec((r, w3), lambda i: (i, 0))
    nxt = pl.BlockSpec((HALO, w3), _halo_next(r, t))
    return pl.pallas_call(
        body, name=name, grid=(t // r,),
        in_specs=[row, pl.BlockSpec((HALO, w3), _halo_prev(r)), nxt, pl.BlockSpec((DN_TAPS, w3), lambda i: (0, 0)), row, nxt, ANY],
        out_specs=[row, pl.BlockSpec((8, w3), lambda i: (0, 0))],
        out_shape=[jax.ShapeDtypeStruct(dproj.shape, dproj.dtype), jax.ShapeDtypeStruct((8, w3), F32)],
        input_output_aliases={6: 0}, compiler_params=_cparams("arbitrary"),
    )(proj, proj, proj, conv_w, dqkvn, dqkvn, dproj)


def _ba_terms(x, pv, heads):
    lane = lax.broadcasted_iota(jnp.int32, x.shape, 1)
    is_b = lane < heads
    is_a = jnp.logical_and(lane >= heads, lane < 2 * heads)
    beta = _sig(x)
    z = x + pv[1:2, :]
    nexp = -jnp.exp(pv[0:1, :])
    sp = jnp.maximum(z, 0.0) + jnp.log1p(jnp.exp(-jnp.abs(z)))
    return is_b, is_a, beta, z, nexp, nexp * sp


def _ba_fwd(pba, pvec, heads, name):
    t = pba.shape[0]
    r = _row_tile(t)

    def body(x_ref, pv_ref, o_ref):
        is_b, is_a, beta, _, _, g = _ba_terms(x_ref[...], pv_ref[...], heads)
        rows = pl.program_id(0) * r + lax.broadcasted_iota(jnp.int32, (r, 1), 0)
        o_ref[...] = jnp.where(rows >= PAD_ROWS, jnp.where(is_b, beta, jnp.where(is_a, g, 0.0)), 0.0)

    row = pl.BlockSpec((r, LANES), lambda i: (i, 0))
    return pl.pallas_call(
        body, name=name, grid=(t // r,), in_specs=[row, pl.BlockSpec((8, LANES), lambda i: (0, 0))],
        out_specs=row, out_shape=jax.ShapeDtypeStruct((t, LANES), F32), compiler_params=_cparams("parallel"),
    )(pba, pvec)


def _ba_bwd(pba, pvec, dbg, heads, name):
    t = pba.shape[0]
    r = _row_tile(t)

    def body(x_ref, pv_ref, d_ref, o_ref, g_ref):
        @pl.when(pl.program_id(0) == 0)
        def _():
            g_ref[...] = jnp.zeros_like(g_ref)

        is_b, is_a, beta, z, nexp, g = _ba_terms(x_ref[...], pv_ref[...], heads)
        rows = pl.program_id(0) * r + lax.broadcasted_iota(jnp.int32, (r, 1), 0)
        dd = jnp.where(rows >= PAD_ROWS, d_ref[...], 0.0)
        dz = dd * nexp * _sig(z)
        o_ref[...] = jnp.where(is_b, dd * beta * (1.0 - beta), jnp.where(is_a, dz, 0.0)).astype(o_ref.dtype)
        g_ref[0:1, :] += jnp.sum(jnp.where(is_a, dd * g, 0.0), axis=0, keepdims=True)
        g_ref[1:2, :] += jnp.sum(jnp.where(is_a, dz, 0.0), axis=0, keepdims=True)

    row = pl.BlockSpec((r, LANES), lambda i: (i, 0))
    par = pl.BlockSpec((8, LANES), lambda i: (0, 0))
    return pl.pallas_call(
        body, name=name, grid=(t // r,), in_specs=[row, par, row], out_specs=[row, par],
        out_shape=[jax.ShapeDtypeStruct((t, LANES), MM_DTYPE), jax.ShapeDtypeStruct((8, LANES), F32)],
        compiler_params=_cparams("arbitrary"),
    )(pba, pvec, dbg)


def _chunk_consts():
    ri = lax.broadcasted_iota(jnp.int32, (CHUNK, CHUNK), 0)
    ci = lax.broadcasted_iota(jnp.int32, (CHUNK, CHUNK), 1)
    return ri >= ci, ri > ci, (ri == ci).astype(F32), (ri >= ci).astype(F32), (ri <= ci).astype(F32)


def _chunk_decay(gc, gr, incl):
    return jnp.where(incl, jnp.exp(jnp.where(incl, gc - gr, 0.0)), 0.0)


def _unit_lower_inverse(n, eye):
    x = [eye + a for a in n]
    p = list(n)
    for _ in range(5):
        p = [_dot(a, a) for a in p]
        x = [a + _dot(a, b) for a, b in zip(x, p)]
    r = [eye - a + _dot3(b, a) for a, b in zip(x, n)]
    return [a + _dot(a, b) for a, b in zip(x, r)]


def _dn_fwd(qkvn, bg, g_row, d, heads, name):
    t = qkvn.shape[0]
    nc = t // CHUNK

    def body(qkv_ref, bg_ref, gr_ref, o_ref, sall_ref, ainv_ref, s_ref):
        @pl.when(pl.program_id(0) == 0)
        def _():
            s_ref[...] = jnp.zeros_like(s_ref)

        incl, strict, eye, tril, triu = _chunk_consts()
        bgv = bg_ref[...]
        gc_all = _dot(tril, bgv, NN, HI)
        gr_all = _dot(gr_ref[0], triu, NN, HI)
        hs = range(heads)
        q = [qkv_ref[:, h * DK:(h + 1) * DK] for h in hs]
        k = [qkv_ref[:, d + h * DK:d + (h + 1) * DK] for h in hs]
        v = [qkv_ref[:, 2 * d + h * DK:2 * d + (h + 1) * DK] for h in hs]
        kb = [a.astype(MM_DTYPE) for a in k]
        beta = [bgv[:, h:h + 1] for h in hs]
        gc = [gc_all[:, heads + h:heads + h + 1] for h in hs]
        decay = [_chunk_decay(gc[h], gr_all[h:h + 1, :], incl) for h in hs]
        eg = [jnp.exp(a) for a in gc]
        n = [jnp.where(strict, -(beta[h] * _dot(kb[h], kb[h], NT) * decay[h]), 0.0) for h in hs]
        x = _unit_lower_inverse(n, eye)
        sol = [_dot3(x[h], jnp.concatenate([v[h] * beta[h], k[h] * (beta[h] * eg[h])], axis=1)) for h in hs]
        attn = [_dot(q[h], kb[h], NT) * decay[h] for h in hs]
        s = [s_ref[h] for h in hs]
        sb = [a.astype(MM_DTYPE) for a in s]
        wv = [sol[h][:, :DK] - _dot(sol[h][:, DK:], sb[h]) for h in hs]
        o = [_dot(q[h] * eg[h], sb[h]) + _dot(attn[h], wv[h]) for h in hs]
        glast = [a[CHUNK - 1:CHUNK, :] for a in gc]
        s_new = [s[h] * jnp.exp(glast[h]) + _dot(k[h] * jnp.exp(glast[h] - gc[h]), wv[h], TN) for h in hs]
        for h in hs:
            o_ref[:, h * DK:(h + 1) * DK] = o[h]
            sall_ref[0, h] = s[h]
            ainv_ref[0, h] = x[h]
            s_ref[h] = s_new[h]

    return pl.pallas_call(
        body, name=name, grid=(nc,),
        in_specs=[pl.BlockSpec((CHUNK, 3 * d), lambda i: (i, 0)), pl.BlockSpec((CHUNK, LANES), lambda i: (i, 0)),
                  pl.BlockSpec((1, heads, CHUNK), lambda i: (i, 0, 0))],
        out_specs=[pl.BlockSpec((CHUNK, d), lambda i: (i, 0)), pl.BlockSpec((1, heads, DK, DK), lambda i: (i, 0, 0, 0)),
                   pl.BlockSpec((1, heads, CHUNK, CHUNK), lambda i: (i, 0, 0, 0))],
        out_shape=[jax.ShapeDtypeStruct((t, d), F32), jax.ShapeDtypeStruct((nc, heads, DK, DK), F32),
                   jax.ShapeDtypeStruct((nc, heads, CHUNK, CHUNK), F32)],
        scratch_shapes=[pltpu.VMEM((heads, DK, DK), F32)], compiler_params=_cparams("arbitrary"),
    )(qkvn, bg, g_row)


def _dn_bwd(qkvn, bg, g_row, do, sall, ainv, d, heads, name):
    t = qkvn.shape[0]
    nc = t // CHUNK
    rev = lambda i: nc - 1 - i

    def body(qkv_ref, bg_ref, gr_ref, do_ref, sall_ref, ainv_ref, dqkv_ref, dbg_ref, dgr_ref, ds_ref):
        @pl.when(pl.program_id(0) == 0)
        def _():
            ds_ref[...] = jnp.zeros_like(ds_ref)

        incl, strict, eye, tril, triu = _chunk_consts()
        bgv = bg_ref[...]
        gc_all = _dot(tril, bgv, NN, HI)
        gr_all = _dot(gr_ref[0], triu, NN, HI)
        lane = lax.broadcasted_iota(jnp.int32, (CHUNK, LANES), 1)
        hrow = lax.broadcasted_iota(jnp.int32, (heads, CHUNK), 0)
        last_row = lax.broadcasted_iota(jnp.int32, (CHUNK, 1), 0) == CHUNK - 1
        dbeta_slab = jnp.zeros((CHUNK, LANES), F32)
        dgc_slab = jnp.zeros((CHUNK, LANES), F32)
        dgr_slab = jnp.zeros((heads, CHUNK), F32)
        hs = range(heads)
        cols = lambda ref, off: [ref[:, off + h * DK:off + (h + 1) * DK] for h in hs]
        qs, ks, vs, douts = cols(qkv_ref, 0), cols(qkv_ref, d), cols(qkv_ref, 2 * d), cols(do_ref, 0)
        ss, xs_, dsns = [sall_ref[0, h] for h in hs], [ainv_ref[0, h] for h in hs], [ds_ref[h] for h in hs]

        def recompute(h, z):
            z.q, z.k, z.v, z.dout, z.s, z.x, z.dsn = qs[h], ks[h], vs[h], douts[h], ss[h], xs_[h], dsns[h]
            z.kb, z.qb, z.sb, z.dsnb, z.doutb = (a.astype(MM_DTYPE) for a in (z.k, z.q, z.s, z.dsn, z.dout))
            z.beta = bgv[:, h:h + 1]
            gc = gc_all[:, heads + h:heads + h + 1]
            z.decay = _chunk_decay(gc, gr_all[h:h + 1, :], incl)
            z.eg = jnp.exp(gc)
            glast = gc[CHUNK - 1:CHUNK, :]
            z.eglast = jnp.exp(glast)
            z.ek = jnp.exp(glast - gc)
            z.kk = _dot(z.kb, z.kb, NT)
            z.qk = _dot(z.qb, z.kb, NT)
            sol = _dot3(z.x, jnp.concatenate([z.v * z.beta, z.k * (z.beta * z.eg)], axis=1))
            z.u, z.wb = sol[:, :DK], sol[:, DK:].astype(MM_DTYPE)
            z.attn = z.qk * z.decay
            z.qg = z.q * z.eg
            z.kend = z.k * z.ek

        def pseudo_values(h, z):
            z.wvb = (z.u - _dot(z.wb, z.sb)).astype(MM_DTYPE)

        def scan_step(h, z):
            z.d_wv = _dot(z.attn, z.doutb, TN) + _dot(z.kend, z.dsnb)
            z.d_attn = _dot(z.doutb, z.wvb, NT)
            z.d_qg = _dot(z.doutb, z.sb, NT)
            z.d_kend = _dot(z.wvb, z.dsnb, NT)
            z.ds_new = _dot(z.qg, z.doutb, TN) + z.eglast * z.dsn
            z.d_glast = z.eglast * _sum_all(z.dsn * z.s) + _sum_all(z.d_kend * z.kend)

        def state_terms(h, z):
            z.ds_new = z.ds_new - _dot(z.wb, z.d_wv, TN)
            z.d_w = -_dot(z.d_wv, z.sb, NT)

        def solve_transpose(h, z):
            d_rhs = _dot3(z.x, jnp.concatenate([z.d_wv, z.d_w], axis=1), TN)
            z.d_ru, z.d_rw = d_rhs[:, :DK], d_rhs[:, DK:]

        def lower_terms(h, z):
            z.d_low = jnp.where(strict, -(_dot(z.d_ru, z.u, NT) + _dot(z.d_rw, z.wb, NT)), 0.0)

        def outputs(h, z):
            rw_k = jnp.sum(z.d_rw * z.k, axis=-1, keepdims=True)
            z.dbeta = (jnp.sum(z.d_ru * z.v, axis=-1, keepdims=True) + rw_k * z.eg
                       + jnp.sum(z.d_low * z.kk * z.decay, axis=-1, keepdims=True))
            z.dv = z.d_ru * z.beta
            d_kk = (z.d_low * z.beta * z.decay).astype(MM_DTYPE)
            d_qk = (z.d_attn * z.decay).astype(MM_DTYPE)
            z.dk = (z.d_rw * (z.beta * z.eg) + _dot(d_kk, z.kb) + _dot(d_kk, z.kb, TN) + _dot(d_qk, z.qb, TN)
                    + z.d_kend * z.ek)
            z.dq = _dot(d_qk, z.kb) + z.d_qg * z.eg
            e = (z.d_low * z.beta * z.kk + z.d_attn * z.qk) * z.decay
            z.dgc = (rw_k * z.beta * z.eg + jnp.sum(e, axis=-1, keepdims=True)
                     + jnp.sum(z.d_qg * z.qg, axis=-1, keepdims=True) - jnp.sum(z.d_kend * z.kend, axis=-1, keepdims=True)
                     + jnp.where(last_row, z.d_glast, 0.0))
            z.dgr = -jnp.sum(e, axis=0, keepdims=True)

        class _Head:
            pass

        st = [_Head() for _ in hs]
        for phase in (recompute, pseudo_values, scan_step, state_terms, solve_transpose, lower_terms, outputs):
            for h in hs:
                phase(h, st[h])
        res = [(z.dq, z.dk, z.dv, z.ds_new, z.dbeta, z.dgc, z.dgr) for z in st]
        for h, (dq, dk, dv, ds_new, dbeta, dgc, dgr) in enumerate(res):
            dqkv_ref[:, h * DK:(h + 1) * DK] = dq
            dqkv_ref[:, d + h * DK:d + (h + 1) * DK] = dk
            dqkv_ref[:, 2 * d + h * DK:2 * d + (h + 1) * DK] = dv
            ds_ref[h] = ds_new
            dbeta_slab = jnp.where(lane == h, dbeta, dbeta_slab)
            dgc_slab = jnp.where(lane == heads + h, dgc, dgc_slab)
            dgr_slab = jnp.where(hrow == h, dgr, dgr_slab)
        dbg_ref[...] = dbeta_slab + _dot(triu, dgc_slab, NN, HI)
        dgr_ref[0] = _dot(dgr_slab, tril, NN, HI)

    return pl.pallas_call(
        body, name=name, grid=(nc,),
        in_specs=[pl.BlockSpec((CHUNK, 3 * d), lambda i: (rev(i), 0)), pl.BlockSpec((CHUNK, LANES), lambda i: (rev(i), 0)),
                  pl.BlockSpec((1, heads, CHUNK), lambda i: (rev(i), 0, 0)), pl.BlockSpec((CHUNK, d), lambda i: (rev(i), 0)),
                  pl.BlockSpec((1, heads, DK, DK), lambda i: (rev(i), 0, 0, 0)),
                  pl.BlockSpec((1, heads, CHUNK, CHUNK), lambda i: (rev(i), 0, 0, 0))],
        out_specs=[pl.BlockSpec((CHUNK, 3 * d), lambda i: (rev(i), 0)), pl.BlockSpec((CHUNK, LANES), lambda i: (rev(i), 0)),
                   pl.BlockSpec((1, heads, CHUNK), lambda i: (rev(i), 0, 0))],
        out_shape=[jax.ShapeDtypeStruct((t, 3 * d), F32), jax.ShapeDtypeStruct((t, LANES), F32),
                   jax.ShapeDtypeStruct((nc, heads, CHUNK), F32)],
        scratch_shapes=[pltpu.VMEM((heads, DK, DK), F32)], compiler_params=_cparams("arbitrary"),
    )(qkvn, bg, g_row, do, sall, ainv)


def _dn_post(o, proj, w, w_dn, d, heads, name):
    t = o.shape[0]
    r = _row_tile(t)

    def prologue(o_ref, ex, ou):
        z_ref, w_ref = ex
        y_ref = ou[0]
        for h in range(heads):
            sl = slice(h * DK, (h + 1) * DK)
            oh, z = o_ref[:, sl], z_ref[:, sl]
            rs = lax.rsqrt(jnp.mean(oh * oh, axis=-1, keepdims=True) + EPS)
            y_ref[:, sl] = (oh * rs * w_ref[...] * (z * _sig(z))).astype(y_ref.dtype)
        return y_ref[...]

    def epilogue(p, i, ex, ou):
        ou[1][...] = p

    row = pl.BlockSpec((r, d), lambda i, kk: (i, 0))
    return _mm_rows(
        o, w_dn, nt=False, tm=r, tk=d,
        extras=[(proj, pl.BlockSpec((r, d), lambda i, kk: (i, 3))), (w, pl.BlockSpec((1, DK), lambda i, kk: (0, 0)))],
        outs=[(jax.ShapeDtypeStruct((t, d), MM_DTYPE), row), (jax.ShapeDtypeStruct((t, d), F32), row)],
        prologue=prologue, epilogue=epilogue, name=name)


def _dn_post_bwd(dya, w_dn, o, proj, w, dproj, d, heads, name):
    t = o.shape[0]
    r = _row_tile(t)

    def epilogue(dy, i, ex, ou):
        o_ref, z_ref, w_ref, _ = ex
        do_ref, dz_ref, gw_ref = ou

        @pl.when(i == 0)
        def _():
            gw_ref[...] = jnp.zeros_like(gw_ref)

        gw = jnp.zeros((1, DK), F32)
        for h in range(heads):
            sl = slice(h * DK, (h + 1) * DK)
            oh, z, dyh = o_ref[:, sl], z_ref[:, sl], dy[:, sl]
            sz = _sig(z)
            rs = lax.rsqrt(jnp.mean(oh * oh, axis=-1, keepdims=True) + EPS)
            xh = oh * rs
            dn = dyh * (z * sz)
            dz_ref[:, sl] = (dyh * (xh * w_ref[...]) * _dsilu(z, sz)).astype(dz_ref.dtype)
            dxh = dn * w_ref[...]
            do_ref[:, sl] = rs * (dxh - xh * jnp.mean(dxh * xh, axis=-1, keepdims=True))
            gw = gw + jnp.sum(dn * xh, axis=0, keepdims=True)
        gw_ref[0:1, :] += gw

    row = pl.BlockSpec((r, d), lambda i, kk: (i, 0))
    za = pl.BlockSpec((r, d), lambda i, kk: (i, 3))
    return _mm_rows(
        dya, w_dn, nt=True, tm=r, tk=d,
        extras=[(o, row), (proj, za), (w, pl.BlockSpec((1, DK), lambda i, kk: (0, 0))), (dproj, ANY)],
        outs=[(jax.ShapeDtypeStruct((t, d), F32), row), (jax.ShapeDtypeStruct(dproj.shape, dproj.dtype), za),
              (jax.ShapeDtypeStruct((8, DK), F32), pl.BlockSpec((8, DK), lambda i, kk: (0, 0)))],
        aliases={3: 1}, epilogue=epilogue, name=name)


def _glu_window(g_ref, gh_ref, i, d, cs):
    cs2 = slice(d + cs.start, d + cs.stop)
    cur = g_ref[:, cs] * _sig(g_ref[:, cs2])
    prev = gh_ref[HALO - 32:HALO, cs] * _sig(gh_ref[HALO - 32:HALO, cs2])
    return jnp.concatenate([jnp.where(i > 0, prev, 0.0), cur], axis=0)


def _ln_stats(c1):
    mu = jnp.mean(c1, axis=-1, keepdims=True)
    cc = c1 - mu
    return cc * lax.rsqrt(jnp.mean(cc * cc, axis=-1, keepdims=True) + EPS), lax.rsqrt(jnp.mean(cc * cc, axis=-1, keepdims=True) + EPS)


def _cf_fwd(proj, dw_w, dw_b, ln_w, ln_b, d, name):
    t = proj.shape[0]
    r = _conv_rows(t)

    def body(g_ref, gh_ref, zb_ref, w_ref, b_ref, lw_ref, lb_ref, c1_ref, c3_ref):
        for cb in range(d // LANES):
            cs = slice(cb * LANES, (cb + 1) * LANES)
            win = _glu_window(g_ref, gh_ref, pl.program_id(0), d, cs)
            acc = jnp.broadcast_to(b_ref[:, cs], (r, LANES))
            for sub in range(8):
                rot = _shift(win, sub)
                for s in range(sub, CF_TAPS, 8):
                    acc = acc + rot[32 - (s - sub):32 - (s - sub) + r, :] * w_ref[CF_TAPS - 1 - s:CF_TAPS - s, cs]
            c1_ref[:, cs] = acc
        xh, _ = _ln_stats(c1_ref[...])
        ln = xh * lw_ref[...] + lb_ref[...]
        zb = zb_ref[...]
        c3_ref[...] = ((ln * _sig(ln)) * (zb * _sig(zb))).astype(c3_ref.dtype)

    row = pl.BlockSpec((r, d), lambda i: (i, 0))
    par = pl.BlockSpec((1, d), lambda i: (0, 0))
    return pl.pallas_call(
        body, name=name, grid=(t // r,),
        in_specs=[pl.BlockSpec((r, 2 * d), lambda i: (i, 2)), pl.BlockSpec((HALO, 2 * d), lambda i: (_halo_prev(r)(i)[0], 2)),
                  pl.BlockSpec((r, d), lambda i: (i, 8)), pl.BlockSpec((32, d), lambda i: (0, 0)), par, par, par],
        out_specs=[row, row],
        out_shape=[jax.ShapeDtypeStruct((t, d), F32), jax.ShapeDtypeStruct((t, d), MM_DTYPE)],
        compiler_params=_cparams("parallel"),
    )(proj, proj, proj, dw_w, dw_b, ln_w, ln_b)


def _cf_bwd1(dyb, w_cf, c1, proj, ln_w, ln_b, dproj, d, name):
    t = c1.shape[0]
    r = _row_tile(t)

    def epilogue(dc3v, i, ex, ou):
        c1_ref, zb_ref, lw_ref, lb_ref, _ = ex
        dc1_ref, dzb_ref, g_ref = ou

        @pl.when(i == 0)
        def _():
            g_ref[...] = jnp.zeros_like(g_ref)

        xh, rs = _ln_stats(c1_ref[...])
        ln = xh * lw_ref[...] + lb_ref[...]
        sl, zb = _sig(ln), zb_ref[...]
        szb = _sig(zb)
        dzb_ref[...] = (dc3v * (ln * sl) * _dsilu(zb, szb)).astype(dzb_ref.dtype)
        dln = dc3v * (zb * szb) * _dsilu(ln, sl)
        dxh = dln * lw_ref[...]
        dc1 = rs * (dxh - jnp.mean(dxh, axis=-1, keepdims=True) - xh * jnp.mean(dxh * xh, axis=-1, keepdims=True))
        dc1_ref[...] = dc1
        g_ref[0:1, :] += jnp.sum(dln * xh, axis=0, keepdims=True)
        g_ref[1:2, :] += jnp.sum(dln, axis=0, keepdims=True)
        g_ref[2:3, :] += jnp.sum(dc1, axis=0, keepdims=True)

    row = pl.BlockSpec((r, d), lambda i, kk: (i, 0))
    zbs = pl.BlockSpec((r, d), lambda i, kk: (i, 8))
    par = pl.BlockSpec((1, d), lambda i, kk: (0, 0))
    return _mm_rows(
        dyb, w_cf, nt=True, tm=r, tk=d, extras=[(c1, row), (proj, zbs), (ln_w, par), (ln_b, par), (dproj, ANY)],
        outs=[(jax.ShapeDtypeStruct((t, d), F32), row), (jax.ShapeDtypeStruct(dproj.shape, dproj.dtype), zbs),
              (jax.ShapeDtypeStruct((8, d), F32), pl.BlockSpec((8, d), lambda i, kk: (0, 0)))],
        aliases={4: 1}, epilogue=epilogue, name=name)


def _cf_bwd2(dc1, proj, dw_w, dproj, d, name):
    t = dc1.shape[0]
    r = _conv_rows(t)
    last = t // r - 1

    def body(dc_ref, dcn_ref, g_ref, gh_ref, w_ref, _, dg_ref, gw_ref):
        i = pl.program_id(0)

        @pl.when(i == 0)
        def _():
            gw_ref[...] = jnp.zeros_like(gw_ref)

        for cb in range(d // LANES):
            cs = slice(cb * LANES, (cb + 1) * LANES)
            cs2 = slice(d + cs.start, d + cs.stop)
            dcur = dc_ref[:, cs]
            dwin = jnp.concatenate([dcur, jnp.where(i < last, dcn_ref[0:32, cs], 0.0)], axis=0)
            win = _glu_window(g_ref, gh_ref, i, d, cs)
            acc = None
            for sub in range(8):
                drot = _shift(dwin, -sub)
                rot = _shift(win, sub)
                for s in range(sub, CF_TAPS, 8):
                    j = CF_TAPS - 1 - s
                    term = drot[s - sub:s - sub + r, :] * w_ref[j:j + 1, cs]
                    acc = term if acc is None else acc + term
                    gw_ref[j:j + 1, cs] += jnp.sum(dcur * rot[32 - (s - sub):32 - (s - sub) + r, :], axis=0, keepdims=True)
            ga, sb = g_ref[:, cs], _sig(g_ref[:, cs2])
            dg_ref[:, cs] = (acc * sb).astype(dg_ref.dtype)
            dg_ref[:, cs2] = (acc * ga * sb * (1.0 - sb)).astype(dg_ref.dtype)

    row = pl.BlockSpec((r, d), lambda i: (i, 0))
    glu = pl.BlockSpec((r, 2 * d), lambda i: (i, 2))
    return pl.pallas_call(
        body, name=name, grid=(t // r,),
        in_specs=[row, pl.BlockSpec((HALO, d), _halo_next(r, t)), glu,
                  pl.BlockSpec((HALO, 2 * d), lambda i: (_halo_prev(r)(i)[0], 2)), pl.BlockSpec((32, d), lambda i: (0, 0)), ANY],
        out_specs=[glu, pl.BlockSpec((32, d), lambda i: (0, 0))],
        out_shape=[jax.ShapeDtypeStruct(dproj.shape, dproj.dtype), jax.ShapeDtypeStruct((32, d), F32)],
        input_output_aliases={5: 0}, compiler_params=_cparams("arbitrary"),
    )(dc1, dc1, proj, proj, dw_w, dproj)


def _merge_bwd(dxo_mm, w_o, proj, ya, yb, b, d, name):
    t = ya.shape[0]
    r = _row_tile(t)

    def epilogue(dmv, i, ex, ou):
        g_ref, ya_ref, yb_ref, b_ref = ex
        dya_ref, dyb_ref, dg_ref, gb_ref = ou

        @pl.when(i == 0)
        def _():
            gb_ref[...] = jnp.zeros_like(gb_ref)

        sa, sb = _sig(g_ref[:, :d]), _sig(g_ref[:, d:])
        dyb = dmv * sb
        dya_ref[...] = (dmv * sa).astype(dya_ref.dtype)
        dyb_ref[...] = dyb.astype(dyb_ref.dtype)
        dg_ref[:, :d] = (dmv * ya_ref[...] * sa * (1.0 - sa)).astype(dg_ref.dtype)
        dg_ref[:, d:] = (dmv * (yb_ref[...] + b_ref[...]) * sb * (1.0 - sb)).astype(dg_ref.dtype)
        gb_ref[0:1, :] += jnp.sum(dyb, axis=0, keepdims=True)

    row = pl.BlockSpec((r, d), lambda i, kk: (i, 0))
    gate = pl.BlockSpec((r, 2 * d), lambda i, kk: (i, 3))
    const = lambda shape: pl.BlockSpec(shape, lambda i, kk: (0, 0))
    return _mm_rows(
        dxo_mm, w_o, nt=True, tm=r, tk=d, extras=[(proj, gate), (ya, row), (yb, row), (b, const((1, d)))],
        outs=[(jax.ShapeDtypeStruct((t, d), MM_DTYPE), row), (jax.ShapeDtypeStruct((t, d), MM_DTYPE), row),
              (jax.ShapeDtypeStruct((t, 9 * d), MM_DTYPE), gate), (jax.ShapeDtypeStruct((8, d), F32), const((8, d)))],
        epilogue=epilogue, name=name)


def _place():
    return lax.axis_index("x"), lax.axis_index("y"), lax.axis_index("c")


def _slot(p):
    return 4 * p[0] + 2 * p[1] + p[2]


def _allgather(blocks, name):
    n = len(blocks)

    def body(*refs):
        ins, outs = refs[:n], refs[n:2 * n]
        send_sems, recv_sems, local_sems = refs[2 * n:]
        x, y, c = _place()
        me, sibling = (x, y, c), (x, y, 1 - c)
        chips = [(1 - x, y), (x, 1 - y), (1 - x, 1 - y)]

        def copy(a, k, block, to, src=None):
            dst = outs[a].at[_slot(block)]
            return pltpu.make_async_remote_copy(
                src_ref=dst if src is None else src, dst_ref=dst, send_sem=send_sems.at[7 * a + k],
                recv_sem=recv_sems.at[7 * a + k], device_id=to, device_id_type=MESH)

        mine = [pltpu.make_async_copy(ins[a], outs[a].at[_slot(me)], local_sems.at[a]) for a in range(n)]
        for cp in mine:
            cp.start()
        first = []
        for a in range(n):
            first.append(copy(a, 0, me, sibling, src=ins[a]))
            first += [copy(a, 1 + j, me, (*chip, c), src=ins[a]) for j, chip in enumerate(chips)]
        for cp in first:
            cp.start()
        passed = []
        for j, chip in enumerate(chips):
            for a in range(n):
                copy(a, 1 + j, (*chip, c), me).wait_recv()
                cp = copy(a, 4 + j, (*chip, c), sibling)
                cp.start()
                passed.append(cp)
        for a in range(n):
            copy(a, 0, sibling, me).wait_recv()
            for j, chip in enumerate(chips):
                copy(a, 4 + j, (*chip, 1 - c), me).wait_recv()
        for cp in first + passed:
            cp.wait_send()
        for cp in mine:
            cp.wait()

    vmem = pl.BlockSpec(memory_space=pltpu.VMEM)
    return pl.pallas_call(
        body, name=name, in_specs=[vmem] * n, out_specs=[vmem] * n,
        out_shape=[jax.ShapeDtypeStruct((8, *b.shape), b.dtype) for b in blocks],
        scratch_shapes=[pltpu.SemaphoreType.DMA((7 * n,)), pltpu.SemaphoreType.DMA((7 * n,)), pltpu.SemaphoreType.DMA((n,))],
        compiler_params=pltpu.CompilerParams(vmem_limit_bytes=VMEM_LIMIT),
    )(*blocks)


_FLIPS = [(0, 0, 1), (0, 1, 0), (0, 1, 1), (1, 0, 0), (1, 0, 1), (1, 1, 0), (1, 1, 1)]


_HBM = pl.BlockSpec(memory_space=pltpu.HBM)
_SEM = pl.BlockSpec(memory_space=pltpu.SEMAPHORE)
_EFFECT = pltpu.SideEffectType.DATAFLOW_SIDE_EFFECTING


def _peers(me):
    return [tuple(1 - me[ax] if f[ax] else me[ax] for ax in range(3)) for f in _FLIPS]


def _alltoall_copy(srcs, lands, send_sems, recv_sems, a, k, me, peers, own):
    dst = lands[a].at[_slot(me if own else peers[k])]
    src = srcs[a].at[0 if srcs[a].shape[0] == 1 else _slot(peers[k])]
    return pltpu.make_async_remote_copy(
        src_ref=src, dst_ref=dst, send_sem=send_sems.at[7 * a + k], recv_sem=recv_sems.at[7 * a + k],
        device_id=peers[k], device_id_type=MESH)


def _alltoall_start(arrays, lands, name):
    n = len(arrays)

    def body(*refs):
        srcs, lnds = refs[:n], refs[n:2 * n]
        send_sems, recv_sems = refs[2 * n:2 * n + 2]
        token = refs[-1]
        me = _place()
        peers = _peers(me)
        for a in range(n):
            for k in range(7):
                _alltoall_copy(srcs, lnds, send_sems, recv_sems, a, k, me, peers, True).start()
        token[...] = jnp.zeros_like(token)

    hbm = lambda v: pltpu.with_memory_space_constraint(v, pltpu.HBM)
    out = pl.pallas_call(
        body, name=name, in_specs=[_HBM] * (2 * n),
        out_specs=(_SEM, _SEM, *[_HBM] * (2 * n), pl.BlockSpec(memory_space=pltpu.VMEM)),
        out_shape=(pltpu.SemaphoreType.DMA((7 * n,)), pltpu.SemaphoreType.DMA((7 * n,)),
                   *[pltpu.HBM(v.shape, v.dtype) for v in (*arrays, *lands)], jax.ShapeDtypeStruct((8, LANES), F32)),
        input_output_aliases={i: 2 + i for i in range(2 * n)},
        compiler_params=pltpu.CompilerParams(has_side_effects=_EFFECT),
    )(*[hbm(v) for v in (*arrays, *lands)])
    return out[0], out[1], out[2:2 + 2 * n], out[-1]


def _alltoall_wait(send_sems, recv_sems, thru, after, name):
    n = len(thru) // 2

    def body(*refs):
        srcs, lnds = refs[:n], refs[n:2 * n]
        send_sems, recv_sems = refs[2 * n:2 * n + 2]
        me = _place()
        peers = _peers(me)
        for a in range(n):
            for k in range(7):
                _alltoall_copy(srcs, lnds, send_sems, recv_sems, a, k, me, peers, True).wait_send()
                _alltoall_copy(srcs, lnds, send_sems, recv_sems, a, k, me, peers, False).wait_recv()

    out = pl.pallas_call(
        body, name=name, in_specs=[*[_HBM] * (2 * n), _SEM, _SEM, ANY], out_specs=[_HBM] * (2 * n),
        out_shape=[pltpu.HBM(v.shape, v.dtype) for v in thru],
        input_output_aliases={i: i for i in range(2 * n)},
        compiler_params=pltpu.CompilerParams(has_side_effects=_EFFECT),
    )(*thru, send_sems, recv_sems, after)
    return out[n:]


def _sibling_exchange(arrays, name):
    n = len(arrays)

    def body(*refs):
        ins, outs = refs[:n], refs[n:2 * n]
        send_sems, recv_sems, local_sems = refs[2 * n:]
        x, y, c = _place()

        def copy(a, slot):
            return pltpu.make_async_remote_copy(
                src_ref=ins[a], dst_ref=outs[a].at[slot], send_sem=send_sems.at[a], recv_sem=recv_sems.at[a],
                device_id=(x, y, 1 - c), device_id_type=MESH)

        mine = [pltpu.make_async_copy(ins[a], outs[a].at[c], local_sems.at[a]) for a in range(n)]
        for cp in mine:
            cp.start()
        sent = [copy(a, c) for a in range(n)]
        for cp in sent:
            cp.start()
        for a in range(n):
            copy(a, 1 - c).wait_recv()
        for cp in sent:
            cp.wait_send()
        for cp in mine:
            cp.wait()

    vmem = pl.BlockSpec(memory_space=pltpu.VMEM)
    return pl.pallas_call(
        body, name=name, in_specs=[vmem] * n, out_specs=[vmem] * n,
        compiler_params=pltpu.CompilerParams(vmem_limit_bytes=VMEM_LIMIT),
        out_shape=[jax.ShapeDtypeStruct((2, *a.shape), a.dtype) for a in arrays],
        scratch_shapes=[pltpu.SemaphoreType.DMA((n,)), pltpu.SemaphoreType.DMA((n,)), pltpu.SemaphoreType.DMA((n,))],
    )(*arrays)


def _sum8(parts, name):
    _, rr, cc = parts.shape
    r = _pick(rr, (128, 64, 32, 16, 8))

    def body(p_ref, o_ref):
        acc = p_ref[0].astype(F32)
        for j in range(1, 8):
            acc = acc + p_ref[j].astype(F32)
        o_ref[...] = acc

    return pl.pallas_call(
        body, name=name, grid=(rr // r,), in_specs=[pl.BlockSpec((8, r, cc), lambda i: (0, i, 0))],
        out_specs=pl.BlockSpec((r, cc), lambda i: (i, 0)), out_shape=jax.ShapeDtypeStruct((rr, cc), F32),
        compiler_params=_cparams("parallel"),
    )(parts)


def _adamw(g, w, m, v, name):
    rr, cc = g.shape
    r = _pick(rr, (128, 64, 32, 16, 8))

    def body(g_ref, w_ref, m_ref, v_ref, d_ref, nm_ref, nv_ref):
        gv = g_ref[...]
        nm = ADAM_B1 * m_ref[...] + (1.0 - ADAM_B1) * gv
        nv = ADAM_B2 * v_ref[...] + (1.0 - ADAM_B2) * (gv * gv)
        m_hat = nm / (1.0 - ADAM_B1 ** ADAM_STEP)
        v_hat = nv / (1.0 - ADAM_B2 ** ADAM_STEP)
        d_ref[...] = -ADAM_LR * (m_hat / (jnp.sqrt(v_hat) + ADAM_EPS) + ADAM_WD * w_ref[...])
        nm_ref[...] = nm
        nv_ref[...] = nv

    blk = pl.BlockSpec((r, cc), lambda i: (i, 0))
    return pl.pallas_call(
        body, name=name, grid=(rr // r,), in_specs=[blk] * 4, out_specs=[blk] * 3,
        out_shape=[jax.ShapeDtypeStruct((rr, cc), F32)] * 3, compiler_params=_cparams("parallel"),
    )(g, w, m, v)


def _columns(sources, start, stop):
    out, at = [], 0
    for arr, off, width in sources:
        lo, hi = max(start, at), min(stop, at + width)
        if lo < hi:
            out.append(arr[:, off + lo - at:off + hi - at])
        at += width
    return out


def _flat_pack(parts, width):
    flat = jnp.concatenate([p.reshape(-1) for p in parts])
    total = -(-flat.shape[0] // (8 * width)) * (8 * width)
    return jnp.pad(flat, (0, total - flat.shape[0])).reshape(-1, width)


def _flat_unpack(pack, shapes):
    flat = pack.reshape(-1)
    out, at = [], 0
    for s in shapes:
        size = 1
        for e in s:
            size *= e
        out.append(flat[at:at + size].reshape(s))
        at += size
    return out


def kernel(x, meta, norm_w, w_in, conv_qkv_w, a_log, dt_bias, dn_norm_w, w_dn_out, dw_w, dw_b, ln_w, ln_b, w_cf_out, b_cf_out, w_o, final_norm_w, loss_target, m_meta, m_norm_w, m_w_in, m_conv_qkv_w, m_a_log, m_dt_bias, m_dn_norm_w, m_w_dn_out, m_dw_w, m_dw_b, m_ln_w, m_ln_b, m_w_cf_out, m_b_cf_out, m_w_o, m_final_norm_w, v_meta, v_norm_w, v_w_in, v_conv_qkv_w, v_a_log, v_dt_bias, v_dn_norm_w, v_w_dn_out, v_dw_w, v_dw_b, v_ln_w, v_ln_b, v_w_cf_out, v_b_cf_out, v_w_o, v_final_norm_w):
    d = x.shape[-1]
    heads = a_log.shape[-1]
    seq = x.shape[1]
    t = HDR + seq
    nc = t // CHUNK
    in_w = 9 * d + 2 * heads
    assert heads * DK == d and seq % LANES == 0 and w_in.shape[-1] * 4 == in_w
    xi, yi, ci = _place()
    shard = 2 * xi + yi
    ds = d // 4

    w_in_half = lax.dynamic_slice_in_dim(w_in[0].astype(MM_DTYPE), ci * (d // 2), d // 2, axis=0)
    w3_half = lax.dynamic_slice_in_dim(jnp.stack([w_dn_out[0], w_cf_out[0], w_o[0]]).astype(MM_DTYPE), ci * (d // 8), d // 8, axis=1)
    small = jnp.concatenate([
        jnp.pad(conv_qkv_w[0], ((0, 4), (0, 0))), jnp.pad(meta, ((0, 0), (0, 2 * ds))),
        jnp.pad(dw_w[0], ((0, 1), (0, 2 * ds)))], axis=0)
    g_in, g_small = _allgather([w_in_half, small], "gather_weights")
    my_slot = _slot((xi, yi, ci))
    w3_land = lax.dynamic_update_slice_in_dim(lax.empty((8, *w3_half.shape), w3_half.dtype), w3_half[None], my_slot, axis=0)
    w3_send, w3_recv, w3_in_flight, w3_token = _alltoall_start([w3_half[None]], [w3_land], "w3_start")
    sw = in_w // 4
    g_in = g_in.reshape(4, d, sw)
    w_cols = [(g_in[s], 0, sw) for s in range(4)]
    o_glu = 4 * d + 2 * heads
    w_main = jnp.concatenate(
        _columns(w_cols, 0, 4 * d) + _columns(w_cols, o_glu, o_glu + 2 * d) + _columns(w_cols, o_glu + 3 * d, o_glu + 5 * d)
        + _columns(w_cols, o_glu + 2 * d, o_glu + 3 * d), axis=1)
    w_ba = jnp.pad(jnp.concatenate(_columns(w_cols, 4 * d, o_glu), axis=1), ((0, 0), (0, LANES - 2 * heads)))
    small4 = g_small[0::2]
    conv_w_f = small4[:, 0:DN_TAPS, :].transpose(1, 0, 2).reshape(DN_TAPS, 3 * d)
    meta_f = small4[:, 8:8 + N_META, :ds].transpose(1, 0, 2).reshape(N_META, d)
    dw_w_f = small4[:, 24:56, :ds].transpose(1, 0, 2).reshape(32, d)

    xs = jnp.concatenate([jnp.pad(meta_f, ((PAD_ROWS, 0), (0, 0))), x[0]], axis=0)
    tgt = loss_target[0]
    h = _rms_fwd(xs, norm_w + w3_token[0:1, 0:1], "rms_fwd")
    proj = _mm(h, w_main, name="proj_main")
    pba = _mm(h, w_ba, name="proj_ba")
    qkvn = _dn_prep(proj, conv_w_f, d, heads, "dn_prep")
    pvec = jnp.zeros((8, LANES), F32).at[0, heads:2 * heads].set(a_log[0]).at[1, heads:2 * heads].set(dt_bias[0])
    bg = _ba_fwd(pba, pvec, heads, "ba_fwd")
    g_row = bg[:, heads:2 * heads].reshape(nc, CHUNK, heads).transpose(0, 2, 1)
    o, sall, ainv = _dn_fwd(qkvn, bg, g_row, d, heads, "dn_fwd")
    g_w3, = _alltoall_wait(w3_send, w3_recv, w3_in_flight, o, "w3_wait")
    w3_full = g_w3.transpose(1, 0, 2, 3).reshape(3, d, d)
    w_dn_f, w_cf_f, w_o_f = w3_full[0], w3_full[1], w3_full[2]
    ya_in, ya = _dn_post(o, proj, dn_norm_w, w_dn_f, d, heads, "dn_post")
    c1, c3 = _cf_fwd(proj, dw_w_f, dw_b, ln_w, ln_b, d, "cf_fwd")
    yb = _mm(c3, w_cf_f, name="yb")
    dxo, dxo_mm, loss_acc, g_fnw, merged = _merge_final(ya, yb, b_cf_out, proj, w_o_f, xs, final_norm_w.reshape(1, d), tgt,
                                                        "merge_final")

    gw_o = _mm(merged, dxo_mm, ta=True, out_dtype=MM_DTYPE, name="gw_o")
    dya, dyb, dproj, g_bcf = _merge_bwd(dxo_mm, w_o_f, proj, ya, yb, b_cf_out, d, "merge_bwd")
    gw_cf = _mm(c3, dyb, ta=True, out_dtype=MM_DTYPE, name="gw_cf")
    gw_dn = _mm(ya_in, dya, ta=True, out_dtype=MM_DTYPE, name="gw_dn")
    own_home = lambda v: lax.dynamic_update_slice_in_dim(
        lax.empty(v.shape, v.dtype), lax.dynamic_slice_in_dim(v, my_slot, 1, axis=0), my_slot, axis=0)
    send_w3 = jnp.stack([gw_dn, gw_cf, gw_o]).reshape(3, 8, d // 8, d).transpose(1, 0, 2, 3).reshape(8, 3 * d // 8, d)
    gw3_send, gw3_recv, gw3_in_flight, _ = _alltoall_start([send_w3], [own_home(send_w3)], "exchange_w3_start")
    dc1, dproj, g_ln = _cf_bwd1(dyb, w_cf_f, c1, proj, ln_w, ln_b, dproj, d, "cf_bwd1")
    dproj, g_dww = _cf_bwd2(dc1, proj, dw_w_f, dproj, d, "cf_bwd2")
    do, dproj, g_dnw = _dn_post_bwd(dya, w_dn_f, o, proj, dn_norm_w, dproj, d, heads, "dn_post_bwd")
    dqkvn, dbg, dg_row = _dn_bwd(qkvn, bg, g_row, do, sall, ainv, d, heads, "dn_bwd")
    dproj, g_convw = _dn_prep_bwd(proj, conv_w_f, dqkvn, dproj, d, heads, "dn_prep_bwd")
    dbg = dbg + jnp.pad(dg_row.transpose(0, 2, 1).reshape(t, heads), ((0, 0), (heads, LANES - 2 * heads)))
    dpba, g_ba = _ba_bwd(pba, pvec, dbg, heads, "ba_bwd")
    gw_main = _mm(h, dproj, ta=True, out_dtype=MM_DTYPE, name="gw_main")
    gw_ba = _mm(h, dpba, ta=True, out_dtype=MM_DTYPE, name="gw_ba")

    g_cols = [(gw_main, 0, 4 * d), (gw_ba, 0, 2 * heads), (gw_main, 4 * d, 2 * d), (gw_main, 8 * d, d),
              (gw_main, 6 * d, 2 * d)]
    send_in = jnp.stack([jnp.concatenate(_columns(g_cols, s * sw, (s + 1) * sw), axis=1) for s in range(4)])
    send_in = send_in.reshape(8, d // 2, sw)
    send_sems, recv_sems, in_flight, token = _alltoall_start([send_in], [own_home(send_in)], "exchange_start")
    dxs, g_nw = _dh_rms_bwd(dproj, w_main, dpba, w_ba, xs, norm_w, dxo, token, "dh_rms_bwd")
    grad_x = dxs[HDR:][None]
    got_w3, = _alltoall_wait(gw3_send, gw3_recv, gw3_in_flight, dxs, "exchange_w3_wait")
    got_in, = _alltoall_wait(send_sems, recv_sems, in_flight, dxs, "exchange_wait")

    wide = lambda a: jnp.pad(a, ((0, 0), (0, d - a.shape[1])))
    small_g = jnp.concatenate([
        g_convw.reshape(24, d), dxs[PAD_ROWS:HDR], g_dww, g_nw, g_ln, g_bcf, g_fnw, wide(g_ba), wide(g_dnw), wide(loss_acc)],
        axis=0)
    red_in = _sum8(got_in, "sum_w_in")
    red_w3 = _sum8(got_w3, "sum_w3")
    pair_in, pair_w3 = _sibling_exchange([red_in, red_w3], "pair_grads")
    g_w_in = pair_in.reshape(d, in_w // 4)
    g_w3 = pair_w3.reshape(2, 3, d // 8, d).transpose(1, 0, 2, 3).reshape(3 * ds, d)
    all_small, = _allgather([small_g], "gather_small_grads")
    sg = _sum8(all_small, "sum_small")
    g_conv = lax.dynamic_slice_in_dim(sg[0:24].reshape(8, 3 * d)[0:DN_TAPS], shard * 3 * ds, 3 * ds, axis=1)
    g_meta = lax.dynamic_slice_in_dim(sg[24:40], shard * ds, ds, axis=1)
    g_dw_w = lax.dynamic_slice_in_dim(sg[40:40 + CF_TAPS], shard * ds, ds, axis=1)
    g_rep = {"norm_w": sg[72:73], "ln_w": sg[80:81], "ln_b": sg[81:82], "dw_b": sg[82:83], "b_cf_out": sg[88:89],
             "final_norm_w": sg[96], "a_log": sg[104:105, heads:2 * heads], "dt_bias": sg[105:106, heads:2 * heads],
             "dn_norm_w": sg[112:113, 0:DK]}
    loss = sg[120, 0]

    res = {}
    dl, nm, nv = _adamw(g_w_in, w_in[0], m_w_in[0], v_w_in[0], "adamw_w_in")
    res["w_in"] = (g_w_in[None], dl[None], nm[None], nv[None])
    stack3 = lambda a, b, c: jnp.concatenate([a[0], b[0], c[0]], axis=0)
    dl, nm, nv = _adamw(g_w3, stack3(w_dn_out, w_cf_out, w_o), stack3(m_w_dn_out, m_w_cf_out, m_w_o),
                        stack3(v_w_dn_out, v_w_cf_out, v_w_o), "adamw_w3")
    for j, nme in enumerate(("w_dn_out", "w_cf_out", "w_o")):
        res[nme] = tuple(a[j * ds:(j + 1) * ds][None] for a in (g_w3, dl, nm, nv))
    names = ["meta", "conv_qkv_w", "dw_w", "norm_w", "dw_b", "ln_w", "ln_b", "b_cf_out", "final_norm_w", "a_log", "dt_bias",
             "dn_norm_w"]
    grads = {"meta": g_meta, "conv_qkv_w": g_conv[None], "dw_w": g_dw_w[None], **g_rep}
    given = dict(meta=(meta, m_meta, v_meta), conv_qkv_w=(conv_qkv_w, m_conv_qkv_w, v_conv_qkv_w), dw_w=(dw_w, m_dw_w, v_dw_w),
                 norm_w=(norm_w, m_norm_w, v_norm_w), dw_b=(dw_b, m_dw_b, v_dw_b), ln_w=(ln_w, m_ln_w, v_ln_w),
                 ln_b=(ln_b, m_ln_b, v_ln_b), b_cf_out=(b_cf_out, m_b_cf_out, v_b_cf_out),
                 final_norm_w=(final_norm_w, m_final_norm_w, v_final_norm_w), a_log=(a_log, m_a_log, v_a_log),
                 dt_bias=(dt_bias, m_dt_bias, v_dt_bias), dn_norm_w=(dn_norm_w, m_dn_norm_w, v_dn_norm_w))
    shapes = [given[nme][0].shape for nme in names]
    packs = [_flat_pack([grads[nme] for nme in names], LANES)] + [_flat_pack([given[nme][j] for nme in names], LANES) for j in range(3)]
    outs = _adamw(*packs, "adamw_small")
    unpacked = [_flat_unpack(p, shapes) for p in (packs[0], *outs)]
    for j, nme in enumerate(names):
        res[nme] = tuple(u[j] for u in unpacked)

    order = ["meta", "norm_w", "w_in", "conv_qkv_w", "a_log", "dt_bias", "dn_norm_w", "w_dn_out", "dw_w", "dw_b", "ln_w", "ln_b",
             "w_cf_out", "b_cf_out", "w_o", "final_norm_w"]
    return (loss, grad_x, *[res[nme][0] for nme in order], *[res[nme][1] for nme in order],
            *[res[nme][2] for nme in order], *[res[nme][3] for nme in order])
```

```python
import jax
import jax.numpy as jnp
from jax import lax
from jax.experimental import pallas as pl
from jax.experimental.pallas import tpu as pltpu

F32 = jnp.float32
MM_DTYPE = jnp.bfloat16
EPS = 1e-6
N_META = 16
HDR = 128
PAD_ROWS = HDR - N_META
CHUNK = 64
DK = 128
CF_TAPS = 31
DN_TAPS = 4
HALO = 64
LANES = 128
ADAM_LR, ADAM_B1, ADAM_B2, ADAM_EPS, ADAM_WD, ADAM_STEP = 0.001, 0.9, 0.999, 1e-08, 0.01, 10
VMEM_LIMIT = 48 * 1024 * 1024
MESH = pl.DeviceIdType.MESH
HI = lax.Precision.HIGHEST
LO = lax.Precision.DEFAULT
NN = (((1,), (0,)), ((), ()))
NT = (((1,), (1,)), ((), ()))
TN = (((0,), (0,)), ((), ()))
ANY = pl.BlockSpec(memory_space=pl.ANY)


def _dot(a, b, dims=NN, prec=LO):
    if prec == LO:
        a, b = a.astype(MM_DTYPE), b.astype(MM_DTYPE)
    return lax.dot_general(a, b, dims, precision=prec, preferred_element_type=F32)


def _split(a):
    hi = a.astype(MM_DTYPE)
    return hi, (a - hi.astype(F32)).astype(MM_DTYPE)


def _dot3(a, b, dims=NN):
    ah, al = _split(a)
    bh, bl = _split(b)
    return _dot(ah, bh, dims) + (_dot(ah, bl, dims) + _dot(al, bh, dims))


def _pick(n, options):
    for o in options:
        if n % o == 0:
            return o
    return n


def _cparams(*sem):
    return pltpu.CompilerParams(dimension_semantics=sem, vmem_limit_bytes=VMEM_LIMIT)


def _sig(x):
    return jax.nn.sigmoid(x)


def _sum_all(a):
    return jnp.sum(jnp.sum(a, axis=-1, keepdims=True), axis=0, keepdims=True)


def _dsilu(x, s):
    return s + x * s * (1.0 - s)


def _shift(win, s):
    n = win.shape[0]
    s = s % n
    return win if s == 0 else pltpu.roll(win, s, 0)


def _row_tile(t):
    return _pick(t, (320, 128))


def _mm(a, b, *, nt=False, ta=False, out_dtype=F32, name):
    k, m = a.shape if ta else a.shape[::-1]
    n = b.shape[0] if nt else b.shape[1]
    tm = _pick(m, (1664, 1024, 640, 512, 384, 256, 128))
    tn = _pick(n, (1024, 768, 512, 256, 128))
    tk = _pick(k, (1664, 1024, 640, 512, 256, 128))
    nk = k // tk
    dims = NT if nt else (TN if ta else NN)

    def body(a_ref, b_ref, o_ref, *acc):
        p = lax.dot_general(a_ref[...], b_ref[...], dims, preferred_element_type=F32)
        if nk == 1:
            o_ref[...] = p.astype(o_ref.dtype)
            return
        acc_ref, = acc
        kk = pl.program_id(2)

        @pl.when(kk == 0)
        def _():
            acc_ref[...] = p

        @pl.when(kk > 0)
        def _():
            acc_ref[...] += p

        @pl.when(kk == nk - 1)
        def _():
            o_ref[...] = acc_ref[...].astype(o_ref.dtype)

    a_spec = pl.BlockSpec((tk, tm), lambda i, j, kk: (kk, i)) if ta else pl.BlockSpec((tm, tk), lambda i, j, kk: (i, kk))
    b_spec = pl.BlockSpec((tn, tk), lambda i, j, kk: (j, kk)) if nt else pl.BlockSpec((tk, tn), lambda i, j, kk: (kk, j))
    return pl.pallas_call(
        body, name=name, grid=(m // tm, n // tn, nk),
        in_specs=[a_spec, b_spec],
        out_specs=pl.BlockSpec((tm, tn), lambda i, j, kk: (i, j)),
        out_shape=jax.ShapeDtypeStruct((m, n), out_dtype),
        scratch_shapes=[] if nk == 1 else [pltpu.VMEM((tm, tn), F32)],
        compiler_params=_cparams("parallel", "parallel", "arbitrary"),
    )(a, b)


def _rms_fwd(xs, w, name):
    t, d = xs.shape
    r = _row_tile(t)

    def body(x_ref, w_ref, h_ref):
        x = x_ref[...]
        rs = lax.rsqrt(jnp.mean(x * x, axis=-1, keepdims=True) + EPS)
        h_ref[...] = (x * rs * w_ref[...]).astype(h_ref.dtype)

    return pl.pallas_call(
        body, name=name, grid=(t // r,),
        in_specs=[pl.BlockSpec((r, d), lambda i: (i, 0)), pl.BlockSpec((1, d), lambda i: (0, 0))],
        out_specs=pl.BlockSpec((r, d), lambda i: (i, 0)),
        out_shape=jax.ShapeDtypeStruct((t, d), MM_DTYPE), compiler_params=_cparams("parallel"),
    )(xs, w)


def _mm_rows(a, b, *, nt, tm, tk, extras, outs, epilogue, name, prologue=None, aliases=None):
    m, k = a.shape
    n = b.shape[0] if nt else b.shape[1]
    assert m % tm == 0 and k % tk == 0
    nk = k // tk
    ne = len(extras)

    def body(a_ref, b_ref, *rest):
        ex, ou = rest[:ne], rest[ne:ne + len(outs)]
        i, kk = pl.program_id(0), pl.program_id(1)
        lhs = a_ref[...] if prologue is None else prologue(a_ref, ex, ou)
        p = lax.dot_general(lhs, b_ref[...], NT if nt else NN, preferred_element_type=F32)
        if nk == 1:
            epilogue(p, i, ex, ou)
            return
        acc_ref = rest[-1]

        @pl.when(kk == 0)
        def _():
            acc_ref[...] = p

        @pl.when(jnp.logical_and(kk > 0, kk < nk - 1))
        def _():
            acc_ref[...] += p

        @pl.when(kk == nk - 1)
        def _():
            epilogue(acc_ref[...] + p, i, ex, ou)

    b_spec = pl.BlockSpec((n, tk), lambda i, kk: (0, kk)) if nt else pl.BlockSpec((tk, n), lambda i, kk: (kk, 0))
    return pl.pallas_call(
        body, name=name, grid=(m // tm, nk),
        in_specs=[pl.BlockSpec((tm, tk), lambda i, kk: (i, kk)), b_spec] + [s for _, s in extras],
        out_specs=[s for _, s in outs], out_shape=[s for s, _ in outs],
        scratch_shapes=[] if nk == 1 else [pltpu.VMEM((tm, n), F32)],
        input_output_aliases={2 + e: o for e, o in (aliases or {}).items()},
        compiler_params=_cparams("arbitrary", "arbitrary"),
    )(a, b, *[e for e, _ in extras])


def _dh_rms_bwd(dproj, w_main, dpba, w_ba, xs, w, dres, after, name):
    t, d = xs.shape
    tm = _pick(t, (640, 128))
    tk = _pick(w_main.shape[1], (2304, 768))

    def epilogue(dh, i, ex, ou):
        dpba_ref, wba_ref, x_ref, w_ref, dr_ref, _ = ex
        dx_ref, gw_ref = ou

        @pl.when(i == 0)
        def _():
            gw_ref[...] = jnp.zeros_like(gw_ref)

        dh = dh + lax.dot_general(dpba_ref[...], wba_ref[...], NT, preferred_element_type=F32)
        x = x_ref[...]
        rs = lax.rsqrt(jnp.mean(x * x, axis=-1, keepdims=True) + EPS)
        xh = x * rs
        dxh = dh * w_ref[...]
        dx_ref[...] = rs * (dxh - xh * jnp.mean(dxh * xh, axis=-1, keepdims=True)) + dr_ref[...]
        gw_ref[0:1, :] += jnp.sum(dh * xh, axis=0, keepdims=True)

    row = pl.BlockSpec((tm, d), lambda i, kk: (i, 0))
    const = lambda shape: pl.BlockSpec(shape, lambda i, kk: (0, 0))
    return _mm_rows(
        dproj, w_main, nt=True, tm=tm, tk=tk,
        extras=[(dpba, pl.BlockSpec((tm, LANES), lambda i, kk: (i, 0))), (w_ba, const(w_ba.shape)), (xs, row), (w, const((1, d))),
                (dres, row), (after, ANY)],
        outs=[(jax.ShapeDtypeStruct((t, d), F32), row), (jax.ShapeDtypeStruct((8, d), F32), const((8, d)))],
        epilogue=epilogue, name=name)


def _merge_final(ya, yb, b, proj, w_o, xs, w, tgt, name):
    t, d = xs.shape
    r = _row_tile(t)
    pieces = r // HALO

    def prologue(ya_ref, ex, ou):
        yb_ref, b_ref, g_ref = ex[2:5]
        m_ref = ou[4]
        g = g_ref[...].astype(F32)
        m_ref[...] = (_sig(g[:, :d]) * ya_ref[...] + _sig(g[:, d:]) * (yb_ref[...] + b_ref[...])).astype(m_ref.dtype)
        return m_ref[...]

    def epilogue(mo, i, ex, ou):
        x_ref, w_ref = ex[:2]
        target = jnp.concatenate([t_ref[...] for t_ref in ex[5:]], axis=0)
        dx_ref, dxm_ref, loss_ref, gw_ref = ou[:4]

        @pl.when(i == 0)
        def _():
            loss_ref[...] = jnp.zeros_like(loss_ref)
            gw_ref[...] = jnp.zeros_like(gw_ref)

        xo = x_ref[...] + mo
        rs = lax.rsqrt(jnp.mean(xo * xo, axis=-1, keepdims=True) + EPS)
        xh = xo * rs
        y = xh * w_ref[...]
        rows = i * r + lax.broadcasted_iota(jnp.int32, (r, 1), 0)
        err = jnp.where(rows >= HDR, y - target, 0.0)
        loss_ref[...] += 0.5 * _sum_all(err * err) / d
        dy = err / d
        gw_ref[0:1, :] += jnp.sum(dy * xh, axis=0, keepdims=True)
        dxh = dy * w_ref[...]
        dx = rs * (dxh - xh * jnp.mean(dxh * xh, axis=-1, keepdims=True))
        dx_ref[...] = dx
        dxm_ref[...] = dx.astype(dxm_ref.dtype)

    row = pl.BlockSpec((r, d), lambda i, kk: (i, 0))
    const = lambda shape: pl.BlockSpec(shape, lambda i, kk: (0, 0))
    return _mm_rows(
        ya, w_o, nt=False, tm=r, tk=d,
        extras=[(xs, row), (w, const((1, d))), (yb, row), (b, const((1, d))),
                (proj, pl.BlockSpec((r, 2 * d), lambda i, kk: (i, 3)))]
        + [(tgt, pl.BlockSpec((HALO, d), lambda i, kk, j=j: (jnp.maximum(i * pieces - HDR // HALO + j, 0), 0)))
           for j in range(pieces)],
        outs=[(jax.ShapeDtypeStruct((t, d), F32), row), (jax.ShapeDtypeStruct((t, d), MM_DTYPE), row),
              (jax.ShapeDtypeStruct((8, LANES), F32), const((8, LANES))), (jax.ShapeDtypeStruct((8, d), F32), const((8, d))),
              (jax.ShapeDtypeStruct((t, d), MM_DTYPE), row)],
        prologue=prologue, epilogue=epilogue, name=name)


def _halo_prev(r):
    return lambda i: (jnp.maximum(i * (r // HALO) - 1, 0), 0)


def _halo_next(r, t):
    return lambda i: (jnp.minimum((i + 1) * (r // HALO), t // HALO - 1), 0)


def _conv_rows(t):
    return _pick(t, (320, 128))


def _qkv_window(p_ref, ph_ref, i, cs):
    prev = ph_ref[HALO - 16:HALO, cs].astype(F32)[8:16]
    return jnp.concatenate([jnp.where(i > 0, prev, 0.0), p_ref[:, cs].astype(F32)], axis=0)


def _qkv_conv(win, cw, r):
    y = None
    for s in range(DN_TAPS):
        term = _shift(win, s)[8:8 + r, :] * cw[DN_TAPS - 1 - s:DN_TAPS - s, :]
        y = term if y is None else y + term
    return y


def _dn_prep(proj, conv_w, d, heads, name):
    t = proj.shape[0]
    r = _conv_rows(t)
    w3 = 3 * d

    def body(p_ref, ph_ref, cw_ref, o_ref):
        i = pl.program_id(0)
        for cb in range(w3 // DK):
            cs = slice(cb * DK, (cb + 1) * DK)
            y = _qkv_conv(_qkv_window(p_ref, ph_ref, i, cs), cw_ref[:, cs], r)
            a = y * _sig(y)
            if cb < 2 * heads:
                sc = DK ** -0.5 if cb < heads else 1.0
                a = a * (lax.rsqrt(jnp.sum(a * a, axis=-1, keepdims=True) + EPS) * sc)
            o_ref[:, cs] = a

    return pl.pallas_call(
        body, name=name, grid=(t // r,),
        in_specs=[pl.BlockSpec((r, w3), lambda i: (i, 0)), pl.BlockSpec((HALO, w3), _halo_prev(r)),
                  pl.BlockSpec((DN_TAPS, w3), lambda i: (0, 0))],
        out_specs=pl.BlockSpec((r, w3), lambda i: (i, 0)),
        out_shape=jax.ShapeDtypeStruct((t, w3), F32), compiler_params=_cparams("parallel"),
    )(proj, proj, conv_w)


def _dn_prep_bwd(proj, conv_w, dqkvn, dproj, d, heads, name):
    t = proj.shape[0]
    r = _conv_rows(t)
    w3 = 3 * d
    last = t // r - 1

    def body(p_ref, ph_ref, pn_ref, cw_ref, dn_ref, dnn_ref, _, dp_ref, gw_ref):
        i = pl.program_id(0)

        @pl.when(i == 0)
        def _():
            gw_ref[...] = jnp.zeros_like(gw_ref)

        for cb in range(w3 // DK):
            cs = slice(cb * DK, (cb + 1) * DK)
            cw = cw_ref[:, cs]
            win = jnp.concatenate([_qkv_window(p_ref, ph_ref, i, cs), jnp.where(i < last, pn_ref[0:16, cs].astype(F32)[0:8], 0.0)], axis=0)
            y = _qkv_conv(win, cw, r + 8)
            sy = _sig(y)
            da = jnp.concatenate([dn_ref[:, cs], jnp.where(i < last, dnn_ref[0:8, cs], 0.0)], axis=0)
            if cb < 2 * heads:
                a = y * sy
                rn = lax.rsqrt(jnp.sum(a * a, axis=-1, keepdims=True) + EPS)
                n0 = a * rn
                dn0 = da * (DK ** -0.5 if cb < heads else 1.0)
                da = rn * (dn0 - n0 * jnp.sum(dn0 * n0, axis=-1, keepdims=True))
            dconv = da * _dsilu(y, sy)
            acc = None
            for j in range(DN_TAPS):
                s = DN_TAPS - 1 - j
                term = _shift(dconv, -s)[0:r, :] * cw[j:j + 1, :]
                acc = term if acc is None else acc + term
                gw_ref[j:j + 1, cs] += jnp.sum(dconv[0:r, :] * _shift(win, s)[8:8 + r, :], axis=0, keepdims=True)
            dp_ref[:, cs] = acc.astype(dp_ref.dtype)

    row = pl.BlockSpec((r, w3), lambda i: (i, 0))
    nxt = pl.BlockSpec((HALO, w3), _halo_next(r, t))
    return pl.pallas_call(
        body, name=name, grid=(t // r,),
        in_specs=[row, pl.BlockSpec((HALO, w3), _halo_prev(r)), nxt, pl.BlockSpec((DN_TAPS, w3), lambda i: (0, 0)), row, nxt, ANY],
        out_specs=[row, pl.BlockSpec((8, w3), lambda i: (0, 0))],
        out_shape=[jax.ShapeDtypeStruct(dproj.shape, dproj.dtype), jax.ShapeDtypeStruct((8, w3), F32)],
        input_output_aliases={6: 0}, compiler_params=_cparams("arbitrary"),
    )(proj, proj, proj, conv_w, dqkvn, dqkvn, dproj)


def _ba_terms(x, pv, heads):
    lane = lax.broadcasted_iota(jnp.int32, x.shape, 1)
    is_b = lane < heads
    is_a = jnp.logical_and(lane >= heads, lane < 2 * heads)
    beta = _sig(x)
    z = x + pv[1:2, :]
    nexp = -jnp.exp(pv[0:1, :])
    sp = jnp.maximum(z, 0.0) + jnp.log1p(jnp.exp(-jnp.abs(z)))
    return is_b, is_a, beta, z, nexp, nexp * sp


def _ba_fwd(pba, pvec, heads, name):
    t = pba.shape[0]
    r = _row_tile(t)

    def body(x_ref, pv_ref, o_ref):
        is_b, is_a, beta, _, _, g = _ba_terms(x_ref[...], pv_ref[...], heads)
        rows = pl.program_id(0) * r + lax.broadcasted_iota(jnp.int32, (r, 1), 0)
        o_ref[...] = jnp.where(rows >= PAD_ROWS, jnp.where(is_b, beta, jnp.where(is_a, g, 0.0)), 0.0)

    row = pl.BlockSpec((r, LANES), lambda i: (i, 0))
    return pl.pallas_call(
        body, name=name, grid=(t // r,), in_specs=[row, pl.BlockSpec((8, LANES), lambda i: (0, 0))],
        out_specs=row, out_shape=jax.ShapeDtypeStruct((t, LANES), F32), compiler_params=_cparams("parallel"),
    )(pba, pvec)


def _ba_bwd(pba, pvec, dbg, heads, name):
    t = pba.shape[0]
    r = _row_tile(t)

    def body(x_ref, pv_ref, d_ref, o_ref, g_ref):
        @pl.when(pl.program_id(0) == 0)
        def _():
            g_ref[...] = jnp.zeros_like(g_ref)

        is_b, is_a, beta, z, nexp, g = _ba_terms(x_ref[...], pv_ref[...], heads)
        rows = pl.program_id(0) * r + lax.broadcasted_iota(jnp.int32, (r, 1), 0)
        dd = jnp.where(rows >= PAD_ROWS, d_ref[...], 0.0)
        dz = dd * nexp * _sig(z)
        o_ref[...] = jnp.where(is_b, dd * beta * (1.0 - beta), jnp.where(is_a, dz, 0.0)).astype(o_ref.dtype)
        g_ref[0:1, :] += jnp.sum(jnp.where(is_a, dd * g, 0.0), axis=0, keepdims=True)
        g_ref[1:2, :] += jnp.sum(jnp.where(is_a, dz, 0.0), axis=0, keepdims=True)

    row = pl.BlockSpec((r, LANES), lambda i: (i, 0))
    par = pl.BlockSpec((8, LANES), lambda i: (0, 0))
    return pl.pallas_call(
        body, name=name, grid=(t // r,), in_specs=[row, par, row], out_specs=[row, par],
        out_shape=[jax.ShapeDtypeStruct((t, LANES), MM_DTYPE), jax.ShapeDtypeStruct((8, LANES), F32)],
        compiler_params=_cparams("arbitrary"),
    )(pba, pvec, dbg)


def _chunk_consts():
    ri = lax.broadcasted_iota(jnp.int32, (CHUNK, CHUNK), 0)
    ci = lax.broadcasted_iota(jnp.int32, (CHUNK, CHUNK), 1)
    return ri >= ci, ri > ci, (ri == ci).astype(F32), (ri >= ci).astype(F32), (ri <= ci).astype(F32)


def _chunk_decay(gc, gr, incl):
    return jnp.where(incl, jnp.exp(jnp.where(incl, gc - gr, 0.0)), 0.0)


def _unit_lower_inverse(n, eye):
    x = [eye + a for a in n]
    p = list(n)
    for _ in range(5):
        p = [_dot(a, a) for a in p]
        x = [a + _dot(a, b) for a, b in zip(x, p)]
    r = [eye - a + _dot3(b, a) for a, b in zip(x, n)]
    return [a + _dot(a, b) for a, b in zip(x, r)]


def _dn_fwd(qkvn, bg, g_row, d, heads, name):
    t = qkvn.shape[0]
    nc = t // CHUNK

    def body(qkv_ref, bg_ref, gr_ref, o_ref, sall_ref, ainv_ref, s_ref):
        @pl.when(pl.program_id(0) == 0)
        def _():
            s_ref[...] = jnp.zeros_like(s_ref)

        incl, strict, eye, tril, triu = _chunk_consts()
        bgv = bg_ref[...]
        gc_all = _dot(tril, bgv, NN, HI)
        gr_all = _dot(gr_ref[0], triu, NN, HI)
        hs = range(heads)
        q = [qkv_ref[:, h * DK:(h + 1) * DK] for h in hs]
        k = [qkv_ref[:, d + h * DK:d + (h + 1) * DK] for h in hs]
        v = [qkv_ref[:, 2 * d + h * DK:2 * d + (h + 1) * DK] for h in hs]
        kb = [a.astype(MM_DTYPE) for a in k]
        beta = [bgv[:, h:h + 1] for h in hs]
        gc = [gc_all[:, heads + h:heads + h + 1] for h in hs]
        decay = [_chunk_decay(gc[h], gr_all[h:h + 1, :], incl) for h in hs]
        eg = [jnp.exp(a) for a in gc]
        n = [jnp.where(strict, -(beta[h] * _dot(kb[h], kb[h], NT) * decay[h]), 0.0) for h in hs]
        x = _unit_lower_inverse(n, eye)
        sol = [_dot3(x[h], jnp.concatenate([v[h] * beta[h], k[h] * (beta[h] * eg[h])], axis=1)) for h in hs]
        attn = [_dot(q[h], kb[h], NT) * decay[h] for h in hs]
        s = [s_ref[h] for h in hs]
        sb = [a.astype(MM_DTYPE) for a in s]
        wv = [sol[h][:, :DK] - _dot(sol[h][:, DK:], sb[h]) for h in hs]
        o = [_dot(q[h] * eg[h], sb[h]) + _dot(attn[h], wv[h]) for h in hs]
        glast = [a[CHUNK - 1:CHUNK, :] for a in gc]
        s_new = [s[h] * jnp.exp(glast[h]) + _dot(k[h] * jnp.exp(glast[h] - gc[h]), wv[h], TN) for h in hs]
        for h in hs:
            o_ref[:, h * DK:(h + 1) * DK] = o[h]
            sall_ref[0, h] = s[h]
            ainv_ref[0, h] = x[h]
            s_ref[h] = s_new[h]

    return pl.pallas_call(
        body, name=name, grid=(nc,),
        in_specs=[pl.BlockSpec((CHUNK, 3 * d), lambda i: (i, 0)), pl.BlockSpec((CHUNK, LANES), lambda i: (i, 0)),
                  pl.BlockSpec((1, heads, CHUNK), lambda i: (i, 0, 0))],
        out_specs=[pl.BlockSpec((CHUNK, d), lambda i: (i, 0)), pl.BlockSpec((1, heads, DK, DK), lambda i: (i, 0, 0, 0)),
                   pl.BlockSpec((1, heads, CHUNK, CHUNK), lambda i: (i, 0, 0, 0))],
        out_shape=[jax.ShapeDtypeStruct((t, d), F32), jax.ShapeDtypeStruct((nc, heads, DK, DK), F32),
                   jax.ShapeDtypeStruct((nc, heads, CHUNK, CHUNK), F32)],
        scratch_shapes=[pltpu.VMEM((heads, DK, DK), F32)], compiler_params=_cparams("arbitrary"),
    )(qkvn, bg, g_row)


def _dn_bwd(qkvn, bg, g_row, do, sall, ainv, d, heads, name):
    t = qkvn.shape[0]
    nc = t // CHUNK
    rev = lambda i: nc - 1 - i

    def body(qkv_ref, bg_ref, gr_ref, do_ref, sall_ref, ainv_ref, dqkv_ref, dbg_ref, dgr_ref, ds_ref):
        @pl.when(pl.program_id(0) == 0)
        def _():
            ds_ref[...] = jnp.zeros_like(ds_ref)

        incl, strict, eye, tril, triu = _chunk_consts()
        bgv = bg_ref[...]
        gc_all = _dot(tril, bgv, NN, HI)
        gr_all = _dot(gr_ref[0], triu, NN, HI)
        lane = lax.broadcasted_iota(jnp.int32, (CHUNK, LANES), 1)
        hrow = lax.broadcasted_iota(jnp.int32, (heads, CHUNK), 0)
        last_row = lax.broadcasted_iota(jnp.int32, (CHUNK, 1), 0) == CHUNK - 1
        dbeta_slab = jnp.zeros((CHUNK, LANES), F32)
        dgc_slab = jnp.zeros((CHUNK, LANES), F32)
        dgr_slab = jnp.zeros((heads, CHUNK), F32)
        hs = range(heads)
        cols = lambda ref, off: [ref[:, off + h * DK:off + (h + 1) * DK] for h in hs]
        qs, ks, vs, douts = cols(qkv_ref, 0), cols(qkv_ref, d), cols(qkv_ref, 2 * d), cols(do_ref, 0)
        ss, xs_, dsns = [sall_ref[0, h] for h in hs], [ainv_ref[0, h] for h in hs], [ds_ref[h] for h in hs]

        def recompute(h, z):
            z.q, z.k, z.v, z.dout, z.s, z.x, z.dsn = qs[h], ks[h], vs[h], douts[h], ss[h], xs_[h], dsns[h]
            z.kb, z.qb, z.sb, z.dsnb, z.doutb = (a.astype(MM_DTYPE) for a in (z.k, z.q, z.s, z.dsn, z.dout))
            z.beta = bgv[:, h:h + 1]
            gc = gc_all[:, heads + h:heads + h + 1]
            z.decay = _chunk_decay(gc, gr_all[h:h + 1, :], incl)
            z.eg = jnp.exp(gc)
            glast = gc[CHUNK - 1:CHUNK, :]
            z.eglast = jnp.exp(glast)
            z.ek = jnp.exp(glast - gc)
            z.kk = _dot(z.kb, z.kb, NT)
            z.qk = _dot(z.qb, z.kb, NT)
            sol = _dot3(z.x, jnp.concatenate([z.v * z.beta, z.k * (z.beta * z.eg)], axis=1))
            z.u, z.wb = sol[:, :DK], sol[:, DK:].astype(MM_DTYPE)
            z.attn = z.qk * z.decay
            z.qg = z.q * z.eg
            z.kend = z.k * z.ek

        def pseudo_values(h, z):
            z.wvb = (z.u - _dot(z.wb, z.sb)).astype(MM_DTYPE)

        def scan_step(h, z):
            z.d_wv = _dot(z.attn, z.doutb, TN) + _dot(z.kend, z.dsnb)
            z.d_attn = _dot(z.doutb, z.wvb, NT)
            z.d_qg = _dot(z.doutb, z.sb, NT)
            z.d_kend = _dot(z.wvb, z.dsnb, NT)
            z.ds_new = _dot(z.qg, z.doutb, TN) + z.eglast * z.dsn
            z.d_glast = z.eglast * _sum_all(z.dsn * z.s) + _sum_all(z.d_kend * z.kend)

        def state_terms(h, z):
            z.ds_new = z.ds_new - _dot(z.wb, z.d_wv, TN)
            z.d_w = -_dot(z.d_wv, z.sb, NT)

        def solve_transpose(h, z):
            d_rhs = _dot3(z.x, jnp.concatenate([z.d_wv, z.d_w], axis=1), TN)
            z.d_ru, z.d_rw = d_rhs[:, :DK], d_rhs[:, DK:]

        def lower_terms(h, z):
            z.d_low = jnp.where(strict, -(_dot(z.d_ru, z.u, NT) + _dot(z.d_rw, z.wb, NT)), 0.0)

        def outputs(h, z):
            rw_k = jnp.sum(z.d_rw * z.k, axis=-1, keepdims=True)
            z.dbeta = (jnp.sum(z.d_ru * z.v, axis=-1, keepdims=True) + rw_k * z.eg
                       + jnp.sum(z.d_low * z.kk * z.decay, axis=-1, keepdims=True))
            z.dv = z.d_ru * z.beta
            d_kk = (z.d_low * z.beta * z.decay).astype(MM_DTYPE)
            d_qk = (z.d_attn * z.decay).astype(MM_DTYPE)
            z.dk = (z.d_rw * (z.beta * z.eg) + _dot(d_kk, z.kb) + _dot(d_kk, z.kb, TN) + _dot(d_qk, z.qb, TN)
                    + z.d_kend * z.ek)
            z.dq = _dot(d_qk, z.kb) + z.d_qg * z.eg
            e = (z.d_low * z.beta * z.kk + z.d_attn * z.qk) * z.decay
            z.dgc = (rw_k * z.beta * z.eg + jnp.sum(e, axis=-1, keepdims=True)
                     + jnp.sum(z.d_qg * z.qg, axis=-1, keepdims=True) - jnp.sum(z.d_kend * z.kend, axis=-1, keepdims=True)
                     + jnp.where(last_row, z.d_glast, 0.0))
            z.dgr = -jnp.sum(e, axis=0, keepdims=True)

        class _Head:
            pass

        st = [_Head() for _ in hs]
        for phase in (recompute, pseudo_values, scan_step, state_terms, solve_transpose, lower_terms, outputs):
            for h in hs:
                phase(h, st[h])
        res = [(z.dq, z.dk, z.dv, z.ds_new, z.dbeta, z.dgc, z.dgr) for z in st]
        for h, (dq, dk, dv, ds_new, dbeta, dgc, dgr) in enumerate(res):
            dqkv_ref[:, h * DK:(h + 1) * DK] = dq
            dqkv_ref[:, d + h * DK:d + (h + 1) * DK] = dk
            dqkv_ref[:, 2 * d + h * DK:2 * d + (h + 1) * DK] = dv
            ds_ref[h] = ds_new
            dbeta_slab = jnp.where(lane == h, dbeta, dbeta_slab)
            dgc_slab = jnp.where(lane == heads + h, dgc, dgc_slab)
            dgr_slab = jnp.where(hrow == h, dgr, dgr_slab)
        dbg_ref[...] = dbeta_slab + _dot(triu, dgc_slab, NN, HI)
        dgr_ref[0] = _dot(dgr_slab, tril, NN, HI)

    return pl.pallas_call(
        body, name=name, grid=(nc,),
        in_specs=[pl.BlockSpec((CHUNK, 3 * d), lambda i: (rev(i), 0)), pl.BlockSpec((CHUNK, LANES), lambda i: (rev(i), 0)),
                  pl.BlockSpec((1, heads, CHUNK), lambda i: (rev(i), 0, 0)), pl.BlockSpec((CHUNK, d), lambda i: (rev(i), 0)),
                  pl.BlockSpec((1, heads, DK, DK), lambda i: (rev(i), 0, 0, 0)),
                  pl.BlockSpec((1, heads, CHUNK, CHUNK), lambda i: (rev(i), 0, 0, 0))],
        out_specs=[pl.BlockSpec((CHUNK, 3 * d), lambda i: (rev(i), 0)), pl.BlockSpec((CHUNK, LANES), lambda i: (rev(i), 0)),
                   pl.BlockSpec((1, heads, CHUNK), lambda i: (rev(i), 0, 0))],
        out_shape=[jax.ShapeDtypeStruct((t, 3 * d), F32), jax.ShapeDtypeStruct((t, LANES), F32),
                   jax.ShapeDtypeStruct((nc, heads, CHUNK), F32)],
        scratch_shapes=[pltpu.VMEM((heads, DK, DK), F32)], compiler_params=_cparams("arbitrary"),
    )(qkvn, bg, g_row, do, sall, ainv)


def _dn_post(o, proj, w, w_dn, d, heads, name):
    t = o.shape[0]
    r = _row_tile(t)

    def prologue(o_ref, ex, ou):
        z_ref, w_ref = ex
        y_ref = ou[0]
        for h in range(heads):
            sl = slice(h * DK, (h + 1) * DK)
            oh, z = o_ref[:, sl], z_ref[:, sl].astype(F32)
            rs = lax.rsqrt(jnp.mean(oh * oh, axis=-1, keepdims=True) + EPS)
            y_ref[:, sl] = (oh * rs * w_ref[...] * (z * _sig(z))).astype(y_ref.dtype)
        return y_ref[...]

    def epilogue(p, i, ex, ou):
        ou[1][...] = p

    row = pl.BlockSpec((r, d), lambda i, kk: (i, 0))
    return _mm_rows(
        o, w_dn, nt=False, tm=r, tk=d,
        extras=[(proj, pl.BlockSpec((r, d), lambda i, kk: (i, 3))), (w, pl.BlockSpec((1, DK), lambda i, kk: (0, 0)))],
        outs=[(jax.ShapeDtypeStruct((t, d), MM_DTYPE), row), (jax.ShapeDtypeStruct((t, d), F32), row)],
        prologue=prologue, epilogue=epilogue, name=name)


def _dn_post_bwd(dya, w_dn, o, proj, w, dproj, d, heads, name):
    t = o.shape[0]
    r = _row_tile(t)

    def epilogue(dy, i, ex, ou):
        o_ref, z_ref, w_ref, _ = ex
        do_ref, dz_ref, gw_ref = ou

        @pl.when(i == 0)
        def _():
            gw_ref[...] = jnp.zeros_like(gw_ref)

        gw = jnp.zeros((1, DK), F32)
        for h in range(heads):
            sl = slice(h * DK, (h + 1) * DK)
            oh, z, dyh = o_ref[:, sl], z_ref[:, sl].astype(F32), dy[:, sl]
            sz = _sig(z)
            rs = lax.rsqrt(jnp.mean(oh * oh, axis=-1, keepdims=True) + EPS)
            xh = oh * rs
            dn = dyh * (z * sz)
            dz_ref[:, sl] = (dyh * (xh * w_ref[...]) * _dsilu(z, sz)).astype(dz_ref.dtype)
            dxh = dn * w_ref[...]
            do_ref[:, sl] = rs * (dxh - xh * jnp.mean(dxh * xh, axis=-1, keepdims=True))
            gw = gw + jnp.sum(dn * xh, axis=0, keepdims=True)
        gw_ref[0:1, :] += gw

    row = pl.BlockSpec((r, d), lambda i, kk: (i, 0))
    za = pl.BlockSpec((r, d), lambda i, kk: (i, 3))
    return _mm_rows(
        dya, w_dn, nt=True, tm=r, tk=d,
        extras=[(o, row), (proj, za), (w, pl.BlockSpec((1, DK), lambda i, kk: (0, 0))), (dproj, ANY)],
        outs=[(jax.ShapeDtypeStruct((t, d), F32), row), (jax.ShapeDtypeStruct(dproj.shape, dproj.dtype), za),
              (jax.ShapeDtypeStruct((8, DK), F32), pl.BlockSpec((8, DK), lambda i, kk: (0, 0)))],
        aliases={3: 1}, epilogue=epilogue, name=name)


def _glu_window(g_ref, gh_ref, i, d, cs):
    cs2 = slice(d + cs.start, d + cs.stop)
    cur = g_ref[:, cs].astype(F32) * _sig(g_ref[:, cs2].astype(F32))
    prev = gh_ref[HALO - 32:HALO, cs].astype(F32) * _sig(gh_ref[HALO - 32:HALO, cs2].astype(F32))
    return jnp.concatenate([jnp.where(i > 0, prev, 0.0), cur], axis=0)


def _ln_stats(c1):
    mu = jnp.mean(c1, axis=-1, keepdims=True)
    cc = c1 - mu
    return cc * lax.rsqrt(jnp.mean(cc * cc, axis=-1, keepdims=True) + EPS), lax.rsqrt(jnp.mean(cc * cc, axis=-1, keepdims=True) + EPS)


def _cf_fwd(proj, dw_w, dw_b, ln_w, ln_b, d, name):
    t = proj.shape[0]
    r = _conv_rows(t)

    def body(g_ref, gh_ref, zb_ref, w_ref, b_ref, lw_ref, lb_ref, c1_ref, c3_ref):
        for cb in range(d // LANES):
            cs = slice(cb * LANES, (cb + 1) * LANES)
            win = _glu_window(g_ref, gh_ref, pl.program_id(0), d, cs)
            acc = jnp.broadcast_to(b_ref[:, cs], (r, LANES))
            for sub in range(8):
                rot = _shift(win, sub)
                for s in range(sub, CF_TAPS, 8):
                    acc = acc + rot[32 - (s - sub):32 - (s - sub) + r, :] * w_ref[CF_TAPS - 1 - s:CF_TAPS - s, cs]
            c1_ref[:, cs] = acc
        xh, _ = _ln_stats(c1_ref[...])
        ln = xh * lw_ref[...] + lb_ref[...]
        zb = zb_ref[...].astype(F32)
        c3_ref[...] = ((ln * _sig(ln)) * (zb * _sig(zb))).astype(c3_ref.dtype)

    row = pl.BlockSpec((r, d), lambda i: (i, 0))
    par = pl.BlockSpec((1, d), lambda i: (0, 0))
    return pl.pallas_call(
        body, name=name, grid=(t // r,),
        in_specs=[pl.BlockSpec((r, 2 * d), lambda i: (i, 2)), pl.BlockSpec((HALO, 2 * d), lambda i: (_halo_prev(r)(i)[0], 2)),
                  pl.BlockSpec((r, d), lambda i: (i, 8)), pl.BlockSpec((32, d), lambda i: (0, 0)), par, par, par],
        out_specs=[row, row],
        out_shape=[jax.ShapeDtypeStruct((t, d), F32), jax.ShapeDtypeStruct((t, d), MM_DTYPE)],
        compiler_params=_cparams("parallel"),
    )(proj, proj, proj, dw_w, dw_b, ln_w, ln_b)


def _cf_bwd1(dyb, w_cf, c1, proj, ln_w, ln_b, dproj, d, name):
    t = c1.shape[0]
    r = _row_tile(t)

    def epilogue(dc3v, i, ex, ou):
        c1_ref, zb_ref, lw_ref, lb_ref, _ = ex
        dc1_ref, dzb_ref, g_ref = ou

        @pl.when(i == 0)
        def _():
            g_ref[...] = jnp.zeros_like(g_ref)

        xh, rs = _ln_stats(c1_ref[...])
        ln = xh * lw_ref[...] + lb_ref[...]
        sl, zb = _sig(ln), zb_ref[...].astype(F32)
        szb = _sig(zb)
        dzb_ref[...] = (dc3v * (ln * sl) * _dsilu(zb, szb)).astype(dzb_ref.dtype)
        dln = dc3v * (zb * szb) * _dsilu(ln, sl)
        dxh = dln * lw_ref[...]
        dc1 = rs * (dxh - jnp.mean(dxh, axis=-1, keepdims=True) - xh * jnp.mean(dxh * xh, axis=-1, keepdims=True))
        dc1_ref[...] = dc1
        g_ref[0:1, :] += jnp.sum(dln * xh, axis=0, keepdims=True)
        g_ref[1:2, :] += jnp.sum(dln, axis=0, keepdims=True)
        g_ref[2:3, :] += jnp.sum(dc1, axis=0, keepdims=True)

    row = pl.BlockSpec((r, d), lambda i, kk: (i, 0))
    zbs = pl.BlockSpec((r, d), lambda i, kk: (i, 8))
    par = pl.BlockSpec((1, d), lambda i, kk: (0, 0))
    return _mm_rows(
        dyb, w_cf, nt=True, tm=r, tk=d, extras=[(c1, row), (proj, zbs), (ln_w, par), (ln_b, par), (dproj, ANY)],
        outs=[(jax.ShapeDtypeStruct((t, d), F32), row), (jax.ShapeDtypeStruct(dproj.shape, dproj.dtype), zbs),
              (jax.ShapeDtypeStruct((8, d), F32), pl.BlockSpec((8, d), lambda i, kk: (0, 0)))],
        aliases={4: 1}, epilogue=epilogue, name=name)


def _cf_bwd2(dc1, proj, dw_w, dproj, d, name):
    t = dc1.shape[0]
    r = _conv_rows(t)
    last = t // r - 1

    def body(dc_ref, dcn_ref, g_ref, gh_ref, w_ref, _, dg_ref, gw_ref):
        i = pl.program_id(0)

        @pl.when(i == 0)
        def _():
            gw_ref[...] = jnp.zeros_like(gw_ref)

        for cb in range(d // LANES):
            cs = slice(cb * LANES, (cb + 1) * LANES)
            cs2 = slice(d + cs.start, d + cs.stop)
            dcur = dc_ref[:, cs]
            dwin = jnp.concatenate([dcur, jnp.where(i < last, dcn_ref[0:32, cs], 0.0)], axis=0)
            win = _glu_window(g_ref, gh_ref, i, d, cs)
            acc = None
            for sub in range(8):
                drot = _shift(dwin, -sub)
                rot = _shift(win, sub)
                for s in range(sub, CF_TAPS, 8):
                    j = CF_TAPS - 1 - s
                    term = drot[s - sub:s - sub + r, :] * w_ref[j:j + 1, cs]
                    acc = term if acc is None else acc + term
                    gw_ref[j:j + 1, cs] += jnp.sum(dcur * rot[32 - (s - sub):32 - (s - sub) + r, :], axis=0, keepdims=True)
            ga, sb = g_ref[:, cs].astype(F32), _sig(g_ref[:, cs2].astype(F32))
            dg_ref[:, cs] = (acc * sb).astype(dg_ref.dtype)
            dg_ref[:, cs2] = (acc * ga * sb * (1.0 - sb)).astype(dg_ref.dtype)

    row = pl.BlockSpec((r, d), lambda i: (i, 0))
    glu = pl.BlockSpec((r, 2 * d), lambda i: (i, 2))
    return pl.pallas_call(
        body, name=name, grid=(t // r,),
        in_specs=[row, pl.BlockSpec((HALO, d), _halo_next(r, t)), glu,
                  pl.BlockSpec((HALO, 2 * d), lambda i: (_halo_prev(r)(i)[0], 2)), pl.BlockSpec((32, d), lambda i: (0, 0)), ANY],
        out_specs=[glu, pl.BlockSpec((32, d), lambda i: (0, 0))],
        out_shape=[jax.ShapeDtypeStruct(dproj.shape, dproj.dtype), jax.ShapeDtypeStruct((32, d), F32)],
        input_output_aliases={5: 0}, compiler_params=_cparams("arbitrary"),
    )(dc1, dc1, proj, proj, dw_w, dproj)


def _merge_bwd(dxo_mm, w_o, proj, ya, yb, b, d, name):
    t = ya.shape[0]
    r = _row_tile(t)

    def epilogue(dmv, i, ex, ou):
        g_ref, ya_ref, yb_ref, b_ref = ex
        dya_ref, dyb_ref, dg_ref, gb_ref = ou

        @pl.when(i == 0)
        def _():
            gb_ref[...] = jnp.zeros_like(gb_ref)

        sa, sb = _sig(g_ref[:, :d].astype(F32)), _sig(g_ref[:, d:].astype(F32))
        dyb = dmv * sb
        dya_ref[...] = (dmv * sa).astype(dya_ref.dtype)
        dyb_ref[...] = dyb.astype(dyb_ref.dtype)
        dg_ref[:, :d] = (dmv * ya_ref[...] * sa * (1.0 - sa)).astype(dg_ref.dtype)
        dg_ref[:, d:] = (dmv * (yb_ref[...] + b_ref[...]) * sb * (1.0 - sb)).astype(dg_ref.dtype)
        gb_ref[0:1, :] += jnp.sum(dyb, axis=0, keepdims=True)

    row = pl.BlockSpec((r, d), lambda i, kk: (i, 0))
    gate = pl.BlockSpec((r, 2 * d), lambda i, kk: (i, 3))
    const = lambda shape: pl.BlockSpec(shape, lambda i, kk: (0, 0))
    return _mm_rows(
        dxo_mm, w_o, nt=True, tm=r, tk=d, extras=[(proj, gate), (ya, row), (yb, row), (b, const((1, d)))],
        outs=[(jax.ShapeDtypeStruct((t, d), MM_DTYPE), row), (jax.ShapeDtypeStruct((t, d), MM_DTYPE), row),
              (jax.ShapeDtypeStruct((t, 9 * d), MM_DTYPE), gate), (jax.ShapeDtypeStruct((8, d), F32), const((8, d)))],
        epilogue=epilogue, name=name)


def _place():
    return lax.axis_index("x"), lax.axis_index("y"), lax.axis_index("c")


def _slot(p):
    return 4 * p[0] + 2 * p[1] + p[2]


def _allgather(blocks, name):
    n = len(blocks)

    def body(*refs):
        ins, outs = refs[:n], refs[n:2 * n]
        send_sems, recv_sems, local_sems = refs[2 * n:]
        x, y, c = _place()
        me, sibling = (x, y, c), (x, y, 1 - c)
        chips = [(1 - x, y), (x, 1 - y), (1 - x, 1 - y)]

        def copy(a, k, block, to, src=None):
            dst = outs[a].at[_slot(block)]
            return pltpu.make_async_remote_copy(
                src_ref=dst if src is None else src, dst_ref=dst, send_sem=send_sems.at[7 * a + k],
                recv_sem=recv_sems.at[7 * a + k], device_id=to, device_id_type=MESH)

        mine = [pltpu.make_async_copy(ins[a], outs[a].at[_slot(me)], local_sems.at[a]) for a in range(n)]
        for cp in mine:
            cp.start()
        first = []
        for a in range(n):
            first.append(copy(a, 0, me, sibling, src=ins[a]))
            first += [copy(a, 1 + j, me, (*chip, c), src=ins[a]) for j, chip in enumerate(chips)]
        for cp in first:
            cp.start()
        passed = []
        for j, chip in enumerate(chips):
            for a in range(n):
                copy(a, 1 + j, (*chip, c), me).wait_recv()
                cp = copy(a, 4 + j, (*chip, c), sibling)
                cp.start()
                passed.append(cp)
        for a in range(n):
            copy(a, 0, sibling, me).wait_recv()
            for j, chip in enumerate(chips):
                copy(a, 4 + j, (*chip, 1 - c), me).wait_recv()
        for cp in first + passed:
            cp.wait_send()
        for cp in mine:
            cp.wait()

    vmem = pl.BlockSpec(memory_space=pltpu.VMEM)
    return pl.pallas_call(
        body, name=name, in_specs=[vmem] * n, out_specs=[vmem] * n,
        out_shape=[jax.ShapeDtypeStruct((8, *b.shape), b.dtype) for b in blocks],
        scratch_shapes=[pltpu.SemaphoreType.DMA((7 * n,)), pltpu.SemaphoreType.DMA((7 * n,)), pltpu.SemaphoreType.DMA((n,))],
        compiler_params=pltpu.CompilerParams(vmem_limit_bytes=VMEM_LIMIT),
    )(*blocks)


_FLIPS = [(0, 0, 1), (0, 1, 0), (0, 1, 1), (1, 0, 0), (1, 0, 1), (1, 1, 0), (1, 1, 1)]


_HBM = pl.BlockSpec(memory_space=pltpu.HBM)
_SEM = pl.BlockSpec(memory_space=pltpu.SEMAPHORE)
_EFFECT = pltpu.SideEffectType.DATAFLOW_SIDE_EFFECTING


def _peers(me):
    return [tuple(1 - me[ax] if f[ax] else me[ax] for ax in range(3)) for f in _FLIPS]


def _alltoall_copy(srcs, lands, send_sems, recv_sems, a, k, me, peers, own):
    dst = lands[a].at[_slot(me if own else peers[k])]
    src = srcs[a].at[0 if srcs[a].shape[0] == 1 else _slot(peers[k])]
    return pltpu.make_async_remote_copy(
        src_ref=src, dst_ref=dst, send_sem=send_sems.at[7 * a + k], recv_sem=recv_sems.at[7 * a + k],
        device_id=peers[k], device_id_type=MESH)


def _alltoall_start(arrays, lands, name):
    n = len(arrays)

    def body(*refs):
        srcs, lnds = refs[:n], refs[n:2 * n]
        send_sems, recv_sems = refs[2 * n:2 * n + 2]
        token = refs[-1]
        me = _place()
        peers = _peers(me)
        for a in range(n):
            for k in range(7):
                _alltoall_copy(srcs, lnds, send_sems, recv_sems, a, k, me, peers, True).start()
        token[...] = jnp.zeros_like(token)

    hbm = lambda v: pltpu.with_memory_space_constraint(v, pltpu.HBM)
    out = pl.pallas_call(
        body, name=name, in_specs=[_HBM] * (2 * n),
        out_specs=(_SEM, _SEM, *[_HBM] * (2 * n), pl.BlockSpec(memory_space=pltpu.VMEM)),
        out_shape=(pltpu.SemaphoreType.DMA((7 * n,)), pltpu.SemaphoreType.DMA((7 * n,)),
                   *[pltpu.HBM(v.shape, v.dtype) for v in (*arrays, *lands)], jax.ShapeDtypeStruct((8, LANES), F32)),
        input_output_aliases={i: 2 + i for i in range(2 * n)},
        compiler_params=pltpu.CompilerParams(has_side_effects=_EFFECT),
    )(*[hbm(v) for v in (*arrays, *lands)])
    return out[0], out[1], out[2:2 + 2 * n], out[-1]


def _alltoall_wait(send_sems, recv_sems, thru, after, name):
    n = len(thru) // 2

    def body(*refs):
        srcs, lnds = refs[:n], refs[n:2 * n]
        send_sems, recv_sems = refs[2 * n:2 * n + 2]
        me = _place()
        peers = _peers(me)
        for a in range(n):
            for k in range(7):
                _alltoall_copy(srcs, lnds, send_sems, recv_sems, a, k, me, peers, True).wait_send()
                _alltoall_copy(srcs, lnds, send_sems, recv_sems, a, k, me, peers, False).wait_recv()

    out = pl.pallas_call(
        body, name=name, in_specs=[*[_HBM] * (2 * n), _SEM, _SEM, ANY], out_specs=[_HBM] * (2 * n),
        out_shape=[pltpu.HBM(v.shape, v.dtype) for v in thru],
        input_output_aliases={i: i for i in range(2 * n)},
        compiler_params=pltpu.CompilerParams(has_side_effects=_EFFECT),
    )(*thru, send_sems, recv_sems, after)
    return out[n:]


def _sibling_exchange(arrays, name):
    n = len(arrays)

    def body(*refs):
        ins, outs = refs[:n], refs[n:2 * n]
        send_sems, recv_sems, local_sems = refs[2 * n:]
        x, y, c = _place()

        def copy(a, slot):
            return pltpu.make_async_remote_copy(
                src_ref=ins[a], dst_ref=outs[a].at[slot], send_sem=send_sems.at[a], recv_sem=recv_sems.at[a],
                device_id=(x, y, 1 - c), device_id_type=MESH)

        mine = [pltpu.make_async_copy(ins[a], outs[a].at[c], local_sems.at[a]) for a in range(n)]
        for cp in mine:
            cp.start()
        sent = [copy(a, c) for a in range(n)]
        for cp in sent:
            cp.start()
        for a in range(n):
            copy(a, 1 - c).wait_recv()
        for cp in sent:
            cp.wait_send()
        for cp in mine:
            cp.wait()

    vmem = pl.BlockSpec(memory_space=pltpu.VMEM)
    return pl.pallas_call(
        body, name=name, in_specs=[vmem] * n, out_specs=[vmem] * n,
        compiler_params=pltpu.CompilerParams(vmem_limit_bytes=VMEM_LIMIT),
        out_shape=[jax.ShapeDtypeStruct((2, *a.shape), a.dtype) for a in arrays],
        scratch_shapes=[pltpu.SemaphoreType.DMA((n,)), pltpu.SemaphoreType.DMA((n,)), pltpu.SemaphoreType.DMA((n,))],
    )(*arrays)


def _sum8(parts, name):
    _, rr, cc = parts.shape
    r = _pick(rr, (128, 64, 32, 16, 8))

    def body(p_ref, o_ref):
        acc = p_ref[0].astype(F32)
        for j in range(1, 8):
            acc = acc + p_ref[j].astype(F32)
        o_ref[...] = acc

    return pl.pallas_call(
        body, name=name, grid=(rr // r,), in_specs=[pl.BlockSpec((8, r, cc), lambda i: (0, i, 0))],
        out_specs=pl.BlockSpec((r, cc), lambda i: (i, 0)), out_shape=jax.ShapeDtypeStruct((rr, cc), F32),
        compiler_params=_cparams("parallel"),
    )(parts)


def _adamw(g, w, m, v, name):
    rr, cc = g.shape
    r = _pick(rr, (128, 64, 32, 16, 8))

    def body(g_ref, w_ref, m_ref, v_ref, d_ref, nm_ref, nv_ref):
        gv = g_ref[...]
        nm = ADAM_B1 * m_ref[...] + (1.0 - ADAM_B1) * gv
        nv = ADAM_B2 * v_ref[...] + (1.0 - ADAM_B2) * (gv * gv)
        m_hat = nm / (1.0 - ADAM_B1 ** ADAM_STEP)
        v_hat = nv / (1.0 - ADAM_B2 ** ADAM_STEP)
        d_ref[...] = -ADAM_LR * (m_hat / (jnp.sqrt(v_hat) + ADAM_EPS) + ADAM_WD * w_ref[...])
        nm_ref[...] = nm
        nv_ref[...] = nv

    blk = pl.BlockSpec((r, cc), lambda i: (i, 0))
    return pl.pallas_call(
        body, name=name, grid=(rr // r,), in_specs=[blk] * 4, out_specs=[blk] * 3,
        out_shape=[jax.ShapeDtypeStruct((rr, cc), F32)] * 3, compiler_params=_cparams("parallel"),
    )(g, w, m, v)


def _columns(sources, start, stop):
    out, at = [], 0
    for arr, off, width in sources:
        lo, hi = max(start, at), min(stop, at + width)
        if lo < hi:
            out.append(arr[:, off + lo - at:off + hi - at])
        at += width
    return out


def _flat_pack(parts, width):
    flat = jnp.concatenate([p.reshape(-1) for p in parts])
    total = -(-flat.shape[0] // (8 * width)) * (8 * width)
    return jnp.pad(flat, (0, total - flat.shape[0])).reshape(-1, width)


def _flat_unpack(pack, shapes):
    flat = pack.reshape(-1)
    out, at = [], 0
    for s in shapes:
        size = 1
        for e in s:
            size *= e
        out.append(flat[at:at + size].reshape(s))
        at += size
    return out


def kernel(x, meta, norm_w, w_in, conv_qkv_w, a_log, dt_bias, dn_norm_w, w_dn_out, dw_w, dw_b, ln_w, ln_b, w_cf_out, b_cf_out, w_o, final_norm_w, loss_target, m_meta, m_norm_w, m_w_in, m_conv_qkv_w, m_a_log, m_dt_bias, m_dn_norm_w, m_w_dn_out, m_dw_w, m_dw_b, m_ln_w, m_ln_b, m_w_cf_out, m_b_cf_out, m_w_o, m_final_norm_w, v_meta, v_norm_w, v_w_in, v_conv_qkv_w, v_a_log, v_dt_bias, v_dn_norm_w, v_w_dn_out, v_dw_w, v_dw_b, v_ln_w, v_ln_b, v_w_cf_out, v_b_cf_out, v_w_o, v_final_norm_w):
    d = x.shape[-1]
    heads = a_log.shape[-1]
    seq = x.shape[1]
    t = HDR + seq
    nc = t // CHUNK
    in_w = 9 * d + 2 * heads
    assert heads * DK == d and seq % LANES == 0 and w_in.shape[-1] * 4 == in_w
    xi, yi, ci = _place()
    shard = 2 * xi + yi
    ds = d // 4

    w_in_half = lax.dynamic_slice_in_dim(w_in[0].astype(MM_DTYPE), ci * (d // 2), d // 2, axis=0)
    w3_half = lax.dynamic_slice_in_dim(jnp.stack([w_dn_out[0], w_cf_out[0], w_o[0]]).astype(MM_DTYPE), ci * (d // 8), d // 8, axis=1)
    small = jnp.concatenate([
        jnp.pad(conv_qkv_w[0], ((0, 4), (0, 0))), jnp.pad(meta, ((0, 0), (0, 2 * ds))),
        jnp.pad(dw_w[0], ((0, 1), (0, 2 * ds)))], axis=0)
    g_in, g_small = _allgather([w_in_half, small], "gather_weights")
    my_slot = _slot((xi, yi, ci))
    w3_land = lax.dynamic_update_slice_in_dim(lax.empty((8, *w3_half.shape), w3_half.dtype), w3_half[None], my_slot, axis=0)
    w3_send, w3_recv, w3_in_flight, w3_token = _alltoall_start([w3_half[None]], [w3_land], "w3_start")
    sw = in_w // 4
    g_in = g_in.reshape(4, d, sw)
    w_cols = [(g_in[s], 0, sw) for s in range(4)]
    o_glu = 4 * d + 2 * heads
    w_main = jnp.concatenate(
        _columns(w_cols, 0, 4 * d) + _columns(w_cols, o_glu, o_glu + 2 * d) + _columns(w_cols, o_glu + 3 * d, o_glu + 5 * d)
        + _columns(w_cols, o_glu + 2 * d, o_glu + 3 * d), axis=1)
    w_ba = jnp.pad(jnp.concatenate(_columns(w_cols, 4 * d, o_glu), axis=1), ((0, 0), (0, LANES - 2 * heads)))
    small4 = g_small[0::2]
    conv_w_f = small4[:, 0:DN_TAPS, :].transpose(1, 0, 2).reshape(DN_TAPS, 3 * d)
    meta_f = small4[:, 8:8 + N_META, :ds].transpose(1, 0, 2).reshape(N_META, d)
    dw_w_f = small4[:, 24:56, :ds].transpose(1, 0, 2).reshape(32, d)

    xs = jnp.concatenate([jnp.pad(meta_f, ((PAD_ROWS, 0), (0, 0))), x[0]], axis=0)
    tgt = loss_target[0]
    h = _rms_fwd(xs, norm_w + w3_token[0:1, 0:1], "rms_fwd")
    proj = _mm(h, w_main, out_dtype=MM_DTYPE, name="proj_main")
    pba = _mm(h, w_ba, name="proj_ba")
    qkvn = _dn_prep(proj, conv_w_f, d, heads, "dn_prep")
    pvec = jnp.zeros((8, LANES), F32).at[0, heads:2 * heads].set(a_log[0]).at[1, heads:2 * heads].set(dt_bias[0])
    bg = _ba_fwd(pba, pvec, heads, "ba_fwd")
    g_row = bg[:, heads:2 * heads].reshape(nc, CHUNK, heads).transpose(0, 2, 1)
    o, sall, ainv = _dn_fwd(qkvn, bg, g_row, d, heads, "dn_fwd")
    g_w3, = _alltoall_wait(w3_send, w3_recv, w3_in_flight, o, "w3_wait")
    w3_full = g_w3.transpose(1, 0, 2, 3).reshape(3, d, d)
    w_dn_f, w_cf_f, w_o_f = w3_full[0], w3_full[1], w3_full[2]
    ya_in, ya = _dn_post(o, proj, dn_norm_w, w_dn_f, d, heads, "dn_post")
    c1, c3 = _cf_fwd(proj, dw_w_f, dw_b, ln_w, ln_b, d, "cf_fwd")
    yb = _mm(c3, w_cf_f, name="yb")
    dxo, dxo_mm, loss_acc, g_fnw, merged = _merge_final(ya, yb, b_cf_out, proj, w_o_f, xs, final_norm_w.reshape(1, d), tgt,
                                                        "merge_final")

    gw_o = _mm(merged, dxo_mm, ta=True, out_dtype=MM_DTYPE, name="gw_o")
    dya, dyb, dproj, g_bcf = _merge_bwd(dxo_mm, w_o_f, proj, ya, yb, b_cf_out, d, "merge_bwd")
    gw_cf = _mm(c3, dyb, ta=True, out_dtype=MM_DTYPE, name="gw_cf")
    gw_dn = _mm(ya_in, dya, ta=True, out_dtype=MM_DTYPE, name="gw_dn")
    own_home = lambda v: lax.dynamic_update_slice_in_dim(
        lax.empty(v.shape, v.dtype), lax.dynamic_slice_in_dim(v, my_slot, 1, axis=0), my_slot, axis=0)
    send_w3 = jnp.stack([gw_dn, gw_cf, gw_o]).reshape(3, 8, d // 8, d).transpose(1, 0, 2, 3).reshape(8, 3 * d // 8, d)
    gw3_send, gw3_recv, gw3_in_flight, _ = _alltoall_start([send_w3], [own_home(send_w3)], "exchange_w3_start")
    dc1, dproj, g_ln = _cf_bwd1(dyb, w_cf_f, c1, proj, ln_w, ln_b, dproj, d, "cf_bwd1")
    dproj, g_dww = _cf_bwd2(dc1, proj, dw_w_f, dproj, d, "cf_bwd2")
    do, dproj, g_dnw = _dn_post_bwd(dya, w_dn_f, o, proj, dn_norm_w, dproj, d, heads, "dn_post_bwd")
    dqkvn, dbg, dg_row = _dn_bwd(qkvn, bg, g_row, do, sall, ainv, d, heads, "dn_bwd")
    dproj, g_convw = _dn_prep_bwd(proj, conv_w_f, dqkvn, dproj, d, heads, "dn_prep_bwd")
    dbg = dbg + jnp.pad(dg_row.transpose(0, 2, 1).reshape(t, heads), ((0, 0), (heads, LANES - 2 * heads)))
    dpba, g_ba = _ba_bwd(pba, pvec, dbg, heads, "ba_bwd")
    gw_main = _mm(h, dproj, ta=True, out_dtype=MM_DTYPE, name="gw_main")
    gw_ba = _mm(h, dpba, ta=True, out_dtype=MM_DTYPE, name="gw_ba")

    g_cols = [(gw_main, 0, 4 * d), (gw_ba, 0, 2 * heads), (gw_main, 4 * d, 2 * d), (gw_main, 8 * d, d),
              (gw_main, 6 * d, 2 * d)]
    send_in = jnp.stack([jnp.concatenate(_columns(g_cols, s * sw, (s + 1) * sw), axis=1) for s in range(4)])
    send_in = send_in.reshape(8, d // 2, sw)
    send_sems, recv_sems, in_flight, token = _alltoall_start([send_in], [own_home(send_in)], "exchange_start")
    dxs, g_nw = _dh_rms_bwd(dproj, w_main, dpba, w_ba, xs, norm_w, dxo, token, "dh_rms_bwd")
    grad_x = dxs[HDR:][None]
    got_w3, = _alltoall_wait(gw3_send, gw3_recv, gw3_in_flight, dxs, "exchange_w3_wait")
    got_in, = _alltoall_wait(send_sems, recv_sems, in_flight, dxs, "exchange_wait")

    wide = lambda a: jnp.pad(a, ((0, 0), (0, d - a.shape[1])))
    small_g = jnp.concatenate([
        g_convw.reshape(24, d), dxs[PAD_ROWS:HDR], g_dww, g_nw, g_ln, g_bcf, g_fnw, wide(g_ba), wide(g_dnw), wide(loss_acc)],
        axis=0)
    red_in = _sum8(got_in, "sum_w_in")
    red_w3 = _sum8(got_w3, "sum_w3")
    pair_in, pair_w3 = _sibling_exchange([red_in, red_w3], "pair_grads")
    g_w_in = pair_in.reshape(d, in_w // 4)
    g_w3 = pair_w3.reshape(2, 3, d // 8, d).transpose(1, 0, 2, 3).reshape(3 * ds, d)
    all_small, = _allgather([small_g], "gather_small_grads")
    sg = _sum8(all_small, "sum_small")
    g_conv = lax.dynamic_slice_in_dim(sg[0:24].reshape(8, 3 * d)[0:DN_TAPS], shard * 3 * ds, 3 * ds, axis=1)
    g_meta = lax.dynamic_slice_in_dim(sg[24:40], shard * ds, ds, axis=1)
    g_dw_w = lax.dynamic_slice_in_dim(sg[40:40 + CF_TAPS], shard * ds, ds, axis=1)
    g_rep = {"norm_w": sg[72:73], "ln_w": sg[80:81], "ln_b": sg[81:82], "dw_b": sg[82:83], "b_cf_out": sg[88:89],
             "final_norm_w": sg[96], "a_log": sg[104:105, heads:2 * heads], "dt_bias": sg[105:106, heads:2 * heads],
             "dn_norm_w": sg[112:113, 0:DK]}
    loss = sg[120, 0]

    res = {}
    dl, nm, nv = _adamw(g_w_in, w_in[0], m_w_in[0], v_w_in[0], "adamw_w_in")
    res["w_in"] = (g_w_in[None], dl[None], nm[None], nv[None])
    stack3 = lambda a, b, c: jnp.concatenate([a[0], b[0], c[0]], axis=0)
    dl, nm, nv = _adamw(g_w3, stack3(w_dn_out, w_cf_out, w_o), stack3(m_w_dn_out, m_w_cf_out, m_w_o),
                        stack3(v_w_dn_out, v_w_cf_out, v_w_o), "adamw_w3")
    for j, nme in enumerate(("w_dn_out", "w_cf_out", "w_o")):
        res[nme] = tuple(a[j * ds:(j + 1) * ds][None] for a in (g_w3, dl, nm, nv))
    names = ["meta", "conv_qkv_w", "dw_w", "norm_w", "dw_b", "ln_w", "ln_b", "b_cf_out", "final_norm_w", "a_log", "dt_bias",
             "dn_norm_w"]
    grads = {"meta": g_meta, "conv_qkv_w": g_conv[None], "dw_w": g_dw_w[None], **g_rep}
    given = dict(meta=(meta, m_meta, v_meta), conv_qkv_w=(conv_qkv_w, m_conv_qkv_w, v_conv_qkv_w), dw_w=(dw_w, m_dw_w, v_dw_w),
                 norm_w=(norm_w, m_norm_w, v_norm_w), dw_b=(dw_b, m_dw_b, v_dw_b), ln_w=(ln_w, m_ln_w, v_ln_w),
                 ln_b=(ln_b, m_ln_b, v_ln_b), b_cf_out=(b_cf_out, m_b_cf_out, v_b_cf_out),
                 final_norm_w=(final_norm_w, m_final_norm_w, v_final_norm_w), a_log=(a_log, m_a_log, v_a_log),
                 dt_bias=(dt_bias, m_dt_bias, v_dt_bias), dn_norm_w=(dn_norm_w, m_dn_norm_w, v_dn_norm_w))
    shapes = [given[nme][0].shape for nme in names]
    packs = [_flat_pack([grads[nme] for nme in names], LANES)] + [_flat_pack([given[nme][j] for nme in names], LANES) for j in range(3)]
    outs = _adamw(*packs, "adamw_small")
    unpacked = [_flat_unpack(p, shapes) for p in (packs[0], *outs)]
    for j, nme in enumerate(names):
        res[nme] = tuple(u[j] for u in unpacked)

    order = ["meta", "norm_w", "w_in", "conv_qkv_w", "a_log", "dt_bias", "dn_norm_w", "w_dn_out", "dw_w", "dw_b", "ln_w", "ln_b",
             "w_cf_out", "b_cf_out", "w_o", "final_norm_w"]
    return (loss, grad_x, *[res[nme][0] for nme in order], *[res[nme][1] for nme in order],
            *[res[nme][2] for nme in order], *[res[nme][3] for nme in order])
```
